```python
import math
import jax
import jax.numpy as jnp
from jax import lax
import numpy as np

D_MODEL = 4096
BATCH = 16
SEQ = 256
DEPTH = 2
DEC_BATCH = 4
DEC_SEQ = 2048
PAST_LEN = 256

GRID_W = 64
HEAD_DIM = 128
N_HEADS = D_MODEL // 256
ATTN_WIDTH = N_HEADS * HEAD_DIM
ATTN_SCALE = HEAD_DIM ** -0.5
WIN_R = 8
WIN_C = 16
Q_COLS = 16
K_COLS = 32
N_COL_BLOCKS = GRID_W // Q_COLS
CTX_Q_BLOCK = 128
NEG_INF = -1e30
POOL_WIDTH = D_MODEL // 2
POOL_SIZES = (2, 4, 8, 16)
POOL_GROUP = POOL_WIDTH // len(POOL_SIZES)
HYENA_WIDTH = D_MODEL // 2
HYENA_ORDER = 2
FILTER_EMB = 33
FILTER_HIDDEN = 64
DECAY_TARGET = 1e-2
FAST_DECAY_PCT = 0.3
SLOW_DECAY_PCT = 1.5
D_FF = ((8 * D_MODEL + 767) // 768) * 256
IN_SPLITS = (POOL_WIDTH,
             POOL_WIDTH + ATTN_WIDTH,
             POOL_WIDTH + 2 * ATTN_WIDTH,
             POOL_WIDTH + 3 * ATTN_WIDTH,
             POOL_WIDTH + 3 * ATTN_WIDTH + (HYENA_ORDER + 1) * HYENA_WIDTH)
IN_WIDTH = IN_SPLITS[-1] + 3 * D_MODEL

WEIGHT_NAMES = ("ada_w", "ada_b", "norm1_g", "norm2_g", "w_in", "pool_w", "pool_scale",
                "q_norm_g", "k_norm_g", "rpb", "hy_conv_w", "hy_conv_b", "filt_w1", "filt_b1",
                "filt_w2", "filt_b2", "filt_w3", "filt_freq", "hy_bias", "w_branch_pool",
                "w_branch_attn", "w_branch_hyena", "w_out", "w_gate", "w_up", "w_down")

kernel_name = "hybrid_pool_natten_hyena_diffusion_step"


def _rms(x, g, eps=1e-6):
    xf = x.astype(jnp.float32)
    y = xf * lax.rsqrt(jnp.mean(xf * xf, axis=-1, keepdims=True) + eps)
    return (y * g.astype(jnp.float32)).astype(x.dtype)


def _modulation(cvec, ada_w, ada_b):
    m = jax.nn.silu(cvec) @ ada_w + ada_b
    return jnp.split(m[..., None, :], 6, axis=-1)


def _pool_mixer(u, pool_w, pool_scale):
    B, L, _ = u.shape
    uf = u.astype(jnp.float32)
    cs = jnp.pad(jnp.cumsum(uf, axis=1), ((0, 0), (1, 0), (0, 0)))
    t = np.arange(L)
    outs = []
    for g, win in enumerate(POOL_SIZES):
        sl = slice(g * POOL_GROUP, (g + 1) * POOL_GROUP)
        lo = np.maximum(t - win // 2, 0)
        hi = np.minimum(t + win - win // 2, L)
        cnt = jnp.asarray((hi - lo)[None, :, None], jnp.float32)
        csg = cs[:, :, sl]
        mean = (csg[:, hi] - csg[:, lo]) / cnt
        outs.append((mean - uf[:, :, sl]).astype(u.dtype) @ pool_w[g])
    return jnp.concatenate(outs, axis=-1) * pool_scale


def _context_attention(q, k, v):
    B, L, H, Dh = q.shape
    nb = L // CTX_Q_BLOCK
    qb = jnp.moveaxis(q.reshape(B, nb, CTX_Q_BLOCK, H, Dh), 1, 0)

    def one_block(qi):
        s = jnp.einsum("bqhd,bkhd->bhqk", qi, k, preferred_element_type=jnp.float32) * ATTN_SCALE
        p = jax.nn.softmax(s, axis=-1).astype(v.dtype)
        return jnp.einsum("bhqk,bkhd->bqhd", p, v)

    o = lax.map(one_block, qb)
    return jnp.moveaxis(o, 0, 1).reshape(B, L, H * Dh)


def _col_geometry():
    qcol = np.arange(GRID_W).reshape(N_COL_BLOCKS, Q_COLS)
    kstart = np.clip(np.arange(N_COL_BLOCKS) * Q_COLS - WIN_C // 2, 0, GRID_W - K_COLS)
    kcol = kstart[:, None] + np.arange(K_COLS)
    c0 = np.clip(qcol - WIN_C // 2, 0, GRID_W - WIN_C)
    kc3 = kcol[:, None, :]
    valid = (kc3 >= c0[:, :, None]) & (kc3 < c0[:, :, None] + WIN_C)
    dc = np.clip(kc3 - qcol[:, :, None] + WIN_C - 1, 0, 2 * WIN_C - 2)
    return kcol, valid, dc


def _neighbourhood_attention(q, k, v, kc, vc, rpb):
    B, L, H, Dh = q.shape
    rows = L // GRID_W
    wr = min(WIN_R, rows)
    kcol, valid, dc = _col_geometry()
    qg = q.reshape(B, rows, N_COL_BLOCKS, Q_COLS, H, Dh)
    kg = k.reshape(B, rows, GRID_W, H, Dh)
    vg = v.reshape(B, rows, GRID_W, H, Dh)
    mask = valid[:, :, None, None, :]
    n_nb = wr * K_COLS

    def one_row(r):
        r0 = jnp.clip(r - wr // 2, 0, rows - wr)
        kn = lax.dynamic_slice_in_dim(kg, r0, wr, axis=1)[:, :, kcol]
        vn = lax.dynamic_slice_in_dim(vg, r0, wr, axis=1)[:, :, kcol]
        qr = lax.dynamic_index_in_dim(qg, r, axis=1, keepdims=False)
        dr = r0 + jnp.arange(wr) - r + WIN_R - 1
        bias = rpb[:, dr[:, None, None, None], dc[None]]
        bias = jnp.transpose(bias, (2, 3, 0, 1, 4)).astype(jnp.float32)
        s_n = jnp.einsum("bjqhd,bwjkhd->bjqhwk", qr, kn, preferred_element_type=jnp.float32) * ATTN_SCALE
        s_n = jnp.where(mask, s_n + bias, NEG_INF)
        s_c = jnp.einsum("bjqhd,bphd->bjqhp", qr, kc, preferred_element_type=jnp.float32) * ATTN_SCALE
        s = jnp.concatenate([s_n.reshape(s_n.shape[:4] + (n_nb,)), s_c], axis=-1)
        p = jax.nn.softmax(s, axis=-1).astype(v.dtype)
        p_n = p[..., :n_nb].reshape(s_n.shape)
        p_c = p[..., n_nb:]
        return (jnp.einsum("bjqhwk,bwjkhd->bjqhd", p_n, vn)
                + jnp.einsum("bjqhp,bphd->bjqhd", p_c, vc))

    o = lax.map(one_row, jnp.arange(rows))
    return jnp.moveaxis(o, 0, 1).reshape(B, L, H * Dh)


def _short_conv(u, w, b):
    up = jnp.pad(u, ((0, 0), (1, 1), (0, 0)))
    return up[:, :-2] * w[0] + up[:, 1:-1] * w[1] + up[:, 2:] * w[2] + b


def _hyena_filters(L, w1, b1, w2, b2, w3, freq):
    f32 = jnp.float32
    t = jnp.linspace(0.0, 1.0, L, dtype=f32)[:, None]
    bands = (FILTER_EMB - 1) // 2
    ang = (2.0 * math.pi / L) * jnp.arange(L, dtype=f32)[:, None] * jnp.linspace(1e-4, bands - 1, bands, dtype=f32)[None]
    z = jnp.concatenate([t, jnp.cos(ang), -jnp.sin(ang)], axis=-1)
    fr = freq.astype(f32)
    h = jnp.sin(fr * (z @ w1.astype(f32) + b1.astype(f32)))
    h = jnp.sin(fr * (h @ w2.astype(f32) + b2.astype(f32)))
    h = (h @ w3.astype(f32)).reshape(L, HYENA_ORDER, 2, HYENA_WIDTH)
    deltas = jnp.abs(jnp.linspace(math.log(DECAY_TARGET) / SLOW_DECAY_PCT,
                                  math.log(DECAY_TARGET) / FAST_DECAY_PCT, HYENA_WIDTH, dtype=f32))
    h = h * jnp.exp(-t * deltas)[:, None, None, :]
    fwd, bwd = h[:, :, 0], h[:, :, 1]
    filt = jnp.concatenate([fwd, jnp.zeros_like(fwd[:1]), bwd[:0:-1]], axis=0)
    filt = filt / jnp.sum(jnp.abs(filt), axis=0, keepdims=True)
    return jnp.fft.rfft(filt, axis=0)


def _hyena(u, conv_w, conv_b, w1, b1, w2, b2, w3, freq, hy_bias):
    B, L, _ = u.shape
    uc = _short_conv(u, conv_w, conv_b).astype(jnp.float32)
    v, x1, x2 = jnp.split(uc, HYENA_ORDER + 1, axis=-1)
    F = _hyena_filters(L, w1, b1, w2, b2, w3, freq)
    bias = hy_bias.astype(jnp.float32)
    z = v
    for o, gate in enumerate((x1, x2)):
        conv = jnp.fft.irfft(jnp.fft.rfft(z, n=2 * L, axis=1) * F[None, :, o], n=2 * L, axis=1)[:, :L]
        z = gate * (conv + bias[o] * z)
    return z.astype(u.dtype)


def _block(x, mod, p, ctx_kv):
    sh1, sc1, g1, sh2, sc2, g2 = mod
    B, L, _ = x.shape
    h = _rms(x, p["norm1_g"]) * (1 + sc1) + sh1
    u_pool, q, k, v, u_hy, gates = jnp.split(h @ p["w_in"], IN_SPLITS, axis=-1)
    q = _rms(q.reshape(B, L, N_HEADS, HEAD_DIM), p["q_norm_g"])
    k = _rms(k.reshape(B, L, N_HEADS, HEAD_DIM), p["k_norm_g"])
    v = v.reshape(B, L, N_HEADS, HEAD_DIM)
    if ctx_kv is None:
        attn = _context_attention(q, k, v)
    else:
        attn = _neighbourhood_attention(q, k, v, ctx_kv[0], ctx_kv[1], p["rpb"])
    br_pool = _pool_mixer(u_pool, p["pool_w"], p["pool_scale"]) @ p["w_branch_pool"]
    br_attn = attn @ p["w_branch_attn"]
    br_hy = _hyena(u_hy, p["hy_conv_w"], p["hy_conv_b"], p["filt_w1"], p["filt_b1"], p["filt_w2"],
                   p["filt_b2"], p["filt_w3"], p["filt_freq"], p["hy_bias"]) @ p["w_branch_hyena"]
    gp, ga, gh = jnp.split(jax.nn.sigmoid(gates), 3, axis=-1)
    mixed = (gp * br_pool + ga * br_attn + gh * br_hy) @ p["w_out"]
    x = x + g1 * mixed
    h = _rms(x, p["norm2_g"]) * (1 + sc2) + sh2
    x = x + g2 * ((jax.nn.silu(h @ p["w_gate"]) * (h @ p["w_up"])) @ p["w_down"])
    return x, k, v


def _layer_params(weights, layer):
    return dict(zip(WEIGHT_NAMES, tuple(w[layer] for w in weights)))


def setup_inputs(seed: int = 0) -> dict:
    key = jax.random.key(seed)
    keys = iter(jax.random.split(key, 40))

    def nrm(shape, scale):
        return jax.random.normal(next(keys), shape, jnp.float32) * scale

    Lq, D = DEPTH, D_MODEL
    return {
        "x_prompt": nrm((BATCH, SEQ, D), 1.0),
        "x_sample": nrm((DEC_BATCH, DEC_SEQ, D), 1.0),
        "cache_k": nrm((DEC_BATCH, DEPTH, PAST_LEN, N_HEADS, HEAD_DIM), 1.0),
        "cache_v": nrm((DEC_BATCH, DEPTH, PAST_LEN, N_HEADS, HEAD_DIM), 1.0),
        "c": nrm((DEC_BATCH, D), 1.0),
        "c_ctx": nrm((D,), 1.0),
        "ada_w": nrm((Lq, D, 6 * D), D ** -0.5),
        "ada_b": nrm((Lq, 6 * D), 0.01),
        "norm1_g": 1.0 + nrm((Lq, D), 0.02),
        "norm2_g": 1.0 + nrm((Lq, D), 0.02),
        "w_in": nrm((Lq, D, IN_WIDTH), D ** -0.5),
        "pool_w": nrm((Lq, len(POOL_SIZES), POOL_GROUP, POOL_GROUP), POOL_GROUP ** -0.5),
        "pool_scale": 1.0 + nrm((Lq, POOL_WIDTH), 0.02),
        "q_norm_g": 1.0 + nrm((Lq, HEAD_DIM), 0.02),
        "k_norm_g": 1.0 + nrm((Lq, HEAD_DIM), 0.02),
        "rpb": nrm((Lq, N_HEADS, 2 * WIN_R - 1, 2 * WIN_C - 1), 0.1),
        "hy_conv_w": nrm((Lq, 3, (HYENA_ORDER + 1) * HYENA_WIDTH), 3 ** -0.5),
        "hy_conv_b": nrm((Lq, (HYENA_ORDER + 1) * HYENA_WIDTH), 0.01),
        "filt_w1": nrm((Lq, FILTER_EMB, FILTER_HIDDEN), FILTER_EMB ** -0.5),
        "filt_b1": nrm((Lq, FILTER_HIDDEN), 0.1),
        "filt_w2": nrm((Lq, FILTER_HIDDEN, FILTER_HIDDEN), FILTER_HIDDEN ** -0.5),
        "filt_b2": nrm((Lq, FILTER_HIDDEN), 0.1),
        "filt_w3": nrm((Lq, FILTER_HIDDEN, HYENA_ORDER * 2 * HYENA_WIDTH), FILTER_HIDDEN ** -0.5),
        "filt_freq": 1.0 + nrm((Lq, FILTER_HIDDEN), 0.1),
        "hy_bias": nrm((Lq, HYENA_ORDER, HYENA_WIDTH), 0.5),
        "w_branch_pool": nrm((Lq, POOL_WIDTH, D), POOL_WIDTH ** -0.5),
        "w_branch_attn": nrm((Lq, ATTN_WIDTH, D), ATTN_WIDTH ** -0.5),
        "w_branch_hyena": nrm((Lq, HYENA_WIDTH, D), HYENA_WIDTH ** -0.5),
        "w_out": nrm((Lq, D, D), D ** -0.5),
        "w_gate": nrm((Lq, D, D_FF), D ** -0.5),
        "w_up": nrm((Lq, D, D_FF), D ** -0.5),
        "w_down": nrm((Lq, D_FF, D), D_FF ** -0.5),
    }


def reference(x_prompt, x_sample, cache_k, cache_v, c, c_ctx, ada_w, ada_b, norm1_g, norm2_g, w_in,
              pool_w, pool_scale, q_norm_g, k_norm_g, rpb, hy_conv_w, hy_conv_b, filt_w1, filt_b1,
              filt_w2, filt_b2, filt_w3, filt_freq, hy_bias, w_branch_pool, w_branch_attn,
              w_branch_hyena, w_out, w_gate, w_up, w_down):
    weights = (ada_w, ada_b, norm1_g, norm2_g, w_in, pool_w, pool_scale, q_norm_g, k_norm_g, rpb,
               hy_conv_w, hy_conv_b, filt_w1, filt_b1, filt_w2, filt_b2, filt_w3, filt_freq, hy_bias,
               w_branch_pool, w_branch_attn, w_branch_hyena, w_out, w_gate, w_up, w_down)

    y_prompt = x_prompt
    ks, vs = [], []
    for layer in range(DEPTH):
        p = _layer_params(weights, layer)
        mod = _modulation(c_ctx, p["ada_w"], p["ada_b"])
        y_prompt, k_l, v_l = _block(y_prompt, mod, p, None)
        ks.append(k_l)
        vs.append(v_l)
    new_k = jnp.stack(ks, axis=1)
    new_v = jnp.stack(vs, axis=1)

    y_sample = x_sample
    for layer in range(DEPTH):
        p = _layer_params(weights, layer)
        mod = _modulation(c, p["ada_w"], p["ada_b"])
        y_sample, _, _ = _block(y_sample, mod, p, (cache_k[:, layer], cache_v[:, layer]))

    return (y_prompt, y_sample, new_k, new_v)
```

```python
import functools
import math

import jax
import jax.numpy as jnp
from jax import lax
from jax.experimental import pallas as pl
from jax.experimental.pallas import tpu as pltpu

F32 = jnp.float32
BF16 = jnp.bfloat16

D_MODEL = 4096
DEPTH = 2
GRID_W = 64
HEAD_DIM = 128
N_HEADS = D_MODEL // 256
ATTN_WIDTH = N_HEADS * HEAD_DIM
ATTN_SCALE = HEAD_DIM ** -0.5
WIN_R = 8
WIN_C = 16
NEG_INF = -1e30
POOL_WIDTH = D_MODEL // 2
POOL_SIZES = (2, 4, 8, 16)
POOL_GROUP = POOL_WIDTH // len(POOL_SIZES)
HYENA_WIDTH = D_MODEL // 2
HYENA_ORDER = 2
FILTER_EMB = 33
FILTER_HIDDEN = 64
DECAY_TARGET = 1e-2
FAST_DECAY_PCT = 0.3
SLOW_DECAY_PCT = 1.5
D_FF = ((8 * D_MODEL + 767) // 768) * 256
COL_POOL = 0
COL_Q = POOL_WIDTH
COL_K = COL_Q + ATTN_WIDTH
COL_V = COL_K + ATTN_WIDTH
COL_HY = COL_V + ATTN_WIDTH
COL_GATES = COL_HY + (HYENA_ORDER + 1) * HYENA_WIDTH
MOD_ROWS = 8
RMS_EPS = 1e-6

LANES = 128
SUBLANES = 8
VMEM_LIMIT = 56 * 1024 * 1024


def _cparams(*sem):
    return pltpu.CompilerParams(dimension_semantics=sem, vmem_limit_bytes=VMEM_LIMIT)


def _mod_kernel(c_ref, w_ref, b_ref, o_ref):
    c = c_ref[...]
    s = (c * jax.nn.sigmoid(c)).astype(BF16)
    o_ref[...] = jnp.dot(s, w_ref[...].astype(BF16), preferred_element_type=F32) + b_ref[...]


def _modulation(cmat, ada_w, ada_b):
    tn = 512
    n6 = ada_w.shape[-1]
    return pl.pallas_call(
        _mod_kernel,
        out_shape=jax.ShapeDtypeStruct((DEPTH, MOD_ROWS, n6), F32),
        grid=(DEPTH, n6 // tn),
        in_specs=[pl.BlockSpec((MOD_ROWS, D_MODEL), lambda l, j: (0, 0)),
                  pl.BlockSpec((None, D_MODEL, tn), lambda l, j: (l, 0, j)),
                  pl.BlockSpec((None, 1, tn), lambda l, j: (l, 0, j))],
        out_specs=pl.BlockSpec((None, MOD_ROWS, tn), lambda l, j: (l, 0, j)),
        compiler_params=_cparams("parallel", "parallel"),
        name="modulation",
    )(cmat, ada_w, ada_b.reshape(DEPTH, 1, n6))


class _Stream:
    def __init__(self, n_ctx_tok, lat_len):
        self.n_ctx_tok = n_ctx_tok
        self.lat_len = lat_len

    def mod_row(self, i, tm):
        return jnp.maximum((i * tm - self.n_ctx_tok) // self.lat_len + 1, 0)

    def mod_index(self, layer, i, tm, which):
        return (layer * MOD_ROWS + self.mod_row(i, tm)) * 6 + which


def _normmod_kernel(x_ref, g_ref, sc_ref, sh_ref, o_ref):
    x = x_ref[...]
    y = x * lax.rsqrt(jnp.mean(x * x, axis=-1, keepdims=True) + RMS_EPS)
    y = y * g_ref[...]
    o_ref[...] = (y * (1.0 + sc_ref[0]) + sh_ref[0]).astype(o_ref.dtype)


def _normmod(x, g, modflat, st, layer, which_scale, which_shift):
    m_tok, d = x.shape
    tm = 256
    return pl.pallas_call(
        _normmod_kernel,
        out_shape=jax.ShapeDtypeStruct((m_tok, d), BF16),
        grid=(m_tok // tm,),
        in_specs=[pl.BlockSpec((tm, d), lambda i: (i, 0)),
                  pl.BlockSpec((None, 1, d), lambda i: (layer, 0, 0)),
                  pl.BlockSpec((1, 1, d), lambda i: (st.mod_index(layer, i, tm, which_scale), 0, 0)),
                  pl.BlockSpec((1, 1, d), lambda i: (st.mod_index(layer, i, tm, which_shift), 0, 0))],
        out_specs=pl.BlockSpec((tm, d), lambda i: (i, 0)),
        compiler_params=_cparams("parallel"),
        name="normmod",
    )(x, g.reshape(DEPTH, 1, d), modflat, modflat)


def _cast_weight_once(w_ref, wbf_ref):
    @pl.when(pl.program_id(1) == 0)
    def _():
        wbf_ref[...] = w_ref[...].astype(BF16)


def _lin_plain_kernel(x_ref, w_ref, o_ref, wbf_ref):
    _cast_weight_once(w_ref, wbf_ref)
    o_ref[...] = jnp.dot(x_ref[...], wbf_ref[...], preferred_element_type=F32).astype(o_ref.dtype)


def _lin_sigmoid_kernel(x_ref, w_ref, o_ref, wbf_ref):
    _cast_weight_once(w_ref, wbf_ref)
    acc = jnp.dot(x_ref[...], wbf_ref[...], preferred_element_type=F32)
    o_ref[...] = jax.nn.sigmoid(acc).astype(o_ref.dtype)


def _lin_rmshead_kernel(x_ref, w_ref, g_ref, o_ref, wbf_ref):
    _cast_weight_once(w_ref, wbf_ref)
    acc = jnp.dot(x_ref[...], wbf_ref[...], preferred_element_type=F32)
    g = g_ref[...]
    for h in range(acc.shape[1] // HEAD_DIM):
        a = acc[:, h * HEAD_DIM:(h + 1) * HEAD_DIM]
        y = a * lax.rsqrt(jnp.mean(a * a, axis=-1, keepdims=True) + RMS_EPS)
        o_ref[:, h * HEAD_DIM:(h + 1) * HEAD_DIM] = (y * g).astype(o_ref.dtype)


def _lin_residual_kernel(x_ref, w_ref, res_ref, gate_ref, o_ref, wbf_ref):
    _cast_weight_once(w_ref, wbf_ref)
    acc = jnp.dot(x_ref[...], wbf_ref[...], preferred_element_type=F32)
    o_ref[...] = res_ref[...] + gate_ref[0] * acc


def _linear(x, w, layer, col_off, ncols, *, tm, tn, kind, out_dtype, head_gain=None,
            residual=None, modflat=None, st=None, which_gate=None):
    m_tok, k = x.shape
    cb = col_off // tn
    in_specs = [pl.BlockSpec((tm, k), lambda n, m: (m, 0)),
                pl.BlockSpec((None, k, tn), lambda n, m: (layer, 0, cb + n))]
    args = [x, w]
    if kind == "plain":
        body = _lin_plain_kernel
    elif kind == "sigmoid":
        body = _lin_sigmoid_kernel
    elif kind == "rmshead":
        body = _lin_rmshead_kernel
        in_specs.append(pl.BlockSpec((None, 1, HEAD_DIM), lambda n, m: (layer, 0, 0)))
        args.append(head_gain.reshape(DEPTH, 1, HEAD_DIM))
    else:
        body = _lin_residual_kernel
        in_specs.append(pl.BlockSpec((tm, tn), lambda n, m: (m, n)))
        in_specs.append(pl.BlockSpec(
            (1, 1, tn), lambda n, m: (st.mod_index(layer, m, tm, which_gate), 0, n)))
        args += [residual, modflat]
    return pl.pallas_call(
        body,
        out_shape=jax.ShapeDtypeStruct((m_tok, ncols), out_dtype),
        grid=(ncols // tn, m_tok // tm),
        in_specs=in_specs,
        out_specs=pl.BlockSpec((tm, tn), lambda n, m: (m, n)),
        scratch_shapes=[pltpu.VMEM((k, tn), BF16)],
        compiler_params=_cparams("parallel", "arbitrary"),
        name="linear_" + kind,
    )(*args)


def _mix_kernel(xp_ref, xa_ref, xh_ref, wp_ref, wa_ref, wh_ref, gp_ref, ga_ref, gh_ref, o_ref,
                wpb, wab, whb):
    @pl.when(pl.program_id(1) == 0)
    def _():
        wpb[...] = wp_ref[...].astype(BF16)
        wab[...] = wa_ref[...].astype(BF16)
        whb[...] = wh_ref[...].astype(BF16)

    acc = gp_ref[...] * jnp.dot(xp_ref[...], wpb[...], preferred_element_type=F32)
    acc = acc + ga_ref[...] * jnp.dot(xa_ref[...], wab[...], preferred_element_type=F32)
    acc = acc + gh_ref[...] * jnp.dot(xh_ref[...], whb[...], preferred_element_type=F32)
    o_ref[...] = acc.astype(o_ref.dtype)


def _mix(xp, xa, xh, wp, wa, wh, gates, layer):
    m_tok, k = xp.shape
    n = wp.shape[-1]
    tm, tn = 512, 512
    nb = n // tn
    xspec = pl.BlockSpec((tm, k), lambda j, i: (i, 0))
    wspec = pl.BlockSpec((None, k, tn), lambda j, i: (layer, 0, j))
    return pl.pallas_call(
        _mix_kernel,
        out_shape=jax.ShapeDtypeStruct((m_tok, n), BF16),
        grid=(nb, m_tok // tm),
        in_specs=[xspec, xspec, xspec, wspec, wspec, wspec,
                  pl.BlockSpec((tm, tn), lambda j, i: (i, j)),
                  pl.BlockSpec((tm, tn), lambda j, i: (i, nb + j)),
                  pl.BlockSpec((tm, tn), lambda j, i: (i, 2 * nb + j))],
        out_specs=pl.BlockSpec((tm, tn), lambda j, i: (i, j)),
        scratch_shapes=[pltpu.VMEM((k, tn), BF16)] * 3,
        compiler_params=_cparams("parallel", "arbitrary"),
        name="mix",
    )(xp, xa, xh, wp, wa, wh, gates, gates, gates)


def _ffn1_kernel(x_ref, wg_ref, wu_ref, o_ref, wgb, wub):
    @pl.when(pl.program_id(1) == 0)
    def _():
        wgb[...] = wg_ref[...].astype(BF16)
        wub[...] = wu_ref[...].astype(BF16)

    x = x_ref[...]
    g = jnp.dot(x, wgb[...], preferred_element_type=F32)
    u = jnp.dot(x, wub[...], preferred_element_type=F32)
    o_ref[...] = (g * jax.nn.sigmoid(g) * u).astype(o_ref.dtype)


def _ffn1(x, wg, wu, layer):
    m_tok, k = x.shape
    n = wg.shape[-1]
    tm, tn = 512, 256
    wspec = pl.BlockSpec((None, k, tn), lambda j, i: (layer, 0, j))
    return pl.pallas_call(
        _ffn1_kernel,
        out_shape=jax.ShapeDtypeStruct((m_tok, n), BF16),
        grid=(n // tn, m_tok // tm),
        in_specs=[pl.BlockSpec((tm, k), lambda j, i: (i, 0)), wspec, wspec],
        out_specs=pl.BlockSpec((tm, tn), lambda j, i: (i, j)),
        scratch_shapes=[pltpu.VMEM((k, tn), BF16)] * 2,
        compiler_params=_cparams("parallel", "arbitrary"),
        name="ffn_gate_up",
    )(x, wg, wu)


_POOL_PAD = 16
_ROW_CHUNK = 64


def _pool_kernel(u_ref, w_ref, sc_ref, o_ref, pad_ref, *, seq):
    g = pl.program_id(1)
    width = u_ref.shape[1]
    zeros = jnp.zeros((_POOL_PAD, width), F32)
    pad_ref[0:_POOL_PAD, :] = zeros
    pad_ref[_POOL_PAD + seq:2 * _POOL_PAD + seq, :] = zeros
    pad_ref[_POOL_PAD:_POOL_PAD + seq, :] = u_ref[...]
    wbf = w_ref[...].astype(BF16)
    scale = sc_ref[...]

    for gi, win in enumerate(POOL_SIZES):
        half = win // 2

        @pl.when(g == gi)
        def _(win=win, half=half):
            def chunk(ci, carry):
                base = pl.multiple_of(ci * _ROW_CHUNK, _ROW_CHUNK)
                window = pad_ref[pl.ds(base, _ROW_CHUNK + 2 * _POOL_PAD), :]
                lo = _POOL_PAD - half
                acc = window[lo:lo + _ROW_CHUNK]
                for j in range(1, win):
                    acc = acc + window[lo + j:lo + j + _ROW_CHUNK]
                t = base + lax.broadcasted_iota(jnp.int32, (_ROW_CHUNK, width), 0)
                cnt = (jnp.minimum(t + (win - half), seq) - jnp.maximum(t - half, 0)).astype(F32)
                centre = window[_POOL_PAD:_POOL_PAD + _ROW_CHUNK]
                d = (acc / cnt - centre).astype(BF16)
                y = jnp.dot(d, wbf, preferred_element_type=F32) * scale
                o_ref[pl.ds(base, _ROW_CHUNK), :] = y.astype(o_ref.dtype)
                return carry

            lax.fori_loop(0, seq // _ROW_CHUNK, chunk, 0)


def _pool(u, pool_w, pool_scale, layer, *, batch, seq, row_block_off):
    return pl.pallas_call(
        functools.partial(_pool_kernel, seq=seq),
        out_shape=jax.ShapeDtypeStruct((batch * seq, POOL_WIDTH), BF16),
        grid=(batch, len(POOL_SIZES)),
        in_specs=[pl.BlockSpec((seq, POOL_GROUP), lambda b, g: (row_block_off + b, g)),
                  pl.BlockSpec((None, None, POOL_GROUP, POOL_GROUP), lambda b, g: (layer, g, 0, 0)),
                  pl.BlockSpec((None, 1, POOL_GROUP), lambda b, g: (layer, 0, g))],
        out_specs=pl.BlockSpec((seq, POOL_GROUP), lambda b, g: (b, g)),
        scratch_shapes=[pltpu.VMEM((seq + 2 * _POOL_PAD, POOL_GROUP), F32)],
        compiler_params=_cparams("parallel", "parallel"),
        name="pool_mixer",
    )(u, pool_w, pool_scale.reshape(DEPTH, 1, POOL_WIDTH))


def _ctx_attn_kernel(q_ref, k_ref, v_ref, o_ref):
    for h in range(N_HEADS):
        sl = slice(h * HEAD_DIM, (h + 1) * HEAD_DIM)
        q = q_ref[:, sl]
        k = k_ref[:, sl].astype(BF16)
        v = v_ref[:, sl].astype(BF16)
        s = lax.dot_general(q, k, (((1,), (1,)), ((), ())), preferred_element_type=F32) * ATTN_SCALE
        m = jnp.max(s, axis=-1, keepdims=True)
        p = jnp.exp(s - m)
        denom = jnp.sum(p, axis=-1, keepdims=True)
        o = jnp.dot(p.astype(BF16), v, preferred_element_type=F32) / denom
        o_ref[:, sl] = o.astype(o_ref.dtype)


def _ctx_attention(q, k, v, *, batch, seq):
    spec = pl.BlockSpec((seq, ATTN_WIDTH), lambda b: (b, 0))
    return pl.pallas_call(
        _ctx_attn_kernel,
        out_shape=jax.ShapeDtypeStruct((batch * seq, ATTN_WIDTH), BF16),
        grid=(batch,),
        in_specs=[spec, spec, spec],
        out_specs=spec,
        compiler_params=_cparams("parallel"),
        name="context_attention",
    )(q, k, v)


_BIAS_PAIRS = 2 * WIN_R - 2


def _bias_table_kernel(rpb_ref, o_ref, *, layer):
    h = pl.program_id(0)
    shape = (GRID_W, 2 * GRID_W)
    qc = lax.broadcasted_iota(jnp.int32, shape, 0)
    lane = lax.broadcasted_iota(jnp.int32, shape, 1)
    kc = lane % GRID_W
    upper = lane >= GRID_W
    c0 = jnp.clip(qc - WIN_C // 2, 0, GRID_W - WIN_C)
    valid = (kc >= c0) & (kc < c0 + WIN_C)
    rel = kc - qc + (WIN_C - 1)
    n_dc = 2 * WIN_C - 1
    n_dr = 2 * WIN_R - 1
    row_base = (layer * N_HEADS + h) * n_dr
    for d in range(_BIAS_PAIRS):
        tile = jnp.zeros(shape, F32)
        for dc in range(n_dc):
            lo = rpb_ref[(row_base + d) * n_dc + dc]
            hi = rpb_ref[(row_base + d + 1) * n_dc + dc]
            tile = jnp.where(rel == dc, jnp.where(upper, hi, lo), tile)
        o_ref[d] = jnp.where(valid, tile, NEG_INF)


def _bias_table(rpb, layer):
    return pl.pallas_call(
        functools.partial(_bias_table_kernel, layer=layer),
        out_shape=jax.ShapeDtypeStruct((N_HEADS, _BIAS_PAIRS, GRID_W, 2 * GRID_W), F32),
        grid=(N_HEADS,),
        in_specs=[pl.BlockSpec(memory_space=pltpu.SMEM)],
        out_specs=pl.BlockSpec((None, _BIAS_PAIRS, GRID_W, 2 * GRID_W), lambda h: (h, 0, 0, 0)),
        compiler_params=_cparams("parallel"),
        name="rel_pos_bias_table",
    )(rpb.reshape(-1))


def _nbr_attn_kernel(q_ref, k_ref, v_ref, kc_ref, vc_ref, bias_ref, o_ref, kbf, vbf, *, rows):
    kbf[...] = k_ref[...].astype(BF16)
    vbf[...] = v_ref[...].astype(BF16)
    kc = kc_ref[...].astype(BF16)
    vc = vc_ref[...].astype(BF16)
    nk = WIN_R * GRID_W
    dims = (((1,), (1,)), ((), ()))

    def one_row(r, carry):
        r0 = jnp.clip(r - WIN_R // 2, 0, rows - WIN_R)
        d0 = r0 - r + (WIN_R - 1)
        qoff = pl.multiple_of(r * GRID_W, GRID_W)
        koff = pl.multiple_of(r0 * GRID_W, GRID_W)
        q = q_ref[pl.ds(qoff, GRID_W), :]
        ks = kbf[pl.ds(koff, nk), :]
        vs = vbf[pl.ds(koff, nk), :]
        bias = jnp.concatenate([bias_ref[d0 + 2 * j] for j in range(WIN_R // 2)], axis=1)
        s_n = lax.dot_general(q, ks, dims, preferred_element_type=F32) * ATTN_SCALE + bias
        s_c = lax.dot_general(q, kc, dims, preferred_element_type=F32) * ATTN_SCALE
        m = jnp.maximum(jnp.max(s_n, axis=-1, keepdims=True), jnp.max(s_c, axis=-1, keepdims=True))
        p_n = jnp.exp(s_n - m)
        p_c = jnp.exp(s_c - m)
        denom = jnp.sum(p_n, axis=-1, keepdims=True) + jnp.sum(p_c, axis=-1, keepdims=True)
        o = (jnp.dot(p_n.astype(BF16), vs, preferred_element_type=F32)
             + jnp.dot(p_c.astype(BF16), vc, preferred_element_type=F32)) / denom
        o_ref[pl.ds(qoff, GRID_W), :] = o.astype(o_ref.dtype)
        return carry

    lax.fori_loop(0, rows, one_row, 0)


def _nbr_attention(q, k, v, cache_k, cache_v, bias, layer, *, batch, seq, row_block_off):
    rows = seq // GRID_W
    assert rows >= WIN_R
    past = cache_k.shape[2]
    ck = cache_k.reshape(batch, DEPTH, past, ATTN_WIDTH)
    cv = cache_v.reshape(batch, DEPTH, past, ATTN_WIDTH)
    tok_spec = pl.BlockSpec((seq, HEAD_DIM), lambda b, h: (row_block_off + b, h))
    cache_spec = pl.BlockSpec((None, None, past, HEAD_DIM), lambda b, h: (b, layer, 0, h))
    return pl.pallas_call(
        functools.partial(_nbr_attn_kernel, rows=rows),
        out_shape=jax.ShapeDtypeStruct((batch * seq, ATTN_WIDTH), BF16),
        grid=(batch, N_HEADS),
        in_specs=[tok_spec, tok_spec, tok_spec, cache_spec, cache_spec,
                  pl.BlockSpec((None, _BIAS_PAIRS, GRID_W, 2 * GRID_W), lambda b, h: (h, 0, 0, 0))],
        out_specs=pl.BlockSpec((seq, HEAD_DIM), lambda b, h: (b, h)),
        scratch_shapes=[pltpu.VMEM((seq, HEAD_DIM), BF16)] * 2,
        compiler_params=_cparams("parallel", "parallel"),
        name="neighbourhood_attention",
    )(q, k, v, ck, cv, bias)


_CONV_PAD = 8


def _short_conv_kernel(u_ref, w_ref, b_ref, o_ref, pad_ref, *, seq):
    width = u_ref.shape[1]
    zeros = jnp.zeros((_CONV_PAD, width), F32)
    pad_ref[0:_CONV_PAD, :] = zeros
    pad_ref[_CONV_PAD + seq:2 * _CONV_PAD + seq, :] = zeros
    pad_ref[_CONV_PAD:_CONV_PAD + seq, :] = u_ref[...]
    w0 = w_ref[0:1, :]
    w1 = w_ref[1:2, :]
    w2 = w_ref[2:3, :]
    b = b_ref[...]

    def chunk(ci, carry):
        base = pl.multiple_of(ci * _ROW_CHUNK, _ROW_CHUNK)
        window = pad_ref[pl.ds(base, _ROW_CHUNK + 2 * _CONV_PAD), :]
        prev = window[_CONV_PAD - 1:_CONV_PAD - 1 + _ROW_CHUNK]
        cur = window[_CONV_PAD:_CONV_PAD + _ROW_CHUNK]
        nxt = window[_CONV_PAD + 1:_CONV_PAD + 1 + _ROW_CHUNK]
        o_ref[pl.ds(base, _ROW_CHUNK), :] = prev * w0 + cur * w1 + nxt * w2 + b
        return carry

    lax.fori_loop(0, seq // _ROW_CHUNK, chunk, 0)


def _short_conv(u, conv_w, conv_b, layer, *, batch, seq, row_block_off):
    width = u.shape[1]
    tc = 512
    return pl.pallas_call(
        functools.partial(_short_conv_kernel, seq=seq),
        out_shape=jax.ShapeDtypeStruct((batch * seq, width), F32),
        grid=(batch, width // tc),
        in_specs=[pl.BlockSpec((seq, tc), lambda b, j: (row_block_off + b, j)),
                  pl.BlockSpec((None, 3, tc), lambda b, j: (layer, 0, j)),
                  pl.BlockSpec((None, 1, tc), lambda b, j: (layer, 0, j))],
        out_specs=pl.BlockSpec((seq, tc), lambda b, j: (b, j)),
        scratch_shapes=[pltpu.VMEM((seq + 2 * _CONV_PAD, tc), F32)],
        compiler_params=_cparams("parallel", "parallel"),
        name="hyena_short_conv",
    )(u, conv_w, conv_b.reshape(DEPTH, 1, width))


def _dft_matrices(seq):
    f = jnp.arange(seq, dtype=jnp.int32)[:, None]
    t = jnp.arange(seq, dtype=jnp.int32)[None, :]
    ang = ((f * t) % (2 * seq)).astype(F32) * (math.pi / seq)
    top = jnp.cos(ang)
    bot = jnp.where(f == 0, (1 - 2 * (t % 2)).astype(F32), -jnp.sin(ang))
    colscale = jnp.where(jnp.arange(seq) == 0, 0.5, 1.0).astype(F32)[None, :] / seq
    inv = jnp.concatenate([top.T * colscale, bot.T * colscale], axis=1)
    return top.astype(BF16), bot.astype(BF16), inv.astype(BF16)


def _filter_mlp_kernel(z_ref, w1_ref, b1_ref, w2_ref, b2_ref, fr_ref, o_ref):
    fr = fr_ref[...]
    h = jnp.sin(fr * (jnp.dot(z_ref[...], w1_ref[...], preferred_element_type=F32) + b1_ref[...]))
    o_ref[...] = jnp.sin(fr * (jnp.dot(h, w2_ref[...], preferred_element_type=F32) + b2_ref[...]))


def _filter_mlp(seq, w1, b1, w2, b2, freq, layer):
    t = jnp.linspace(0.0, 1.0, seq, dtype=F32)[:, None]
    bands = (FILTER_EMB - 1) // 2
    ang = (2.0 * math.pi / seq) * jnp.arange(seq, dtype=F32)[:, None] * jnp.linspace(
        1e-4, bands - 1, bands, dtype=F32)[None]
    z = jnp.concatenate([t, jnp.cos(ang), -jnp.sin(ang)], axis=-1)
    z = jnp.pad(z, ((0, 0), (0, LANES - FILTER_EMB)))
    w1p = jnp.pad(w1, ((0, 0), (0, LANES - FILTER_EMB), (0, 0)))
    vec = lambda a: a.reshape(DEPTH, 1, FILTER_HIDDEN)
    vspec = pl.BlockSpec((None, 1, FILTER_HIDDEN), lambda i: (layer, 0, 0))
    return pl.pallas_call(
        _filter_mlp_kernel,
        out_shape=jax.ShapeDtypeStruct((seq, FILTER_HIDDEN), F32),
        grid=(1,),
        in_specs=[pl.BlockSpec((seq, LANES), lambda i: (0, 0)),
                  pl.BlockSpec((None, LANES, FILTER_HIDDEN), lambda i: (layer, 0, 0)),
                  vspec,
                  pl.BlockSpec((None, FILTER_HIDDEN, FILTER_HIDDEN), lambda i: (layer, 0, 0)),
                  vspec, vspec],
        out_specs=pl.BlockSpec((seq, FILTER_HIDDEN), lambda i: (0, 0)),
        compiler_params=_cparams("arbitrary"),
        name="hyena_filter_mlp",
    )(z, w1p, vec(b1), w2, vec(b2), vec(freq))


def _filter_taps_kernel(h_ref, dl_ref, wf_ref, wb_ref, a_ref, d_ref, nyq_ref, *, seq):
    h = h_ref[...]
    shape = (seq, dl_ref.shape[1])
    row = lax.broadcasted_iota(jnp.int32, shape, 0)
    t = row.astype(F32) / float(seq - 1)
    dec = jnp.exp(-t * dl_ref[...])
    fwd = jnp.dot(h, wf_ref[...], preferred_element_type=F32) * dec
    bwd = jnp.dot(h, wb_ref[...], preferred_element_type=F32) * dec
    bwd = jnp.where(row == 0, 0.0, bwd)
    inv = 1.0 / jnp.sum(jnp.abs(fwd) + jnp.abs(bwd), axis=0, keepdims=True)
    a = (fwd + bwd) * inv
    a_ref[...] = a
    d_ref[...] = (fwd - bwd) * inv
    sign = jnp.where(row % 2 == 0, 1.0, -1.0)
    nyq_ref[...] = jnp.sum(a * sign, axis=0, keepdims=True)


def _filter_taps(hidden, w3, layer, *, seq):
    tc = 256
    nct = HYENA_WIDTH // tc
    deltas = jnp.abs(jnp.linspace(math.log(DECAY_TARGET) / SLOW_DECAY_PCT,
                                  math.log(DECAY_TARGET) / FAST_DECAY_PCT, HYENA_WIDTH, dtype=F32))
    tap_shape = jax.ShapeDtypeStruct((HYENA_ORDER, seq, HYENA_WIDTH), F32)
    tap_spec = pl.BlockSpec((None, seq, tc), lambda o, j: (o, 0, j))
    return pl.pallas_call(
        functools.partial(_filter_taps_kernel, seq=seq),
        out_shape=(tap_shape, tap_shape, jax.ShapeDtypeStruct((HYENA_ORDER, 1, HYENA_WIDTH), F32)),
        grid=(HYENA_ORDER, nct),
        in_specs=[pl.BlockSpec((seq, FILTER_HIDDEN), lambda o, j: (0, 0)),
                  pl.BlockSpec((1, tc), lambda o, j: (0, j)),
                  pl.BlockSpec((None, FILTER_HIDDEN, tc), lambda o, j: (layer, 0, (2 * o) * nct + j)),
                  pl.BlockSpec((None, FILTER_HIDDEN, tc), lambda o, j: (layer, 0, (2 * o + 1) * nct + j))],
        out_specs=(tap_spec, tap_spec, pl.BlockSpec((None, 1, tc), lambda o, j: (o, 0, j))),
        compiler_params=_cparams("parallel", "parallel"),
        name="hyena_filter_taps",
    )(hidden, deltas.reshape(1, HYENA_WIDTH), w3, w3)


def _filter_spec_kernel(wt_ref, wb_ref, a_ref, d_ref, nyq_ref, o_ref, abf, dbf):
    i = pl.program_id(2)

    @pl.when(i == 0)
    def _():
        abf[...] = a_ref[...].astype(BF16)
        dbf[...] = d_ref[...].astype(BF16)

    top = jnp.dot(wt_ref[...], abf[...], preferred_element_type=F32)
    bot = jnp.dot(wb_ref[...], dbf[...], preferred_element_type=F32)
    row = lax.broadcasted_iota(jnp.int32, bot.shape, 0)
    bot = jnp.where((row == 0) & (i == 0), nyq_ref[...], bot)
    o_ref[0] = top
    o_ref[1] = bot


def _filter_spectrum(dft_top, dft_bot, taps_a, taps_d, nyq, *, seq):
    tc = 512
    tf = min(seq, 512)
    wspec = pl.BlockSpec((tf, seq), lambda o, c, i: (i, 0))
    tspec = pl.BlockSpec((None, seq, tc), lambda o, c, i: (o, 0, c))
    return pl.pallas_call(
        _filter_spec_kernel,
        out_shape=jax.ShapeDtypeStruct((HYENA_ORDER, 2, seq, HYENA_WIDTH), F32),
        grid=(HYENA_ORDER, HYENA_WIDTH // tc, seq // tf),
        in_specs=[wspec, wspec, tspec, tspec,
                  pl.BlockSpec((None, 1, tc), lambda o, c, i: (o, 0, c))],
        out_specs=pl.BlockSpec((None, 2, tf, tc), lambda o, c, i: (o, 0, i, c)),
        scratch_shapes=[pltpu.VMEM((seq, tc), BF16)] * 2,
        compiler_params=_cparams("parallel", "parallel", "arbitrary"),
        name="hyena_filter_spectrum",
    )(dft_top, dft_bot, taps_a, taps_d, nyq)


def _fwd_dft_kernel(wt_ref, wb_ref, z_ref, p_ref, o_ref, zbf):
    i = pl.program_id(2)

    @pl.when(i == 0)
    def _():
        zbf[...] = z_ref[...].astype(BF16)

    top = jnp.dot(wt_ref[...], zbf[...], preferred_element_type=F32)
    bot = jnp.dot(wb_ref[...], zbf[...], preferred_element_type=F32)
    pt = p_ref[0]
    pb = p_ref[1]
    row = lax.broadcasted_iota(jnp.int32, top.shape, 0)
    real_row = (row == 0) & (i == 0)
    o_ref[0] = jnp.where(real_row, top * pt, top * pt - bot * pb).astype(o_ref.dtype)
    o_ref[1] = jnp.where(real_row, bot * pb, top * pb + bot * pt).astype(o_ref.dtype)


def _fwd_dft(dft_top, dft_bot, z, z_col_off, spec, order, *, batch, seq):
    tc = 512
    tf = min(seq, 512)
    zc = z_col_off // tc
    wspec = pl.BlockSpec((tf, seq), lambda b, c, i: (i, 0))
    return pl.pallas_call(
        _fwd_dft_kernel,
        out_shape=jax.ShapeDtypeStruct((batch, 2, seq, HYENA_WIDTH), BF16),
        grid=(batch, HYENA_WIDTH // tc, seq // tf),
        in_specs=[wspec, wspec,
                  pl.BlockSpec((seq, tc), lambda b, c, i: (b, zc + c)),
                  pl.BlockSpec((None, 2, tf, tc), lambda b, c, i: (order, 0, i, c))],
        out_specs=pl.BlockSpec((None, 2, tf, tc), lambda b, c, i: (b, 0, i, c)),
        scratch_shapes=[pltpu.VMEM((seq, tc), BF16)],
        compiler_params=_cparams("parallel", "parallel", "arbitrary"),
        name="hyena_fwd_dft",
    )(dft_top, dft_bot, z, spec)


def _inv_dft_kernel(wi_ref, s_ref, gate_ref, z_ref, bias_ref, o_ref):
    s = s_ref[...]
    s2 = s.reshape(s.shape[0] * s.shape[1], s.shape[2])
    conv = jnp.dot(wi_ref[...], s2, preferred_element_type=F32)
    o_ref[...] = (gate_ref[...] * (conv + bias_ref[...] * z_ref[...])).astype(o_ref.dtype)


def _inv_dft(dft_inv, s, gate, gate_col_off, z, z_col_off, hy_bias, layer, order, out_dtype, *, batch, seq):
    tc = 512
    tl = min(seq, 512)
    nrb = seq // tl
    gc = gate_col_off // tc
    zc = z_col_off // tc
    return pl.pallas_call(
        _inv_dft_kernel,
        out_shape=jax.ShapeDtypeStruct((batch * seq, HYENA_WIDTH), out_dtype),
        grid=(batch, HYENA_WIDTH // tc, nrb),
        in_specs=[pl.BlockSpec((tl, 2 * seq), lambda b, c, i: (i, 0)),
                  pl.BlockSpec((None, 2, seq, tc), lambda b, c, i: (b, 0, 0, c)),
                  pl.BlockSpec((tl, tc), lambda b, c, i: (b * nrb + i, gc + c)),
                  pl.BlockSpec((tl, tc), lambda b, c, i: (b * nrb + i, zc + c)),
                  pl.BlockSpec((None, 1, tc), lambda b, c, i: (layer * HYENA_ORDER + order, 0, c))],
        out_specs=pl.BlockSpec((tl, tc), lambda b, c, i: (b * nrb + i, c)),
        compiler_params=_cparams("parallel", "parallel", "arbitrary"),
        name="hyena_inv_dft",
    )(dft_inv, s, gate, z, hy_bias.reshape(DEPTH * HYENA_ORDER, 1, HYENA_WIDTH))


def _hyena(u, p, layer, *, batch, seq, row_block_off):
    uc = _short_conv(u, p["hy_conv_w"], p["hy_conv_b"], layer, batch=batch, seq=seq,
                     row_block_off=row_block_off)
    dft_top, dft_bot, dft_inv = _dft_matrices(seq)
    hidden = _filter_mlp(seq, p["filt_w1"], p["filt_b1"], p["filt_w2"], p["filt_b2"], p["filt_freq"], layer)
    taps_a, taps_d, nyq = _filter_taps(hidden, p["filt_w3"], layer, seq=seq)
    spec = _filter_spectrum(dft_top, dft_bot, taps_a, taps_d, nyq, seq=seq)
    s = _fwd_dft(dft_top, dft_bot, uc, 0, spec, 0, batch=batch, seq=seq)
    z1 = _inv_dft(dft_inv, s, uc, HYENA_WIDTH, uc, 0, p["hy_bias"], layer, 0, F32, batch=batch, seq=seq)
    s = _fwd_dft(dft_top, dft_bot, z1, 0, spec, 1, batch=batch, seq=seq)
    return _inv_dft(dft_inv, s, uc, 2 * HYENA_WIDTH, z1, 0, p["hy_bias"], layer, 1, BF16, batch=batch, seq=seq)


_WEIGHT_NAMES = ("ada_w", "ada_b", "norm1_g", "norm2_g", "w_in", "pool_w", "pool_scale",
                 "q_norm_g", "k_norm_g", "rpb", "hy_conv_w", "hy_conv_b", "filt_w1", "filt_b1",
                 "filt_w2", "filt_b2", "filt_w3", "filt_freq", "hy_bias", "w_branch_pool",
                 "w_branch_attn", "w_branch_hyena", "w_out", "w_gate", "w_up", "w_down")


def kernel(x_prompt, x_sample, cache_k, cache_v, c, c_ctx, ada_w, ada_b, norm1_g, norm2_g, w_in, pool_w, pool_scale, q_norm_g, k_norm_g, rpb, hy_conv_w, hy_conv_b, filt_w1, filt_b1, filt_w2, filt_b2, filt_w3, filt_freq, hy_bias, w_branch_pool, w_branch_attn, w_branch_hyena, w_out, w_gate, w_up, w_down):
    p = dict(zip(_WEIGHT_NAMES, (ada_w, ada_b, norm1_g, norm2_g, w_in, pool_w, pool_scale, q_norm_g,
                                 k_norm_g, rpb, hy_conv_w, hy_conv_b, filt_w1, filt_b1, filt_w2, filt_b2,
                                 filt_w3, filt_freq, hy_bias, w_branch_pool, w_branch_attn,
                                 w_branch_hyena, w_out, w_gate, w_up, w_down)))
    nb, ns, d = x_prompt.shape
    lb, ls, _ = x_sample.shape
    n_ctx = nb * ns
    assert d == D_MODEL and lb + 1 <= MOD_ROWS and n_ctx % ls == 0
    st = _Stream(n_ctx, ls)
    lat_off = n_ctx // ls

    x = jnp.concatenate([x_prompt.reshape(n_ctx, d), x_sample.reshape(lb * ls, d)], axis=0)
    cmat = jnp.concatenate([c_ctx[None, :], c, jnp.zeros((MOD_ROWS - 1 - lb, d), F32)], axis=0)
    mod = _modulation(cmat, ada_w, ada_b)
    modflat = mod.reshape(DEPTH * MOD_ROWS * 6, 1, d)

    new_k, new_v = [], []
    for layer in range(DEPTH):
        h = _normmod(x, norm1_g, modflat, st, layer, 1, 0)
        lin = functools.partial(_linear, h, w_in, layer, tm=512, tn=512)
        u_pool = lin(COL_POOL, POOL_WIDTH, kind="plain", out_dtype=F32)
        q = lin(COL_Q, ATTN_WIDTH, kind="rmshead", out_dtype=BF16, head_gain=q_norm_g)
        k = lin(COL_K, ATTN_WIDTH, kind="rmshead", out_dtype=F32, head_gain=k_norm_g)
        v = lin(COL_V, ATTN_WIDTH, kind="plain", out_dtype=F32)
        u_hy = lin(COL_HY, (HYENA_ORDER + 1) * HYENA_WIDTH, kind="plain", out_dtype=F32)
        gates = lin(COL_GATES, 3 * D_MODEL, kind="sigmoid", out_dtype=F32)
        new_k.append(k[:n_ctx].reshape(nb, ns, N_HEADS, HEAD_DIM))
        new_v.append(v[:n_ctx].reshape(nb, ns, N_HEADS, HEAD_DIM))

        pool = jnp.concatenate([
            _pool(u_pool, pool_w, pool_scale, layer, batch=nb, seq=ns, row_block_off=0),
            _pool(u_pool, pool_w, pool_scale, layer, batch=lb, seq=ls, row_block_off=lat_off)], axis=0)
        bias = _bias_table(rpb, layer)
        attn = jnp.concatenate([
            _ctx_attention(q, k, v, batch=nb, seq=ns),
            _nbr_attention(q, k, v, cache_k, cache_v, bias, layer, batch=lb, seq=ls,
                           row_block_off=lat_off)], axis=0)
        hy = jnp.concatenate([
            _hyena(u_hy, p, layer, batch=nb, seq=ns, row_block_off=0),
            _hyena(u_hy, p, layer, batch=lb, seq=ls, row_block_off=lat_off)], axis=0)

        mixed = _mix(pool, attn, hy, w_branch_pool, w_branch_attn, w_branch_hyena, gates, layer)
        x = _linear(mixed, w_out, layer, 0, D_MODEL, tm=512, tn=512, kind="residual", out_dtype=F32,
                    residual=x, modflat=modflat, st=st, which_gate=2)
        h = _normmod(x, norm2_g, modflat, st, layer, 4, 3)
        a = _ffn1(h, w_gate, w_up, layer)
        x = _linear(a, w_down, layer, 0, D_MODEL, tm=256, tn=256, kind="residual", out_dtype=F32,
                    residual=x, modflat=modflat, st=st, which_gate=5)

    y_prompt = x[:n_ctx].reshape(nb, ns, d)
    y_sample = x[n_ctx:].reshape(lb, ls, d)
    return (y_prompt, y_sample, jnp.stack(new_k, axis=1), jnp.stack(new_v, axis=1))
```

```python
import functools
import math

import jax
import jax.numpy as jnp
from jax import lax
from jax.experimental import pallas as pl
from jax.experimental.pallas import tpu as pltpu

F32 = jnp.float32
BF16 = jnp.bfloat16

D_MODEL = 4096
DEPTH = 2
GRID_W = 64
HEAD_DIM = 128
N_HEADS = D_MODEL // 256
ATTN_WIDTH = N_HEADS * HEAD_DIM
ATTN_SCALE = HEAD_DIM ** -0.5
WIN_R = 8
WIN_C = 16
NEG_INF = -1e30
POOL_WIDTH = D_MODEL // 2
POOL_SIZES = (2, 4, 8, 16)
POOL_GROUP = POOL_WIDTH // len(POOL_SIZES)
HYENA_WIDTH = D_MODEL // 2
HYENA_ORDER = 2
FILTER_EMB = 33
FILTER_HIDDEN = 64
DECAY_TARGET = 1e-2
FAST_DECAY_PCT = 0.3
SLOW_DECAY_PCT = 1.5
D_FF = ((8 * D_MODEL + 767) // 768) * 256
COL_POOL = 0
COL_Q = POOL_WIDTH
COL_K = COL_Q + ATTN_WIDTH
COL_V = COL_K + ATTN_WIDTH
COL_HY = COL_V + ATTN_WIDTH
COL_GATES = COL_HY + (HYENA_ORDER + 1) * HYENA_WIDTH
MOD_ROWS = 8
RMS_EPS = 1e-6

LANES = 128
VMEM_LIMIT = 56 * 1024 * 1024
SUB_M = 512


def _cparams(*sem):
    return pltpu.CompilerParams(dimension_semantics=sem, vmem_limit_bytes=VMEM_LIMIT)


def _row_chunks(tm):
    return [slice(r, r + min(SUB_M, tm)) for r in range(0, tm, min(SUB_M, tm))]


def _token_tile(m_tok):
    return 1024 if m_tok % 1024 == 0 else 512


def _mod_kernel(c_ref, w_ref, b_ref, o_ref):
    c = c_ref[...]
    s = (c * jax.nn.sigmoid(c)).astype(BF16)
    o_ref[...] = jnp.dot(s, w_ref[...].astype(BF16), preferred_element_type=F32) + b_ref[...]


def _modulation(cmat, ada_w, ada_b):
    tn = 512
    n6 = ada_w.shape[-1]
    return pl.pallas_call(
        _mod_kernel,
        out_shape=jax.ShapeDtypeStruct((DEPTH, MOD_ROWS, n6), F32),
        grid=(DEPTH, n6 // tn),
        in_specs=[pl.BlockSpec((MOD_ROWS, D_MODEL), lambda l, j: (0, 0)),
                  pl.BlockSpec((None, D_MODEL, tn), lambda l, j: (l, 0, j)),
                  pl.BlockSpec((None, 1, tn), lambda l, j: (l, 0, j))],
        out_specs=pl.BlockSpec((None, MOD_ROWS, tn), lambda l, j: (l, 0, j)),
        compiler_params=_cparams("parallel", "parallel"),
        name="modulation",
    )(cmat, ada_w, ada_b.reshape(DEPTH, 1, n6))


class _Stream:
    def __init__(self, batch, seq, mod_base, per_batch_mod):
        self.batch = batch
        self.seq = seq
        self.mod_base = mod_base
        self.per_batch_mod = per_batch_mod

    def mod_index(self, layer, i, tm, which):
        row = self.mod_base + ((i * tm) // self.seq if self.per_batch_mod else 0)
        return (layer * MOD_ROWS + row) * 6 + which


def _normmod_kernel(x_ref, g_ref, sc_ref, sh_ref, o_ref):
    x = x_ref[...]
    y = x * lax.rsqrt(jnp.mean(x * x, axis=-1, keepdims=True) + RMS_EPS)
    y = y * g_ref[...]
    o_ref[...] = (y * (1.0 + sc_ref[0]) + sh_ref[0]).astype(o_ref.dtype)


def _normmod(x, g, modflat, st, layer, which_scale, which_shift):
    m_tok, d = x.shape
    tm = 256
    return pl.pallas_call(
        _normmod_kernel,
        out_shape=jax.ShapeDtypeStruct((m_tok, d), BF16),
        grid=(m_tok // tm,),
        in_specs=[pl.BlockSpec((tm, d), lambda i: (i, 0)),
                  pl.BlockSpec((None, 1, d), lambda i: (layer, 0, 0)),
                  pl.BlockSpec((1, 1, d), lambda i: (st.mod_index(layer, i, tm, which_scale), 0, 0)),
                  pl.BlockSpec((1, 1, d), lambda i: (st.mod_index(layer, i, tm, which_shift), 0, 0))],
        out_specs=pl.BlockSpec((tm, d), lambda i: (i, 0)),
        compiler_params=_cparams("parallel"),
        name="normmod",
    )(x, g.reshape(DEPTH, 1, d), modflat, modflat)


def _cast_weight_once(w_ref, wbf_ref):
    @pl.when(pl.program_id(1) == 0)
    def _():
        wbf_ref[...] = w_ref[...].astype(BF16)


def _lin_plain_kernel(x_ref, w_ref, o_ref, wbf_ref):
    _cast_weight_once(w_ref, wbf_ref)
    for rows in _row_chunks(x_ref.shape[0]):
        acc = jnp.dot(x_ref[rows, :], wbf_ref[...], preferred_element_type=F32)
        o_ref[rows, :] = acc.astype(o_ref.dtype)


def _lin_sigmoid_kernel(x_ref, w_ref, o_ref, wbf_ref):
    _cast_weight_once(w_ref, wbf_ref)
    for rows in _row_chunks(x_ref.shape[0]):
        acc = jnp.dot(x_ref[rows, :], wbf_ref[...], preferred_element_type=F32)
        o_ref[rows, :] = jax.nn.sigmoid(acc).astype(o_ref.dtype)


def _lin_rmshead_kernel(x_ref, w_ref, g_ref, o_ref, wbf_ref):
    _cast_weight_once(w_ref, wbf_ref)
    g = g_ref[...]
    for rows in _row_chunks(x_ref.shape[0]):
        acc = jnp.dot(x_ref[rows, :], wbf_ref[...], preferred_element_type=F32)
        for h in range(acc.shape[1] // HEAD_DIM):
            a = acc[:, h * HEAD_DIM:(h + 1) * HEAD_DIM]
            y = a * lax.rsqrt(jnp.mean(a * a, axis=-1, keepdims=True) + RMS_EPS)
            o_ref[rows, h * HEAD_DIM:(h + 1) * HEAD_DIM] = (y * g).astype(o_ref.dtype)


def _lin_residual_kernel(x_ref, w_ref, res_ref, gate_ref, o_ref, wbf_ref):
    _cast_weight_once(w_ref, wbf_ref)
    for rows in _row_chunks(x_ref.shape[0]):
        acc = jnp.dot(x_ref[rows, :], wbf_ref[...], preferred_element_type=F32)
        o_ref[rows, :] = res_ref[rows, :] + gate_ref[0] * acc


def _linear(x, w, layer, col_off, ncols, *, tn, kind, out_dtype, tm=None, k_off=0, k_len=None,
            head_gain=None, residual=None, modflat=None, st=None, which_gate=None):
    m_tok = x.shape[0]
    k_len = x.shape[1] if k_len is None else k_len
    tm = _token_tile(m_tok) if tm is None else tm
    cb = col_off // tn
    kb = k_off // k_len
    in_specs = [pl.BlockSpec((tm, k_len), lambda n, m: (m, kb)),
                pl.BlockSpec((None, k_len, tn), lambda n, m: (layer, kb, cb + n))]
    args = [x, w]
    if kind == "plain":
        body = _lin_plain_kernel
    elif kind == "sigmoid":
        body = _lin_sigmoid_kernel
    elif kind == "rmshead":
        body = _lin_rmshead_kernel
        in_specs.append(pl.BlockSpec((None, 1, HEAD_DIM), lambda n, m: (layer, 0, 0)))
        args.append(head_gain.reshape(DEPTH, 1, HEAD_DIM))
    else:
        body = _lin_residual_kernel
        in_specs.append(pl.BlockSpec((tm, tn), lambda n, m: (m, n)))
        in_specs.append(pl.BlockSpec(
            (1, 1, tn), lambda n, m: (st.mod_index(layer, m, tm, which_gate), 0, n)))
        args += [residual, modflat]
    return pl.pallas_call(
        body,
        out_shape=jax.ShapeDtypeStruct((m_tok, ncols), out_dtype),
        grid=(ncols // tn, m_tok // tm),
        in_specs=in_specs,
        out_specs=pl.BlockSpec((tm, tn), lambda n, m: (m, n)),
        scratch_shapes=[pltpu.VMEM((k_len, tn), BF16)],
        compiler_params=_cparams("parallel", "arbitrary"),
        name="linear_" + kind,
    )(*args)


def _mix_kernel(xp_ref, xa_ref, xh_ref, wp_ref, wa_ref, wh_ref, gp_ref, ga_ref, gh_ref, o_ref,
                wpb, wab, whb):
    @pl.when(pl.program_id(1) == 0)
    def _():
        wpb[...] = wp_ref[...].astype(BF16)
        wab[...] = wa_ref[...].astype(BF16)
        whb[...] = wh_ref[...].astype(BF16)

    acc = gp_ref[...].astype(F32) * jnp.dot(xp_ref[...], wpb[...], preferred_element_type=F32)
    acc = acc + ga_ref[...].astype(F32) * jnp.dot(xa_ref[...], wab[...], preferred_element_type=F32)
    acc = acc + gh_ref[...].astype(F32) * jnp.dot(xh_ref[...], whb[...], preferred_element_type=F32)
    o_ref[...] = acc.astype(o_ref.dtype)


def _mix(xp, xa, xh, wp, wa, wh, gates, layer):
    m_tok, k = xp.shape
    n = wp.shape[-1]
    tm, tn = 512, 512
    nb = n // tn
    xspec = pl.BlockSpec((tm, k), lambda j, i: (i, 0))
    wspec = pl.BlockSpec((None, k, tn), lambda j, i: (layer, 0, j))
    return pl.pallas_call(
        _mix_kernel,
        out_shape=jax.ShapeDtypeStruct((m_tok, n), BF16),
        grid=(nb, m_tok // tm),
        in_specs=[xspec, xspec, xspec, wspec, wspec, wspec,
                  pl.BlockSpec((tm, tn), lambda j, i: (i, j)),
                  pl.BlockSpec((tm, tn), lambda j, i: (i, nb + j)),
                  pl.BlockSpec((tm, tn), lambda j, i: (i, 2 * nb + j))],
        out_specs=pl.BlockSpec((tm, tn), lambda j, i: (i, j)),
        scratch_shapes=[pltpu.VMEM((k, tn), BF16)] * 3,
        compiler_params=_cparams("parallel", "arbitrary"),
        name="mix",
    )(xp, xa, xh, wp, wa, wh, gates, gates, gates)


def _ffn1_kernel(x_ref, wg_ref, wu_ref, o_ref, wgb, wub):
    @pl.when(pl.program_id(1) == 0)
    def _():
        wgb[...] = wg_ref[...].astype(BF16)
        wub[...] = wu_ref[...].astype(BF16)

    for rows in _row_chunks(x_ref.shape[0]):
        x = x_ref[rows, :]
        g = jnp.dot(x, wgb[...], preferred_element_type=F32)
        u = jnp.dot(x, wub[...], preferred_element_type=F32)
        o_ref[rows, :] = (g * jax.nn.sigmoid(g) * u).astype(o_ref.dtype)


def _ffn1(x, wg, wu, layer):
    m_tok, k = x.shape
    n = wg.shape[-1]
    tm, tn = _token_tile(m_tok), 256
    wspec = pl.BlockSpec((None, k, tn), lambda j, i: (layer, 0, j))
    return pl.pallas_call(
        _ffn1_kernel,
        out_shape=jax.ShapeDtypeStruct((m_tok, n), BF16),
        grid=(n // tn, m_tok // tm),
        in_specs=[pl.BlockSpec((tm, k), lambda j, i: (i, 0)), wspec, wspec],
        out_specs=pl.BlockSpec((tm, tn), lambda j, i: (i, j)),
        scratch_shapes=[pltpu.VMEM((k, tn), BF16)] * 2,
        compiler_params=_cparams("parallel", "arbitrary"),
        name="ffn_gate_up",
    )(x, wg, wu)


_POOL_PAD = 16
_ROW_CHUNK = 64


def _pool_kernel(u_ref, w_ref, sc_ref, o_ref, pad_ref, *, seq):
    g = pl.program_id(1)
    width = u_ref.shape[1]
    zeros = jnp.zeros((_POOL_PAD, width), F32)
    pad_ref[0:_POOL_PAD, :] = zeros
    pad_ref[_POOL_PAD + seq:2 * _POOL_PAD + seq, :] = zeros
    pad_ref[_POOL_PAD:_POOL_PAD + seq, :] = u_ref[...]
    wbf = w_ref[...].astype(BF16)
    scale = sc_ref[...]

    for gi, win in enumerate(POOL_SIZES):
        half = win // 2

        @pl.when(g == gi)
        def _(win=win, half=half):
            def chunk(ci, carry):
                base = pl.multiple_of(ci * _ROW_CHUNK, _ROW_CHUNK)
                window = pad_ref[pl.ds(base, _ROW_CHUNK + 2 * _POOL_PAD), :]
                lo = _POOL_PAD - half
                acc = window[lo:lo + _ROW_CHUNK]
                for j in range(1, win):
                    acc = acc + window[lo + j:lo + j + _ROW_CHUNK]
                t = base + lax.broadcasted_iota(jnp.int32, (_ROW_CHUNK, width), 0)
                cnt = (jnp.minimum(t + (win - half), seq) - jnp.maximum(t - half, 0)).astype(F32)
                centre = window[_POOL_PAD:_POOL_PAD + _ROW_CHUNK]
                d = (acc / cnt - centre).astype(BF16)
                y = jnp.dot(d, wbf, preferred_element_type=F32) * scale
                o_ref[pl.ds(base, _ROW_CHUNK), :] = y.astype(o_ref.dtype)
                return carry

            lax.fori_loop(0, seq // _ROW_CHUNK, chunk, 0)


def _pool(u, pool_w, pool_scale, layer, st):
    seq = st.seq
    return pl.pallas_call(
        functools.partial(_pool_kernel, seq=seq),
        out_shape=jax.ShapeDtypeStruct((st.batch * seq, POOL_WIDTH), BF16),
        grid=(st.batch, len(POOL_SIZES)),
        in_specs=[pl.BlockSpec((seq, POOL_GROUP), lambda b, g: (b, g)),
                  pl.BlockSpec((None, None, POOL_GROUP, POOL_GROUP), lambda b, g: (layer, g, 0, 0)),
                  pl.BlockSpec((None, 1, POOL_GROUP), lambda b, g: (layer, 0, g))],
        out_specs=pl.BlockSpec((seq, POOL_GROUP), lambda b, g: (b, g)),
        scratch_shapes=[pltpu.VMEM((seq + 2 * _POOL_PAD, POOL_GROUP), F32)],
        compiler_params=_cparams("parallel", "parallel"),
        name="pool_mixer",
    )(u, pool_w, pool_scale.reshape(DEPTH, 1, POOL_WIDTH))


def _ctx_attn_kernel(q_ref, k_ref, v_ref, o_ref):
    for h in range(N_HEADS):
        sl = slice(h * HEAD_DIM, (h + 1) * HEAD_DIM)
        q = q_ref[:, sl]
        k = k_ref[:, sl].astype(BF16)
        v = v_ref[:, sl].astype(BF16)
        s = lax.dot_general(q, k, (((1,), (1,)), ((), ())), preferred_element_type=F32) * ATTN_SCALE
        m = jnp.max(s, axis=-1, keepdims=True)
        p = jnp.exp(s - m)
        denom = jnp.sum(p, axis=-1, keepdims=True)
        o = jnp.dot(p.astype(BF16), v, preferred_element_type=F32) / denom
        o_ref[:, sl] = o.astype(o_ref.dtype)


def _ctx_attention(q, k, v, st):
    spec = pl.BlockSpec((st.seq, ATTN_WIDTH), lambda b: (b, 0))
    return pl.pallas_call(
        _ctx_attn_kernel,
        out_shape=jax.ShapeDtypeStruct((st.batch * st.seq, ATTN_WIDTH), BF16),
        grid=(st.batch,),
        in_specs=[spec, spec, spec],
        out_specs=spec,
        compiler_params=_cparams("parallel"),
        name="context_attention",
    )(q, k, v)


_BIAS_PAIRS = 2 * WIN_R - 2


def _bias_table_kernel(rpb_ref, o_ref, *, layer):
    h = pl.program_id(0)
    shape = (GRID_W, 2 * GRID_W)
    qc = lax.broadcasted_iota(jnp.int32, shape, 0)
    lane = lax.broadcasted_iota(jnp.int32, shape, 1)
    kc = lane % GRID_W
    upper = lane >= GRID_W
    c0 = jnp.clip(qc - WIN_C // 2, 0, GRID_W - WIN_C)
    valid = (kc >= c0) & (kc < c0 + WIN_C)
    rel = kc - qc + (WIN_C - 1)
    n_dc = 2 * WIN_C - 1
    n_dr = 2 * WIN_R - 1
    row_base = (layer * N_HEADS + h) * n_dr
    for d in range(_BIAS_PAIRS):
        tile = jnp.zeros(shape, F32)
        for dc in range(n_dc):
            lo = rpb_ref[(row_base + d) * n_dc + dc]
            hi = rpb_ref[(row_base + d + 1) * n_dc + dc]
            tile = jnp.where(rel == dc, jnp.where(upper, hi, lo), tile)
        o_ref[d] = jnp.where(valid, tile, NEG_INF)


def _bias_table(rpb, layer):
    return pl.pallas_call(
        functools.partial(_bias_table_kernel, layer=layer),
        out_shape=jax.ShapeDtypeStruct((N_HEADS, _BIAS_PAIRS, GRID_W, 2 * GRID_W), F32),
        grid=(N_HEADS,),
        in_specs=[pl.BlockSpec(memory_space=pltpu.SMEM)],
        out_specs=pl.BlockSpec((None, _BIAS_PAIRS, GRID_W, 2 * GRID_W), lambda h: (h, 0, 0, 0)),
        compiler_params=_cparams("parallel"),
        name="rel_pos_bias_table",
    )(rpb.reshape(-1))


_ROWS_PER_ITER = 4


def _nbr_attn_kernel(q_ref, k_ref, v_ref, kc_ref, vc_ref, bias_ref, o_ref, kbf, vbf, *, rows):
    kbf[...] = k_ref[...].astype(BF16)
    vbf[...] = v_ref[...].astype(BF16)
    kc = kc_ref[...].astype(BF16)
    vc = vc_ref[...].astype(BF16)
    nk = WIN_R * GRID_W
    dims = (((1,), (1,)), ((), ()))

    def one_row(r):
        r0 = jnp.clip(r - WIN_R // 2, 0, rows - WIN_R)
        d0 = r0 - r + (WIN_R - 1)
        qoff = pl.multiple_of(r * GRID_W, GRID_W)
        koff = pl.multiple_of(r0 * GRID_W, GRID_W)
        q = q_ref[pl.ds(qoff, GRID_W), :]
        ks = kbf[pl.ds(koff, nk), :]
        vs = vbf[pl.ds(koff, nk), :]
        bias = jnp.concatenate([bias_ref[d0 + 2 * j] for j in range(WIN_R // 2)], axis=1)
        s_n = lax.dot_general(q, ks, dims, preferred_element_type=F32) * ATTN_SCALE + bias
        s_c = lax.dot_general(q, kc, dims, preferred_element_type=F32) * ATTN_SCALE
        m = jnp.maximum(jnp.max(s_n, axis=-1, keepdims=True), jnp.max(s_c, axis=-1, keepdims=True))
        p_n = jnp.exp(s_n - m)
        p_c = jnp.exp(s_c - m)
        denom = jnp.sum(p_n, axis=-1, keepdims=True) + jnp.sum(p_c, axis=-1, keepdims=True)
        o = (jnp.dot(p_n.astype(BF16), vs, preferred_element_type=F32)
             + jnp.dot(p_c.astype(BF16), vc, preferred_element_type=F32)) / denom
        o_ref[pl.ds(qoff, GRID_W), :] = o.astype(o_ref.dtype)

    def trip(i, carry):
        for j in range(_ROWS_PER_ITER):
            one_row(i * _ROWS_PER_ITER + j)
        return carry

    lax.fori_loop(0, rows // _ROWS_PER_ITER, trip, 0)


def _nbr_attention(q, k, v, cache_k, cache_v, bias, layer, st):
    seq = st.seq
    rows = seq // GRID_W
    assert rows >= WIN_R and rows % _ROWS_PER_ITER == 0
    past = cache_k.shape[2]
    ck = cache_k.reshape(st.batch, DEPTH, past, ATTN_WIDTH)
    cv = cache_v.reshape(st.batch, DEPTH, past, ATTN_WIDTH)
    tok_spec = pl.BlockSpec((seq, HEAD_DIM), lambda b, h: (b, h))
    cache_spec = pl.BlockSpec((None, None, past, HEAD_DIM), lambda b, h: (b, layer, 0, h))
    return pl.pallas_call(
        functools.partial(_nbr_attn_kernel, rows=rows),
        out_shape=jax.ShapeDtypeStruct((st.batch * seq, ATTN_WIDTH), BF16),
        grid=(st.batch, N_HEADS),
        in_specs=[tok_spec, tok_spec, tok_spec, cache_spec, cache_spec,
                  pl.BlockSpec((None, _BIAS_PAIRS, GRID_W, 2 * GRID_W), lambda b, h: (h, 0, 0, 0))],
        out_specs=tok_spec,
        scratch_shapes=[pltpu.VMEM((seq, HEAD_DIM), BF16)] * 2,
        compiler_params=_cparams("parallel", "parallel"),
        name="neighbourhood_attention",
    )(q, k, v, ck, cv, bias)


_CONV_PAD = 8


def _short_conv_kernel(u_ref, w_ref, b_ref, o_ref, pad_ref, *, seq):
    width = u_ref.shape[1]
    zeros = jnp.zeros((_CONV_PAD, width), F32)
    pad_ref[0:_CONV_PAD, :] = zeros
    pad_ref[_CONV_PAD + seq:2 * _CONV_PAD + seq, :] = zeros
    pad_ref[_CONV_PAD:_CONV_PAD + seq, :] = u_ref[...]
    w0 = w_ref[0:1, :]
    w1 = w_ref[1:2, :]
    w2 = w_ref[2:3, :]
    b = b_ref[...]

    def chunk(ci, carry):
        base = pl.multiple_of(ci * _ROW_CHUNK, _ROW_CHUNK)
        window = pad_ref[pl.ds(base, _ROW_CHUNK + 2 * _CONV_PAD), :]
        prev = window[_CONV_PAD - 1:_CONV_PAD - 1 + _ROW_CHUNK]
        cur = window[_CONV_PAD:_CONV_PAD + _ROW_CHUNK]
        nxt = window[_CONV_PAD + 1:_CONV_PAD + 1 + _ROW_CHUNK]
        o_ref[pl.ds(base, _ROW_CHUNK), :] = prev * w0 + cur * w1 + nxt * w2 + b
        return carry

    lax.fori_loop(0, seq // _ROW_CHUNK, chunk, 0)


def _short_conv(u, conv_w, conv_b, layer, st):
    width = u.shape[1]
    seq = st.seq
    tc = 512
    return pl.pallas_call(
        functools.partial(_short_conv_kernel, seq=seq),
        out_shape=jax.ShapeDtypeStruct((st.batch * seq, width), F32),
        grid=(st.batch, width // tc),
        in_specs=[pl.BlockSpec((seq, tc), lambda b, j: (b, j)),
                  pl.BlockSpec((None, 3, tc), lambda b, j: (layer, 0, j)),
                  pl.BlockSpec((None, 1, tc), lambda b, j: (layer, 0, j))],
        out_specs=pl.BlockSpec((seq, tc), lambda b, j: (b, j)),
        scratch_shapes=[pltpu.VMEM((seq + 2 * _CONV_PAD, tc), F32)],
        compiler_params=_cparams("parallel", "parallel"),
        name="hyena_short_conv",
    )(u, conv_w, conv_b.reshape(DEPTH, 1, width))


def _dft_matrices(seq):
    f = jnp.arange(seq, dtype=jnp.int32)[:, None]
    t = jnp.arange(seq, dtype=jnp.int32)[None, :]
    ang = ((f * t) % (2 * seq)).astype(F32) * (math.pi / seq)
    top = jnp.cos(ang)
    bot = jnp.where(f == 0, (1 - 2 * (t % 2)).astype(F32), -jnp.sin(ang))
    colscale = jnp.where(jnp.arange(seq) == 0, 0.5, 1.0).astype(F32)[None, :] / seq
    inv = jnp.concatenate([top.T * colscale, bot.T * colscale], axis=1)
    return top.astype(BF16), bot.astype(BF16), inv.astype(BF16)


def _filter_mlp_kernel(z_ref, w1_ref, b1_ref, w2_ref, b2_ref, fr_ref, o_ref):
    fr = fr_ref[...]
    h = jnp.sin(fr * (jnp.dot(z_ref[...], w1_ref[...], preferred_element_type=F32) + b1_ref[...]))
    o_ref[...] = jnp.sin(fr * (jnp.dot(h, w2_ref[...], preferred_element_type=F32) + b2_ref[...]))


def _filter_mlp(seq, w1, b1, w2, b2, freq, layer):
    t = jnp.linspace(0.0, 1.0, seq, dtype=F32)[:, None]
    bands = (FILTER_EMB - 1) // 2
    ang = (2.0 * math.pi / seq) * jnp.arange(seq, dtype=F32)[:, None] * jnp.linspace(
        1e-4, bands - 1, bands, dtype=F32)[None]
    z = jnp.concatenate([t, jnp.cos(ang), -jnp.sin(ang)], axis=-1)
    z = jnp.pad(z, ((0, 0), (0, LANES - FILTER_EMB)))
    w1p = jnp.pad(w1, ((0, 0), (0, LANES - FILTER_EMB), (0, 0)))
    vec = lambda a: a.reshape(DEPTH, 1, FILTER_HIDDEN)
    vspec = pl.BlockSpec((None, 1, FILTER_HIDDEN), lambda i: (layer, 0, 0))
    return pl.pallas_call(
        _filter_mlp_kernel,
        out_shape=jax.ShapeDtypeStruct((seq, FILTER_HIDDEN), F32),
        grid=(1,),
        in_specs=[pl.BlockSpec((seq, LANES), lambda i: (0, 0)),
                  pl.BlockSpec((None, LANES, FILTER_HIDDEN), lambda i: (layer, 0, 0)),
                  vspec,
                  pl.BlockSpec((None, FILTER_HIDDEN, FILTER_HIDDEN), lambda i: (layer, 0, 0)),
                  vspec, vspec],
        out_specs=pl.BlockSpec((seq, FILTER_HIDDEN), lambda i: (0, 0)),
        compiler_params=_cparams("arbitrary"),
        name="hyena_filter_mlp",
    )(z, w1p, vec(b1), w2, vec(b2), vec(freq))


def _filter_taps_kernel(h_ref, dl_ref, wf_ref, wb_ref, a_ref, d_ref, nyq_ref, *, seq):
    h = h_ref[...]
    shape = (seq, dl_ref.shape[1])
    row = lax.broadcasted_iota(jnp.int32, shape, 0)
    t = row.astype(F32) / float(seq - 1)
    dec = jnp.exp(-t * dl_ref[...])
    fwd = jnp.dot(h, wf_ref[...], preferred_element_type=F32) * dec
    bwd = jnp.dot(h, wb_ref[...], preferred_element_type=F32) * dec
    bwd = jnp.where(row == 0, 0.0, bwd)
    inv = 1.0 / jnp.sum(jnp.abs(fwd) + jnp.abs(bwd), axis=0, keepdims=True)
    a = (fwd + bwd) * inv
    a_ref[...] = a
    d_ref[...] = (fwd - bwd) * inv
    sign = jnp.where(row % 2 == 0, 1.0, -1.0)
    nyq_ref[...] = jnp.sum(a * sign, axis=0, keepdims=True)


def _filter_taps(hidden, w3, layer, *, seq):
    tc = 256
    nct = HYENA_WIDTH // tc
    deltas = jnp.abs(jnp.linspace(math.log(DECAY_TARGET) / SLOW_DECAY_PCT,
                                  math.log(DECAY_TARGET) / FAST_DECAY_PCT, HYENA_WIDTH, dtype=F32))
    tap_shape = jax.ShapeDtypeStruct((HYENA_ORDER, seq, HYENA_WIDTH), F32)
    tap_spec = pl.BlockSpec((None, seq, tc), lambda o, j: (o, 0, j))
    return pl.pallas_call(
        functools.partial(_filter_taps_kernel, seq=seq),
        out_shape=(tap_shape, tap_shape, jax.ShapeDtypeStruct((HYENA_ORDER, 1, HYENA_WIDTH), F32)),
        grid=(HYENA_ORDER, nct),
        in_specs=[pl.BlockSpec((seq, FILTER_HIDDEN), lambda o, j: (0, 0)),
                  pl.BlockSpec((1, tc), lambda o, j: (0, j)),
                  pl.BlockSpec((None, FILTER_HIDDEN, tc), lambda o, j: (layer, 0, (2 * o) * nct + j)),
                  pl.BlockSpec((None, FILTER_HIDDEN, tc), lambda o, j: (layer, 0, (2 * o + 1) * nct + j))],
        out_specs=(tap_spec, tap_spec, pl.BlockSpec((None, 1, tc), lambda o, j: (o, 0, j))),
        compiler_params=_cparams("parallel", "parallel"),
        name="hyena_filter_taps",
    )(hidden, deltas.reshape(1, HYENA_WIDTH), w3, w3)


def _filter_spec_kernel(wt_ref, wb_ref, a_ref, d_ref, nyq_ref, o_ref, abf, dbf):
    i = pl.program_id(2)

    @pl.when(i == 0)
    def _():
        abf[...] = a_ref[...].astype(BF16)
        dbf[...] = d_ref[...].astype(BF16)

    top = jnp.dot(wt_ref[...], abf[...], preferred_element_type=F32)
    bot = jnp.dot(wb_ref[...], dbf[...], preferred_element_type=F32)
    row = lax.broadcasted_iota(jnp.int32, bot.shape, 0)
    bot = jnp.where((row == 0) & (i == 0), nyq_ref[...], bot)
    o_ref[0] = top
    o_ref[1] = bot


def _filter_spectrum(dft_top, dft_bot, taps_a, taps_d, nyq, *, seq):
    tc = 512
    tf = min(seq, 512)
    wspec = pl.BlockSpec((tf, seq), lambda o, c, i: (i, 0))
    tspec = pl.BlockSpec((None, seq, tc), lambda o, c, i: (o, 0, c))
    return pl.pallas_call(
        _filter_spec_kernel,
        out_shape=jax.ShapeDtypeStruct((HYENA_ORDER, 2, seq, HYENA_WIDTH), F32),
        grid=(HYENA_ORDER, HYENA_WIDTH // tc, seq // tf),
        in_specs=[wspec, wspec, tspec, tspec,
                  pl.BlockSpec((None, 1, tc), lambda o, c, i: (o, 0, c))],
        out_specs=pl.BlockSpec((None, 2, tf, tc), lambda o, c, i: (o, 0, i, c)),
        scratch_shapes=[pltpu.VMEM((seq, tc), BF16)] * 2,
        compiler_params=_cparams("parallel", "parallel", "arbitrary"),
        name="hyena_filter_spectrum",
    )(dft_top, dft_bot, taps_a, taps_d, nyq)


def _fwd_dft_kernel(wt_ref, wb_ref, z_ref, p_ref, o_ref, zbf):
    i = pl.program_id(2)

    @pl.when(i == 0)
    def _():
        zbf[...] = z_ref[...].astype(BF16)

    top = jnp.dot(wt_ref[...], zbf[...], preferred_element_type=F32)
    bot = jnp.dot(wb_ref[...], zbf[...], preferred_element_type=F32)
    pt = p_ref[0]
    pb = p_ref[1]
    row = lax.broadcasted_iota(jnp.int32, top.shape, 0)
    real_row = (row == 0) & (i == 0)
    o_ref[0] = jnp.where(real_row, top * pt, top * pt - bot * pb).astype(o_ref.dtype)
    o_ref[1] = jnp.where(real_row, bot * pb, top * pb + bot * pt).astype(o_ref.dtype)


def _fwd_dft(dft_top, dft_bot, z, z_col_off, spec, order, *, batch, seq):
    tc = 512
    tf = min(seq, 512)
    zc = z_col_off // tc
    wspec = pl.BlockSpec((tf, seq), lambda b, c, i: (i, 0))
    return pl.pallas_call(
        _fwd_dft_kernel,
        out_shape=jax.ShapeDtypeStruct((batch, 2, seq, HYENA_WIDTH), BF16),
        grid=(batch, HYENA_WIDTH // tc, seq // tf),
        in_specs=[wspec, wspec,
                  pl.BlockSpec((seq, tc), lambda b, c, i: (b, zc + c)),
                  pl.BlockSpec((None, 2, tf, tc), lambda b, c, i: (order, 0, i, c))],
        out_specs=pl.BlockSpec((None, 2, tf, tc), lambda b, c, i: (b, 0, i, c)),
        scratch_shapes=[pltpu.VMEM((seq, tc), BF16)],
        compiler_params=_cparams("parallel", "parallel", "arbitrary"),
        name="hyena_fwd_dft",
    )(dft_top, dft_bot, z, spec)


def _inv_dft_kernel(wi_ref, s_ref, gate_ref, z_ref, bias_ref, o_ref):
    s = s_ref[...]
    s2 = s.reshape(s.shape[0] * s.shape[1], s.shape[2])
    conv = jnp.dot(wi_ref[...], s2, preferred_element_type=F32)
    o_ref[...] = (gate_ref[...] * (conv + bias_ref[...] * z_ref[...])).astype(o_ref.dtype)


def _inv_dft(dft_inv, s, gate, gate_col_off, z, z_col_off, hy_bias, layer, order, out_dtype, *, batch, seq):
    tc = 512
    tl = min(seq, 512)
    nrb = seq // tl
    gc = gate_col_off // tc
    zc = z_col_off // tc
    return pl.pallas_call(
        _inv_dft_kernel,
        out_shape=jax.ShapeDtypeStruct((batch * seq, HYENA_WIDTH), out_dtype),
        grid=(batch, HYENA_WIDTH // tc, nrb),
        in_specs=[pl.BlockSpec((tl, 2 * seq), lambda b, c, i: (i, 0)),
                  pl.BlockSpec((None, 2, seq, tc), lambda b, c, i: (b, 0, 0, c)),
                  pl.BlockSpec((tl, tc), lambda b, c, i: (b * nrb + i, gc + c)),
                  pl.BlockSpec((tl, tc), lambda b, c, i: (b * nrb + i, zc + c)),
                  pl.BlockSpec((None, 1, tc), lambda b, c, i: (layer * HYENA_ORDER + order, 0, c))],
        out_specs=pl.BlockSpec((tl, tc), lambda b, c, i: (b * nrb + i, c)),
        compiler_params=_cparams("parallel", "parallel", "arbitrary"),
        name="hyena_inv_dft",
    )(dft_inv, s, gate, z, hy_bias.reshape(DEPTH * HYENA_ORDER, 1, HYENA_WIDTH))


def _hyena(u, p, layer, st):
    batch, seq = st.batch, st.seq
    uc = _short_conv(u, p["hy_conv_w"], p["hy_conv_b"], layer, st)
    dft_top, dft_bot, dft_inv = _dft_matrices(seq)
    hidden = _filter_mlp(seq, p["filt_w1"], p["filt_b1"], p["filt_w2"], p["filt_b2"], p["filt_freq"], layer)
    taps_a, taps_d, nyq = _filter_taps(hidden, p["filt_w3"], layer, seq=seq)
    spec = _filter_spectrum(dft_top, dft_bot, taps_a, taps_d, nyq, seq=seq)
    s = _fwd_dft(dft_top, dft_bot, uc, 0, spec, 0, batch=batch, seq=seq)
    z1 = _inv_dft(dft_inv, s, uc, HYENA_WIDTH, uc, 0, p["hy_bias"], layer, 0, F32, batch=batch, seq=seq)
    s = _fwd_dft(dft_top, dft_bot, z1, 0, spec, 1, batch=batch, seq=seq)
    return _inv_dft(dft_inv, s, uc, 2 * HYENA_WIDTH, z1, 0, p["hy_bias"], layer, 1, BF16, batch=batch, seq=seq)


def _block(x, p, modflat, layer, st, ctx_kv):
    h = _normmod(x, p["norm1_g"], modflat, st, layer, 1, 0)
    lin = functools.partial(_linear, h, p["w_in"], layer, tn=512)
    u_pool = lin(COL_POOL, POOL_WIDTH, kind="plain", out_dtype=F32)
    q = lin(COL_Q, ATTN_WIDTH, kind="rmshead", out_dtype=BF16, head_gain=p["q_norm_g"])
    k = lin(COL_K, ATTN_WIDTH, kind="rmshead", out_dtype=F32, head_gain=p["k_norm_g"])
    v = lin(COL_V, ATTN_WIDTH, kind="plain", out_dtype=F32)
    u_hy = lin(COL_HY, (HYENA_ORDER + 1) * HYENA_WIDTH, kind="plain", out_dtype=F32)
    gates = lin(COL_GATES, 3 * D_MODEL, kind="sigmoid", out_dtype=BF16)

    pool = _pool(u_pool, p["pool_w"], p["pool_scale"], layer, st)
    if ctx_kv is None:
        attn = _ctx_attention(q, k, v, st)
    else:
        attn = _nbr_attention(q, k, v, ctx_kv[0], ctx_kv[1], ctx_kv[2], layer, st)
    hy = _hyena(u_hy, p, layer, st)

    mixed = _mix(pool, attn, hy, p["w_branch_pool"], p["w_branch_attn"], p["w_branch_hyena"], gates, layer)
    x = _linear(mixed, p["w_out"], layer, 0, D_MODEL, tn=512, kind="residual", out_dtype=F32,
                residual=x, modflat=modflat, st=st, which_gate=2)
    h = _normmod(x, p["norm2_g"], modflat, st, layer, 4, 3)
    a = _ffn1(h, p["w_gate"], p["w_up"], layer)
    half = D_FF // 2
    for k_off in (0, half):
        x = _linear(a, p["w_down"], layer, 0, D_MODEL, tn=512, tm=512, kind="residual", out_dtype=F32,
                    k_off=k_off, k_len=half, residual=x, modflat=modflat, st=st, which_gate=5)
    return x, k, v


_WEIGHT_NAMES = ("ada_w", "ada_b", "norm1_g", "norm2_g", "w_in", "pool_w", "pool_scale",
                 "q_norm_g", "k_norm_g", "rpb", "hy_conv_w", "hy_conv_b", "filt_w1", "filt_b1",
                 "filt_w2", "filt_b2", "filt_w3", "filt_freq", "hy_bias", "w_branch_pool",
                 "w_branch_attn", "w_branch_hyena", "w_out", "w_gate", "w_up", "w_down")


def kernel(x_prompt, x_sample, cache_k, cache_v, c, c_ctx, ada_w, ada_b, norm1_g, norm2_g, w_in, pool_w, pool_scale, q_norm_g, k_norm_g, rpb, hy_conv_w, hy_conv_b, filt_w1, filt_b1, filt_w2, filt_b2, filt_w3, filt_freq, hy_bias, w_branch_pool, w_branch_attn, w_branch_hyena, w_out, w_gate, w_up, w_down):
    p = dict(zip(_WEIGHT_NAMES, (ada_w, ada_b, norm1_g, norm2_g, w_in, pool_w, pool_scale, q_norm_g,
                                 k_norm_g, rpb, hy_conv_w, hy_conv_b, filt_w1, filt_b1, filt_w2, filt_b2,
                                 filt_w3, filt_freq, hy_bias, w_branch_pool, w_branch_attn,
                                 w_branch_hyena, w_out, w_gate, w_up, w_down)))
    nb, ns, d = x_prompt.shape
    lb, ls, _ = x_sample.shape
    assert d == D_MODEL and lb + 1 <= MOD_ROWS
    ctx = _Stream(nb, ns, 0, False)
    lat = _Stream(lb, ls, 1, True)

    cmat = jnp.concatenate([c_ctx[None, :], c, jnp.zeros((MOD_ROWS - 1 - lb, d), F32)], axis=0)
    mod = _modulation(cmat, ada_w, ada_b)
    modflat = mod.reshape(DEPTH * MOD_ROWS * 6, 1, d)

    y_ctx = x_prompt.reshape(nb * ns, d)
    y_lat = x_sample.reshape(lb * ls, d)
    new_k, new_v = [], []
    for layer in range(DEPTH):
        y_ctx, k_l, v_l = _block(y_ctx, p, modflat, layer, ctx, None)
        new_k.append(k_l.reshape(nb, ns, N_HEADS, HEAD_DIM))
        new_v.append(v_l.reshape(nb, ns, N_HEADS, HEAD_DIM))
        bias = _bias_table(rpb, layer)
        y_lat, _, _ = _block(y_lat, p, modflat, layer, lat, (cache_k, cache_v, bias))

    return (y_ctx.reshape(nb, ns, d), y_lat.reshape(lb, ls, d),
            jnp.stack(new_k, axis=1), jnp.stack(new_v, axis=1))
```

```python
import functools
import math

import jax
import jax.numpy as jnp
from jax import lax
from jax.experimental import pallas as pl
from jax.experimental.pallas import tpu as pltpu

F32 = jnp.float32
BF16 = jnp.bfloat16

D_MODEL = 4096
DEPTH = 2
GRID_W = 64
HEAD_DIM = 128
N_HEADS = D_MODEL // 256
ATTN_WIDTH = N_HEADS * HEAD_DIM
ATTN_SCALE = HEAD_DIM ** -0.5
WIN_R = 8
WIN_C = 16
NEG_INF = -1e30
POOL_WIDTH = D_MODEL // 2
POOL_SIZES = (2, 4, 8, 16)
POOL_GROUP = POOL_WIDTH // len(POOL_SIZES)
HYENA_WIDTH = D_MODEL // 2
HYENA_ORDER = 2
FILTER_EMB = 33
FILTER_HIDDEN = 64
DECAY_TARGET = 1e-2
FAST_DECAY_PCT = 0.3
SLOW_DECAY_PCT = 1.5
D_FF = ((8 * D_MODEL + 767) // 768) * 256
COL_POOL = 0
COL_Q = POOL_WIDTH
COL_K = COL_Q + ATTN_WIDTH
COL_V = COL_K + ATTN_WIDTH
COL_HY = COL_V + ATTN_WIDTH
COL_GATES = COL_HY + (HYENA_ORDER + 1) * HYENA_WIDTH
MOD_ROWS = 8
RMS_EPS = 1e-6

LANES = 128
VMEM_LIMIT = 56 * 1024 * 1024
SUB_M = 512


def _cparams(*sem):
    return pltpu.CompilerParams(dimension_semantics=sem, vmem_limit_bytes=VMEM_LIMIT)


def _row_chunks(tm):
    return [slice(r, r + min(SUB_M, tm)) for r in range(0, tm, min(SUB_M, tm))]


def _token_tile(m_tok):
    return 1024 if m_tok % 1024 == 0 else 512


def _mod_kernel(c_ref, w_ref, b_ref, o_ref):
    c = c_ref[...]
    s = (c * jax.nn.sigmoid(c)).astype(BF16)
    o_ref[...] = jnp.dot(s, w_ref[...].astype(BF16), preferred_element_type=F32) + b_ref[...]


def _modulation(cmat, ada_w, ada_b):
    tn = 512
    n6 = ada_w.shape[-1]
    return pl.pallas_call(
        _mod_kernel,
        out_shape=jax.ShapeDtypeStruct((DEPTH, MOD_ROWS, n6), F32),
        grid=(DEPTH, n6 // tn),
        in_specs=[pl.BlockSpec((MOD_ROWS, D_MODEL), lambda l, j: (0, 0)),
                  pl.BlockSpec((None, D_MODEL, tn), lambda l, j: (l, 0, j)),
                  pl.BlockSpec((None, 1, tn), lambda l, j: (l, 0, j))],
        out_specs=pl.BlockSpec((None, MOD_ROWS, tn), lambda l, j: (l, 0, j)),
        compiler_params=_cparams("parallel", "parallel"),
        name="modulation",
    )(cmat, ada_w, ada_b.reshape(DEPTH, 1, n6))


class _Stream:
    def __init__(self, batch, seq, mod_base, per_batch_mod):
        self.batch = batch
        self.seq = seq
        self.mod_base = mod_base
        self.per_batch_mod = per_batch_mod

    def mod_index(self, layer, i, tm, which):
        row = self.mod_base + ((i * tm) // self.seq if self.per_batch_mod else 0)
        return (layer * MOD_ROWS + row) * 6 + which


def _normmod_kernel(x_ref, g_ref, sc_ref, sh_ref, o_ref):
    x = x_ref[...]
    y = x * lax.rsqrt(jnp.mean(x * x, axis=-1, keepdims=True) + RMS_EPS)
    y = y * g_ref[...]
    o_ref[...] = (y * (1.0 + sc_ref[0]) + sh_ref[0]).astype(o_ref.dtype)


def _normmod(x, g, modflat, st, layer, which_scale, which_shift):
    m_tok, d = x.shape
    tm = 256
    return pl.pallas_call(
        _normmod_kernel,
        out_shape=jax.ShapeDtypeStruct((m_tok, d), BF16),
        grid=(m_tok // tm,),
        in_specs=[pl.BlockSpec((tm, d), lambda i: (i, 0)),
                  pl.BlockSpec((None, 1, d), lambda i: (layer, 0, 0)),
                  pl.BlockSpec((1, 1, d), lambda i: (st.mod_index(layer, i, tm, which_scale), 0, 0)),
                  pl.BlockSpec((1, 1, d), lambda i: (st.mod_index(layer, i, tm, which_shift), 0, 0))],
        out_specs=pl.BlockSpec((tm, d), lambda i: (i, 0)),
        compiler_params=_cparams("parallel"),
        name="normmod",
    )(x, g.reshape(DEPTH, 1, d), modflat, modflat)


def _cast_weight_once(w_ref, wbf_ref):
    @pl.when(pl.program_id(1) == 0)
    def _():
        wbf_ref[...] = w_ref[...].astype(BF16)


def _lin_plain_kernel(x_ref, w_ref, o_ref, wbf_ref):
    _cast_weight_once(w_ref, wbf_ref)
    for rows in _row_chunks(x_ref.shape[0]):
        acc = jnp.dot(x_ref[rows, :], wbf_ref[...], preferred_element_type=F32)
        o_ref[rows, :] = acc.astype(o_ref.dtype)


def _lin_sigmoid_kernel(x_ref, w_ref, o_ref, wbf_ref):
    _cast_weight_once(w_ref, wbf_ref)
    for rows in _row_chunks(x_ref.shape[0]):
        acc = jnp.dot(x_ref[rows, :], wbf_ref[...], preferred_element_type=F32)
        o_ref[rows, :] = jax.nn.sigmoid(acc).astype(o_ref.dtype)


def _lin_rmshead_kernel(x_ref, w_ref, g_ref, o_ref, wbf_ref):
    _cast_weight_once(w_ref, wbf_ref)
    g = g_ref[...]
    for rows in _row_chunks(x_ref.shape[0]):
        acc = jnp.dot(x_ref[rows, :], wbf_ref[...], preferred_element_type=F32)
        for h in range(acc.shape[1] // HEAD_DIM):
            a = acc[:, h * HEAD_DIM:(h + 1) * HEAD_DIM]
            y = a * lax.rsqrt(jnp.mean(a * a, axis=-1, keepdims=True) + RMS_EPS)
            o_ref[rows, h * HEAD_DIM:(h + 1) * HEAD_DIM] = (y * g).astype(o_ref.dtype)


def _lin_residual_kernel(x_ref, w_ref, res_ref, gate_ref, o_ref, wbf_ref):
    _cast_weight_once(w_ref, wbf_ref)
    for rows in _row_chunks(x_ref.shape[0]):
        acc = jnp.dot(x_ref[rows, :], wbf_ref[...], preferred_element_type=F32)
        o_ref[rows, :] = res_ref[rows, :] + gate_ref[0] * acc


def _linear(x, w, layer, col_off, ncols, *, tn, kind, out_dtype, tm=None, k_off=0, k_len=None,
            head_gain=None, residual=None, modflat=None, st=None, which_gate=None):
    m_tok = x.shape[0]
    k_len = x.shape[1] if k_len is None else k_len
    tm = _token_tile(m_tok) if tm is None else tm
    cb = col_off // tn
    kb = k_off // k_len
    in_specs = [pl.BlockSpec((tm, k_len), lambda n, m: (m, kb)),
                pl.BlockSpec((None, k_len, tn), lambda n, m: (layer, kb, cb + n))]
    args = [x, w]
    if kind == "plain":
        body = _lin_plain_kernel
    elif kind == "sigmoid":
        body = _lin_sigmoid_kernel
    elif kind == "rmshead":
        body = _lin_rmshead_kernel
        in_specs.append(pl.BlockSpec((None, 1, HEAD_DIM), lambda n, m: (layer, 0, 0)))
        args.append(head_gain.reshape(DEPTH, 1, HEAD_DIM))
    else:
        body = _lin_residual_kernel
        in_specs.append(pl.BlockSpec((tm, tn), lambda n, m: (m, n)))
        in_specs.append(pl.BlockSpec(
            (1, 1, tn), lambda n, m: (st.mod_index(layer, m, tm, which_gate), 0, n)))
        args += [residual, modflat]
    return pl.pallas_call(
        body,
        out_shape=jax.ShapeDtypeStruct((m_tok, ncols), out_dtype),
        grid=(ncols // tn, m_tok // tm),
        in_specs=in_specs,
        out_specs=pl.BlockSpec((tm, tn), lambda n, m: (m, n)),
        scratch_shapes=[pltpu.VMEM((k_len, tn), BF16)],
        compiler_params=_cparams("parallel", "arbitrary"),
        name="linear_" + kind,
    )(*args)


def _mix_kernel(xp_ref, xa_ref, xh_ref, wp_ref, wa_ref, wh_ref, gp_ref, ga_ref, gh_ref, o_ref,
                wpb, wab, whb):
    @pl.when(pl.program_id(1) == 0)
    def _():
        wpb[...] = wp_ref[...].astype(BF16)
        wab[...] = wa_ref[...].astype(BF16)
        whb[...] = wh_ref[...].astype(BF16)

    acc = gp_ref[...].astype(F32) * jnp.dot(xp_ref[...], wpb[...], preferred_element_type=F32)
    acc = acc + ga_ref[...].astype(F32) * jnp.dot(xa_ref[...], wab[...], preferred_element_type=F32)
    acc = acc + gh_ref[...].astype(F32) * jnp.dot(xh_ref[...], whb[...], preferred_element_type=F32)
    o_ref[...] = acc.astype(o_ref.dtype)


def _mix(xp, xa, xh, wp, wa, wh, gates, layer):
    m_tok, k = xp.shape
    n = wp.shape[-1]
    tm, tn = 512, 512
    nb = n // tn
    xspec = pl.BlockSpec((tm, k), lambda j, i: (i, 0))
    wspec = pl.BlockSpec((None, k, tn), lambda j, i: (layer, 0, j))
    return pl.pallas_call(
        _mix_kernel,
        out_shape=jax.ShapeDtypeStruct((m_tok, n), BF16),
        grid=(nb, m_tok // tm),
        in_specs=[xspec, xspec, xspec, wspec, wspec, wspec,
                  pl.BlockSpec((tm, tn), lambda j, i: (i, j)),
                  pl.BlockSpec((tm, tn), lambda j, i: (i, nb + j)),
                  pl.BlockSpec((tm, tn), lambda j, i: (i, 2 * nb + j))],
        out_specs=pl.BlockSpec((tm, tn), lambda j, i: (i, j)),
        scratch_shapes=[pltpu.VMEM((k, tn), BF16)] * 3,
        compiler_params=_cparams("parallel", "arbitrary"),
        name="mix",
    )(xp, xa, xh, wp, wa, wh, gates, gates, gates)


def _ffn1_kernel(x_ref, wg_ref, wu_ref, o_ref, wgb, wub):
    @pl.when(pl.program_id(1) == 0)
    def _():
        wgb[...] = wg_ref[...].astype(BF16)
        wub[...] = wu_ref[...].astype(BF16)

    for rows in _row_chunks(x_ref.shape[0]):
        x = x_ref[rows, :]
        g = jnp.dot(x, wgb[...], preferred_element_type=F32)
        u = jnp.dot(x, wub[...], preferred_element_type=F32)
        o_ref[rows, :] = (g * jax.nn.sigmoid(g) * u).astype(o_ref.dtype)


def _ffn1(x, wg, wu, layer):
    m_tok, k = x.shape
    n = wg.shape[-1]
    tm, tn = _token_tile(m_tok), 256
    wspec = pl.BlockSpec((None, k, tn), lambda j, i: (layer, 0, j))
    return pl.pallas_call(
        _ffn1_kernel,
        out_shape=jax.ShapeDtypeStruct((m_tok, n), BF16),
        grid=(n // tn, m_tok // tm),
        in_specs=[pl.BlockSpec((tm, k), lambda j, i: (i, 0)), wspec, wspec],
        out_specs=pl.BlockSpec((tm, tn), lambda j, i: (i, j)),
        scratch_shapes=[pltpu.VMEM((k, tn), BF16)] * 2,
        compiler_params=_cparams("parallel", "arbitrary"),
        name="ffn_gate_up",
    )(x, wg, wu)


_POOL_PAD = 16
_ROW_CHUNK = 64


def _pool_kernel(u_ref, w_ref, sc_ref, o_ref, pad_ref, *, seq):
    g = pl.program_id(1)
    width = u_ref.shape[1]
    zeros = jnp.zeros((_POOL_PAD, width), F32)
    pad_ref[0:_POOL_PAD, :] = zeros
    pad_ref[_POOL_PAD + seq:2 * _POOL_PAD + seq, :] = zeros
    pad_ref[_POOL_PAD:_POOL_PAD + seq, :] = u_ref[...]
    wbf = w_ref[...].astype(BF16)
    scale = sc_ref[...]

    for gi, win in enumerate(POOL_SIZES):
        half = win // 2

        @pl.when(g == gi)
        def _(win=win, half=half):
            def chunk(ci, carry):
                base = pl.multiple_of(ci * _ROW_CHUNK, _ROW_CHUNK)
                window = pad_ref[pl.ds(base, _ROW_CHUNK + 2 * _POOL_PAD), :]
                lo = _POOL_PAD - half
                acc = window[lo:lo + _ROW_CHUNK]
                for j in range(1, win):
                    acc = acc + window[lo + j:lo + j + _ROW_CHUNK]
                t = base + lax.broadcasted_iota(jnp.int32, (_ROW_CHUNK, width), 0)
                cnt = (jnp.minimum(t + (win - half), seq) - jnp.maximum(t - half, 0)).astype(F32)
                centre = window[_POOL_PAD:_POOL_PAD + _ROW_CHUNK]
                d = (acc / cnt - centre).astype(BF16)
                y = jnp.dot(d, wbf, preferred_element_type=F32) * scale
                o_ref[pl.ds(base, _ROW_CHUNK), :] = y.astype(o_ref.dtype)
                return carry

            lax.fori_loop(0, seq // _ROW_CHUNK, chunk, 0)


def _pool(u, pool_w, pool_scale, layer, st):
    seq = st.seq
    return pl.pallas_call(
        functools.partial(_pool_kernel, seq=seq),
        out_shape=jax.ShapeDtypeStruct((st.batch * seq, POOL_WIDTH), BF16),
        grid=(st.batch, len(POOL_SIZES)),
        in_specs=[pl.BlockSpec((seq, POOL_GROUP), lambda b, g: (b, g)),
                  pl.BlockSpec((None, None, POOL_GROUP, POOL_GROUP), lambda b, g: (layer, g, 0, 0)),
                  pl.BlockSpec((None, 1, POOL_GROUP), lambda b, g: (layer, 0, g))],
        out_specs=pl.BlockSpec((seq, POOL_GROUP), lambda b, g: (b, g)),
        scratch_shapes=[pltpu.VMEM((seq + 2 * _POOL_PAD, POOL_GROUP), F32)],
        compiler_params=_cparams("parallel", "parallel"),
        name="pool_mixer",
    )(u, pool_w, pool_scale.reshape(DEPTH, 1, POOL_WIDTH))


def _ctx_attn_kernel(q_ref, k_ref, v_ref, o_ref):
    for h in range(N_HEADS):
        sl = slice(h * HEAD_DIM, (h + 1) * HEAD_DIM)
        q = q_ref[:, sl]
        k = k_ref[:, sl].astype(BF16)
        v = v_ref[:, sl].astype(BF16)
        s = lax.dot_general(q, k, (((1,), (1,)), ((), ())), preferred_element_type=F32) * ATTN_SCALE
        m = jnp.max(s, axis=-1, keepdims=True)
        p = jnp.exp(s - m)
        denom = jnp.sum(p, axis=-1, keepdims=True)
        o = jnp.dot(p.astype(BF16), v, preferred_element_type=F32) / denom
        o_ref[:, sl] = o.astype(o_ref.dtype)


def _ctx_attention(q, k, v, st):
    spec = pl.BlockSpec((st.seq, ATTN_WIDTH), lambda b: (b, 0))
    return pl.pallas_call(
        _ctx_attn_kernel,
        out_shape=jax.ShapeDtypeStruct((st.batch * st.seq, ATTN_WIDTH), BF16),
        grid=(st.batch,),
        in_specs=[spec, spec, spec],
        out_specs=spec,
        compiler_params=_cparams("parallel"),
        name="context_attention",
    )(q, k, v)


_BIAS_TILES = 2 * WIN_R
_Q_GROUP = 8
_BAND = 2 * WIN_R


def _bias_table_kernel(rpb_ref, o_ref, *, layer):
    h = pl.program_id(0)
    shape = (GRID_W, 2 * GRID_W)
    qc = lax.broadcasted_iota(jnp.int32, shape, 0)
    lane = lax.broadcasted_iota(jnp.int32, shape, 1)
    kc = lane % GRID_W
    upper = lane >= GRID_W
    c0 = jnp.clip(qc - WIN_C // 2, 0, GRID_W - WIN_C)
    in_window = (kc >= c0) & (kc < c0 + WIN_C)
    rel = kc - qc + (WIN_C - 1)
    n_dc = 2 * WIN_C - 1
    n_dr = 2 * WIN_R - 1
    row_base = (layer * N_HEADS + h) * n_dr
    for e in range(_BIAS_TILES):
        d_lo = max(e - 1, 0)
        d_hi = min(e, n_dr - 1)
        tile = jnp.zeros(shape, F32)
        for dc in range(n_dc):
            lo = rpb_ref[(row_base + d_lo) * n_dc + dc]
            hi = rpb_ref[(row_base + d_hi) * n_dc + dc]
            tile = jnp.where(rel == dc, jnp.where(upper, hi, lo), tile)
        valid = in_window
        if e == 0:
            valid = valid & upper
        if e == _BIAS_TILES - 1:
            valid = valid & jnp.logical_not(upper)
        o_ref[e] = jnp.where(valid, tile, NEG_INF)


def _bias_table(rpb, layer):
    return pl.pallas_call(
        functools.partial(_bias_table_kernel, layer=layer),
        out_shape=jax.ShapeDtypeStruct((N_HEADS, _BIAS_TILES, GRID_W, 2 * GRID_W), F32),
        grid=(N_HEADS,),
        in_specs=[pl.BlockSpec(memory_space=pltpu.SMEM)],
        out_specs=pl.BlockSpec((None, _BIAS_TILES, GRID_W, 2 * GRID_W), lambda h: (h, 0, 0, 0)),
        compiler_params=_cparams("parallel"),
        name="rel_pos_bias_table",
    )(rpb.reshape(-1))


def _nbr_attn_kernel(q_ref, k_ref, v_ref, kc_ref, vc_ref, bias_ref, o_ref,
                     kbf, vbf, s_ref, sc_ref, p_ref, pc_ref, *, rows):
    kbf[...] = k_ref[...].astype(BF16)
    vbf[...] = v_ref[...].astype(BF16)
    kc = kc_ref[...].astype(BF16)
    vc = vc_ref[...].astype(BF16)
    dims = (((1,), (1,)), ((), ()))
    pair = 2 * GRID_W
    lower = lax.broadcasted_iota(jnp.int32, (GRID_W, pair), 1) < GRID_W
    zero_tile = jnp.zeros((GRID_W, pair), BF16)

    for g in range(rows // _Q_GROUP):
        slot = g % 2
        ks = min(max(g * _Q_GROUP - WIN_R // 2, 0), rows - _BAND)
        qsl = slice(g * _Q_GROUP * GRID_W, (g + 1) * _Q_GROUP * GRID_W)
        ksl = slice(ks * GRID_W, (ks + _BAND) * GRID_W)
        qg = q_ref[qsl, :]
        s_ref[slot] = lax.dot_general(qg, kbf[ksl, :], dims, preferred_element_type=F32) * ATTN_SCALE
        sc_ref[slot] = lax.dot_general(qg, kc, dims, preferred_element_type=F32) * ATTN_SCALE
        for i in range(_Q_GROUP):
            qr = g * _Q_GROUP + i
            r0 = min(max(qr - WIN_R // 2, 0), rows - WIN_R)
            first = r0 - ks
            j0, j1 = first // 2, (first + WIN_R + 1) // 2
            rsl = slice(i * GRID_W, (i + 1) * GRID_W)
            tiles = []
            for j in range(j0, j1):
                kr0 = ks + 2 * j
                t = s_ref[slot, rsl, j * pair:(j + 1) * pair] + bias_ref[kr0 - qr + WIN_R]
                if kr0 < r0:
                    t = jnp.where(lower, NEG_INF, t)
                if kr0 + 1 >= r0 + WIN_R:
                    t = jnp.where(lower, t, NEG_INF)
                tiles.append(t)
            sc = sc_ref[slot, rsl, :]
            mt = functools.reduce(jnp.maximum, tiles)
            m = jnp.maximum(jnp.max(mt, axis=-1, keepdims=True), jnp.max(sc, axis=-1, keepdims=True))
            ps = [jnp.exp(t - m) for t in tiles]
            pc = jnp.exp(sc - m)
            denom = (jnp.sum(functools.reduce(jnp.add, ps), axis=-1, keepdims=True)
                     + jnp.sum(pc, axis=-1, keepdims=True))
            inv = 1.0 / denom
            for j in range(_BAND // 2):
                val = (ps[j - j0] * inv).astype(BF16) if j0 <= j < j1 else zero_tile
                p_ref[slot, rsl, j * pair:(j + 1) * pair] = val
            pc_ref[slot, rsl, :] = (pc * inv).astype(BF16)
        o = (jnp.dot(p_ref[slot], vbf[ksl, :], preferred_element_type=F32)
             + jnp.dot(pc_ref[slot], vc, preferred_element_type=F32))
        o_ref[qsl, :] = o.astype(o_ref.dtype)


def _nbr_attention(q, k, v, cache_k, cache_v, bias, layer, st):
    seq = st.seq
    rows = seq // GRID_W
    assert rows >= _BAND and rows % _Q_GROUP == 0
    past = cache_k.shape[2]
    ck = cache_k.reshape(st.batch, DEPTH, past, ATTN_WIDTH)
    cv = cache_v.reshape(st.batch, DEPTH, past, ATTN_WIDTH)
    tok_spec = pl.BlockSpec((seq, HEAD_DIM), lambda b, h: (b, h))
    cache_spec = pl.BlockSpec((None, None, past, HEAD_DIM), lambda b, h: (b, layer, 0, h))
    nq = _Q_GROUP * GRID_W
    nk = _BAND * GRID_W
    return pl.pallas_call(
        functools.partial(_nbr_attn_kernel, rows=rows),
        out_shape=jax.ShapeDtypeStruct((st.batch * seq, ATTN_WIDTH), BF16),
        grid=(st.batch, N_HEADS),
        in_specs=[tok_spec, tok_spec, tok_spec, cache_spec, cache_spec,
                  pl.BlockSpec((None, _BIAS_TILES, GRID_W, 2 * GRID_W), lambda b, h: (h, 0, 0, 0))],
        out_specs=tok_spec,
        scratch_shapes=[pltpu.VMEM((seq, HEAD_DIM), BF16), pltpu.VMEM((seq, HEAD_DIM), BF16),
                        pltpu.VMEM((2, nq, nk), F32), pltpu.VMEM((2, nq, past), F32),
                        pltpu.VMEM((2, nq, nk), BF16), pltpu.VMEM((2, nq, past), BF16)],
        compiler_params=_cparams("parallel", "parallel"),
        name="neighbourhood_attention",
    )(q, k, v, ck, cv, bias)


_CONV_PAD = 8


def _short_conv_kernel(u_ref, w_ref, b_ref, o_ref, pad_ref, *, seq):
    width = u_ref.shape[1]
    zeros = jnp.zeros((_CONV_PAD, width), F32)
    pad_ref[0:_CONV_PAD, :] = zeros
    pad_ref[_CONV_PAD + seq:2 * _CONV_PAD + seq, :] = zeros
    pad_ref[_CONV_PAD:_CONV_PAD + seq, :] = u_ref[...]
    w0 = w_ref[0:1, :]
    w1 = w_ref[1:2, :]
    w2 = w_ref[2:3, :]
    b = b_ref[...]

    rc = max(8, _ROW_CHUNK * SUB_M // width)

    def chunk(ci, carry):
        base = pl.multiple_of(ci * rc, rc)
        window = pad_ref[pl.ds(base, rc + 2 * _CONV_PAD), :]
        prev = window[_CONV_PAD - 1:_CONV_PAD - 1 + rc]
        cur = window[_CONV_PAD:_CONV_PAD + rc]
        nxt = window[_CONV_PAD + 1:_CONV_PAD + 1 + rc]
        o_ref[pl.ds(base, rc), :] = prev * w0 + cur * w1 + nxt * w2 + b
        return carry

    lax.fori_loop(0, seq // rc, chunk, 0)


def _short_conv(u, conv_w, conv_b, layer, st):
    width = u.shape[1]
    seq = st.seq
    tc = HYENA_WIDTH if seq <= SUB_M else SUB_M
    return pl.pallas_call(
        functools.partial(_short_conv_kernel, seq=seq),
        out_shape=jax.ShapeDtypeStruct((st.batch * seq, width), F32),
        grid=(st.batch, width // tc),
        in_specs=[pl.BlockSpec((seq, tc), lambda b, j: (b, j)),
                  pl.BlockSpec((None, 3, tc), lambda b, j: (layer, 0, j)),
                  pl.BlockSpec((None, 1, tc), lambda b, j: (layer, 0, j))],
        out_specs=pl.BlockSpec((seq, tc), lambda b, j: (b, j)),
        scratch_shapes=[pltpu.VMEM((seq + 2 * _CONV_PAD, tc), F32)],
        compiler_params=_cparams("parallel", "parallel"),
        name="hyena_short_conv",
    )(u, conv_w, conv_b.reshape(DEPTH, 1, width))


def _dft_matrices(seq):
    f = jnp.arange(seq, dtype=jnp.int32)[:, None]
    t = jnp.arange(seq, dtype=jnp.int32)[None, :]
    ang = ((f * t) % (2 * seq)).astype(F32) * (math.pi / seq)
    top = jnp.cos(ang)
    bot = jnp.where(f == 0, (1 - 2 * (t % 2)).astype(F32), -jnp.sin(ang))
    colscale = jnp.where(jnp.arange(seq) == 0, 0.5, 1.0).astype(F32)[None, :] / seq
    inv = jnp.concatenate([top.T * colscale, bot.T * colscale], axis=1)
    return top.astype(BF16), bot.astype(BF16), inv.astype(BF16)


def _filter_mlp_kernel(z_ref, w1_ref, b1_ref, w2_ref, b2_ref, fr_ref, o_ref):
    fr = fr_ref[...]
    h = jnp.sin(fr * (jnp.dot(z_ref[...], w1_ref[...], preferred_element_type=F32) + b1_ref[...]))
    o_ref[...] = jnp.sin(fr * (jnp.dot(h, w2_ref[...], preferred_element_type=F32) + b2_ref[...]))


def _filter_mlp(seq, w1, b1, w2, b2, freq, layer):
    t = jnp.linspace(0.0, 1.0, seq, dtype=F32)[:, None]
    bands = (FILTER_EMB - 1) // 2
    ang = (2.0 * math.pi / seq) * jnp.arange(seq, dtype=F32)[:, None] * jnp.linspace(
        1e-4, bands - 1, bands, dtype=F32)[None]
    z = jnp.concatenate([t, jnp.cos(ang), -jnp.sin(ang)], axis=-1)
    z = jnp.pad(z, ((0, 0), (0, LANES - FILTER_EMB)))
    w1p = jnp.pad(w1, ((0, 0), (0, LANES - FILTER_EMB), (0, 0)))
    vec = lambda a: a.reshape(DEPTH, 1, FILTER_HIDDEN)
    vspec = pl.BlockSpec((None, 1, FILTER_HIDDEN), lambda i: (layer, 0, 0))
    return pl.pallas_call(
        _filter_mlp_kernel,
        out_shape=jax.ShapeDtypeStruct((seq, FILTER_HIDDEN), F32),
        grid=(1,),
        in_specs=[pl.BlockSpec((seq, LANES), lambda i: (0, 0)),
                  pl.BlockSpec((None, LANES, FILTER_HIDDEN), lambda i: (layer, 0, 0)),
                  vspec,
                  pl.BlockSpec((None, FILTER_HIDDEN, FILTER_HIDDEN), lambda i: (layer, 0, 0)),
                  vspec, vspec],
        out_specs=pl.BlockSpec((seq, FILTER_HIDDEN), lambda i: (0, 0)),
        compiler_params=_cparams("arbitrary"),
        name="hyena_filter_mlp",
    )(z, w1p, vec(b1), w2, vec(b2), vec(freq))


def _filter_taps_kernel(h_ref, dl_ref, wf_ref, wb_ref, a_ref, d_ref, nyq_ref, *, seq):
    h = h_ref[...]
    shape = (seq, dl_ref.shape[1])
    row = lax.broadcasted_iota(jnp.int32, shape, 0)
    t = row.astype(F32) / float(seq - 1)
    dec = jnp.exp(-t * dl_ref[...])
    fwd = jnp.dot(h, wf_ref[...], preferred_element_type=F32) * dec
    bwd = jnp.dot(h, wb_ref[...], preferred_element_type=F32) * dec
    bwd = jnp.where(row == 0, 0.0, bwd)
    inv = 1.0 / jnp.sum(jnp.abs(fwd) + jnp.abs(bwd), axis=0, keepdims=True)
    a = (fwd + bwd) * inv
    a_ref[...] = a
    d_ref[...] = (fwd - bwd) * inv
    sign = jnp.where(row % 2 == 0, 1.0, -1.0)
    nyq_ref[...] = jnp.sum(a * sign, axis=0, keepdims=True)


def _filter_taps(hidden, w3, layer, *, seq):
    tc = 256
    nct = HYENA_WIDTH // tc
    deltas = jnp.abs(jnp.linspace(math.log(DECAY_TARGET) / SLOW_DECAY_PCT,
                                  math.log(DECAY_TARGET) / FAST_DECAY_PCT, HYENA_WIDTH, dtype=F32))
    tap_shape = jax.ShapeDtypeStruct((HYENA_ORDER, seq, HYENA_WIDTH), F32)
    tap_spec = pl.BlockSpec((None, seq, tc), lambda o, j: (o, 0, j))
    return pl.pallas_call(
        functools.partial(_filter_taps_kernel, seq=seq),
        out_shape=(tap_shape, tap_shape, jax.ShapeDtypeStruct((HYENA_ORDER, 1, HYENA_WIDTH), F32)),
        grid=(HYENA_ORDER, nct),
        in_specs=[pl.BlockSpec((seq, FILTER_HIDDEN), lambda o, j: (0, 0)),
                  pl.BlockSpec((1, tc), lambda o, j: (0, j)),
                  pl.BlockSpec((None, FILTER_HIDDEN, tc), lambda o, j: (layer, 0, (2 * o) * nct + j)),
                  pl.BlockSpec((None, FILTER_HIDDEN, tc), lambda o, j: (layer, 0, (2 * o + 1) * nct + j))],
        out_specs=(tap_spec, tap_spec, pl.BlockSpec((None, 1, tc), lambda o, j: (o, 0, j))),
        compiler_params=_cparams("parallel", "parallel"),
        name="hyena_filter_taps",
    )(hidden, deltas.reshape(1, HYENA_WIDTH), w3, w3)


def _filter_spec_kernel(wt_ref, wb_ref, a_ref, d_ref, nyq_ref, o_ref, abf, dbf):
    i = pl.program_id(2)

    @pl.when(i == 0)
    def _():
        abf[...] = a_ref[...].astype(BF16)
        dbf[...] = d_ref[...].astype(BF16)

    for rows in _row_chunks(wt_ref.shape[0]):
        top = jnp.dot(wt_ref[rows, :], abf[...], preferred_element_type=F32)
        bot = jnp.dot(wb_ref[rows, :], dbf[...], preferred_element_type=F32)
        if rows.start == 0:
            row = lax.broadcasted_iota(jnp.int32, bot.shape, 0)
            bot = jnp.where((row == 0) & (i == 0), nyq_ref[...], bot)
        o_ref[0, rows, :] = top
        o_ref[1, rows, :] = bot


def _dft_tiles(seq):
    return min(seq, 2 * SUB_M), (HYENA_WIDTH if seq <= SUB_M else SUB_M)


def _filter_spectrum(dft_top, dft_bot, taps_a, taps_d, nyq, *, seq):
    tf, tc = _dft_tiles(seq)
    wspec = pl.BlockSpec((tf, seq), lambda o, c, i: (i, 0))
    tspec = pl.BlockSpec((None, seq, tc), lambda o, c, i: (o, 0, c))
    return pl.pallas_call(
        _filter_spec_kernel,
        out_shape=jax.ShapeDtypeStruct((HYENA_ORDER, 2, seq, HYENA_WIDTH), F32),
        grid=(HYENA_ORDER, HYENA_WIDTH // tc, seq // tf),
        in_specs=[wspec, wspec, tspec, tspec,
                  pl.BlockSpec((None, 1, tc), lambda o, c, i: (o, 0, c))],
        out_specs=pl.BlockSpec((None, 2, tf, tc), lambda o, c, i: (o, 0, i, c)),
        scratch_shapes=[pltpu.VMEM((seq, tc), BF16)] * 2,
        compiler_params=_cparams("parallel", "parallel", "arbitrary"),
        name="hyena_filter_spectrum",
    )(dft_top, dft_bot, taps_a, taps_d, nyq)


def _fwd_dft_kernel(wt_ref, wb_ref, z_ref, p_ref, o_ref, zbf):
    i = pl.program_id(2)

    @pl.when(i == 0)
    def _():
        zbf[...] = z_ref[...].astype(BF16)

    for rows in _row_chunks(wt_ref.shape[0]):
        top = jnp.dot(wt_ref[rows, :], zbf[...], preferred_element_type=F32)
        bot = jnp.dot(wb_ref[rows, :], zbf[...], preferred_element_type=F32)
        pt = p_ref[0, rows, :]
        pb = p_ref[1, rows, :]
        re = top * pt - bot * pb
        im = top * pb + bot * pt
        if rows.start == 0:
            row = lax.broadcasted_iota(jnp.int32, top.shape, 0)
            real_row = (row == 0) & (i == 0)
            re = jnp.where(real_row, top * pt, re)
            im = jnp.where(real_row, bot * pb, im)
        o_ref[0, rows, :] = re.astype(o_ref.dtype)
        o_ref[1, rows, :] = im.astype(o_ref.dtype)


def _fwd_dft(dft_top, dft_bot, z, z_col_off, spec, order, *, batch, seq):
    tf, tc = _dft_tiles(seq)
    zc = z_col_off // tc
    wspec = pl.BlockSpec((tf, seq), lambda b, c, i: (i, 0))
    return pl.pallas_call(
        _fwd_dft_kernel,
        out_shape=jax.ShapeDtypeStruct((batch, 2, seq, HYENA_WIDTH), BF16),
        grid=(batch, HYENA_WIDTH // tc, seq // tf),
        in_specs=[wspec, wspec,
                  pl.BlockSpec((seq, tc), lambda b, c, i: (b, zc + c)),
                  pl.BlockSpec((None, 2, tf, tc), lambda b, c, i: (order, 0, i, c))],
        out_specs=pl.BlockSpec((None, 2, tf, tc), lambda b, c, i: (b, 0, i, c)),
        scratch_shapes=[pltpu.VMEM((seq, tc), BF16)],
        compiler_params=_cparams("parallel", "parallel", "arbitrary"),
        name="hyena_fwd_dft",
    )(dft_top, dft_bot, z, spec)


def _inv_dft_kernel(wi_ref, s_ref, gate_ref, z_ref, bias_ref, o_ref):
    s = s_ref[...]
    s2 = s.reshape(s.shape[0] * s.shape[1], s.shape[2])
    for rows in _row_chunks(wi_ref.shape[0]):
        conv = jnp.dot(wi_ref[rows, :], s2, preferred_element_type=F32)
        o_ref[rows, :] = (gate_ref[rows, :] * (conv + bias_ref[...] * z_ref[rows, :])).astype(o_ref.dtype)


def _inv_dft(dft_inv, s, gate, gate_col_off, z, z_col_off, hy_bias, layer, order, out_dtype, *, batch, seq):
    tl, tc = _dft_tiles(seq)
    nrb = seq // tl
    gc = gate_col_off // tc
    zc = z_col_off // tc
    return pl.pallas_call(
        _inv_dft_kernel,
        out_shape=jax.ShapeDtypeStruct((batch * seq, HYENA_WIDTH), out_dtype),
        grid=(batch, HYENA_WIDTH // tc, nrb),
        in_specs=[pl.BlockSpec((tl, 2 * seq), lambda b, c, i: (i, 0)),
                  pl.BlockSpec((None, 2, seq, tc), lambda b, c, i: (b, 0, 0, c)),
                  pl.BlockSpec((tl, tc), lambda b, c, i: (b * nrb + i, gc + c)),
                  pl.BlockSpec((tl, tc), lambda b, c, i: (b * nrb + i, zc + c)),
                  pl.BlockSpec((None, 1, tc), lambda b, c, i: (layer * HYENA_ORDER + order, 0, c))],
        out_specs=pl.BlockSpec((tl, tc), lambda b, c, i: (b * nrb + i, c)),
        compiler_params=_cparams("parallel", "parallel", "arbitrary"),
        name="hyena_inv_dft",
    )(dft_inv, s, gate, z, hy_bias.reshape(DEPTH * HYENA_ORDER, 1, HYENA_WIDTH))


def _hyena(u, p, layer, st):
    batch, seq = st.batch, st.seq
    uc = _short_conv(u, p["hy_conv_w"], p["hy_conv_b"], layer, st)
    dft_top, dft_bot, dft_inv = _dft_matrices(seq)
    hidden = _filter_mlp(seq, p["filt_w1"], p["filt_b1"], p["filt_w2"], p["filt_b2"], p["filt_freq"], layer)
    taps_a, taps_d, nyq = _filter_taps(hidden, p["filt_w3"], layer, seq=seq)
    spec = _filter_spectrum(dft_top, dft_bot, taps_a, taps_d, nyq, seq=seq)
    s = _fwd_dft(dft_top, dft_bot, uc, 0, spec, 0, batch=batch, seq=seq)
    z1 = _inv_dft(dft_inv, s, uc, HYENA_WIDTH, uc, 0, p["hy_bias"], layer, 0, F32, batch=batch, seq=seq)
    s = _fwd_dft(dft_top, dft_bot, z1, 0, spec, 1, batch=batch, seq=seq)
    return _inv_dft(dft_inv, s, uc, 2 * HYENA_WIDTH, z1, 0, p["hy_bias"], layer, 1, BF16, batch=batch, seq=seq)


def _block(x, p, modflat, layer, st, ctx_kv):
    h = _normmod(x, p["norm1_g"], modflat, st, layer, 1, 0)
    lin = functools.partial(_linear, h, p["w_in"], layer, tn=512)
    u_pool = lin(COL_POOL, POOL_WIDTH, kind="plain", out_dtype=F32)
    q = lin(COL_Q, ATTN_WIDTH, kind="rmshead", out_dtype=BF16, head_gain=p["q_norm_g"])
    k = lin(COL_K, ATTN_WIDTH, kind="rmshead", out_dtype=F32, head_gain=p["k_norm_g"])
    v = lin(COL_V, ATTN_WIDTH, kind="plain", out_dtype=F32)
    u_hy = lin(COL_HY, (HYENA_ORDER + 1) * HYENA_WIDTH, kind="plain", out_dtype=F32)
    gates = lin(COL_GATES, 3 * D_MODEL, kind="sigmoid", out_dtype=BF16)

    pool = _pool(u_pool, p["pool_w"], p["pool_scale"], layer, st)
    if ctx_kv is None:
        attn = _ctx_attention(q, k, v, st)
    else:
        attn = _nbr_attention(q, k, v, ctx_kv[0], ctx_kv[1], ctx_kv[2], layer, st)
    hy = _hyena(u_hy, p, layer, st)

    mixed = _mix(pool, attn, hy, p["w_branch_pool"], p["w_branch_attn"], p["w_branch_hyena"], gates, layer)
    x = _linear(mixed, p["w_out"], layer, 0, D_MODEL, tn=512, kind="residual", out_dtype=F32,
                residual=x, modflat=modflat, st=st, which_gate=2)
    h = _normmod(x, p["norm2_g"], modflat, st, layer, 4, 3)
    a = _ffn1(h, p["w_gate"], p["w_up"], layer)
    half = D_FF // 2
    for k_off in (0, half):
        x = _linear(a, p["w_down"], layer, 0, D_MODEL, tn=512, tm=512, kind="residual", out_dtype=F32,
                    k_off=k_off, k_len=half, residual=x, modflat=modflat, st=st, which_gate=5)
    return x, k, v


_WEIGHT_NAMES = ("ada_w", "ada_b", "norm1_g", "norm2_g", "w_in", "pool_w", "pool_scale",
                 "q_norm_g", "k_norm_g", "rpb", "hy_conv_w", "hy_conv_b", "filt_w1", "filt_b1",
                 "filt_w2", "filt_b2", "filt_w3", "filt_freq", "hy_bias", "w_branch_pool",
                 "w_branch_attn", "w_branch_hyena", "w_out", "w_gate", "w_up", "w_down")


def kernel(x_prompt, x_sample, cache_k, cache_v, c, c_ctx, ada_w, ada_b, norm1_g, norm2_g, w_in, pool_w, pool_scale, q_norm_g, k_norm_g, rpb, hy_conv_w, hy_conv_b, filt_w1, filt_b1, filt_w2, filt_b2, filt_w3, filt_freq, hy_bias, w_branch_pool, w_branch_attn, w_branch_hyena, w_out, w_gate, w_up, w_down):
    p = dict(zip(_WEIGHT_NAMES, (ada_w, ada_b, norm1_g, norm2_g, w_in, pool_w, pool_scale, q_norm_g,
                                 k_norm_g, rpb, hy_conv_w, hy_conv_b, filt_w1, filt_b1, filt_w2, filt_b2,
                                 filt_w3, filt_freq, hy_bias, w_branch_pool, w_branch_attn,
                                 w_branch_hyena, w_out, w_gate, w_up, w_down)))
    nb, ns, d = x_prompt.shape
    lb, ls, _ = x_sample.shape
    assert d == D_MODEL and lb + 1 <= MOD_ROWS
    ctx = _Stream(nb, ns, 0, False)
    lat = _Stream(lb, ls, 1, True)

    cmat = jnp.concatenate([c_ctx[None, :], c, jnp.zeros((MOD_ROWS - 1 - lb, d), F32)], axis=0)
    mod = _modulation(cmat, ada_w, ada_b)
    modflat = mod.reshape(DEPTH * MOD_ROWS * 6, 1, d)

    y_ctx = x_prompt.reshape(nb * ns, d)
    y_lat = x_sample.reshape(lb * ls, d)
    new_k, new_v = [], []
    for layer in range(DEPTH):
        y_ctx, k_l, v_l = _block(y_ctx, p, modflat, layer, ctx, None)
        new_k.append(k_l.reshape(nb, ns, N_HEADS, HEAD_DIM))
        new_v.append(v_l.reshape(nb, ns, N_HEADS, HEAD_DIM))
        bias = _bias_table(rpb, layer)
        y_lat, _, _ = _block(y_lat, p, modflat, layer, lat, (cache_k, cache_v, bias))

    return (y_ctx.reshape(nb, ns, d), y_lat.reshape(lb, ls, d),
            jnp.stack(new_k, axis=1), jnp.stack(new_v, axis=1))
```

```python
import functools
import math

import jax
import jax.numpy as jnp
from jax import lax
from jax.experimental import pallas as pl
from jax.experimental.pallas import tpu as pltpu

F32 = jnp.float32
BF16 = jnp.bfloat16

D_MODEL = 4096
DEPTH = 2
GRID_W = 64
HEAD_DIM = 128
N_HEADS = D_MODEL // 256
ATTN_WIDTH = N_HEADS * HEAD_DIM
ATTN_SCALE = HEAD_DIM ** -0.5
WIN_R = 8
WIN_C = 16
NEG_INF = -1e30
POOL_WIDTH = D_MODEL // 2
POOL_SIZES = (2, 4, 8, 16)
POOL_GROUP = POOL_WIDTH // len(POOL_SIZES)
HYENA_WIDTH = D_MODEL // 2
HYENA_ORDER = 2
FILTER_EMB = 33
FILTER_HIDDEN = 64
DECAY_TARGET = 1e-2
FAST_DECAY_PCT = 0.3
SLOW_DECAY_PCT = 1.5
D_FF = ((8 * D_MODEL + 767) // 768) * 256
COL_POOL = 0
COL_Q = POOL_WIDTH
COL_K = COL_Q + ATTN_WIDTH
COL_V = COL_K + ATTN_WIDTH
COL_HY = COL_V + ATTN_WIDTH
COL_GATES = COL_HY + (HYENA_ORDER + 1) * HYENA_WIDTH
MOD_ROWS = 8
RMS_EPS = 1e-6

LANES = 128
VMEM_LIMIT = 56 * 1024 * 1024
SUB_M = 256
CH_TILE = 512
SHORT_SEQ = 512


def _cparams(*sem):
    return pltpu.CompilerParams(dimension_semantics=sem, vmem_limit_bytes=VMEM_LIMIT)


def _row_chunks(tm):
    return [slice(r, r + min(SUB_M, tm)) for r in range(0, tm, min(SUB_M, tm))]


def _token_tile(m_tok):
    return 1024 if m_tok % 1024 == 0 else 512


def _mod_kernel(c_ref, w_ref, b_ref, o_ref):
    c = c_ref[...]
    s = (c * jax.nn.sigmoid(c)).astype(BF16)
    o_ref[...] = jnp.dot(s, w_ref[...].astype(BF16), preferred_element_type=F32) + b_ref[...]


def _modulation(cmat, ada_w, ada_b):
    tn = 512
    n6 = ada_w.shape[-1]
    return pl.pallas_call(
        _mod_kernel,
        out_shape=jax.ShapeDtypeStruct((DEPTH, MOD_ROWS, n6), F32),
        grid=(DEPTH, n6 // tn),
        in_specs=[pl.BlockSpec((MOD_ROWS, D_MODEL), lambda l, j: (0, 0)),
                  pl.BlockSpec((None, D_MODEL, tn), lambda l, j: (l, 0, j)),
                  pl.BlockSpec((None, 1, tn), lambda l, j: (l, 0, j))],
        out_specs=pl.BlockSpec((None, MOD_ROWS, tn), lambda l, j: (l, 0, j)),
        compiler_params=_cparams("parallel", "parallel"),
        name="modulation",
    )(cmat, ada_w, ada_b.reshape(DEPTH, 1, n6))


class _Stream:
    def __init__(self, batch, seq, mod_base, per_batch_mod):
        self.batch = batch
        self.seq = seq
        self.mod_base = mod_base
        self.per_batch_mod = per_batch_mod

    def mod_index(self, layer, i, tm, which):
        row = self.mod_base + ((i * tm) // self.seq if self.per_batch_mod else 0)
        return (layer * MOD_ROWS + row) * 6 + which


_STAT_ROWS = 32
_NORM_ROWS = 16


def _normmod_kernel(x_ref, g_ref, sc_ref, sh_ref, o_ref, rstd_ref, gain_ref, shift_ref):
    tm, d = x_ref.shape
    gain_ref[...] = jnp.broadcast_to(g_ref[...] * (1.0 + sc_ref[0]), (_NORM_ROWS, d))
    shift_ref[...] = jnp.broadcast_to(sh_ref[0], (_NORM_ROWS, d))

    def stats(ci, carry):
        base = pl.multiple_of(ci * _STAT_ROWS, _STAT_ROWS)
        x = x_ref[pl.ds(base, _STAT_ROWS), :]
        sq = x * x
        while sq.shape[1] > LANES:
            half = sq.shape[1] // 2
            sq = sq[:, :half] + sq[:, half:]
        rstd_ref[pl.ds(base, _STAT_ROWS), :] = sq
        return carry

    lax.fori_loop(0, tm // _STAT_ROWS, stats, 0)
    rstd = lax.rsqrt(jnp.sum(rstd_ref[...], axis=-1, keepdims=True) * (1.0 / d) + RMS_EPS)
    rstd_ref[...] = jnp.broadcast_to(rstd, (tm, LANES))

    def scale(ci, carry):
        base = pl.multiple_of(ci * _NORM_ROWS, _NORM_ROWS)
        x = x_ref[pl.ds(base, _NORM_ROWS), :]
        rstd = pltpu.repeat(rstd_ref[pl.ds(base, _NORM_ROWS), :], d // LANES, axis=1)
        o_ref[pl.ds(base, _NORM_ROWS), :] = (x * rstd * gain_ref[...] + shift_ref[...]).astype(o_ref.dtype)
        return carry

    lax.fori_loop(0, tm // _NORM_ROWS, scale, 0)


def _normmod(x, g, modflat, st, layer, which_scale, which_shift):
    m_tok, d = x.shape
    tm = 256
    return pl.pallas_call(
        _normmod_kernel,
        out_shape=jax.ShapeDtypeStruct((m_tok, d), BF16),
        grid=(m_tok // tm,),
        in_specs=[pl.BlockSpec((tm, d), lambda i: (i, 0)),
                  pl.BlockSpec((None, 1, d), lambda i: (layer, 0, 0)),
                  pl.BlockSpec((1, 1, d), lambda i: (st.mod_index(layer, i, tm, which_scale), 0, 0)),
                  pl.BlockSpec((1, 1, d), lambda i: (st.mod_index(layer, i, tm, which_shift), 0, 0))],
        out_specs=pl.BlockSpec((tm, d), lambda i: (i, 0)),
        scratch_shapes=[pltpu.VMEM((tm, LANES), F32), pltpu.VMEM((_NORM_ROWS, d), F32),
                        pltpu.VMEM((_NORM_ROWS, d), F32)],
        compiler_params=_cparams("parallel"),
        name="normmod",
    )(x, g.reshape(DEPTH, 1, d), modflat, modflat)


def _cast_weight_once(w_ref, wbf_ref):
    @pl.when(pl.program_id(1) == 0)
    def _():
        wbf_ref[...] = w_ref[...].astype(BF16)


def _lin_plain_kernel(x_ref, w_ref, o_ref, wbf_ref):
    _cast_weight_once(w_ref, wbf_ref)
    for rows in _row_chunks(x_ref.shape[0]):
        acc = jnp.dot(x_ref[rows, :], wbf_ref[...], preferred_element_type=F32)
        o_ref[rows, :] = acc.astype(o_ref.dtype)


def _lin_sigmoid_kernel(x_ref, w_ref, o_ref, wbf_ref):
    _cast_weight_once(w_ref, wbf_ref)
    for rows in _row_chunks(x_ref.shape[0]):
        acc = jnp.dot(x_ref[rows, :], wbf_ref[...], preferred_element_type=F32)
        o_ref[rows, :] = jax.nn.sigmoid(acc).astype(o_ref.dtype)


def _lin_rmshead_kernel(x_ref, w_ref, g_ref, o_ref, wbf_ref):
    _cast_weight_once(w_ref, wbf_ref)
    g = g_ref[...]
    for rows in _row_chunks(x_ref.shape[0]):
        acc = jnp.dot(x_ref[rows, :], wbf_ref[...], preferred_element_type=F32)
        for h in range(acc.shape[1] // HEAD_DIM):
            a = acc[:, h * HEAD_DIM:(h + 1) * HEAD_DIM]
            y = a * lax.rsqrt(jnp.mean(a * a, axis=-1, keepdims=True) + RMS_EPS)
            o_ref[rows, h * HEAD_DIM:(h + 1) * HEAD_DIM] = (y * g).astype(o_ref.dtype)


def _lin_residual_kernel(x_ref, w_ref, res_ref, gate_ref, o_ref, wbf_ref):
    _cast_weight_once(w_ref, wbf_ref)
    for rows in _row_chunks(x_ref.shape[0]):
        acc = jnp.dot(x_ref[rows, :], wbf_ref[...], preferred_element_type=F32)
        o_ref[rows, :] = res_ref[rows, :] + gate_ref[0] * acc


def _linear(x, w, layer, col_off, ncols, *, tn, kind, out_dtype, tm=None, k_off=0, k_len=None,
            head_gain=None, residual=None, modflat=None, st=None, which_gate=None):
    m_tok = x.shape[0]
    k_len = x.shape[1] if k_len is None else k_len
    tm = _token_tile(m_tok) if tm is None else tm
    cb = col_off // tn
    kb = k_off // k_len
    in_specs = [pl.BlockSpec((tm, k_len), lambda n, m: (m, kb)),
                pl.BlockSpec((None, k_len, tn), lambda n, m: (layer, kb, cb + n))]
    args = [x, w]
    if kind == "plain":
        body = _lin_plain_kernel
    elif kind == "sigmoid":
        body = _lin_sigmoid_kernel
    elif kind == "rmshead":
        body = _lin_rmshead_kernel
        in_specs.append(pl.BlockSpec((None, 1, HEAD_DIM), lambda n, m: (layer, 0, 0)))
        args.append(head_gain.reshape(DEPTH, 1, HEAD_DIM))
    else:
        body = _lin_residual_kernel
        in_specs.append(pl.BlockSpec((tm, tn), lambda n, m: (m, n)))
        in_specs.append(pl.BlockSpec(
            (1, 1, tn), lambda n, m: (st.mod_index(layer, m, tm, which_gate), 0, n)))
        args += [residual, modflat]
    return pl.pallas_call(
        body,
        out_shape=jax.ShapeDtypeStruct((m_tok, ncols), out_dtype),
        grid=(ncols // tn, m_tok // tm),
        in_specs=in_specs,
        out_specs=pl.BlockSpec((tm, tn), lambda n, m: (m, n)),
        scratch_shapes=[pltpu.VMEM((k_len, tn), BF16)],
        compiler_params=_cparams("parallel", "arbitrary"),
        name="linear_" + kind,
    )(*args)


def _mix_kernel(xp_ref, xa_ref, xh_ref, wp_ref, wa_ref, wh_ref, gp_ref, ga_ref, gh_ref, o_ref,
                wpb, wab, whb):
    @pl.when(pl.program_id(1) == 0)
    def _():
        wpb[...] = wp_ref[...].astype(BF16)
        wab[...] = wa_ref[...].astype(BF16)
        whb[...] = wh_ref[...].astype(BF16)

    acc = gp_ref[...].astype(F32) * jnp.dot(xp_ref[...], wpb[...], preferred_element_type=F32)
    acc = acc + ga_ref[...].astype(F32) * jnp.dot(xa_ref[...], wab[...], preferred_element_type=F32)
    acc = acc + gh_ref[...].astype(F32) * jnp.dot(xh_ref[...], whb[...], preferred_element_type=F32)
    o_ref[...] = acc.astype(o_ref.dtype)


def _mix(xp, xa, xh, wp, wa, wh, gates, layer):
    m_tok, k = xp.shape
    n = wp.shape[-1]
    tm, tn = 512, 512
    nb = n // tn
    xspec = pl.BlockSpec((tm, k), lambda j, i: (i, 0))
    wspec = pl.BlockSpec((None, k, tn), lambda j, i: (layer, 0, j))
    return pl.pallas_call(
        _mix_kernel,
        out_shape=jax.ShapeDtypeStruct((m_tok, n), BF16),
        grid=(nb, m_tok // tm),
        in_specs=[xspec, xspec, xspec, wspec, wspec, wspec,
                  pl.BlockSpec((tm, tn), lambda j, i: (i, j)),
                  pl.BlockSpec((tm, tn), lambda j, i: (i, nb + j)),
                  pl.BlockSpec((tm, tn), lambda j, i: (i, 2 * nb + j))],
        out_specs=pl.BlockSpec((tm, tn), lambda j, i: (i, j)),
        scratch_shapes=[pltpu.VMEM((k, tn), BF16)] * 3,
        compiler_params=_cparams("parallel", "arbitrary"),
        name="mix",
    )(xp, xa, xh, wp, wa, wh, gates, gates, gates)


def _ffn1_kernel(x_ref, wg_ref, wu_ref, o_ref, wgb, wub):
    @pl.when(pl.program_id(1) == 0)
    def _():
        wgb[...] = wg_ref[...].astype(BF16)
        wub[...] = wu_ref[...].astype(BF16)

    for rows in _row_chunks(x_ref.shape[0]):
        x = x_ref[rows, :]
        g = jnp.dot(x, wgb[...], preferred_element_type=F32)
        u = jnp.dot(x, wub[...], preferred_element_type=F32)
        o_ref[rows, :] = (g * jax.nn.sigmoid(g) * u).astype(o_ref.dtype)


def _ffn1(x, wg, wu, layer):
    m_tok, k = x.shape
    n = wg.shape[-1]
    tm, tn = _token_tile(m_tok), 256
    wspec = pl.BlockSpec((None, k, tn), lambda j, i: (layer, 0, j))
    return pl.pallas_call(
        _ffn1_kernel,
        out_shape=jax.ShapeDtypeStruct((m_tok, n), BF16),
        grid=(n // tn, m_tok // tm),
        in_specs=[pl.BlockSpec((tm, k), lambda j, i: (i, 0)), wspec, wspec],
        out_specs=pl.BlockSpec((tm, tn), lambda j, i: (i, j)),
        scratch_shapes=[pltpu.VMEM((k, tn), BF16)] * 2,
        compiler_params=_cparams("parallel", "arbitrary"),
        name="ffn_gate_up",
    )(x, wg, wu)


_POOL_PAD = 16
_ROW_CHUNK = 64


def _halo_chunks(u_ref, seq, rc, pad, emit):
    zeros = jnp.zeros((pad, u_ref.shape[1]), F32)
    emit(0, jnp.concatenate([zeros, u_ref[0:rc + pad, :]], axis=0))

    def chunk(ci, carry):
        base = pl.multiple_of(ci * rc, rc)
        emit(base, u_ref[pl.ds(pl.multiple_of(base - pad, pad), rc + 2 * pad), :])
        return carry

    lax.fori_loop(1, seq // rc - 1, chunk, 0)
    emit(seq - rc, jnp.concatenate([u_ref[seq - rc - pad:seq, :], zeros], axis=0))


def _pool_kernel(u_ref, w_ref, sc_ref, o_ref, d_ref, *, seq):
    g = pl.program_id(1)
    width = u_ref.shape[1]
    rc = _ROW_CHUNK

    for gi, win in enumerate(POOL_SIZES):
        half = win // 2

        @pl.when(g == gi)
        def _(win=win, half=half):
            def emit(base, window):
                lo = _POOL_PAD - half
                acc = window[lo:lo + rc]
                for j in range(1, win):
                    acc = acc + window[lo + j:lo + j + rc]
                t = base + lax.broadcasted_iota(jnp.int32, (rc, width), 0)
                cnt = (jnp.minimum(t + (win - half), seq) - jnp.maximum(t - half, 0)).astype(F32)
                centre = window[_POOL_PAD:_POOL_PAD + rc]
                d_ref[pl.ds(base, rc), :] = (acc / cnt - centre).astype(BF16)

            _halo_chunks(u_ref, seq, rc, _POOL_PAD, emit)

    wbf = w_ref[...].astype(BF16)
    scale = sc_ref[...]
    for rows in _row_chunks(seq):
        y = jnp.dot(d_ref[rows, :], wbf, preferred_element_type=F32) * scale
        o_ref[rows, :] = y.astype(o_ref.dtype)


def _pool(u, pool_w, pool_scale, layer, st):
    seq = st.seq
    return pl.pallas_call(
        functools.partial(_pool_kernel, seq=seq),
        out_shape=jax.ShapeDtypeStruct((st.batch * seq, POOL_WIDTH), BF16),
        grid=(st.batch, len(POOL_SIZES)),
        in_specs=[pl.BlockSpec((seq, POOL_GROUP), lambda b, g: (b, g)),
                  pl.BlockSpec((None, None, POOL_GROUP, POOL_GROUP), lambda b, g: (layer, g, 0, 0)),
                  pl.BlockSpec((None, 1, POOL_GROUP), lambda b, g: (layer, 0, g))],
        out_specs=pl.BlockSpec((seq, POOL_GROUP), lambda b, g: (b, g)),
        scratch_shapes=[pltpu.VMEM((seq, POOL_GROUP), BF16)],
        compiler_params=_cparams("parallel", "parallel"),
        name="pool_mixer",
    )(u, pool_w, pool_scale.reshape(DEPTH, 1, POOL_WIDTH))


def _ctx_attn_kernel(q_ref, k_ref, v_ref, o_ref):
    for h in range(N_HEADS):
        sl = slice(h * HEAD_DIM, (h + 1) * HEAD_DIM)
        q = q_ref[:, sl]
        k = k_ref[:, sl].astype(BF16)
        v = v_ref[:, sl].astype(BF16)
        s = lax.dot_general(q, k, (((1,), (1,)), ((), ())), preferred_element_type=F32) * ATTN_SCALE
        m = jnp.max(s, axis=-1, keepdims=True)
        p = jnp.exp(s - m)
        denom = jnp.sum(p, axis=-1, keepdims=True)
        o = jnp.dot(p.astype(BF16), v, preferred_element_type=F32) / denom
        o_ref[:, sl] = o.astype(o_ref.dtype)


def _ctx_attention(q, k, v, st):
    spec = pl.BlockSpec((st.seq, ATTN_WIDTH), lambda b: (b, 0))
    return pl.pallas_call(
        _ctx_attn_kernel,
        out_shape=jax.ShapeDtypeStruct((st.batch * st.seq, ATTN_WIDTH), BF16),
        grid=(st.batch,),
        in_specs=[spec, spec, spec],
        out_specs=spec,
        compiler_params=_cparams("parallel"),
        name="context_attention",
    )(q, k, v)


_BIAS_TILES = 2 * WIN_R
_Q_GROUP = 8
_BAND = 2 * WIN_R


def _bias_table_kernel(rpb_ref, o_ref, *, layer):
    h = pl.program_id(0)
    shape = (GRID_W, 2 * GRID_W)
    qc = lax.broadcasted_iota(jnp.int32, shape, 0)
    lane = lax.broadcasted_iota(jnp.int32, shape, 1)
    kc = lane % GRID_W
    upper = lane >= GRID_W
    c0 = jnp.clip(qc - WIN_C // 2, 0, GRID_W - WIN_C)
    in_window = (kc >= c0) & (kc < c0 + WIN_C)
    rel = kc - qc + (WIN_C - 1)
    n_dc = 2 * WIN_C - 1
    n_dr = 2 * WIN_R - 1
    row_base = (layer * N_HEADS + h) * n_dr
    for e in range(_BIAS_TILES):
        d_lo = max(e - 1, 0)
        d_hi = min(e, n_dr - 1)
        tile = jnp.zeros(shape, F32)
        for dc in range(n_dc):
            lo = rpb_ref[(row_base + d_lo) * n_dc + dc]
            hi = rpb_ref[(row_base + d_hi) * n_dc + dc]
            tile = jnp.where(rel == dc, jnp.where(upper, hi, lo), tile)
        valid = in_window
        if e == 0:
            valid = valid & upper
        if e == _BIAS_TILES - 1:
            valid = valid & jnp.logical_not(upper)
        o_ref[e] = jnp.where(valid, tile, NEG_INF)


def _bias_table(rpb, layer):
    return pl.pallas_call(
        functools.partial(_bias_table_kernel, layer=layer),
        out_shape=jax.ShapeDtypeStruct((N_HEADS, _BIAS_TILES, GRID_W, 2 * GRID_W), F32),
        grid=(N_HEADS,),
        in_specs=[pl.BlockSpec(memory_space=pltpu.SMEM)],
        out_specs=pl.BlockSpec((None, _BIAS_TILES, GRID_W, 2 * GRID_W), lambda h: (h, 0, 0, 0)),
        compiler_params=_cparams("parallel"),
        name="rel_pos_bias_table",
    )(rpb.reshape(-1))


def _nbr_attn_kernel(q_ref, k_ref, v_ref, kc_ref, vc_ref, bias_ref, o_ref,
                     kbf, vbf, s_ref, sc_ref, p_ref, pc_ref, *, rows):
    kbf[...] = k_ref[...].astype(BF16)
    vbf[...] = v_ref[...].astype(BF16)
    kc = kc_ref[...].astype(BF16)
    vc = vc_ref[...].astype(BF16)
    dims = (((1,), (1,)), ((), ()))
    pair = 2 * GRID_W
    lower = lax.broadcasted_iota(jnp.int32, (GRID_W, pair), 1) < GRID_W
    zero_tile = jnp.zeros((GRID_W, pair), BF16)

    for g in range(rows // _Q_GROUP):
        slot = g % 2
        ks = min(max(g * _Q_GROUP - WIN_R // 2, 0), rows - _BAND)
        qsl = slice(g * _Q_GROUP * GRID_W, (g + 1) * _Q_GROUP * GRID_W)
        ksl = slice(ks * GRID_W, (ks + _BAND) * GRID_W)
        qg = q_ref[qsl, :]
        s_ref[slot] = lax.dot_general(qg, kbf[ksl, :], dims, preferred_element_type=F32) * ATTN_SCALE
        sc_ref[slot] = lax.dot_general(qg, kc, dims, preferred_element_type=F32) * ATTN_SCALE
        for i in range(_Q_GROUP):
            qr = g * _Q_GROUP + i
            r0 = min(max(qr - WIN_R // 2, 0), rows - WIN_R)
            first = r0 - ks
            j0, j1 = first // 2, (first + WIN_R + 1) // 2
            rsl = slice(i * GRID_W, (i + 1) * GRID_W)
            tiles = []
            for j in range(j0, j1):
                kr0 = ks + 2 * j
                t = s_ref[slot, rsl, j * pair:(j + 1) * pair] + bias_ref[kr0 - qr + WIN_R]
                if kr0 < r0:
                    t = jnp.where(lower, NEG_INF, t)
                if kr0 + 1 >= r0 + WIN_R:
                    t = jnp.where(lower, t, NEG_INF)
                tiles.append(t)
            sc = sc_ref[slot, rsl, :]
            mt = functools.reduce(jnp.maximum, tiles)
            m = jnp.maximum(jnp.max(mt, axis=-1, keepdims=True), jnp.max(sc, axis=-1, keepdims=True))
            ps = [jnp.exp(t - m) for t in tiles]
            pc = jnp.exp(sc - m)
            denom = (jnp.sum(functools.reduce(jnp.add, ps), axis=-1, keepdims=True)
                     + jnp.sum(pc, axis=-1, keepdims=True))
            inv = 1.0 / denom
            for j in range(_BAND // 2):
                val = (ps[j - j0] * inv).astype(BF16) if j0 <= j < j1 else zero_tile
                p_ref[slot, rsl, j * pair:(j + 1) * pair] = val
            pc_ref[slot, rsl, :] = (pc * inv).astype(BF16)
        o = (jnp.dot(p_ref[slot], vbf[ksl, :], preferred_element_type=F32)
             + jnp.dot(pc_ref[slot], vc, preferred_element_type=F32))
        o_ref[qsl, :] = o.astype(o_ref.dtype)


def _nbr_attention(q, k, v, cache_k, cache_v, bias, layer, st):
    seq = st.seq
    rows = seq // GRID_W
    assert rows >= _BAND and rows % _Q_GROUP == 0
    past = cache_k.shape[2]
    ck = cache_k.reshape(st.batch, DEPTH, past, ATTN_WIDTH)
    cv = cache_v.reshape(st.batch, DEPTH, past, ATTN_WIDTH)
    tok_spec = pl.BlockSpec((seq, HEAD_DIM), lambda b, h: (b, h))
    cache_spec = pl.BlockSpec((None, None, past, HEAD_DIM), lambda b, h: (b, layer, 0, h))
    nq = _Q_GROUP * GRID_W
    nk = _BAND * GRID_W
    return pl.pallas_call(
        functools.partial(_nbr_attn_kernel, rows=rows),
        out_shape=jax.ShapeDtypeStruct((st.batch * seq, ATTN_WIDTH), BF16),
        grid=(st.batch, N_HEADS),
        in_specs=[tok_spec, tok_spec, tok_spec, cache_spec, cache_spec,
                  pl.BlockSpec((None, _BIAS_TILES, GRID_W, 2 * GRID_W), lambda b, h: (h, 0, 0, 0))],
        out_specs=tok_spec,
        scratch_shapes=[pltpu.VMEM((seq, HEAD_DIM), BF16), pltpu.VMEM((seq, HEAD_DIM), BF16),
                        pltpu.VMEM((2, nq, nk), F32), pltpu.VMEM((2, nq, past), F32),
                        pltpu.VMEM((2, nq, nk), BF16), pltpu.VMEM((2, nq, past), BF16)],
        compiler_params=_cparams("parallel", "parallel"),
        name="neighbourhood_attention",
    )(q, k, v, ck, cv, bias)


_CONV_PAD = 8


def _short_conv_kernel(u_ref, w_ref, b_ref, o_ref, *, seq):
    width = u_ref.shape[1]
    w0 = w_ref[0:1, :]
    w1 = w_ref[1:2, :]
    w2 = w_ref[2:3, :]
    b = b_ref[...]
    rc = max(8, _ROW_CHUNK * CH_TILE // width)

    def emit(base, window):
        prev = window[_CONV_PAD - 1:_CONV_PAD - 1 + rc]
        cur = window[_CONV_PAD:_CONV_PAD + rc]
        nxt = window[_CONV_PAD + 1:_CONV_PAD + 1 + rc]
        o_ref[pl.ds(base, rc), :] = prev * w0 + cur * w1 + nxt * w2 + b

    _halo_chunks(u_ref, seq, rc, _CONV_PAD, emit)


def _short_conv(u, conv_w, conv_b, layer, st):
    width = u.shape[1]
    seq = st.seq
    tc = HYENA_WIDTH if seq <= SHORT_SEQ else CH_TILE
    return pl.pallas_call(
        functools.partial(_short_conv_kernel, seq=seq),
        out_shape=jax.ShapeDtypeStruct((st.batch * seq, width), F32),
        grid=(st.batch, width // tc),
        in_specs=[pl.BlockSpec((seq, tc), lambda b, j: (b, j)),
                  pl.BlockSpec((None, 3, tc), lambda b, j: (layer, 0, j)),
                  pl.BlockSpec((None, 1, tc), lambda b, j: (layer, 0, j))],
        out_specs=pl.BlockSpec((seq, tc), lambda b, j: (b, j)),
        compiler_params=_cparams("parallel", "parallel"),
        name="hyena_short_conv",
    )(u, conv_w, conv_b.reshape(DEPTH, 1, width))


def _dft_matrices(seq):
    f = jnp.arange(seq, dtype=jnp.int32)[:, None]
    t = jnp.arange(seq, dtype=jnp.int32)[None, :]
    ang = ((f * t) % (2 * seq)).astype(F32) * (math.pi / seq)
    top = jnp.cos(ang)
    bot = jnp.where(f == 0, (1 - 2 * (t % 2)).astype(F32), -jnp.sin(ang))
    colscale = jnp.where(jnp.arange(seq) == 0, 0.5, 1.0).astype(F32)[None, :] / seq
    inv = jnp.concatenate([top.T * colscale, bot.T * colscale], axis=1)
    return top.astype(BF16), bot.astype(BF16), inv.astype(BF16)


def _filter_mlp_kernel(z_ref, w1_ref, b1_ref, w2_ref, b2_ref, fr_ref, o_ref):
    fr = fr_ref[...]
    h = jnp.sin(fr * (jnp.dot(z_ref[...], w1_ref[...], preferred_element_type=F32) + b1_ref[...]))
    o_ref[...] = jnp.sin(fr * (jnp.dot(h, w2_ref[...], preferred_element_type=F32) + b2_ref[...]))


def _filter_mlp(seq, w1, b1, w2, b2, freq, layer):
    t = jnp.linspace(0.0, 1.0, seq, dtype=F32)[:, None]
    bands = (FILTER_EMB - 1) // 2
    ang = (2.0 * math.pi / seq) * jnp.arange(seq, dtype=F32)[:, None] * jnp.linspace(
        1e-4, bands - 1, bands, dtype=F32)[None]
    z = jnp.concatenate([t, jnp.cos(ang), -jnp.sin(ang)], axis=-1)
    z = jnp.pad(z, ((0, 0), (0, LANES - FILTER_EMB)))
    w1p = jnp.pad(w1, ((0, 0), (0, LANES - FILTER_EMB), (0, 0)))
    vec = lambda a: a.reshape(DEPTH, 1, FILTER_HIDDEN)
    vspec = pl.BlockSpec((None, 1, FILTER_HIDDEN), lambda i: (layer, 0, 0))
    return pl.pallas_call(
        _filter_mlp_kernel,
        out_shape=jax.ShapeDtypeStruct((seq, FILTER_HIDDEN), F32),
        grid=(1,),
        in_specs=[pl.BlockSpec((seq, LANES), lambda i: (0, 0)),
                  pl.BlockSpec((None, LANES, FILTER_HIDDEN), lambda i: (layer, 0, 0)),
                  vspec,
                  pl.BlockSpec((None, FILTER_HIDDEN, FILTER_HIDDEN), lambda i: (layer, 0, 0)),
                  vspec, vspec],
        out_specs=pl.BlockSpec((seq, FILTER_HIDDEN), lambda i: (0, 0)),
        compiler_params=_cparams("arbitrary"),
        name="hyena_filter_mlp",
    )(z, w1p, vec(b1), w2, vec(b2), vec(freq))


def _filter_taps_kernel(h_ref, dl_ref, wf_ref, wb_ref, a_ref, d_ref, nyq_ref, *, seq):
    h = h_ref[...]
    shape = (seq, dl_ref.shape[1])
    row = lax.broadcasted_iota(jnp.int32, shape, 0)
    t = row.astype(F32) / float(seq - 1)
    dec = jnp.exp(-t * dl_ref[...])
    fwd = jnp.dot(h, wf_ref[...], preferred_element_type=F32) * dec
    bwd = jnp.dot(h, wb_ref[...], preferred_element_type=F32) * dec
    bwd = jnp.where(row == 0, 0.0, bwd)
    inv = 1.0 / jnp.sum(jnp.abs(fwd) + jnp.abs(bwd), axis=0, keepdims=True)
    a = (fwd + bwd) * inv
    a_ref[...] = a
    d_ref[...] = (fwd - bwd) * inv
    sign = jnp.where(row % 2 == 0, 1.0, -1.0)
    nyq_ref[...] = jnp.sum(a * sign, axis=0, keepdims=True)


def _filter_taps(hidden, w3, layer, *, seq):
    tc = 256
    nct = HYENA_WIDTH // tc
    deltas = jnp.abs(jnp.linspace(math.log(DECAY_TARGET) / SLOW_DECAY_PCT,
                                  math.log(DECAY_TARGET) / FAST_DECAY_PCT, HYENA_WIDTH, dtype=F32))
    tap_shape = jax.ShapeDtypeStruct((HYENA_ORDER, seq, HYENA_WIDTH), F32)
    tap_spec = pl.BlockSpec((None, seq, tc), lambda o, j: (o, 0, j))
    return pl.pallas_call(
        functools.partial(_filter_taps_kernel, seq=seq),
        out_shape=(tap_shape, tap_shape, jax.ShapeDtypeStruct((HYENA_ORDER, 1, HYENA_WIDTH), F32)),
        grid=(HYENA_ORDER, nct),
        in_specs=[pl.BlockSpec((seq, FILTER_HIDDEN), lambda o, j: (0, 0)),
                  pl.BlockSpec((1, tc), lambda o, j: (0, j)),
                  pl.BlockSpec((None, FILTER_HIDDEN, tc), lambda o, j: (layer, 0, (2 * o) * nct + j)),
                  pl.BlockSpec((None, FILTER_HIDDEN, tc), lambda o, j: (layer, 0, (2 * o + 1) * nct + j))],
        out_specs=(tap_spec, tap_spec, pl.BlockSpec((None, 1, tc), lambda o, j: (o, 0, j))),
        compiler_params=_cparams("parallel", "parallel"),
        name="hyena_filter_taps",
    )(hidden, deltas.reshape(1, HYENA_WIDTH), w3, w3)


def _filter_spec_kernel(wt_ref, wb_ref, a_ref, d_ref, nyq_ref, o_ref, abf, dbf):
    i = pl.program_id(2)

    @pl.when(i == 0)
    def _():
        abf[...] = a_ref[...].astype(BF16)
        dbf[...] = d_ref[...].astype(BF16)

    for rows in _row_chunks(wt_ref.shape[0]):
        top = jnp.dot(wt_ref[rows, :], abf[...], preferred_element_type=F32)
        bot = jnp.dot(wb_ref[rows, :], dbf[...], preferred_element_type=F32)
        if rows.start == 0:
            row = lax.broadcasted_iota(jnp.int32, bot.shape, 0)
            bot = jnp.where((row == 0) & (i == 0), nyq_ref[...], bot)
        o_ref[0, rows, :] = top
        o_ref[1, rows, :] = bot


def _dft_tiles(seq):
    return min(seq, 2 * CH_TILE), (HYENA_WIDTH if seq <= SHORT_SEQ else CH_TILE)


def _filter_spectrum(dft_top, dft_bot, taps_a, taps_d, nyq, *, seq):
    tf, tc = _dft_tiles(seq)
    wspec = pl.BlockSpec((tf, seq), lambda o, c, i: (i, 0))
    tspec = pl.BlockSpec((None, seq, tc), lambda o, c, i: (o, 0, c))
    return pl.pallas_call(
        _filter_spec_kernel,
        out_shape=jax.ShapeDtypeStruct((HYENA_ORDER, 2, seq, HYENA_WIDTH), F32),
        grid=(HYENA_ORDER, HYENA_WIDTH // tc, seq // tf),
        in_specs=[wspec, wspec, tspec, tspec,
                  pl.BlockSpec((None, 1, tc), lambda o, c, i: (o, 0, c))],
        out_specs=pl.BlockSpec((None, 2, tf, tc), lambda o, c, i: (o, 0, i, c)),
        scratch_shapes=[pltpu.VMEM((seq, tc), BF16)] * 2,
        compiler_params=_cparams("parallel", "parallel", "arbitrary"),
        name="hyena_filter_spectrum",
    )(dft_top, dft_bot, taps_a, taps_d, nyq)


def _fwd_dft_kernel(wt_ref, wb_ref, z_ref, p_ref, o_ref, zbf):
    i = pl.program_id(2)

    @pl.when(i == 0)
    def _():
        zbf[...] = z_ref[...].astype(BF16)

    for rows in _row_chunks(wt_ref.shape[0]):
        top = jnp.dot(wt_ref[rows, :], zbf[...], preferred_element_type=F32)
        bot = jnp.dot(wb_ref[rows, :], zbf[...], preferred_element_type=F32)
        pt = p_ref[0, rows, :]
        pb = p_ref[1, rows, :]
        re = top * pt - bot * pb
        im = top * pb + bot * pt
        if rows.start == 0:
            row = lax.broadcasted_iota(jnp.int32, top.shape, 0)
            real_row = (row == 0) & (i == 0)
            re = jnp.where(real_row, top * pt, re)
            im = jnp.where(real_row, bot * pb, im)
        o_ref[0, rows, :] = re.astype(o_ref.dtype)
        o_ref[1, rows, :] = im.astype(o_ref.dtype)


def _fwd_dft(dft_top, dft_bot, z, z_col_off, spec, order, *, batch, seq):
    tf, tc = _dft_tiles(seq)
    zc = z_col_off // tc
    wspec = pl.BlockSpec((tf, seq), lambda b, c, i: (i, 0))
    return pl.pallas_call(
        _fwd_dft_kernel,
        out_shape=jax.ShapeDtypeStruct((batch, 2, seq, HYENA_WIDTH), BF16),
        grid=(batch, HYENA_WIDTH // tc, seq // tf),
        in_specs=[wspec, wspec,
                  pl.BlockSpec((seq, tc), lambda b, c, i: (b, zc + c)),
                  pl.BlockSpec((None, 2, tf, tc), lambda b, c, i: (order, 0, i, c))],
        out_specs=pl.BlockSpec((None, 2, tf, tc), lambda b, c, i: (b, 0, i, c)),
        scratch_shapes=[pltpu.VMEM((seq, tc), BF16)],
        compiler_params=_cparams("parallel", "parallel", "arbitrary"),
        name="hyena_fwd_dft",
    )(dft_top, dft_bot, z, spec)


def _inv_dft_kernel(wi_ref, s_ref, gate_ref, z_ref, bias_ref, o_ref):
    s = s_ref[...]
    s2 = s.reshape(s.shape[0] * s.shape[1], s.shape[2])
    for rows in _row_chunks(wi_ref.shape[0]):
        conv = jnp.dot(wi_ref[rows, :], s2, preferred_element_type=F32)
        o_ref[rows, :] = (gate_ref[rows, :] * (conv + bias_ref[...] * z_ref[rows, :])).astype(o_ref.dtype)


def _inv_dft(dft_inv, s, gate, gate_col_off, z, z_col_off, hy_bias, layer, order, out_dtype, *, batch, seq):
    tl, tc = _dft_tiles(seq)
    nrb = seq // tl
    gc = gate_col_off // tc
    zc = z_col_off // tc
    return pl.pallas_call(
        _inv_dft_kernel,
        out_shape=jax.ShapeDtypeStruct((batch * seq, HYENA_WIDTH), out_dtype),
        grid=(batch, HYENA_WIDTH // tc, nrb),
        in_specs=[pl.BlockSpec((tl, 2 * seq), lambda b, c, i: (i, 0)),
                  pl.BlockSpec((None, 2, seq, tc), lambda b, c, i: (b, 0, 0, c)),
                  pl.BlockSpec((tl, tc), lambda b, c, i: (b * nrb + i, gc + c)),
                  pl.BlockSpec((tl, tc), lambda b, c, i: (b * nrb + i, zc + c)),
                  pl.BlockSpec((None, 1, tc), lambda b, c, i: (layer * HYENA_ORDER + order, 0, c))],
        out_specs=pl.BlockSpec((tl, tc), lambda b, c, i: (b * nrb + i, c)),
        compiler_params=_cparams("parallel", "parallel", "arbitrary"),
        name="hyena_inv_dft",
    )(dft_inv, s, gate, z, hy_bias.reshape(DEPTH * HYENA_ORDER, 1, HYENA_WIDTH))


def _hyena(u, p, layer, st):
    batch, seq = st.batch, st.seq
    uc = _short_conv(u, p["hy_conv_w"], p["hy_conv_b"], layer, st)
    dft_top, dft_bot, dft_inv = _dft_matrices(seq)
    hidden = _filter_mlp(seq, p["filt_w1"], p["filt_b1"], p["filt_w2"], p["filt_b2"], p["filt_freq"], layer)
    taps_a, taps_d, nyq = _filter_taps(hidden, p["filt_w3"], layer, seq=seq)
    spec = _filter_spectrum(dft_top, dft_bot, taps_a, taps_d, nyq, seq=seq)
    s = _fwd_dft(dft_top, dft_bot, uc, 0, spec, 0, batch=batch, seq=seq)
    z1 = _inv_dft(dft_inv, s, uc, HYENA_WIDTH, uc, 0, p["hy_bias"], layer, 0, F32, batch=batch, seq=seq)
    s = _fwd_dft(dft_top, dft_bot, z1, 0, spec, 1, batch=batch, seq=seq)
    return _inv_dft(dft_inv, s, uc, 2 * HYENA_WIDTH, z1, 0, p["hy_bias"], layer, 1, BF16, batch=batch, seq=seq)


def _block(x, p, modflat, layer, st, ctx_kv):
    h = _normmod(x, p["norm1_g"], modflat, st, layer, 1, 0)
    lin = functools.partial(_linear, h, p["w_in"], layer, tn=512)
    u_pool = lin(COL_POOL, POOL_WIDTH, kind="plain", out_dtype=F32)
    q = lin(COL_Q, ATTN_WIDTH, kind="rmshead", out_dtype=BF16, head_gain=p["q_norm_g"])
    k = lin(COL_K, ATTN_WIDTH, kind="rmshead", out_dtype=F32, head_gain=p["k_norm_g"])
    v = lin(COL_V, ATTN_WIDTH, kind="plain", out_dtype=F32)
    u_hy = lin(COL_HY, (HYENA_ORDER + 1) * HYENA_WIDTH, kind="plain", out_dtype=F32)
    gates = lin(COL_GATES, 3 * D_MODEL, kind="sigmoid", out_dtype=BF16)

    pool = _pool(u_pool, p["pool_w"], p["pool_scale"], layer, st)
    if ctx_kv is None:
        attn = _ctx_attention(q, k, v, st)
    else:
        attn = _nbr_attention(q, k, v, ctx_kv[0], ctx_kv[1], ctx_kv[2], layer, st)
    hy = _hyena(u_hy, p, layer, st)

    mixed = _mix(pool, attn, hy, p["w_branch_pool"], p["w_branch_attn"], p["w_branch_hyena"], gates, layer)
    x = _linear(mixed, p["w_out"], layer, 0, D_MODEL, tn=512, kind="residual", out_dtype=F32,
                residual=x, modflat=modflat, st=st, which_gate=2)
    h = _normmod(x, p["norm2_g"], modflat, st, layer, 4, 3)
    a = _ffn1(h, p["w_gate"], p["w_up"], layer)
    half = D_FF // 2
    for k_off in (0, half):
        x = _linear(a, p["w_down"], layer, 0, D_MODEL, tn=512, tm=512, kind="residual", out_dtype=F32,
                    k_off=k_off, k_len=half, residual=x, modflat=modflat, st=st, which_gate=5)
    return x, k, v


_WEIGHT_NAMES = ("ada_w", "ada_b", "norm1_g", "norm2_g", "w_in", "pool_w", "pool_scale",
                 "q_norm_g", "k_norm_g", "rpb", "hy_conv_w", "hy_conv_b", "filt_w1", "filt_b1",
                 "filt_w2", "filt_b2", "filt_w3", "filt_freq", "hy_bias", "w_branch_pool",
                 "w_branch_attn", "w_branch_hyena", "w_out", "w_gate", "w_up", "w_down")


def kernel(x_prompt, x_sample, cache_k, cache_v, c, c_ctx, ada_w, ada_b, norm1_g, norm2_g, w_in, pool_w, pool_scale, q_norm_g, k_norm_g, rpb, hy_conv_w, hy_conv_b, filt_w1, filt_b1, filt_w2, filt_b2, filt_w3, filt_freq, hy_bias, w_branch_pool, w_branch_attn, w_branch_hyena, w_out, w_gate, w_up, w_down):
    p = dict(zip(_WEIGHT_NAMES, (ada_w, ada_b, norm1_g, norm2_g, w_in, pool_w, pool_scale, q_norm_g,
                                 k_norm_g, rpb, hy_conv_w, hy_conv_b, filt_w1, filt_b1, filt_w2, filt_b2,
                                 filt_w3, filt_freq, hy_bias, w_branch_pool, w_branch_attn,
                                 w_branch_hyena, w_out, w_gate, w_up, w_down)))
    nb, ns, d = x_prompt.shape
    lb, ls, _ = x_sample.shape
    assert d == D_MODEL and lb + 1 <= MOD_ROWS
    ctx = _Stream(nb, ns, 0, False)
    lat = _Stream(lb, ls, 1, True)

    cmat = jnp.concatenate([c_ctx[None, :], c, jnp.zeros((MOD_ROWS - 1 - lb, d), F32)], axis=0)
    mod = _modulation(cmat, ada_w, ada_b)
    modflat = mod.reshape(DEPTH * MOD_ROWS * 6, 1, d)

    y_ctx = x_prompt.reshape(nb * ns, d)
    y_lat = x_sample.reshape(lb * ls, d)
    new_k, new_v = [], []
    for layer in range(DEPTH):
        y_ctx, k_l, v_l = _block(y_ctx, p, modflat, layer, ctx, None)
        new_k.append(k_l.reshape(nb, ns, N_HEADS, HEAD_DIM))
        new_v.append(v_l.reshape(nb, ns, N_HEADS, HEAD_DIM))
        bias = _bias_table(rpb, layer)
        y_lat, _, _ = _block(y_lat, p, modflat, layer, lat, (cache_k, cache_v, bias))

    return (y_ctx.reshape(nb, ns, d), y_lat.reshape(lb, ls, d),
            jnp.stack(new_k, axis=1), jnp.stack(new_v, axis=1))
```

```python
import functools
import math

import jax
import jax.numpy as jnp
from jax import lax
from jax.experimental import pallas as pl
from jax.experimental.pallas import tpu as pltpu

F32 = jnp.float32
BF16 = jnp.bfloat16

D_MODEL = 4096
DEPTH = 2
GRID_W = 64
HEAD_DIM = 128
N_HEADS = D_MODEL // 256
ATTN_WIDTH = N_HEADS * HEAD_DIM
ATTN_SCALE = HEAD_DIM ** -0.5
WIN_R = 8
WIN_C = 16
NEG_INF = -1e30
POOL_WIDTH = D_MODEL // 2
POOL_SIZES = (2, 4, 8, 16)
POOL_GROUP = POOL_WIDTH // len(POOL_SIZES)
HYENA_WIDTH = D_MODEL // 2
HYENA_ORDER = 2
FILTER_EMB = 33
FILTER_HIDDEN = 64
DECAY_TARGET = 1e-2
FAST_DECAY_PCT = 0.3
SLOW_DECAY_PCT = 1.5
D_FF = ((8 * D_MODEL + 767) // 768) * 256
COL_POOL = 0
COL_Q = POOL_WIDTH
COL_K = COL_Q + ATTN_WIDTH
COL_V = COL_K + ATTN_WIDTH
COL_HY = COL_V + ATTN_WIDTH
COL_GATES = COL_HY + (HYENA_ORDER + 1) * HYENA_WIDTH
MOD_ROWS = 8
RMS_EPS = 1e-6

LANES = 128
VMEM_LIMIT = 56 * 1024 * 1024
SUB_M = 256
CH_TILE = 512
SHORT_SEQ = 512


def _cparams(*sem):
    return pltpu.CompilerParams(dimension_semantics=sem, vmem_limit_bytes=VMEM_LIMIT)


def _row_chunks(tm, sub=SUB_M):
    sub = min(sub, tm)
    return [slice(r, r + sub) for r in range(0, tm, sub)]


def _token_tile(m_tok):
    return 1024 if m_tok % 1024 == 0 else 512


def _mod_kernel(c_ref, w_ref, b_ref, o_ref):
    c = c_ref[...]
    s = (c * jax.nn.sigmoid(c)).astype(BF16)
    o_ref[...] = jnp.dot(s, w_ref[...].astype(BF16), preferred_element_type=F32) + b_ref[...]


def _modulation(cmat, ada_w, ada_b):
    tn = 512
    n6 = ada_w.shape[-1]
    return pl.pallas_call(
        _mod_kernel,
        out_shape=jax.ShapeDtypeStruct((DEPTH, MOD_ROWS, n6), F32),
        grid=(DEPTH, n6 // tn),
        in_specs=[pl.BlockSpec((MOD_ROWS, D_MODEL), lambda l, j: (0, 0)),
                  pl.BlockSpec((None, D_MODEL, tn), lambda l, j: (l, 0, j)),
                  pl.BlockSpec((None, 1, tn), lambda l, j: (l, 0, j))],
        out_specs=pl.BlockSpec((None, MOD_ROWS, tn), lambda l, j: (l, 0, j)),
        compiler_params=_cparams("parallel", "parallel"),
        name="modulation",
    )(cmat, ada_w, ada_b.reshape(DEPTH, 1, n6))


class _Stream:
    def __init__(self, batch, seq, mod_base, per_batch_mod):
        self.batch = batch
        self.seq = seq
        self.mod_base = mod_base
        self.per_batch_mod = per_batch_mod

    def mod_index(self, layer, i, tm, which):
        row = self.mod_base + ((i * tm) // self.seq if self.per_batch_mod else 0)
        return (layer * MOD_ROWS + row) * 6 + which


_STAT_ROWS = 32
_NORM_ROWS = 16


def _normmod_kernel(x_ref, g_ref, sc_ref, sh_ref, o_ref, rstd_ref, gain_ref, shift_ref):
    tm, d = x_ref.shape
    gain_ref[...] = jnp.broadcast_to(g_ref[...] * (1.0 + sc_ref[0]), (_NORM_ROWS, d))
    shift_ref[...] = jnp.broadcast_to(sh_ref[0], (_NORM_ROWS, d))

    def stats(ci, carry):
        base = pl.multiple_of(ci * _STAT_ROWS, _STAT_ROWS)
        x = x_ref[pl.ds(base, _STAT_ROWS), :]
        sq = x * x
        while sq.shape[1] > LANES:
            half = sq.shape[1] // 2
            sq = sq[:, :half] + sq[:, half:]
        rstd_ref[pl.ds(base, _STAT_ROWS), :] = sq
        return carry

    lax.fori_loop(0, tm // _STAT_ROWS, stats, 0)
    rstd = lax.rsqrt(jnp.sum(rstd_ref[...], axis=-1, keepdims=True) * (1.0 / d) + RMS_EPS)
    rstd_ref[...] = jnp.broadcast_to(rstd, (tm, LANES))

    def scale(ci, carry):
        base = pl.multiple_of(ci * _NORM_ROWS, _NORM_ROWS)
        x = x_ref[pl.ds(base, _NORM_ROWS), :]
        rstd = jnp.concatenate([rstd_ref[pl.ds(base, _NORM_ROWS), :]] * (d // LANES), axis=1)
        o_ref[pl.ds(base, _NORM_ROWS), :] = (x * rstd * gain_ref[...] + shift_ref[...]).astype(o_ref.dtype)
        return carry

    lax.fori_loop(0, tm // _NORM_ROWS, scale, 0)


def _normmod(x, g, modflat, st, layer, which_scale, which_shift):
    m_tok, d = x.shape
    tm = 256
    return pl.pallas_call(
        _normmod_kernel,
        out_shape=jax.ShapeDtypeStruct((m_tok, d), BF16),
        grid=(m_tok // tm,),
        in_specs=[pl.BlockSpec((tm, d), lambda i: (i, 0)),
                  pl.BlockSpec((None, 1, d), lambda i: (layer, 0, 0)),
                  pl.BlockSpec((1, 1, d), lambda i: (st.mod_index(layer, i, tm, which_scale), 0, 0)),
                  pl.BlockSpec((1, 1, d), lambda i: (st.mod_index(layer, i, tm, which_shift), 0, 0))],
        out_specs=pl.BlockSpec((tm, d), lambda i: (i, 0)),
        scratch_shapes=[pltpu.VMEM((tm, LANES), F32), pltpu.VMEM((_NORM_ROWS, d), F32),
                        pltpu.VMEM((_NORM_ROWS, d), F32)],
        compiler_params=_cparams("parallel"),
        name="normmod",
    )(x, g.reshape(DEPTH, 1, d), modflat, modflat)


def _cast_weight_once(w_ref, wbf_ref):
    @pl.when(pl.program_id(1) == 0)
    def _():
        wbf_ref[...] = w_ref[...].astype(BF16)


def _lin_plain_kernel(x_ref, w_ref, o_ref, wbf_ref):
    _cast_weight_once(w_ref, wbf_ref)
    for rows in _row_chunks(x_ref.shape[0]):
        acc = jnp.dot(x_ref[rows, :], wbf_ref[...], preferred_element_type=F32)
        o_ref[rows, :] = acc.astype(o_ref.dtype)


def _lin_sigmoid_kernel(x_ref, w_ref, o_ref, wbf_ref):
    _cast_weight_once(w_ref, wbf_ref)
    for rows in _row_chunks(x_ref.shape[0]):
        acc = jnp.dot(x_ref[rows, :], wbf_ref[...], preferred_element_type=F32)
        o_ref[rows, :] = jax.nn.sigmoid(acc).astype(o_ref.dtype)


def _lin_rmshead_kernel(x_ref, w_ref, g_ref, o_ref, wbf_ref):
    _cast_weight_once(w_ref, wbf_ref)
    g = g_ref[...]
    for rows in _row_chunks(x_ref.shape[0]):
        acc = jnp.dot(x_ref[rows, :], wbf_ref[...], preferred_element_type=F32)
        for h in range(acc.shape[1] // HEAD_DIM):
            a = acc[:, h * HEAD_DIM:(h + 1) * HEAD_DIM]
            y = a * lax.rsqrt(jnp.mean(a * a, axis=-1, keepdims=True) + RMS_EPS)
            o_ref[rows, h * HEAD_DIM:(h + 1) * HEAD_DIM] = (y * g).astype(o_ref.dtype)


def _lin_residual_kernel(x_ref, w_ref, res_ref, gate_ref, o_ref, wbf_ref):
    _cast_weight_once(w_ref, wbf_ref)
    for rows in _row_chunks(x_ref.shape[0]):
        acc = jnp.dot(x_ref[rows, :], wbf_ref[...], preferred_element_type=F32)
        o_ref[rows, :] = res_ref[rows, :] + gate_ref[0] * acc


def _linear(x, w, layer, col_off, ncols, *, tn, kind, out_dtype, tm=None, k_off=0, k_len=None,
            head_gain=None, residual=None, modflat=None, st=None, which_gate=None):
    m_tok = x.shape[0]
    k_len = x.shape[1] if k_len is None else k_len
    tm = _token_tile(m_tok) if tm is None else tm
    cb = col_off // tn
    kb = k_off // k_len
    in_specs = [pl.BlockSpec((tm, k_len), lambda n, m: (m, kb)),
                pl.BlockSpec((None, k_len, tn), lambda n, m: (layer, kb, cb + n))]
    args = [x, w]
    if kind == "plain":
        body = _lin_plain_kernel
    elif kind == "sigmoid":
        body = _lin_sigmoid_kernel
    elif kind == "rmshead":
        body = _lin_rmshead_kernel
        in_specs.append(pl.BlockSpec((None, 1, HEAD_DIM), lambda n, m: (layer, 0, 0)))
        args.append(head_gain.reshape(DEPTH, 1, HEAD_DIM))
    else:
        body = _lin_residual_kernel
        in_specs.append(pl.BlockSpec((tm, tn), lambda n, m: (m, n)))
        in_specs.append(pl.BlockSpec(
            (1, 1, tn), lambda n, m: (st.mod_index(layer, m, tm, which_gate), 0, n)))
        args += [residual, modflat]
    return pl.pallas_call(
        body,
        out_shape=jax.ShapeDtypeStruct((m_tok, ncols), out_dtype),
        grid=(ncols // tn, m_tok // tm),
        in_specs=in_specs,
        out_specs=pl.BlockSpec((tm, tn), lambda n, m: (m, n)),
        scratch_shapes=[pltpu.VMEM((k_len, tn), BF16)],
        compiler_params=_cparams("parallel", "arbitrary"),
        name="linear_" + kind,
    )(*args)


def _mix_kernel(xp_ref, xa_ref, xh_ref, wp_ref, wa_ref, wh_ref, gp_ref, ga_ref, gh_ref, o_ref,
                wpb, wab, whb):
    @pl.when(pl.program_id(1) == 0)
    def _():
        wpb[...] = wp_ref[...].astype(BF16)
        wab[...] = wa_ref[...].astype(BF16)
        whb[...] = wh_ref[...].astype(BF16)

    acc = gp_ref[...].astype(F32) * jnp.dot(xp_ref[...], wpb[...], preferred_element_type=F32)
    acc = acc + ga_ref[...].astype(F32) * jnp.dot(xa_ref[...], wab[...], preferred_element_type=F32)
    acc = acc + gh_ref[...].astype(F32) * jnp.dot(xh_ref[...], whb[...], preferred_element_type=F32)
    o_ref[...] = acc.astype(o_ref.dtype)


def _mix(xp, xa, xh, wp, wa, wh, gates, layer):
    m_tok, k = xp.shape
    n = wp.shape[-1]
    tm, tn = 512, 512
    nb = n // tn
    xspec = pl.BlockSpec((tm, k), lambda j, i: (i, 0))
    wspec = pl.BlockSpec((None, k, tn), lambda j, i: (layer, 0, j))
    return pl.pallas_call(
        _mix_kernel,
        out_shape=jax.ShapeDtypeStruct((m_tok, n), BF16),
        grid=(nb, m_tok // tm),
        in_specs=[xspec, xspec, xspec, wspec, wspec, wspec,
                  pl.BlockSpec((tm, tn), lambda j, i: (i, j)),
                  pl.BlockSpec((tm, tn), lambda j, i: (i, nb + j)),
                  pl.BlockSpec((tm, tn), lambda j, i: (i, 2 * nb + j))],
        out_specs=pl.BlockSpec((tm, tn), lambda j, i: (i, j)),
        scratch_shapes=[pltpu.VMEM((k, tn), BF16)] * 3,
        compiler_params=_cparams("parallel", "arbitrary"),
        name="mix",
    )(xp, xa, xh, wp, wa, wh, gates, gates, gates)


def _ffn1_kernel(x_ref, wg_ref, wu_ref, o_ref, wgb, wub):
    @pl.when(pl.program_id(1) == 0)
    def _():
        wgb[...] = wg_ref[...].astype(BF16)
        wub[...] = wu_ref[...].astype(BF16)

    for rows in _row_chunks(x_ref.shape[0], 2 * SUB_M):
        x = x_ref[rows, :]
        g = jnp.dot(x, wgb[...], preferred_element_type=F32)
        u = jnp.dot(x, wub[...], preferred_element_type=F32)
        o_ref[rows, :] = (g * jax.nn.sigmoid(g) * u).astype(o_ref.dtype)


def _ffn1(x, wg, wu, layer):
    m_tok, k = x.shape
    n = wg.shape[-1]
    tm, tn = _token_tile(m_tok), 256
    wspec = pl.BlockSpec((None, k, tn), lambda j, i: (layer, 0, j))
    return pl.pallas_call(
        _ffn1_kernel,
        out_shape=jax.ShapeDtypeStruct((m_tok, n), BF16),
        grid=(n // tn, m_tok // tm),
        in_specs=[pl.BlockSpec((tm, k), lambda j, i: (i, 0)), wspec, wspec],
        out_specs=pl.BlockSpec((tm, tn), lambda j, i: (i, j)),
        scratch_shapes=[pltpu.VMEM((k, tn), BF16)] * 2,
        compiler_params=_cparams("parallel", "arbitrary"),
        name="ffn_gate_up",
    )(x, wg, wu)


_POOL_PAD = 16
_ROW_CHUNK = 64


def _halo_chunks(u_ref, seq, rc, pad, emit):
    zeros = jnp.zeros((pad, u_ref.shape[1]), F32)
    emit(0, jnp.concatenate([zeros, u_ref[0:rc + pad, :]], axis=0))

    def chunk(ci, carry):
        base = pl.multiple_of(ci * rc, rc)
        emit(base, u_ref[pl.ds(pl.multiple_of(base - pad, pad), rc + 2 * pad), :])
        return carry

    lax.fori_loop(1, seq // rc - 1, chunk, 0)
    emit(seq - rc, jnp.concatenate([u_ref[seq - rc - pad:seq, :], zeros], axis=0))


def _pool_kernel(u_ref, w_ref, sc_ref, o_ref, d_ref, *, seq):
    g = pl.program_id(1)
    width = u_ref.shape[1]
    rc = _ROW_CHUNK

    for gi, win in enumerate(POOL_SIZES):
        half = win // 2

        @pl.when(g == gi)
        def _(win=win, half=half):
            def emit(base, window):
                lo = _POOL_PAD - half
                acc = window[lo:lo + rc]
                for j in range(1, win):
                    acc = acc + window[lo + j:lo + j + rc]
                t = base + lax.broadcasted_iota(jnp.int32, (rc, width), 0)
                cnt = (jnp.minimum(t + (win - half), seq) - jnp.maximum(t - half, 0)).astype(F32)
                centre = window[_POOL_PAD:_POOL_PAD + rc]
                d_ref[pl.ds(base, rc), :] = (acc / cnt - centre).astype(BF16)

            _halo_chunks(u_ref, seq, rc, _POOL_PAD, emit)

    wbf = w_ref[...].astype(BF16)
    scale = sc_ref[...]
    for rows in _row_chunks(seq):
        y = jnp.dot(d_ref[rows, :], wbf, preferred_element_type=F32) * scale
        o_ref[rows, :] = y.astype(o_ref.dtype)


def _pool(u, pool_w, pool_scale, layer, st):
    seq = st.seq
    return pl.pallas_call(
        functools.partial(_pool_kernel, seq=seq),
        out_shape=jax.ShapeDtypeStruct((st.batch * seq, POOL_WIDTH), BF16),
        grid=(st.batch, len(POOL_SIZES)),
        in_specs=[pl.BlockSpec((seq, POOL_GROUP), lambda b, g: (b, g)),
                  pl.BlockSpec((None, None, POOL_GROUP, POOL_GROUP), lambda b, g: (layer, g, 0, 0)),
                  pl.BlockSpec((None, 1, POOL_GROUP), lambda b, g: (layer, 0, g))],
        out_specs=pl.BlockSpec((seq, POOL_GROUP), lambda b, g: (b, g)),
        scratch_shapes=[pltpu.VMEM((seq, POOL_GROUP), BF16)],
        compiler_params=_cparams("parallel", "parallel"),
        name="pool_mixer",
    )(u, pool_w, pool_scale.reshape(DEPTH, 1, POOL_WIDTH))


def _ctx_attn_kernel(q_ref, k_ref, v_ref, o_ref):
    for h in range(N_HEADS):
        sl = slice(h * HEAD_DIM, (h + 1) * HEAD_DIM)
        q = q_ref[:, sl]
        k = k_ref[:, sl].astype(BF16)
        v = v_ref[:, sl].astype(BF16)
        s = lax.dot_general(q, k, (((1,), (1,)), ((), ())), preferred_element_type=F32) * ATTN_SCALE
        m = jnp.max(s, axis=-1, keepdims=True)
        p = jnp.exp(s - m)
        denom = jnp.sum(p, axis=-1, keepdims=True)
        o = jnp.dot(p.astype(BF16), v, preferred_element_type=F32) / denom
        o_ref[:, sl] = o.astype(o_ref.dtype)


def _ctx_attention(q, k, v, st):
    spec = pl.BlockSpec((st.seq, ATTN_WIDTH), lambda b: (b, 0))
    return pl.pallas_call(
        _ctx_attn_kernel,
        out_shape=jax.ShapeDtypeStruct((st.batch * st.seq, ATTN_WIDTH), BF16),
        grid=(st.batch,),
        in_specs=[spec, spec, spec],
        out_specs=spec,
        compiler_params=_cparams("parallel"),
        name="context_attention",
    )(q, k, v)


_BIAS_TILES = 2 * WIN_R
_Q_GROUP = 8
_BAND = 2 * WIN_R


def _bias_table_kernel(rpb_ref, o_ref, *, layer):
    h = pl.program_id(0)
    shape = (GRID_W, 2 * GRID_W)
    qc = lax.broadcasted_iota(jnp.int32, shape, 0)
    lane = lax.broadcasted_iota(jnp.int32, shape, 1)
    kc = lane % GRID_W
    upper = lane >= GRID_W
    c0 = jnp.clip(qc - WIN_C // 2, 0, GRID_W - WIN_C)
    in_window = (kc >= c0) & (kc < c0 + WIN_C)
    rel = kc - qc + (WIN_C - 1)
    n_dc = 2 * WIN_C - 1
    n_dr = 2 * WIN_R - 1
    row_base = (layer * N_HEADS + h) * n_dr
    for e in range(_BIAS_TILES):
        d_lo = max(e - 1, 0)
        d_hi = min(e, n_dr - 1)
        tile = jnp.zeros(shape, F32)
        for dc in range(n_dc):
            lo = rpb_ref[(row_base + d_lo) * n_dc + dc]
            hi = rpb_ref[(row_base + d_hi) * n_dc + dc]
            tile = jnp.where(rel == dc, jnp.where(upper, hi, lo), tile)
        valid = in_window
        if e == 0:
            valid = valid & upper
        if e == _BIAS_TILES - 1:
            valid = valid & jnp.logical_not(upper)
        o_ref[e] = jnp.where(valid, tile, NEG_INF)


def _bias_table(rpb, layer):
    return pl.pallas_call(
        functools.partial(_bias_table_kernel, layer=layer),
        out_shape=jax.ShapeDtypeStruct((N_HEADS, _BIAS_TILES, GRID_W, 2 * GRID_W), F32),
        grid=(N_HEADS,),
        in_specs=[pl.BlockSpec(memory_space=pltpu.SMEM)],
        out_specs=pl.BlockSpec((None, _BIAS_TILES, GRID_W, 2 * GRID_W), lambda h: (h, 0, 0, 0)),
        compiler_params=_cparams("parallel"),
        name="rel_pos_bias_table",
    )(rpb.reshape(-1))


def _nbr_attn_kernel(q_ref, k_ref, v_ref, kc_ref, vc_ref, bias_ref, o_ref,
                     kbf, vbf, s_ref, sc_ref, p_ref, pc_ref, *, rows):
    kbf[...] = k_ref[...].astype(BF16)
    vbf[...] = v_ref[...].astype(BF16)
    kc = kc_ref[...].astype(BF16)
    vc = vc_ref[...].astype(BF16)
    dims = (((1,), (1,)), ((), ()))
    pair = 2 * GRID_W
    lower = lax.broadcasted_iota(jnp.int32, (GRID_W, pair), 1) < GRID_W
    zero_tile = jnp.zeros((GRID_W, pair), BF16)

    for g in range(rows // _Q_GROUP):
        slot = g % 2
        ks = min(max(g * _Q_GROUP - WIN_R // 2, 0), rows - _BAND)
        qsl = slice(g * _Q_GROUP * GRID_W, (g + 1) * _Q_GROUP * GRID_W)
        ksl = slice(ks * GRID_W, (ks + _BAND) * GRID_W)
        qg = q_ref[qsl, :]
        s_ref[slot] = lax.dot_general(qg, kbf[ksl, :], dims, preferred_element_type=F32) * ATTN_SCALE
        sc_ref[slot] = lax.dot_general(qg, kc, dims, preferred_element_type=F32) * ATTN_SCALE
        for i in range(_Q_GROUP):
            qr = g * _Q_GROUP + i
            r0 = min(max(qr - WIN_R // 2, 0), rows - WIN_R)
            first = r0 - ks
            j0, j1 = first // 2, (first + WIN_R + 1) // 2
            rsl = slice(i * GRID_W, (i + 1) * GRID_W)
            tiles = []
            for j in range(j0, j1):
                kr0 = ks + 2 * j
                t = s_ref[slot, rsl, j * pair:(j + 1) * pair] + bias_ref[kr0 - qr + WIN_R]
                if kr0 < r0:
                    t = jnp.where(lower, NEG_INF, t)
                if kr0 + 1 >= r0 + WIN_R:
                    t = jnp.where(lower, t, NEG_INF)
                tiles.append(t)
            sc = sc_ref[slot, rsl, :]
            mt = functools.reduce(jnp.maximum, tiles)
            m = jnp.maximum(jnp.max(mt, axis=-1, keepdims=True), jnp.max(sc, axis=-1, keepdims=True))
            ps = [jnp.exp(t - m) for t in tiles]
            pc = jnp.exp(sc - m)
            denom = (jnp.sum(functools.reduce(jnp.add, ps), axis=-1, keepdims=True)
                     + jnp.sum(pc, axis=-1, keepdims=True))
            inv = 1.0 / denom
            for j in range(_BAND // 2):
                val = (ps[j - j0] * inv).astype(BF16) if j0 <= j < j1 else zero_tile
                p_ref[slot, rsl, j * pair:(j + 1) * pair] = val
            pc_ref[slot, rsl, :] = (pc * inv).astype(BF16)
        o = (jnp.dot(p_ref[slot], vbf[ksl, :], preferred_element_type=F32)
             + jnp.dot(pc_ref[slot], vc, preferred_element_type=F32))
        o_ref[qsl, :] = o.astype(o_ref.dtype)


def _nbr_attention(q, k, v, cache_k, cache_v, bias, layer, st):
    seq = st.seq
    rows = seq // GRID_W
    assert rows >= _BAND and rows % _Q_GROUP == 0
    past = cache_k.shape[2]
    ck = cache_k.reshape(st.batch, DEPTH, past, ATTN_WIDTH)
    cv = cache_v.reshape(st.batch, DEPTH, past, ATTN_WIDTH)
    tok_spec = pl.BlockSpec((seq, HEAD_DIM), lambda b, h: (b, h))
    cache_spec = pl.BlockSpec((None, None, past, HEAD_DIM), lambda b, h: (b, layer, 0, h))
    nq = _Q_GROUP * GRID_W
    nk = _BAND * GRID_W
    return pl.pallas_call(
        functools.partial(_nbr_attn_kernel, rows=rows),
        out_shape=jax.ShapeDtypeStruct((st.batch * seq, ATTN_WIDTH), BF16),
        grid=(st.batch, N_HEADS),
        in_specs=[tok_spec, tok_spec, tok_spec, cache_spec, cache_spec,
                  pl.BlockSpec((None, _BIAS_TILES, GRID_W, 2 * GRID_W), lambda b, h: (h, 0, 0, 0))],
        out_specs=tok_spec,
        scratch_shapes=[pltpu.VMEM((seq, HEAD_DIM), BF16), pltpu.VMEM((seq, HEAD_DIM), BF16),
                        pltpu.VMEM((2, nq, nk), F32), pltpu.VMEM((2, nq, past), F32),
                        pltpu.VMEM((2, nq, nk), BF16), pltpu.VMEM((2, nq, past), BF16)],
        compiler_params=_cparams("parallel", "parallel"),
        name="neighbourhood_attention",
    )(q, k, v, ck, cv, bias)


_CONV_PAD = 8


def _short_conv_kernel(u_ref, w_ref, b_ref, o_ref, *, seq):
    width = u_ref.shape[1]
    w0 = w_ref[0:1, :]
    w1 = w_ref[1:2, :]
    w2 = w_ref[2:3, :]
    b = b_ref[...]
    rc = max(8, _ROW_CHUNK * CH_TILE // width)

    def emit(base, window):
        prev = window[_CONV_PAD - 1:_CONV_PAD - 1 + rc]
        cur = window[_CONV_PAD:_CONV_PAD + rc]
        nxt = window[_CONV_PAD + 1:_CONV_PAD + 1 + rc]
        o_ref[pl.ds(base, rc), :] = prev * w0 + cur * w1 + nxt * w2 + b

    _halo_chunks(u_ref, seq, rc, _CONV_PAD, emit)


def _short_conv(u, conv_w, conv_b, layer, st):
    width = u.shape[1]
    seq = st.seq
    tc = HYENA_WIDTH if seq <= SHORT_SEQ else CH_TILE
    return pl.pallas_call(
        functools.partial(_short_conv_kernel, seq=seq),
        out_shape=jax.ShapeDtypeStruct((st.batch * seq, width), F32),
        grid=(st.batch, width // tc),
        in_specs=[pl.BlockSpec((seq, tc), lambda b, j: (b, j)),
                  pl.BlockSpec((None, 3, tc), lambda b, j: (layer, 0, j)),
                  pl.BlockSpec((None, 1, tc), lambda b, j: (layer, 0, j))],
        out_specs=pl.BlockSpec((seq, tc), lambda b, j: (b, j)),
        compiler_params=_cparams("parallel", "parallel"),
        name="hyena_short_conv",
    )(u, conv_w, conv_b.reshape(DEPTH, 1, width))


def _dft_matrices(seq):
    f = jnp.arange(seq, dtype=jnp.int32)[:, None]
    t = jnp.arange(seq, dtype=jnp.int32)[None, :]
    ang = ((f * t) % (2 * seq)).astype(F32) * (math.pi / seq)
    top = jnp.cos(ang)
    bot = jnp.where(f == 0, (1 - 2 * (t % 2)).astype(F32), -jnp.sin(ang))
    colscale = jnp.where(jnp.arange(seq) == 0, 0.5, 1.0).astype(F32)[None, :] / seq
    inv = jnp.concatenate([top.T * colscale, bot.T * colscale], axis=1)
    return top.astype(BF16), bot.astype(BF16), inv.astype(BF16)


def _filter_mlp_kernel(z_ref, w1_ref, b1_ref, w2_ref, b2_ref, fr_ref, o_ref):
    fr = fr_ref[...]
    h = jnp.sin(fr * (jnp.dot(z_ref[...], w1_ref[...], preferred_element_type=F32) + b1_ref[...]))
    o_ref[...] = jnp.sin(fr * (jnp.dot(h, w2_ref[...], preferred_element_type=F32) + b2_ref[...]))


def _filter_mlp(seq, w1, b1, w2, b2, freq, layer):
    t = jnp.linspace(0.0, 1.0, seq, dtype=F32)[:, None]
    bands = (FILTER_EMB - 1) // 2
    ang = (2.0 * math.pi / seq) * jnp.arange(seq, dtype=F32)[:, None] * jnp.linspace(
        1e-4, bands - 1, bands, dtype=F32)[None]
    z = jnp.concatenate([t, jnp.cos(ang), -jnp.sin(ang)], axis=-1)
    z = jnp.pad(z, ((0, 0), (0, LANES - FILTER_EMB)))
    w1p = jnp.pad(w1, ((0, 0), (0, LANES - FILTER_EMB), (0, 0)))
    vec = lambda a: a.reshape(DEPTH, 1, FILTER_HIDDEN)
    vspec = pl.BlockSpec((None, 1, FILTER_HIDDEN), lambda i: (layer, 0, 0))
    return pl.pallas_call(
        _filter_mlp_kernel,
        out_shape=jax.ShapeDtypeStruct((seq, FILTER_HIDDEN), F32),
        grid=(1,),
        in_specs=[pl.BlockSpec((seq, LANES), lambda i: (0, 0)),
                  pl.BlockSpec((None, LANES, FILTER_HIDDEN), lambda i: (layer, 0, 0)),
                  vspec,
                  pl.BlockSpec((None, FILTER_HIDDEN, FILTER_HIDDEN), lambda i: (layer, 0, 0)),
                  vspec, vspec],
        out_specs=pl.BlockSpec((seq, FILTER_HIDDEN), lambda i: (0, 0)),
        compiler_params=_cparams("arbitrary"),
        name="hyena_filter_mlp",
    )(z, w1p, vec(b1), w2, vec(b2), vec(freq))


def _filter_taps_kernel(h_ref, dl_ref, wf_ref, wb_ref, a_ref, d_ref, nyq_ref, *, seq):
    h = h_ref[...]
    shape = (seq, dl_ref.shape[1])
    row = lax.broadcasted_iota(jnp.int32, shape, 0)
    t = row.astype(F32) / float(seq - 1)
    dec = jnp.exp(-t * dl_ref[...])
    fwd = jnp.dot(h, wf_ref[...], preferred_element_type=F32) * dec
    bwd = jnp.dot(h, wb_ref[...], preferred_element_type=F32) * dec
    bwd = jnp.where(row == 0, 0.0, bwd)
    inv = 1.0 / jnp.sum(jnp.abs(fwd) + jnp.abs(bwd), axis=0, keepdims=True)
    a = (fwd + bwd) * inv
    a_ref[...] = a
    d_ref[...] = (fwd - bwd) * inv
    sign = jnp.where(row % 2 == 0, 1.0, -1.0)
    nyq_ref[...] = jnp.sum(a * sign, axis=0, keepdims=True)


def _filter_taps(hidden, w3, layer, *, seq):
    tc = 256
    nct = HYENA_WIDTH // tc
    deltas = jnp.abs(jnp.linspace(math.log(DECAY_TARGET) / SLOW_DECAY_PCT,
                                  math.log(DECAY_TARGET) / FAST_DECAY_PCT, HYENA_WIDTH, dtype=F32))
    tap_shape = jax.ShapeDtypeStruct((HYENA_ORDER, seq, HYENA_WIDTH), F32)
    tap_spec = pl.BlockSpec((None, seq, tc), lambda o, j: (o, 0, j))
    return pl.pallas_call(
        functools.partial(_filter_taps_kernel, seq=seq),
        out_shape=(tap_shape, tap_shape, jax.ShapeDtypeStruct((HYENA_ORDER, 1, HYENA_WIDTH), F32)),
        grid=(HYENA_ORDER, nct),
        in_specs=[pl.BlockSpec((seq, FILTER_HIDDEN), lambda o, j: (0, 0)),
                  pl.BlockSpec((1, tc), lambda o, j: (0, j)),
                  pl.BlockSpec((None, FILTER_HIDDEN, tc), lambda o, j: (layer, 0, (2 * o) * nct + j)),
                  pl.BlockSpec((None, FILTER_HIDDEN, tc), lambda o, j: (layer, 0, (2 * o + 1) * nct + j))],
        out_specs=(tap_spec, tap_spec, pl.BlockSpec((None, 1, tc), lambda o, j: (o, 0, j))),
        compiler_params=_cparams("parallel", "parallel"),
        name="hyena_filter_taps",
    )(hidden, deltas.reshape(1, HYENA_WIDTH), w3, w3)


def _filter_spec_kernel(wt_ref, wb_ref, a_ref, d_ref, nyq_ref, o_ref, abf, dbf):
    i = pl.program_id(2)

    @pl.when(i == 0)
    def _():
        abf[...] = a_ref[...].astype(BF16)
        dbf[...] = d_ref[...].astype(BF16)

    for rows in _row_chunks(wt_ref.shape[0]):
        top = jnp.dot(wt_ref[rows, :], abf[...], preferred_element_type=F32)
        bot = jnp.dot(wb_ref[rows, :], dbf[...], preferred_element_type=F32)
        if rows.start == 0:
            row = lax.broadcasted_iota(jnp.int32, bot.shape, 0)
            bot = jnp.where((row == 0) & (i == 0), nyq_ref[...], bot)
        o_ref[0, rows, :] = top
        o_ref[1, rows, :] = bot


def _dft_tiles(seq):
    return seq, (HYENA_WIDTH if seq <= SHORT_SEQ else CH_TILE)


_RESIDENT = pl.Buffered(1)


def _filter_spectrum(dft_top, dft_bot, taps_a, taps_d, nyq, *, seq):
    tf, tc = _dft_tiles(seq)
    tc = min(tc, CH_TILE // 2)
    wspec = pl.BlockSpec((tf, seq), lambda o, c, i: (i, 0), pipeline_mode=_RESIDENT)
    tspec = pl.BlockSpec((None, seq, tc), lambda o, c, i: (o, 0, c))
    return pl.pallas_call(
        _filter_spec_kernel,
        out_shape=jax.ShapeDtypeStruct((HYENA_ORDER, 2, seq, HYENA_WIDTH), F32),
        grid=(HYENA_ORDER, HYENA_WIDTH // tc, seq // tf),
        in_specs=[wspec, wspec, tspec, tspec,
                  pl.BlockSpec((None, 1, tc), lambda o, c, i: (o, 0, c))],
        out_specs=pl.BlockSpec((None, 2, tf, tc), lambda o, c, i: (o, 0, i, c)),
        scratch_shapes=[pltpu.VMEM((seq, tc), BF16)] * 2,
        compiler_params=_cparams("parallel", "parallel", "arbitrary"),
        name="hyena_filter_spectrum",
    )(dft_top, dft_bot, taps_a, taps_d, nyq)


def _fwd_dft_kernel(wt_ref, wb_ref, z_ref, p_ref, o_ref, zbf):
    i = pl.program_id(2)

    @pl.when(i == 0)
    def _():
        zbf[...] = z_ref[...].astype(BF16)

    for rows in _row_chunks(wt_ref.shape[0]):
        top = jnp.dot(wt_ref[rows, :], zbf[...], preferred_element_type=F32)
        bot = jnp.dot(wb_ref[rows, :], zbf[...], preferred_element_type=F32)
        pt = p_ref[0, rows, :]
        pb = p_ref[1, rows, :]
        re = top * pt - bot * pb
        im = top * pb + bot * pt
        if rows.start == 0:
            row = lax.broadcasted_iota(jnp.int32, top.shape, 0)
            real_row = (row == 0) & (i == 0)
            re = jnp.where(real_row, top * pt, re)
            im = jnp.where(real_row, bot * pb, im)
        o_ref[0, rows, :] = re.astype(o_ref.dtype)
        o_ref[1, rows, :] = im.astype(o_ref.dtype)


def _fwd_dft(dft_top, dft_bot, z, z_col_off, spec, order, *, batch, seq):
    tf, tc = _dft_tiles(seq)
    zc = z_col_off // tc
    wspec = pl.BlockSpec((tf, seq), lambda c, b, i: (i, 0), pipeline_mode=_RESIDENT)
    return pl.pallas_call(
        _fwd_dft_kernel,
        out_shape=jax.ShapeDtypeStruct((batch, 2, seq, HYENA_WIDTH), BF16),
        grid=(HYENA_WIDTH // tc, batch, seq // tf),
        in_specs=[wspec, wspec,
                  pl.BlockSpec((seq, tc), lambda c, b, i: (b, zc + c)),
                  pl.BlockSpec((None, 2, tf, tc), lambda c, b, i: (order, 0, i, c), pipeline_mode=_RESIDENT)],
        out_specs=pl.BlockSpec((None, 2, tf, tc), lambda c, b, i: (b, 0, i, c)),
        scratch_shapes=[pltpu.VMEM((seq, tc), BF16)],
        compiler_params=_cparams("parallel", "parallel", "arbitrary"),
        name="hyena_fwd_dft",
    )(dft_top, dft_bot, z, spec)


def _inv_dft_kernel(wi_ref, s_ref, gate_ref, z_ref, bias_ref, o_ref):
    s = s_ref[...]
    s2 = s.reshape(s.shape[0] * s.shape[1], s.shape[2])
    for rows in _row_chunks(wi_ref.shape[0]):
        conv = jnp.dot(wi_ref[rows, :], s2, preferred_element_type=F32)
        o_ref[rows, :] = (gate_ref[rows, :] * (conv + bias_ref[...] * z_ref[rows, :])).astype(o_ref.dtype)


def _inv_dft(dft_inv, s, gate, gate_col_off, z, z_col_off, hy_bias, layer, order, out_dtype, *, batch, seq):
    tl, tc = _dft_tiles(seq)
    nrb = seq // tl
    gc = gate_col_off // tc
    zc = z_col_off // tc
    return pl.pallas_call(
        _inv_dft_kernel,
        out_shape=jax.ShapeDtypeStruct((batch * seq, HYENA_WIDTH), out_dtype),
        grid=(HYENA_WIDTH // tc, batch, nrb),
        in_specs=[pl.BlockSpec((tl, 2 * seq), lambda c, b, i: (i, 0), pipeline_mode=_RESIDENT),
                  pl.BlockSpec((None, 2, seq, tc), lambda c, b, i: (b, 0, 0, c)),
                  pl.BlockSpec((tl, tc), lambda c, b, i: (b * nrb + i, gc + c)),
                  pl.BlockSpec((tl, tc), lambda c, b, i: (b * nrb + i, zc + c)),
                  pl.BlockSpec((None, 1, tc), lambda c, b, i: (layer * HYENA_ORDER + order, 0, c))],
        out_specs=pl.BlockSpec((tl, tc), lambda c, b, i: (b * nrb + i, c)),
        compiler_params=_cparams("parallel", "parallel", "arbitrary"),
        name="hyena_inv_dft",
    )(dft_inv, s, gate, z, hy_bias.reshape(DEPTH * HYENA_ORDER, 1, HYENA_WIDTH))


def _hyena(u, p, layer, st):
    batch, seq = st.batch, st.seq
    uc = _short_conv(u, p["hy_conv_w"], p["hy_conv_b"], layer, st)
    dft_top, dft_bot, dft_inv = _dft_matrices(seq)
    hidden = _filter_mlp(seq, p["filt_w1"], p["filt_b1"], p["filt_w2"], p["filt_b2"], p["filt_freq"], layer)
    taps_a, taps_d, nyq = _filter_taps(hidden, p["filt_w3"], layer, seq=seq)
    spec = _filter_spectrum(dft_top, dft_bot, taps_a, taps_d, nyq, seq=seq)
    s = _fwd_dft(dft_top, dft_bot, uc, 0, spec, 0, batch=batch, seq=seq)
    z1 = _inv_dft(dft_inv, s, uc, HYENA_WIDTH, uc, 0, p["hy_bias"], layer, 0, F32, batch=batch, seq=seq)
    s = _fwd_dft(dft_top, dft_bot, z1, 0, spec, 1, batch=batch, seq=seq)
    return _inv_dft(dft_inv, s, uc, 2 * HYENA_WIDTH, z1, 0, p["hy_bias"], layer, 1, BF16, batch=batch, seq=seq)


def _block(x, p, modflat, layer, st, ctx_kv):
    h = _normmod(x, p["norm1_g"], modflat, st, layer, 1, 0)
    lin = functools.partial(_linear, h, p["w_in"], layer, tn=512)
    u_pool = lin(COL_POOL, POOL_WIDTH, kind="plain", out_dtype=F32)
    q = lin(COL_Q, ATTN_WIDTH, kind="rmshead", out_dtype=BF16, head_gain=p["q_norm_g"])
    k = lin(COL_K, ATTN_WIDTH, kind="rmshead", out_dtype=F32, head_gain=p["k_norm_g"])
    v = lin(COL_V, ATTN_WIDTH, kind="plain", out_dtype=F32)
    u_hy = lin(COL_HY, (HYENA_ORDER + 1) * HYENA_WIDTH, kind="plain", out_dtype=F32)
    gates = lin(COL_GATES, 3 * D_MODEL, kind="sigmoid", out_dtype=BF16)

    pool = _pool(u_pool, p["pool_w"], p["pool_scale"], layer, st)
    if ctx_kv is None:
        attn = _ctx_attention(q, k, v, st)
    else:
        attn = _nbr_attention(q, k, v, ctx_kv[0], ctx_kv[1], ctx_kv[2], layer, st)
    hy = _hyena(u_hy, p, layer, st)

    mixed = _mix(pool, attn, hy, p["w_branch_pool"], p["w_branch_attn"], p["w_branch_hyena"], gates, layer)
    x = _linear(mixed, p["w_out"], layer, 0, D_MODEL, tn=512, kind="residual", out_dtype=F32,
                residual=x, modflat=modflat, st=st, which_gate=2)
    h = _normmod(x, p["norm2_g"], modflat, st, layer, 4, 3)
    a = _ffn1(h, p["w_gate"], p["w_up"], layer)
    half = D_FF // 2
    for k_off in (0, half):
        x = _linear(a, p["w_down"], layer, 0, D_MODEL, tn=512, tm=512, kind="residual", out_dtype=F32,
                    k_off=k_off, k_len=half, residual=x, modflat=modflat, st=st, which_gate=5)
    return x, k, v


_WEIGHT_NAMES = ("ada_w", "ada_b", "norm1_g", "norm2_g", "w_in", "pool_w", "pool_scale",
                 "q_norm_g", "k_norm_g", "rpb", "hy_conv_w", "hy_conv_b", "filt_w1", "filt_b1",
                 "filt_w2", "filt_b2", "filt_w3", "filt_freq", "hy_bias", "w_branch_pool",
                 "w_branch_attn", "w_branch_hyena", "w_out", "w_gate", "w_up", "w_down")


def kernel(x_prompt, x_sample, cache_k, cache_v, c, c_ctx, ada_w, ada_b, norm1_g, norm2_g, w_in, pool_w, pool_scale, q_norm_g, k_norm_g, rpb, hy_conv_w, hy_conv_b, filt_w1, filt_b1, filt_w2, filt_b2, filt_w3, filt_freq, hy_bias, w_branch_pool, w_branch_attn, w_branch_hyena, w_out, w_gate, w_up, w_down):
    p = dict(zip(_WEIGHT_NAMES, (ada_w, ada_b, norm1_g, norm2_g, w_in, pool_w, pool_scale, q_norm_g,
                                 k_norm_g, rpb, hy_conv_w, hy_conv_b, filt_w1, filt_b1, filt_w2, filt_b2,
                                 filt_w3, filt_freq, hy_bias, w_branch_pool, w_branch_attn,
                                 w_branch_hyena, w_out, w_gate, w_up, w_down)))
    nb, ns, d = x_prompt.shape
    lb, ls, _ = x_sample.shape
    assert d == D_MODEL and lb + 1 <= MOD_ROWS
    ctx = _Stream(nb, ns, 0, False)
    lat = _Stream(lb, ls, 1, True)

    cmat = jnp.concatenate([c_ctx[None, :], c, jnp.zeros((MOD_ROWS - 1 - lb, d), F32)], axis=0)
    mod = _modulation(cmat, ada_w, ada_b)
    modflat = mod.reshape(DEPTH * MOD_ROWS * 6, 1, d)

    y_ctx = x_prompt.reshape(nb * ns, d)
    y_lat = x_sample.reshape(lb * ls, d)
    new_k, new_v = [], []
    for layer in range(DEPTH):
        y_ctx, k_l, v_l = _block(y_ctx, p, modflat, layer, ctx, None)
        new_k.append(k_l.reshape(nb, ns, N_HEADS, HEAD_DIM))
        new_v.append(v_l.reshape(nb, ns, N_HEADS, HEAD_DIM))
        bias = _bias_table(rpb, layer)
        y_lat, _, _ = _block(y_lat, p, modflat, layer, lat, (cache_k, cache_v, bias))

    return (y_ctx.reshape(nb, ns, d), y_lat.reshape(lb, ls, d),
            jnp.stack(new_k, axis=1), jnp.stack(new_v, axis=1))
```

```python
import functools
import math

import jax
import jax.numpy as jnp
from jax import lax
from jax.experimental import pallas as pl
from jax.experimental.pallas import tpu as pltpu

F32 = jnp.float32
BF16 = jnp.bfloat16

D_MODEL = 4096
DEPTH = 2
GRID_W = 64
HEAD_DIM = 128
N_HEADS = D_MODEL // 256
ATTN_WIDTH = N_HEADS * HEAD_DIM
ATTN_SCALE = HEAD_DIM ** -0.5
WIN_R = 8
WIN_C = 16
NEG_INF = -1e30
POOL_WIDTH = D_MODEL // 2
POOL_SIZES = (2, 4, 8, 16)
POOL_GROUP = POOL_WIDTH // len(POOL_SIZES)
HYENA_WIDTH = D_MODEL // 2
HYENA_ORDER = 2
FILTER_EMB = 33
FILTER_HIDDEN = 64
DECAY_TARGET = 1e-2
FAST_DECAY_PCT = 0.3
SLOW_DECAY_PCT = 1.5
D_FF = ((8 * D_MODEL + 767) // 768) * 256
COL_POOL = 0
COL_Q = POOL_WIDTH
COL_K = COL_Q + ATTN_WIDTH
COL_V = COL_K + ATTN_WIDTH
COL_HY = COL_V + ATTN_WIDTH
COL_GATES = COL_HY + (HYENA_ORDER + 1) * HYENA_WIDTH
MOD_ROWS = 8
RMS_EPS = 1e-6

LANES = 128
VMEM_LIMIT = 56 * 1024 * 1024
SUB_M = 256
CH_TILE = 512
SHORT_SEQ = 512


def _cparams(*sem):
    return pltpu.CompilerParams(dimension_semantics=sem, vmem_limit_bytes=VMEM_LIMIT)


def _row_chunks(tm, sub=SUB_M):
    sub = min(sub, tm)
    return [slice(r, r + sub) for r in range(0, tm, sub)]


def _token_tile(m_tok):
    return 1024 if m_tok % 1024 == 0 else 512


def _mod_kernel(c_ref, w_ref, b_ref, o_ref):
    c = c_ref[...]
    s = (c * jax.nn.sigmoid(c)).astype(BF16)
    o_ref[...] = jnp.dot(s, w_ref[...].astype(BF16), preferred_element_type=F32) + b_ref[...]


def _modulation(cmat, ada_w, ada_b):
    tn = 512
    n6 = ada_w.shape[-1]
    return pl.pallas_call(
        _mod_kernel,
        out_shape=jax.ShapeDtypeStruct((DEPTH, MOD_ROWS, n6), F32),
        grid=(DEPTH, n6 // tn),
        in_specs=[pl.BlockSpec((MOD_ROWS, D_MODEL), lambda l, j: (0, 0)),
                  pl.BlockSpec((None, D_MODEL, tn), lambda l, j: (l, 0, j)),
                  pl.BlockSpec((None, 1, tn), lambda l, j: (l, 0, j))],
        out_specs=pl.BlockSpec((None, MOD_ROWS, tn), lambda l, j: (l, 0, j)),
        compiler_params=_cparams("parallel", "parallel"),
        name="modulation",
    )(cmat, ada_w, ada_b.reshape(DEPTH, 1, n6))


class _Stream:
    def __init__(self, batch, seq, mod_base, per_batch_mod):
        self.batch = batch
        self.seq = seq
        self.mod_base = mod_base
        self.per_batch_mod = per_batch_mod

    def mod_index(self, layer, i, tm, which):
        row = self.mod_base + ((i * tm) // self.seq if self.per_batch_mod else 0)
        return (layer * MOD_ROWS + row) * 6 + which


_STAT_ROWS = 32
_NORM_ROWS = 16


def _normmod_kernel(x_ref, g_ref, sc_ref, sh_ref, o_ref, rstd_ref, gain_ref, shift_ref):
    tm, d = x_ref.shape
    gain_ref[...] = jnp.broadcast_to(g_ref[...] * (1.0 + sc_ref[0]), (_NORM_ROWS, d))
    shift_ref[...] = jnp.broadcast_to(sh_ref[0], (_NORM_ROWS, d))

    def stats(ci, carry):
        base = pl.multiple_of(ci * _STAT_ROWS, _STAT_ROWS)
        x = x_ref[pl.ds(base, _STAT_ROWS), :]
        sq = x * x
        while sq.shape[1] > LANES:
            half = sq.shape[1] // 2
            sq = sq[:, :half] + sq[:, half:]
        rstd_ref[pl.ds(base, _STAT_ROWS), :] = sq
        return carry

    lax.fori_loop(0, tm // _STAT_ROWS, stats, 0)
    rstd = lax.rsqrt(jnp.sum(rstd_ref[...], axis=-1, keepdims=True) * (1.0 / d) + RMS_EPS)
    rstd_ref[...] = jnp.broadcast_to(rstd, (tm, LANES))

    def scale(ci, carry):
        base = pl.multiple_of(ci * _NORM_ROWS, _NORM_ROWS)
        x = x_ref[pl.ds(base, _NORM_ROWS), :]
        rstd = jnp.concatenate([rstd_ref[pl.ds(base, _NORM_ROWS), :]] * (d // LANES), axis=1)
        o_ref[pl.ds(base, _NORM_ROWS), :] = (x * rstd * gain_ref[...] + shift_ref[...]).astype(o_ref.dtype)
        return carry

    lax.fori_loop(0, tm // _NORM_ROWS, scale, 0)


def _normmod(x, g, modflat, st, layer, which_scale, which_shift):
    m_tok, d = x.shape
    tm = 256
    return pl.pallas_call(
        _normmod_kernel,
        out_shape=jax.ShapeDtypeStruct((m_tok, d), BF16),
        grid=(m_tok // tm,),
        in_specs=[pl.BlockSpec((tm, d), lambda i: (i, 0)),
                  pl.BlockSpec((None, 1, d), lambda i: (layer, 0, 0)),
                  pl.BlockSpec((1, 1, d), lambda i: (st.mod_index(layer, i, tm, which_scale), 0, 0)),
                  pl.BlockSpec((1, 1, d), lambda i: (st.mod_index(layer, i, tm, which_shift), 0, 0))],
        out_specs=pl.BlockSpec((tm, d), lambda i: (i, 0)),
        scratch_shapes=[pltpu.VMEM((tm, LANES), F32), pltpu.VMEM((_NORM_ROWS, d), F32),
                        pltpu.VMEM((_NORM_ROWS, d), F32)],
        compiler_params=_cparams("parallel"),
        name="normmod",
    )(x, g.reshape(DEPTH, 1, d), modflat, modflat)


def _bf16_weight(w_ref, wbf_ref):
    if wbf_ref is None:
        return w_ref

    @pl.when(pl.program_id(1) == 0)
    def _():
        wbf_ref[...] = w_ref[...].astype(BF16)

    return wbf_ref


def _lin_plain_kernel(x_ref, w_ref, o_ref, wbf_ref=None):
    w = _bf16_weight(w_ref, wbf_ref)
    for rows in _row_chunks(x_ref.shape[0]):
        acc = jnp.dot(x_ref[rows, :], w[...], preferred_element_type=F32)
        o_ref[rows, :] = acc.astype(o_ref.dtype)


def _lin_sigmoid_kernel(x_ref, w_ref, o_ref, wbf_ref=None):
    w = _bf16_weight(w_ref, wbf_ref)
    for rows in _row_chunks(x_ref.shape[0]):
        acc = jnp.dot(x_ref[rows, :], w[...], preferred_element_type=F32)
        o_ref[rows, :] = jax.nn.sigmoid(acc).astype(o_ref.dtype)


def _lin_rmshead_kernel(x_ref, w_ref, g_ref, o_ref, wbf_ref=None):
    w = _bf16_weight(w_ref, wbf_ref)
    g = g_ref[...]
    for rows in _row_chunks(x_ref.shape[0]):
        acc = jnp.dot(x_ref[rows, :], w[...], preferred_element_type=F32)
        for h in range(acc.shape[1] // HEAD_DIM):
            a = acc[:, h * HEAD_DIM:(h + 1) * HEAD_DIM]
            y = a * lax.rsqrt(jnp.mean(a * a, axis=-1, keepdims=True) + RMS_EPS)
            o_ref[rows, h * HEAD_DIM:(h + 1) * HEAD_DIM] = (y * g).astype(o_ref.dtype)


def _lin_residual_kernel(x_ref, w_ref, res_ref, gate_ref, o_ref, wbf_ref=None):
    w = _bf16_weight(w_ref, wbf_ref)
    for rows in _row_chunks(x_ref.shape[0]):
        acc = jnp.dot(x_ref[rows, :], w[...], preferred_element_type=F32)
        o_ref[rows, :] = res_ref[rows, :] + gate_ref[0] * acc


class _SharedWeights:
    def __init__(self):
        self.slabs = {}

    def get(self, key):
        return self.slabs.get(key)

    def put(self, key, value):
        self.slabs[key] = value


def _weight_plumbing(shared, keys, w_arrays, k_len, ncols, tn, wspec_f32):
    have = [shared.get(key) for key in keys]
    slab_spec = pl.BlockSpec((k_len, tn), lambda n, m: (0, n))
    if all(h is not None for h in have):
        return have, [slab_spec] * len(keys), [], [], True
    slab = jax.ShapeDtypeStruct((k_len, ncols), BF16)
    return list(w_arrays), [wspec_f32] * len(keys), [slab] * len(keys), [slab_spec] * len(keys), False


def _linear(x, w, layer, col_off, ncols, *, tn, kind, out_dtype, shared, key, tn_reuse=None, tm=None,
            tm_reuse=None, k_off=0, k_len=None, head_gain=None, residual=None, modflat=None, st=None,
            which_gate=None):
    m_tok = x.shape[0]
    k_len = x.shape[1] if k_len is None else k_len
    if shared.get(key) is not None:
        tn = tn if tn_reuse is None else tn_reuse
        tm = tm if tm_reuse is None else tm_reuse
    tm = _token_tile(m_tok) if tm is None else tm
    cb = col_off // tn
    kb = k_off // k_len
    wspec = pl.BlockSpec((None, k_len, tn), lambda n, m: (layer, kb, cb + n))
    wargs, wspecs, extra_shapes, extra_specs, reuse = _weight_plumbing(
        shared, [key], [w], k_len, ncols, tn, wspec)
    in_specs = [pl.BlockSpec((tm, k_len), lambda n, m: (m, kb))] + wspecs
    args = [x] + wargs
    if kind == "plain":
        body = _lin_plain_kernel
    elif kind == "sigmoid":
        body = _lin_sigmoid_kernel
    elif kind == "rmshead":
        body = _lin_rmshead_kernel
        in_specs.append(pl.BlockSpec((None, 1, HEAD_DIM), lambda n, m: (layer, 0, 0)))
        args.append(head_gain.reshape(DEPTH, 1, HEAD_DIM))
    else:
        body = _lin_residual_kernel
        in_specs.append(pl.BlockSpec((tm, tn), lambda n, m: (m, n)))
        in_specs.append(pl.BlockSpec(
            (1, 1, tn), lambda n, m: (st.mod_index(layer, m, tm, which_gate), 0, n)))
        args += [residual, modflat]
    outs = pl.pallas_call(
        body,
        out_shape=[jax.ShapeDtypeStruct((m_tok, ncols), out_dtype)] + extra_shapes,
        grid=(ncols // tn, m_tok // tm),
        in_specs=in_specs,
        out_specs=[pl.BlockSpec((tm, tn), lambda n, m: (m, n))] + extra_specs,
        compiler_params=_cparams("parallel", "arbitrary"),
        name="linear_" + kind,
    )(*args)
    if not reuse:
        shared.put(key, outs[1])
    return outs[0]


def _mix_kernel(xp_ref, xa_ref, xh_ref, wp_ref, wa_ref, wh_ref, gp_ref, ga_ref, gh_ref, o_ref,
                wpb=None, wab=None, whb=None):
    wp = _bf16_weight(wp_ref, wpb)
    wa = _bf16_weight(wa_ref, wab)
    wh = _bf16_weight(wh_ref, whb)
    for rows in _row_chunks(xp_ref.shape[0], 2 * SUB_M):
        acc = gp_ref[rows, :].astype(F32) * jnp.dot(xp_ref[rows, :], wp[...], preferred_element_type=F32)
        acc = acc + ga_ref[rows, :].astype(F32) * jnp.dot(xa_ref[rows, :], wa[...], preferred_element_type=F32)
        acc = acc + gh_ref[rows, :].astype(F32) * jnp.dot(xh_ref[rows, :], wh[...], preferred_element_type=F32)
        o_ref[rows, :] = acc.astype(o_ref.dtype)


def _mix(xp, xa, xh, wp, wa, wh, gates, layer, shared):
    m_tok, k = xp.shape
    n = wp.shape[-1]
    tn = 512
    keys = ["w_branch_pool", "w_branch_attn", "w_branch_hyena"]
    wspec = pl.BlockSpec((None, k, tn), lambda j, i: (layer, 0, j))
    wargs, wspecs, extra_shapes, extra_specs, reuse = _weight_plumbing(
        shared, keys, [wp, wa, wh], k, n, tn, wspec)
    tm = _token_tile(m_tok) if reuse else 512
    nb = n // tn
    xspec = pl.BlockSpec((tm, k), lambda j, i: (i, 0))
    outs = pl.pallas_call(
        _mix_kernel,
        out_shape=[jax.ShapeDtypeStruct((m_tok, n), BF16)] + extra_shapes,
        grid=(nb, m_tok // tm),
        in_specs=[xspec, xspec, xspec] + wspecs + [
            pl.BlockSpec((tm, tn), lambda j, i: (i, j)),
            pl.BlockSpec((tm, tn), lambda j, i: (i, nb + j)),
            pl.BlockSpec((tm, tn), lambda j, i: (i, 2 * nb + j))],
        out_specs=[pl.BlockSpec((tm, tn), lambda j, i: (i, j))] + extra_specs,
        compiler_params=_cparams("parallel", "arbitrary"),
        name="mix",
    )(xp, xa, xh, *wargs, gates, gates, gates)
    if not reuse:
        for key, slab in zip(keys, outs[1:]):
            shared.put(key, slab)
    return outs[0]


def _ffn1_kernel(x_ref, wg_ref, wu_ref, o_ref, wgb=None, wub=None):
    wg = _bf16_weight(wg_ref, wgb)
    wu = _bf16_weight(wu_ref, wub)
    for rows in _row_chunks(x_ref.shape[0], 2 * SUB_M):
        x = x_ref[rows, :]
        g = jnp.dot(x, wg[...], preferred_element_type=F32)
        u = jnp.dot(x, wu[...], preferred_element_type=F32)
        o_ref[rows, :] = (g * jax.nn.sigmoid(g) * u).astype(o_ref.dtype)


def _ffn1(x, wg, wu, layer, shared):
    m_tok, k = x.shape
    n = wg.shape[-1]
    tm, tn = _token_tile(m_tok), 256
    keys = ["w_gate", "w_up"]
    wspec = pl.BlockSpec((None, k, tn), lambda j, i: (layer, 0, j))
    wargs, wspecs, extra_shapes, extra_specs, reuse = _weight_plumbing(
        shared, keys, [wg, wu], k, n, tn, wspec)
    outs = pl.pallas_call(
        _ffn1_kernel,
        out_shape=[jax.ShapeDtypeStruct((m_tok, n), BF16)] + extra_shapes,
        grid=(n // tn, m_tok // tm),
        in_specs=[pl.BlockSpec((tm, k), lambda j, i: (i, 0))] + wspecs,
        out_specs=[pl.BlockSpec((tm, tn), lambda j, i: (i, j))] + extra_specs,
        compiler_params=_cparams("parallel", "arbitrary"),
        name="ffn_gate_up",
    )(x, *wargs)
    if not reuse:
        for key, slab in zip(keys, outs[1:]):
            shared.put(key, slab)
    return outs[0]


_POOL_PAD = 16
_ROW_CHUNK = 64


def _halo_chunks(u_ref, seq, rc, pad, emit):
    zeros = jnp.zeros((pad, u_ref.shape[1]), F32)
    emit(0, jnp.concatenate([zeros, u_ref[0:rc + pad, :]], axis=0))

    def chunk(ci, carry):
        base = pl.multiple_of(ci * rc, rc)
        emit(base, u_ref[pl.ds(pl.multiple_of(base - pad, pad), rc + 2 * pad), :])
        return carry

    lax.fori_loop(1, seq // rc - 1, chunk, 0)
    emit(seq - rc, jnp.concatenate([u_ref[seq - rc - pad:seq, :], zeros], axis=0))


def _pool_kernel(u_ref, w_ref, sc_ref, o_ref, d_ref, *, seq):
    g = pl.program_id(1)
    width = u_ref.shape[1]
    rc = _ROW_CHUNK

    for gi, win in enumerate(POOL_SIZES):
        half = win // 2

        @pl.when(g == gi)
        def _(win=win, half=half):
            def emit(base, window):
                lo = _POOL_PAD - half
                acc = window[lo:lo + rc]
                for j in range(1, win):
                    acc = acc + window[lo + j:lo + j + rc]
                t = base + lax.broadcasted_iota(jnp.int32, (rc, width), 0)
                cnt = (jnp.minimum(t + (win - half), seq) - jnp.maximum(t - half, 0)).astype(F32)
                centre = window[_POOL_PAD:_POOL_PAD + rc]
                d_ref[pl.ds(base, rc), :] = (acc / cnt - centre).astype(BF16)

            _halo_chunks(u_ref, seq, rc, _POOL_PAD, emit)

    wbf = w_ref[...].astype(BF16)
    scale = sc_ref[...]
    for rows in _row_chunks(seq):
        y = jnp.dot(d_ref[rows, :], wbf, preferred_element_type=F32) * scale
        o_ref[rows, :] = y.astype(o_ref.dtype)


def _pool(u, pool_w, pool_scale, layer, st):
    seq = st.seq
    return pl.pallas_call(
        functools.partial(_pool_kernel, seq=seq),
        out_shape=jax.ShapeDtypeStruct((st.batch * seq, POOL_WIDTH), BF16),
        grid=(st.batch, len(POOL_SIZES)),
        in_specs=[pl.BlockSpec((seq, POOL_GROUP), lambda b, g: (b, g)),
                  pl.BlockSpec((None, None, POOL_GROUP, POOL_GROUP), lambda b, g: (layer, g, 0, 0)),
                  pl.BlockSpec((None, 1, POOL_GROUP), lambda b, g: (layer, 0, g))],
        out_specs=pl.BlockSpec((seq, POOL_GROUP), lambda b, g: (b, g)),
        scratch_shapes=[pltpu.VMEM((seq, POOL_GROUP), BF16)],
        compiler_params=_cparams("parallel", "parallel"),
        name="pool_mixer",
    )(u, pool_w, pool_scale.reshape(DEPTH, 1, POOL_WIDTH))


def _ctx_attn_kernel(q_ref, k_ref, v_ref, o_ref):
    for h in range(N_HEADS):
        sl = slice(h * HEAD_DIM, (h + 1) * HEAD_DIM)
        q = q_ref[:, sl]
        k = k_ref[:, sl].astype(BF16)
        v = v_ref[:, sl].astype(BF16)
        s = lax.dot_general(q, k, (((1,), (1,)), ((), ())), preferred_element_type=F32) * ATTN_SCALE
        m = jnp.max(s, axis=-1, keepdims=True)
        p = jnp.exp(s - m)
        denom = jnp.sum(p, axis=-1, keepdims=True)
        o = jnp.dot(p.astype(BF16), v, preferred_element_type=F32) / denom
        o_ref[:, sl] = o.astype(o_ref.dtype)


def _ctx_attention(q, k, v, st):
    spec = pl.BlockSpec((st.seq, ATTN_WIDTH), lambda b: (b, 0))
    return pl.pallas_call(
        _ctx_attn_kernel,
        out_shape=jax.ShapeDtypeStruct((st.batch * st.seq, ATTN_WIDTH), BF16),
        grid=(st.batch,),
        in_specs=[spec, spec, spec],
        out_specs=spec,
        compiler_params=_cparams("parallel"),
        name="context_attention",
    )(q, k, v)


_BIAS_TILES = 2 * WIN_R
_Q_GROUP = 8
_BAND = 2 * WIN_R


def _bias_table_kernel(rpb_ref, o_ref, *, layer):
    h = pl.program_id(0)
    shape = (GRID_W, 2 * GRID_W)
    qc = lax.broadcasted_iota(jnp.int32, shape, 0)
    lane = lax.broadcasted_iota(jnp.int32, shape, 1)
    kc = lane % GRID_W
    upper = lane >= GRID_W
    c0 = jnp.clip(qc - WIN_C // 2, 0, GRID_W - WIN_C)
    in_window = (kc >= c0) & (kc < c0 + WIN_C)
    rel = kc - qc + (WIN_C - 1)
    n_dc = 2 * WIN_C - 1
    n_dr = 2 * WIN_R - 1
    row_base = (layer * N_HEADS + h) * n_dr
    for e in range(_BIAS_TILES):
        d_lo = max(e - 1, 0)
        d_hi = min(e, n_dr - 1)
        tile = jnp.zeros(shape, F32)
        for dc in range(n_dc):
            lo = rpb_ref[(row_base + d_lo) * n_dc + dc]
            hi = rpb_ref[(row_base + d_hi) * n_dc + dc]
            tile = jnp.where(rel == dc, jnp.where(upper, hi, lo), tile)
        valid = in_window
        if e == 0:
            valid = valid & upper
        if e == _BIAS_TILES - 1:
            valid = valid & jnp.logical_not(upper)
        o_ref[e] = jnp.where(valid, tile, NEG_INF)


def _bias_table(rpb, layer):
    return pl.pallas_call(
        functools.partial(_bias_table_kernel, layer=layer),
        out_shape=jax.ShapeDtypeStruct((N_HEADS, _BIAS_TILES, GRID_W, 2 * GRID_W), F32),
        grid=(N_HEADS,),
        in_specs=[pl.BlockSpec(memory_space=pltpu.SMEM)],
        out_specs=pl.BlockSpec((None, _BIAS_TILES, GRID_W, 2 * GRID_W), lambda h: (h, 0, 0, 0)),
        compiler_params=_cparams("parallel"),
        name="rel_pos_bias_table",
    )(rpb.reshape(-1))


def _nbr_attn_kernel(q_ref, k_ref, v_ref, kc_ref, vc_ref, bias_ref, o_ref,
                     kbf, vbf, s_ref, sc_ref, p_ref, pc_ref, *, rows):
    kbf[...] = k_ref[...].astype(BF16)
    vbf[...] = v_ref[...].astype(BF16)
    kc = kc_ref[...].astype(BF16)
    vc = vc_ref[...].astype(BF16)
    dims = (((1,), (1,)), ((), ()))
    pair = 2 * GRID_W
    lower = lax.broadcasted_iota(jnp.int32, (GRID_W, pair), 1) < GRID_W
    zero_tile = jnp.zeros((GRID_W, pair), BF16)

    for g in range(rows // _Q_GROUP):
        slot = g % 2
        ks = min(max(g * _Q_GROUP - WIN_R // 2, 0), rows - _BAND)
        qsl = slice(g * _Q_GROUP * GRID_W, (g + 1) * _Q_GROUP * GRID_W)
        ksl = slice(ks * GRID_W, (ks + _BAND) * GRID_W)
        qg = q_ref[qsl, :]
        s_ref[slot] = lax.dot_general(qg, kbf[ksl, :], dims, preferred_element_type=F32) * ATTN_SCALE
        sc_ref[slot] = lax.dot_general(qg, kc, dims, preferred_element_type=F32) * ATTN_SCALE
        for i in range(_Q_GROUP):
            qr = g * _Q_GROUP + i
            r0 = min(max(qr - WIN_R // 2, 0), rows - WIN_R)
            first = r0 - ks
            j0, j1 = first // 2, (first + WIN_R + 1) // 2
            rsl = slice(i * GRID_W, (i + 1) * GRID_W)
            tiles = []
            for j in range(j0, j1):
                kr0 = ks + 2 * j
                t = s_ref[slot, rsl, j * pair:(j + 1) * pair] + bias_ref[kr0 - qr + WIN_R]
                if kr0 < r0:
                    t = jnp.where(lower, NEG_INF, t)
                if kr0 + 1 >= r0 + WIN_R:
                    t = jnp.where(lower, t, NEG_INF)
                tiles.append(t)
            sc = sc_ref[slot, rsl, :]
            mt = functools.reduce(jnp.maximum, tiles)
            m = jnp.maximum(jnp.max(mt, axis=-1, keepdims=True), jnp.max(sc, axis=-1, keepdims=True))
            ps = [jnp.exp(t - m) for t in tiles]
            pc = jnp.exp(sc - m)
            denom = (jnp.sum(functools.reduce(jnp.add, ps), axis=-1, keepdims=True)
                     + jnp.sum(pc, axis=-1, keepdims=True))
            inv = 1.0 / denom
            for j in range(_BAND // 2):
                val = (ps[j - j0] * inv).astype(BF16) if j0 <= j < j1 else zero_tile
                p_ref[slot, rsl, j * pair:(j + 1) * pair] = val
            pc_ref[slot, rsl, :] = (pc * inv).astype(BF16)
        o = (jnp.dot(p_ref[slot], vbf[ksl, :], preferred_element_type=F32)
             + jnp.dot(pc_ref[slot], vc, preferred_element_type=F32))
        o_ref[qsl, :] = o.astype(o_ref.dtype)


def _nbr_attention(q, k, v, cache_k, cache_v, bias, layer, st):
    seq = st.seq
    rows = seq // GRID_W
    assert rows >= _BAND and rows % _Q_GROUP == 0
    past = cache_k.shape[2]
    ck = cache_k.reshape(st.batch, DEPTH, past, ATTN_WIDTH)
    cv = cache_v.reshape(st.batch, DEPTH, past, ATTN_WIDTH)
    tok_spec = pl.BlockSpec((seq, HEAD_DIM), lambda b, h: (b, h))
    cache_spec = pl.BlockSpec((None, None, past, HEAD_DIM), lambda b, h: (b, layer, 0, h))
    nq = _Q_GROUP * GRID_W
    nk = _BAND * GRID_W
    return pl.pallas_call(
        functools.partial(_nbr_attn_kernel, rows=rows),
        out_shape=jax.ShapeDtypeStruct((st.batch * seq, ATTN_WIDTH), BF16),
        grid=(st.batch, N_HEADS),
        in_specs=[tok_spec, tok_spec, tok_spec, cache_spec, cache_spec,
                  pl.BlockSpec((None, _BIAS_TILES, GRID_W, 2 * GRID_W), lambda b, h: (h, 0, 0, 0))],
        out_specs=tok_spec,
        scratch_shapes=[pltpu.VMEM((seq, HEAD_DIM), BF16), pltpu.VMEM((seq, HEAD_DIM), BF16),
                        pltpu.VMEM((2, nq, nk), F32), pltpu.VMEM((2, nq, past), F32),
                        pltpu.VMEM((2, nq, nk), BF16), pltpu.VMEM((2, nq, past), BF16)],
        compiler_params=_cparams("parallel", "parallel"),
        name="neighbourhood_attention",
    )(q, k, v, ck, cv, bias)


_CONV_PAD = 8


def _short_conv_kernel(u_ref, w_ref, b_ref, o_ref, *, seq):
    width = u_ref.shape[1]
    w0 = w_ref[0:1, :]
    w1 = w_ref[1:2, :]
    w2 = w_ref[2:3, :]
    b = b_ref[...]
    rc = max(8, _ROW_CHUNK * CH_TILE // width)

    def emit(base, window):
        prev = window[_CONV_PAD - 1:_CONV_PAD - 1 + rc]
        cur = window[_CONV_PAD:_CONV_PAD + rc]
        nxt = window[_CONV_PAD + 1:_CONV_PAD + 1 + rc]
        o_ref[pl.ds(base, rc), :] = prev * w0 + cur * w1 + nxt * w2 + b

    _halo_chunks(u_ref, seq, rc, _CONV_PAD, emit)


def _short_conv(u, conv_w, conv_b, layer, st):
    width = u.shape[1]
    seq = st.seq
    tc = HYENA_WIDTH if seq <= SHORT_SEQ else CH_TILE
    return pl.pallas_call(
        functools.partial(_short_conv_kernel, seq=seq),
        out_shape=jax.ShapeDtypeStruct((st.batch * seq, width), F32),
        grid=(st.batch, width // tc),
        in_specs=[pl.BlockSpec((seq, tc), lambda b, j: (b, j)),
                  pl.BlockSpec((None, 3, tc), lambda b, j: (layer, 0, j)),
                  pl.BlockSpec((None, 1, tc), lambda b, j: (layer, 0, j))],
        out_specs=pl.BlockSpec((seq, tc), lambda b, j: (b, j)),
        compiler_params=_cparams("parallel", "parallel"),
        name="hyena_short_conv",
    )(u, conv_w, conv_b.reshape(DEPTH, 1, width))


def _dft_matrices(seq):
    f = jnp.arange(seq, dtype=jnp.int32)[:, None]
    t = jnp.arange(seq, dtype=jnp.int32)[None, :]
    ang = ((f * t) % (2 * seq)).astype(F32) * (math.pi / seq)
    top = jnp.cos(ang)
    bot = jnp.where(f == 0, (1 - 2 * (t % 2)).astype(F32), -jnp.sin(ang))
    colscale = jnp.where(jnp.arange(seq) == 0, 0.5, 1.0).astype(F32)[None, :] / seq
    inv = jnp.concatenate([top.T * colscale, bot.T * colscale], axis=1)
    return top.astype(BF16), bot.astype(BF16), inv.astype(BF16)


def _filter_mlp_kernel(z_ref, w1_ref, b1_ref, w2_ref, b2_ref, fr_ref, o_ref):
    fr = fr_ref[...]
    h = jnp.sin(fr * (jnp.dot(z_ref[...], w1_ref[...], preferred_element_type=F32) + b1_ref[...]))
    o_ref[...] = jnp.sin(fr * (jnp.dot(h, w2_ref[...], preferred_element_type=F32) + b2_ref[...]))


def _filter_mlp(seq, w1, b1, w2, b2, freq, layer):
    t = jnp.linspace(0.0, 1.0, seq, dtype=F32)[:, None]
    bands = (FILTER_EMB - 1) // 2
    ang = (2.0 * math.pi / seq) * jnp.arange(seq, dtype=F32)[:, None] * jnp.linspace(
        1e-4, bands - 1, bands, dtype=F32)[None]
    z = jnp.concatenate([t, jnp.cos(ang), -jnp.sin(ang)], axis=-1)
    z = jnp.pad(z, ((0, 0), (0, LANES - FILTER_EMB)))
    w1p = jnp.pad(w1, ((0, 0), (0, LANES - FILTER_EMB), (0, 0)))
    vec = lambda a: a.reshape(DEPTH, 1, FILTER_HIDDEN)
    vspec = pl.BlockSpec((None, 1, FILTER_HIDDEN), lambda i: (layer, 0, 0))
    return pl.pallas_call(
        _filter_mlp_kernel,
        out_shape=jax.ShapeDtypeStruct((seq, FILTER_HIDDEN), F32),
        grid=(1,),
        in_specs=[pl.BlockSpec((seq, LANES), lambda i: (0, 0)),
                  pl.BlockSpec((None, LANES, FILTER_HIDDEN), lambda i: (layer, 0, 0)),
                  vspec,
                  pl.BlockSpec((None, FILTER_HIDDEN, FILTER_HIDDEN), lambda i: (layer, 0, 0)),
                  vspec, vspec],
        out_specs=pl.BlockSpec((seq, FILTER_HIDDEN), lambda i: (0, 0)),
        compiler_params=_cparams("arbitrary"),
        name="hyena_filter_mlp",
    )(z, w1p, vec(b1), w2, vec(b2), vec(freq))


def _filter_taps_kernel(h_ref, dl_ref, wf_ref, wb_ref, a_ref, d_ref, nyq_ref, *, seq):
    h = h_ref[...]
    shape = (seq, dl_ref.shape[1])
    row = lax.broadcasted_iota(jnp.int32, shape, 0)
    t = row.astype(F32) / float(seq - 1)
    dec = jnp.exp(-t * dl_ref[...])
    fwd = jnp.dot(h, wf_ref[...], preferred_element_type=F32) * dec
    bwd = jnp.dot(h, wb_ref[...], preferred_element_type=F32) * dec
    bwd = jnp.where(row == 0, 0.0, bwd)
    inv = 1.0 / jnp.sum(jnp.abs(fwd) + jnp.abs(bwd), axis=0, keepdims=True)
    a = (fwd + bwd) * inv
    a_ref[...] = a
    d_ref[...] = (fwd - bwd) * inv
    sign = jnp.where(row % 2 == 0, 1.0, -1.0)
    nyq_ref[...] = jnp.sum(a * sign, axis=0, keepdims=True)


def _filter_taps(hidden, w3, layer, *, seq):
    tc = 256
    nct = HYENA_WIDTH // tc
    deltas = jnp.abs(jnp.linspace(math.log(DECAY_TARGET) / SLOW_DECAY_PCT,
                                  math.log(DECAY_TARGET) / FAST_DECAY_PCT, HYENA_WIDTH, dtype=F32))
    tap_shape = jax.ShapeDtypeStruct((HYENA_ORDER, seq, HYENA_WIDTH), F32)
    tap_spec = pl.BlockSpec((None, seq, tc), lambda o, j: (o, 0, j))
    return pl.pallas_call(
        functools.partial(_filter_taps_kernel, seq=seq),
        out_shape=(tap_shape, tap_shape, jax.ShapeDtypeStruct((HYENA_ORDER, 1, HYENA_WIDTH), F32)),
        grid=(HYENA_ORDER, nct),
        in_specs=[pl.BlockSpec((seq, FILTER_HIDDEN), lambda o, j: (0, 0)),
                  pl.BlockSpec((1, tc), lambda o, j: (0, j)),
                  pl.BlockSpec((None, FILTER_HIDDEN, tc), lambda o, j: (layer, 0, (2 * o) * nct + j)),
                  pl.BlockSpec((None, FILTER_HIDDEN, tc), lambda o, j: (layer, 0, (2 * o + 1) * nct + j))],
        out_specs=(tap_spec, tap_spec, pl.BlockSpec((None, 1, tc), lambda o, j: (o, 0, j))),
        compiler_params=_cparams("parallel", "parallel"),
        name="hyena_filter_taps",
    )(hidden, deltas.reshape(1, HYENA_WIDTH), w3, w3)


def _filter_spec_kernel(wt_ref, wb_ref, a_ref, d_ref, nyq_ref, o_ref, abf, dbf):
    i = pl.program_id(2)

    @pl.when(i == 0)
    def _():
        abf[...] = a_ref[...].astype(BF16)
        dbf[...] = d_ref[...].astype(BF16)

    for rows in _row_chunks(wt_ref.shape[0]):
        top = jnp.dot(wt_ref[rows, :], abf[...], preferred_element_type=F32)
        bot = jnp.dot(wb_ref[rows, :], dbf[...], preferred_element_type=F32)
        if rows.start == 0:
            row = lax.broadcasted_iota(jnp.int32, bot.shape, 0)
            bot = jnp.where((row == 0) & (i == 0), nyq_ref[...], bot)
        o_ref[0, rows, :] = top
        o_ref[1, rows, :] = bot


def _dft_tiles(seq):
    return seq, (HYENA_WIDTH if seq <= SHORT_SEQ else CH_TILE)


_RESIDENT = pl.Buffered(1)


def _filter_spectrum(dft_top, dft_bot, taps_a, taps_d, nyq, *, seq):
    tf, tc = _dft_tiles(seq)
    tc = min(tc, CH_TILE // 2)
    wspec = pl.BlockSpec((tf, seq), lambda o, c, i: (i, 0), pipeline_mode=_RESIDENT)
    tspec = pl.BlockSpec((None, seq, tc), lambda o, c, i: (o, 0, c))
    return pl.pallas_call(
        _filter_spec_kernel,
        out_shape=jax.ShapeDtypeStruct((HYENA_ORDER, 2, seq, HYENA_WIDTH), F32),
        grid=(HYENA_ORDER, HYENA_WIDTH // tc, seq // tf),
        in_specs=[wspec, wspec, tspec, tspec,
                  pl.BlockSpec((None, 1, tc), lambda o, c, i: (o, 0, c))],
        out_specs=pl.BlockSpec((None, 2, tf, tc), lambda o, c, i: (o, 0, i, c)),
        scratch_shapes=[pltpu.VMEM((seq, tc), BF16)] * 2,
        compiler_params=_cparams("parallel", "parallel", "arbitrary"),
        name="hyena_filter_spectrum",
    )(dft_top, dft_bot, taps_a, taps_d, nyq)


def _fwd_dft_kernel(wt_ref, wb_ref, z_ref, p_ref, o_ref, zbf):
    i = pl.program_id(2)

    @pl.when(i == 0)
    def _():
        zbf[...] = z_ref[...].astype(BF16)

    for rows in _row_chunks(wt_ref.shape[0]):
        top = jnp.dot(wt_ref[rows, :], zbf[...], preferred_element_type=F32)
        bot = jnp.dot(wb_ref[rows, :], zbf[...], preferred_element_type=F32)
        pt = p_ref[0, rows, :]
        pb = p_ref[1, rows, :]
        re = top * pt - bot * pb
        im = top * pb + bot * pt
        if rows.start == 0:
            row = lax.broadcasted_iota(jnp.int32, top.shape, 0)
            real_row = (row == 0) & (i == 0)
            re = jnp.where(real_row, top * pt, re)
            im = jnp.where(real_row, bot * pb, im)
        o_ref[0, rows, :] = re.astype(o_ref.dtype)
        o_ref[1, rows, :] = im.astype(o_ref.dtype)


def _fwd_dft(dft_top, dft_bot, z, z_col_off, spec, order, *, batch, seq):
    tf, tc = _dft_tiles(seq)
    zc = z_col_off // tc
    wspec = pl.BlockSpec((tf, seq), lambda c, b, i: (i, 0), pipeline_mode=_RESIDENT)
    return pl.pallas_call(
        _fwd_dft_kernel,
        out_shape=jax.ShapeDtypeStruct((batch, 2, seq, HYENA_WIDTH), BF16),
        grid=(HYENA_WIDTH // tc, batch, seq // tf),
        in_specs=[wspec, wspec,
                  pl.BlockSpec((seq, tc), lambda c, b, i: (b, zc + c)),
                  pl.BlockSpec((None, 2, tf, tc), lambda c, b, i: (order, 0, i, c), pipeline_mode=_RESIDENT)],
        out_specs=pl.BlockSpec((None, 2, tf, tc), lambda c, b, i: (b, 0, i, c)),
        scratch_shapes=[pltpu.VMEM((seq, tc), BF16)],
        compiler_params=_cparams("parallel", "parallel", "arbitrary"),
        name="hyena_fwd_dft",
    )(dft_top, dft_bot, z, spec)


def _inv_dft_kernel(wi_ref, s_ref, gate_ref, z_ref, bias_ref, o_ref):
    s = s_ref[...]
    s2 = s.reshape(s.shape[0] * s.shape[1], s.shape[2])
    for rows in _row_chunks(wi_ref.shape[0]):
        conv = jnp.dot(wi_ref[rows, :], s2, preferred_element_type=F32)
        o_ref[rows, :] = (gate_ref[rows, :] * (conv + bias_ref[...] * z_ref[rows, :])).astype(o_ref.dtype)


def _inv_dft(dft_inv, s, gate, gate_col_off, z, z_col_off, hy_bias, layer, order, out_dtype, *, batch, seq):
    tl, tc = _dft_tiles(seq)
    nrb = seq // tl
    gc = gate_col_off // tc
    zc = z_col_off // tc
    return pl.pallas_call(
        _inv_dft_kernel,
        out_shape=jax.ShapeDtypeStruct((batch * seq, HYENA_WIDTH), out_dtype),
        grid=(HYENA_WIDTH // tc, batch, nrb),
        in_specs=[pl.BlockSpec((tl, 2 * seq), lambda c, b, i: (i, 0), pipeline_mode=_RESIDENT),
                  pl.BlockSpec((None, 2, seq, tc), lambda c, b, i: (b, 0, 0, c)),
                  pl.BlockSpec((tl, tc), lambda c, b, i: (b * nrb + i, gc + c)),
                  pl.BlockSpec((tl, tc), lambda c, b, i: (b * nrb + i, zc + c)),
                  pl.BlockSpec((None, 1, tc), lambda c, b, i: (layer * HYENA_ORDER + order, 0, c))],
        out_specs=pl.BlockSpec((tl, tc), lambda c, b, i: (b * nrb + i, c)),
        compiler_params=_cparams("parallel", "parallel", "arbitrary"),
        name="hyena_inv_dft",
    )(dft_inv, s, gate, z, hy_bias.reshape(DEPTH * HYENA_ORDER, 1, HYENA_WIDTH))


def _hyena(u, p, layer, st):
    batch, seq = st.batch, st.seq
    uc = _short_conv(u, p["hy_conv_w"], p["hy_conv_b"], layer, st)
    dft_top, dft_bot, dft_inv = _dft_matrices(seq)
    hidden = _filter_mlp(seq, p["filt_w1"], p["filt_b1"], p["filt_w2"], p["filt_b2"], p["filt_freq"], layer)
    taps_a, taps_d, nyq = _filter_taps(hidden, p["filt_w3"], layer, seq=seq)
    spec = _filter_spectrum(dft_top, dft_bot, taps_a, taps_d, nyq, seq=seq)
    s = _fwd_dft(dft_top, dft_bot, uc, 0, spec, 0, batch=batch, seq=seq)
    z1 = _inv_dft(dft_inv, s, uc, HYENA_WIDTH, uc, 0, p["hy_bias"], layer, 0, F32, batch=batch, seq=seq)
    s = _fwd_dft(dft_top, dft_bot, z1, 0, spec, 1, batch=batch, seq=seq)
    return _inv_dft(dft_inv, s, uc, 2 * HYENA_WIDTH, z1, 0, p["hy_bias"], layer, 1, BF16, batch=batch, seq=seq)


def _block(x, p, modflat, layer, st, ctx_kv, shared):
    h = _normmod(x, p["norm1_g"], modflat, st, layer, 1, 0)
    lin = functools.partial(_linear, h, p["w_in"], layer, tn=512, tn_reuse=1024, shared=shared)
    u_pool = lin(COL_POOL, POOL_WIDTH, kind="plain", out_dtype=F32, key="w_in:pool")
    q = lin(COL_Q, ATTN_WIDTH, kind="rmshead", out_dtype=BF16, head_gain=p["q_norm_g"], key="w_in:q")
    k = lin(COL_K, ATTN_WIDTH, kind="rmshead", out_dtype=F32, head_gain=p["k_norm_g"], key="w_in:k")
    v = lin(COL_V, ATTN_WIDTH, kind="plain", out_dtype=F32, key="w_in:v")
    u_hy = lin(COL_HY, (HYENA_ORDER + 1) * HYENA_WIDTH, kind="plain", out_dtype=F32, key="w_in:hyena")
    gates = lin(COL_GATES, 3 * D_MODEL, kind="sigmoid", out_dtype=BF16, key="w_in:gates")

    pool = _pool(u_pool, p["pool_w"], p["pool_scale"], layer, st)
    if ctx_kv is None:
        attn = _ctx_attention(q, k, v, st)
    else:
        attn = _nbr_attention(q, k, v, ctx_kv[0], ctx_kv[1], ctx_kv[2], layer, st)
    hy = _hyena(u_hy, p, layer, st)

    mixed = _mix(pool, attn, hy, p["w_branch_pool"], p["w_branch_attn"], p["w_branch_hyena"], gates, layer,
                 shared)
    x = _linear(mixed, p["w_out"], layer, 0, D_MODEL, tn=512, tn_reuse=1024, kind="residual", out_dtype=F32,
                shared=shared, key="w_out", residual=x, modflat=modflat, st=st, which_gate=2)
    h = _normmod(x, p["norm2_g"], modflat, st, layer, 4, 3)
    a = _ffn1(h, p["w_gate"], p["w_up"], layer, shared)
    half = D_FF // 2
    for k_off in (0, half):
        x = _linear(a, p["w_down"], layer, 0, D_MODEL, tn=512, tm=512, tm_reuse=1024, kind="residual",
                    out_dtype=F32, shared=shared, key="w_down:%d" % k_off, k_off=k_off, k_len=half,
                    residual=x, modflat=modflat, st=st, which_gate=5)
    return x, k, v


_WEIGHT_NAMES = ("ada_w", "ada_b", "norm1_g", "norm2_g", "w_in", "pool_w", "pool_scale",
                 "q_norm_g", "k_norm_g", "rpb", "hy_conv_w", "hy_conv_b", "filt_w1", "filt_b1",
                 "filt_w2", "filt_b2", "filt_w3", "filt_freq", "hy_bias", "w_branch_pool",
                 "w_branch_attn", "w_branch_hyena", "w_out", "w_gate", "w_up", "w_down")


def kernel(x_prompt, x_sample, cache_k, cache_v, c, c_ctx, ada_w, ada_b, norm1_g, norm2_g, w_in, pool_w, pool_scale, q_norm_g, k_norm_g, rpb, hy_conv_w, hy_conv_b, filt_w1, filt_b1, filt_w2, filt_b2, filt_w3, filt_freq, hy_bias, w_branch_pool, w_branch_attn, w_branch_hyena, w_out, w_gate, w_up, w_down):
    p = dict(zip(_WEIGHT_NAMES, (ada_w, ada_b, norm1_g, norm2_g, w_in, pool_w, pool_scale, q_norm_g,
                                 k_norm_g, rpb, hy_conv_w, hy_conv_b, filt_w1, filt_b1, filt_w2, filt_b2,
                                 filt_w3, filt_freq, hy_bias, w_branch_pool, w_branch_attn,
                                 w_branch_hyena, w_out, w_gate, w_up, w_down)))
    nb, ns, d = x_prompt.shape
    lb, ls, _ = x_sample.shape
    assert d == D_MODEL and lb + 1 <= MOD_ROWS
    ctx = _Stream(nb, ns, 0, False)
    lat = _Stream(lb, ls, 1, True)

    cmat = jnp.concatenate([c_ctx[None, :], c, jnp.zeros((MOD_ROWS - 1 - lb, d), F32)], axis=0)
    mod = _modulation(cmat, ada_w, ada_b)
    modflat = mod.reshape(DEPTH * MOD_ROWS * 6, 1, d)

    y_ctx = x_prompt.reshape(nb * ns, d)
    y_lat = x_sample.reshape(lb * ls, d)
    new_k, new_v = [], []
    for layer in range(DEPTH):
        shared = _SharedWeights()
        bias = _bias_table(rpb, layer)
        y_lat, _, _ = _block(y_lat, p, modflat, layer, lat, (cache_k, cache_v, bias), shared)
        y_ctx, k_l, v_l = _block(y_ctx, p, modflat, layer, ctx, None, shared)
        new_k.append(k_l.reshape(nb, ns, N_HEADS, HEAD_DIM))
        new_v.append(v_l.reshape(nb, ns, N_HEADS, HEAD_DIM))

    return (y_ctx.reshape(nb, ns, d), y_lat.reshape(lb, ls, d),
            jnp.stack(new_k, axis=1), jnp.stack(new_v, axis=1))
```

```python
import functools
import math

import jax
import jax.numpy as jnp
from jax import lax
from jax.experimental import pallas as pl
from jax.experimental.pallas import tpu as pltpu

F32 = jnp.float32
BF16 = jnp.bfloat16

D_MODEL = 4096
DEPTH = 2
GRID_W = 64
HEAD_DIM = 128
N_HEADS = D_MODEL // 256
ATTN_WIDTH = N_HEADS * HEAD_DIM
ATTN_SCALE = HEAD_DIM ** -0.5
WIN_R = 8
WIN_C = 16
NEG_INF = -1e30
POOL_WIDTH = D_MODEL // 2
POOL_SIZES = (2, 4, 8, 16)
POOL_GROUP = POOL_WIDTH // len(POOL_SIZES)
HYENA_WIDTH = D_MODEL // 2
HYENA_ORDER = 2
FILTER_EMB = 33
FILTER_HIDDEN = 64
DECAY_TARGET = 1e-2
FAST_DECAY_PCT = 0.3
SLOW_DECAY_PCT = 1.5
D_FF = ((8 * D_MODEL + 767) // 768) * 256
COL_POOL = 0
COL_Q = POOL_WIDTH
COL_K = COL_Q + ATTN_WIDTH
COL_V = COL_K + ATTN_WIDTH
COL_HY = COL_V + ATTN_WIDTH
COL_GATES = COL_HY + (HYENA_ORDER + 1) * HYENA_WIDTH
MOD_ROWS = 8
RMS_EPS = 1e-6

LANES = 128
VMEM_LIMIT = 56 * 1024 * 1024
SUB_M = 256
CH_TILE = 512
SHORT_SEQ = 512


def _cparams(*sem):
    return pltpu.CompilerParams(dimension_semantics=sem, vmem_limit_bytes=VMEM_LIMIT)


def _row_chunks(tm, sub=SUB_M):
    sub = min(sub, tm)
    return [slice(r, r + sub) for r in range(0, tm, sub)]


def _token_tile(m_tok):
    return 1024 if m_tok % 1024 == 0 else 512


def _mod_kernel(c_ref, w_ref, b_ref, o_ref):
    c = c_ref[...]
    s = (c * jax.nn.sigmoid(c)).astype(BF16)
    o_ref[...] = jnp.dot(s, w_ref[...].astype(BF16), preferred_element_type=F32) + b_ref[...]


def _modulation(cmat, ada_w, ada_b):
    tn = 512
    n6 = ada_w.shape[-1]
    return pl.pallas_call(
        _mod_kernel,
        out_shape=jax.ShapeDtypeStruct((DEPTH, MOD_ROWS, n6), F32),
        grid=(DEPTH, n6 // tn),
        in_specs=[pl.BlockSpec((MOD_ROWS, D_MODEL), lambda l, j: (0, 0)),
                  pl.BlockSpec((None, D_MODEL, tn), lambda l, j: (l, 0, j)),
                  pl.BlockSpec((None, 1, tn), lambda l, j: (l, 0, j))],
        out_specs=pl.BlockSpec((None, MOD_ROWS, tn), lambda l, j: (l, 0, j)),
        compiler_params=_cparams("parallel", "parallel"),
        name="modulation",
    )(cmat, ada_w, ada_b.reshape(DEPTH, 1, n6))


class _Stream:
    def __init__(self, batch, seq, mod_base, per_batch_mod):
        self.batch = batch
        self.seq = seq
        self.mod_base = mod_base
        self.per_batch_mod = per_batch_mod

    def mod_index(self, layer, i, tm, which):
        row = self.mod_base + ((i * tm) // self.seq if self.per_batch_mod else 0)
        return (layer * MOD_ROWS + row) * 6 + which


_STAT_ROWS = 32
_NORM_ROWS = 16


def _normmod_kernel(x_ref, g_ref, sc_ref, sh_ref, o_ref, rstd_ref, gain_ref, shift_ref):
    tm, d = x_ref.shape
    gain_ref[...] = jnp.broadcast_to(g_ref[...] * (1.0 + sc_ref[0]), (_NORM_ROWS, d))
    shift_ref[...] = jnp.broadcast_to(sh_ref[0], (_NORM_ROWS, d))

    def stats(ci, carry):
        base = pl.multiple_of(ci * _STAT_ROWS, _STAT_ROWS)
        x = x_ref[pl.ds(base, _STAT_ROWS), :]
        sq = x * x
        while sq.shape[1] > LANES:
            half = sq.shape[1] // 2
            sq = sq[:, :half] + sq[:, half:]
        rstd_ref[pl.ds(base, _STAT_ROWS), :] = sq
        return carry

    lax.fori_loop(0, tm // _STAT_ROWS, stats, 0)
    rstd = lax.rsqrt(jnp.sum(rstd_ref[...], axis=-1, keepdims=True) * (1.0 / d) + RMS_EPS)
    rstd_ref[...] = jnp.broadcast_to(rstd, (tm, LANES))

    def scale(ci, carry):
        base = pl.multiple_of(ci * _NORM_ROWS, _NORM_ROWS)
        x = x_ref[pl.ds(base, _NORM_ROWS), :]
        rstd = jnp.concatenate([rstd_ref[pl.ds(base, _NORM_ROWS), :]] * (d // LANES), axis=1)
        o_ref[pl.ds(base, _NORM_ROWS), :] = (x * rstd * gain_ref[...] + shift_ref[...]).astype(o_ref.dtype)
        return carry

    lax.fori_loop(0, tm // _NORM_ROWS, scale, 0)


def _normmod(x, g, modflat, st, layer, which_scale, which_shift):
    m_tok, d = x.shape
    tm = 256
    return pl.pallas_call(
        _normmod_kernel,
        out_shape=jax.ShapeDtypeStruct((m_tok, d), BF16),
        grid=(m_tok // tm,),
        in_specs=[pl.BlockSpec((tm, d), lambda i: (i, 0)),
                  pl.BlockSpec((None, 1, d), lambda i: (layer, 0, 0)),
                  pl.BlockSpec((1, 1, d), lambda i: (st.mod_index(layer, i, tm, which_scale), 0, 0)),
                  pl.BlockSpec((1, 1, d), lambda i: (st.mod_index(layer, i, tm, which_shift), 0, 0))],
        out_specs=pl.BlockSpec((tm, d), lambda i: (i, 0)),
        scratch_shapes=[pltpu.VMEM((tm, LANES), F32), pltpu.VMEM((_NORM_ROWS, d), F32),
                        pltpu.VMEM((_NORM_ROWS, d), F32)],
        compiler_params=_cparams("parallel"),
        name="normmod",
    )(x, g.reshape(DEPTH, 1, d), modflat, modflat)


def _bf16_weight(w_ref, wbf_ref):
    if wbf_ref is None:
        return w_ref

    @pl.when(pl.program_id(1) == 0)
    def _():
        wbf_ref[...] = w_ref[...].astype(BF16)

    return wbf_ref


def _lin_plain_kernel(x_ref, w_ref, o_ref, wbf_ref=None):
    w = _bf16_weight(w_ref, wbf_ref)
    for rows in _row_chunks(x_ref.shape[0]):
        acc = jnp.dot(x_ref[rows, :], w[...], preferred_element_type=F32)
        o_ref[rows, :] = acc.astype(o_ref.dtype)


def _lin_sigmoid_kernel(x_ref, w_ref, o_ref, wbf_ref=None):
    w = _bf16_weight(w_ref, wbf_ref)
    for rows in _row_chunks(x_ref.shape[0]):
        acc = jnp.dot(x_ref[rows, :], w[...], preferred_element_type=F32)
        o_ref[rows, :] = jax.nn.sigmoid(acc).astype(o_ref.dtype)


def _lin_rmshead_kernel(x_ref, w_ref, g_ref, o_ref, wbf_ref=None):
    w = _bf16_weight(w_ref, wbf_ref)
    g = g_ref[...]
    for rows in _row_chunks(x_ref.shape[0]):
        acc = jnp.dot(x_ref[rows, :], w[...], preferred_element_type=F32)
        for h in range(acc.shape[1] // HEAD_DIM):
            a = acc[:, h * HEAD_DIM:(h + 1) * HEAD_DIM]
            y = a * lax.rsqrt(jnp.mean(a * a, axis=-1, keepdims=True) + RMS_EPS)
            o_ref[rows, h * HEAD_DIM:(h + 1) * HEAD_DIM] = (y * g).astype(o_ref.dtype)


def _lin_residual_kernel(x_ref, w_ref, res_ref, gate_ref, o_ref, wbf_ref=None):
    w = _bf16_weight(w_ref, wbf_ref)
    for rows in _row_chunks(x_ref.shape[0]):
        acc = jnp.dot(x_ref[rows, :], w[...], preferred_element_type=F32)
        o_ref[rows, :] = res_ref[rows, :] + gate_ref[0] * acc


class _SharedWeights:
    def __init__(self):
        self.slabs = {}

    def get(self, key):
        return self.slabs.get(key)

    def put(self, key, value):
        self.slabs[key] = value


def _weight_plumbing(shared, keys, w_arrays, k_len, ncols, tn, wspec_f32):
    have = [shared.get(key) for key in keys]
    slab_spec = pl.BlockSpec((k_len, tn), lambda n, m: (0, n))
    if all(h is not None for h in have):
        return have, [slab_spec] * len(keys), [], [], True
    slab = jax.ShapeDtypeStruct((k_len, ncols), BF16)
    return list(w_arrays), [wspec_f32] * len(keys), [slab] * len(keys), [slab_spec] * len(keys), False


def _linear(x, w, layer, col_off, ncols, *, tn, kind, out_dtype, shared, key, tn_reuse=None, tm=None,
            tm_reuse=None, k_off=0, k_len=None, head_gain=None, residual=None, modflat=None, st=None,
            which_gate=None):
    m_tok = x.shape[0]
    k_len = x.shape[1] if k_len is None else k_len
    if shared.get(key) is not None:
        tn = tn if tn_reuse is None else tn_reuse
        tm = tm if tm_reuse is None else tm_reuse
    tm = _token_tile(m_tok) if tm is None else tm
    cb = col_off // tn
    kb = k_off // k_len
    wspec = pl.BlockSpec((None, k_len, tn), lambda n, m: (layer, kb, cb + n))
    wargs, wspecs, extra_shapes, extra_specs, reuse = _weight_plumbing(
        shared, [key], [w], k_len, ncols, tn, wspec)
    in_specs = [pl.BlockSpec((tm, k_len), lambda n, m: (m, kb))] + wspecs
    args = [x] + wargs
    if kind == "plain":
        body = _lin_plain_kernel
    elif kind == "sigmoid":
        body = _lin_sigmoid_kernel
    elif kind == "rmshead":
        body = _lin_rmshead_kernel
        in_specs.append(pl.BlockSpec((None, 1, HEAD_DIM), lambda n, m: (layer, 0, 0)))
        args.append(head_gain.reshape(DEPTH, 1, HEAD_DIM))
    else:
        body = _lin_residual_kernel
        in_specs.append(pl.BlockSpec((tm, tn), lambda n, m: (m, n)))
        in_specs.append(pl.BlockSpec(
            (1, 1, tn), lambda n, m: (st.mod_index(layer, m, tm, which_gate), 0, n)))
        args += [residual, modflat]
    outs = pl.pallas_call(
        body,
        out_shape=[jax.ShapeDtypeStruct((m_tok, ncols), out_dtype)] + extra_shapes,
        grid=(ncols // tn, m_tok // tm),
        in_specs=in_specs,
        out_specs=[pl.BlockSpec((tm, tn), lambda n, m: (m, n))] + extra_specs,
        compiler_params=_cparams("parallel", "arbitrary"),
        name="linear_" + kind,
    )(*args)
    if not reuse:
        shared.put(key, outs[1])
    return outs[0]


def _mix_kernel(xp_ref, xa_ref, xh_ref, wp_ref, wa_ref, wh_ref, gp_ref, ga_ref, gh_ref, o_ref,
                wpb=None, wab=None, whb=None):
    wp = _bf16_weight(wp_ref, wpb)
    wa = _bf16_weight(wa_ref, wab)
    wh = _bf16_weight(wh_ref, whb)
    for rows in _row_chunks(xp_ref.shape[0], 2 * SUB_M):
        acc = gp_ref[rows, :].astype(F32) * jnp.dot(xp_ref[rows, :], wp[...], preferred_element_type=F32)
        acc = acc + ga_ref[rows, :].astype(F32) * jnp.dot(xa_ref[rows, :], wa[...], preferred_element_type=F32)
        acc = acc + gh_ref[rows, :].astype(F32) * jnp.dot(xh_ref[rows, :], wh[...], preferred_element_type=F32)
        o_ref[rows, :] = acc.astype(o_ref.dtype)


def _mix(xp, xa, xh, wp, wa, wh, gates, layer, shared):
    m_tok, k = xp.shape
    n = wp.shape[-1]
    tn = 512
    keys = ["w_branch_pool", "w_branch_attn", "w_branch_hyena"]
    wspec = pl.BlockSpec((None, k, tn), lambda j, i: (layer, 0, j))
    wargs, wspecs, extra_shapes, extra_specs, reuse = _weight_plumbing(
        shared, keys, [wp, wa, wh], k, n, tn, wspec)
    tm = _token_tile(m_tok) if reuse else 512
    nb = n // tn
    xspec = pl.BlockSpec((tm, k), lambda j, i: (i, 0))
    outs = pl.pallas_call(
        _mix_kernel,
        out_shape=[jax.ShapeDtypeStruct((m_tok, n), BF16)] + extra_shapes,
        grid=(nb, m_tok // tm),
        in_specs=[xspec, xspec, xspec] + wspecs + [
            pl.BlockSpec((tm, tn), lambda j, i: (i, j)),
            pl.BlockSpec((tm, tn), lambda j, i: (i, nb + j)),
            pl.BlockSpec((tm, tn), lambda j, i: (i, 2 * nb + j))],
        out_specs=[pl.BlockSpec((tm, tn), lambda j, i: (i, j))] + extra_specs,
        compiler_params=_cparams("parallel", "arbitrary"),
        name="mix",
    )(xp, xa, xh, *wargs, gates, gates, gates)
    if not reuse:
        for key, slab in zip(keys, outs[1:]):
            shared.put(key, slab)
    return outs[0]


def _ffn1_kernel(x_ref, wg_ref, wu_ref, o_ref, wgb=None, wub=None):
    wg = _bf16_weight(wg_ref, wgb)
    wu = _bf16_weight(wu_ref, wub)
    for rows in _row_chunks(x_ref.shape[0], 2 * SUB_M):
        x = x_ref[rows, :]
        g = jnp.dot(x, wg[...], preferred_element_type=F32)
        u = jnp.dot(x, wu[...], preferred_element_type=F32)
        o_ref[rows, :] = (g * jax.nn.sigmoid(g) * u).astype(o_ref.dtype)


def _ffn1(x, wg, wu, layer, shared):
    m_tok, k = x.shape
    n = wg.shape[-1]
    tm, tn = _token_tile(m_tok), 256
    keys = ["w_gate", "w_up"]
    wspec = pl.BlockSpec((None, k, tn), lambda j, i: (layer, 0, j))
    wargs, wspecs, extra_shapes, extra_specs, reuse = _weight_plumbing(
        shared, keys, [wg, wu], k, n, tn, wspec)
    outs = pl.pallas_call(
        _ffn1_kernel,
        out_shape=[jax.ShapeDtypeStruct((m_tok, n), BF16)] + extra_shapes,
        grid=(n // tn, m_tok // tm),
        in_specs=[pl.BlockSpec((tm, k), lambda j, i: (i, 0))] + wspecs,
        out_specs=[pl.BlockSpec((tm, tn), lambda j, i: (i, j))] + extra_specs,
        compiler_params=_cparams("parallel", "arbitrary"),
        name="ffn_gate_up",
    )(x, *wargs)
    if not reuse:
        for key, slab in zip(keys, outs[1:]):
            shared.put(key, slab)
    return outs[0]


_POOL_PAD = 16
_ROW_CHUNK = 64


def _halo_chunks(u_ref, seq, rc, pad, emit):
    zeros = jnp.zeros((pad, u_ref.shape[1]), F32)
    emit(0, jnp.concatenate([zeros, u_ref[0:rc + pad, :]], axis=0))

    def chunk(ci, carry):
        base = pl.multiple_of(ci * rc, rc)
        emit(base, u_ref[pl.ds(pl.multiple_of(base - pad, pad), rc + 2 * pad), :])
        return carry

    lax.fori_loop(1, seq // rc - 1, chunk, 0)
    emit(seq - rc, jnp.concatenate([u_ref[seq - rc - pad:seq, :], zeros], axis=0))


def _pool_kernel(u_ref, w_ref, sc_ref, o_ref, d_ref, *, seq):
    g = pl.program_id(1)
    width = u_ref.shape[1]
    rc = _ROW_CHUNK

    for gi, win in enumerate(POOL_SIZES):
        half = win // 2

        @pl.when(g == gi)
        def _(win=win, half=half):
            def emit(base, window):
                lo = _POOL_PAD - half
                acc = window[lo:lo + rc]
                for j in range(1, win):
                    acc = acc + window[lo + j:lo + j + rc]
                t = base + lax.broadcasted_iota(jnp.int32, (rc, width), 0)
                cnt = (jnp.minimum(t + (win - half), seq) - jnp.maximum(t - half, 0)).astype(F32)
                centre = window[_POOL_PAD:_POOL_PAD + rc]
                d_ref[pl.ds(base, rc), :] = (acc / cnt - centre).astype(BF16)

            _halo_chunks(u_ref, seq, rc, _POOL_PAD, emit)

    wbf = w_ref[...].astype(BF16)
    scale = sc_ref[...]
    for rows in _row_chunks(seq):
        y = jnp.dot(d_ref[rows, :], wbf, preferred_element_type=F32) * scale
        o_ref[rows, :] = y.astype(o_ref.dtype)


def _pool(u, pool_w, pool_scale, layer, st):
    seq = st.seq
    return pl.pallas_call(
        functools.partial(_pool_kernel, seq=seq),
        out_shape=jax.ShapeDtypeStruct((st.batch * seq, POOL_WIDTH), BF16),
        grid=(st.batch, len(POOL_SIZES)),
        in_specs=[pl.BlockSpec((seq, POOL_GROUP), lambda b, g: (b, g)),
                  pl.BlockSpec((None, None, POOL_GROUP, POOL_GROUP), lambda b, g: (layer, g, 0, 0)),
                  pl.BlockSpec((None, 1, POOL_GROUP), lambda b, g: (layer, 0, g))],
        out_specs=pl.BlockSpec((seq, POOL_GROUP), lambda b, g: (b, g)),
        scratch_shapes=[pltpu.VMEM((seq, POOL_GROUP), BF16)],
        compiler_params=_cparams("parallel", "parallel"),
        name="pool_mixer",
    )(u, pool_w, pool_scale.reshape(DEPTH, 1, POOL_WIDTH))


def _ctx_attn_kernel(q_ref, k_ref, v_ref, o_ref):
    for h in range(N_HEADS):
        sl = slice(h * HEAD_DIM, (h + 1) * HEAD_DIM)
        q = q_ref[:, sl]
        k = k_ref[:, sl].astype(BF16)
        v = v_ref[:, sl].astype(BF16)
        s = lax.dot_general(q, k, (((1,), (1,)), ((), ())), preferred_element_type=F32) * ATTN_SCALE
        m = jnp.max(s, axis=-1, keepdims=True)
        p = jnp.exp(s - m)
        denom = jnp.sum(p, axis=-1, keepdims=True)
        o = jnp.dot(p.astype(BF16), v, preferred_element_type=F32) / denom
        o_ref[:, sl] = o.astype(o_ref.dtype)


def _ctx_attention(q, k, v, st):
    spec = pl.BlockSpec((st.seq, ATTN_WIDTH), lambda b: (b, 0))
    return pl.pallas_call(
        _ctx_attn_kernel,
        out_shape=jax.ShapeDtypeStruct((st.batch * st.seq, ATTN_WIDTH), BF16),
        grid=(st.batch,),
        in_specs=[spec, spec, spec],
        out_specs=spec,
        compiler_params=_cparams("parallel"),
        name="context_attention",
    )(q, k, v)


_BIAS_TILES = 2 * WIN_R
_Q_GROUP = 8
_BAND = 2 * WIN_R


def _bias_table_kernel(rpb_ref, o_ref, *, layer):
    h = pl.program_id(0)
    shape = (GRID_W, 2 * GRID_W)
    qc = lax.broadcasted_iota(jnp.int32, shape, 0)
    lane = lax.broadcasted_iota(jnp.int32, shape, 1)
    kc = lane % GRID_W
    upper = lane >= GRID_W
    c0 = jnp.clip(qc - WIN_C // 2, 0, GRID_W - WIN_C)
    in_window = (kc >= c0) & (kc < c0 + WIN_C)
    rel = kc - qc + (WIN_C - 1)
    n_dc = 2 * WIN_C - 1
    n_dr = 2 * WIN_R - 1
    row_base = (layer * N_HEADS + h) * n_dr
    for e in range(_BIAS_TILES):
        d_lo = max(e - 1, 0)
        d_hi = min(e, n_dr - 1)
        tile = jnp.zeros(shape, F32)
        for dc in range(n_dc):
            lo = rpb_ref[(row_base + d_lo) * n_dc + dc]
            hi = rpb_ref[(row_base + d_hi) * n_dc + dc]
            tile = jnp.where(rel == dc, jnp.where(upper, hi, lo), tile)
        valid = in_window
        if e == 0:
            valid = valid & upper
        if e == _BIAS_TILES - 1:
            valid = valid & jnp.logical_not(upper)
        o_ref[e] = jnp.where(valid, tile, NEG_INF)


def _bias_table(rpb, layer):
    return pl.pallas_call(
        functools.partial(_bias_table_kernel, layer=layer),
        out_shape=jax.ShapeDtypeStruct((N_HEADS, _BIAS_TILES, GRID_W, 2 * GRID_W), F32),
        grid=(N_HEADS,),
        in_specs=[pl.BlockSpec(memory_space=pltpu.SMEM)],
        out_specs=pl.BlockSpec((None, _BIAS_TILES, GRID_W, 2 * GRID_W), lambda h: (h, 0, 0, 0)),
        compiler_params=_cparams("parallel"),
        name="rel_pos_bias_table",
    )(rpb.reshape(-1))


def _nbr_attn_kernel(q_ref, k_ref, v_ref, kc_ref, vc_ref, bias_ref, o_ref,
                     kbf, vbf, s_ref, sc_ref, p_ref, pc_ref, *, rows):
    kbf[...] = k_ref[...].astype(BF16)
    vbf[...] = v_ref[...].astype(BF16)
    kc = kc_ref[...].astype(BF16)
    vc = vc_ref[...].astype(BF16)
    dims = (((1,), (1,)), ((), ()))
    pair = 2 * GRID_W
    lower = lax.broadcasted_iota(jnp.int32, (GRID_W, pair), 1) < GRID_W
    zero_tile = jnp.zeros((GRID_W, pair), BF16)

    for g in range(rows // _Q_GROUP):
        slot = g % 2
        ks = min(max(g * _Q_GROUP - WIN_R // 2, 0), rows - _BAND)
        qsl = slice(g * _Q_GROUP * GRID_W, (g + 1) * _Q_GROUP * GRID_W)
        ksl = slice(ks * GRID_W, (ks + _BAND) * GRID_W)
        qg = q_ref[qsl, :]
        s_ref[slot] = lax.dot_general(qg, kbf[ksl, :], dims, preferred_element_type=F32) * ATTN_SCALE
        sc_ref[slot] = lax.dot_general(qg, kc, dims, preferred_element_type=F32) * ATTN_SCALE
        for i in range(_Q_GROUP):
            qr = g * _Q_GROUP + i
            r0 = min(max(qr - WIN_R // 2, 0), rows - WIN_R)
            first = r0 - ks
            j0, j1 = first // 2, (first + WIN_R + 1) // 2
            rsl = slice(i * GRID_W, (i + 1) * GRID_W)
            tiles = []
            for j in range(j0, j1):
                kr0 = ks + 2 * j
                t = s_ref[slot, rsl, j * pair:(j + 1) * pair] + bias_ref[kr0 - qr + WIN_R]
                if kr0 < r0:
                    t = jnp.where(lower, NEG_INF, t)
                if kr0 + 1 >= r0 + WIN_R:
                    t = jnp.where(lower, t, NEG_INF)
                tiles.append(t)
            sc = sc_ref[slot, rsl, :]
            mt = functools.reduce(jnp.maximum, tiles)
            m = jnp.maximum(jnp.max(mt, axis=-1, keepdims=True), jnp.max(sc, axis=-1, keepdims=True))
            ps = [jnp.exp(t - m) for t in tiles]
            pc = jnp.exp(sc - m)
            denom = (jnp.sum(functools.reduce(jnp.add, ps), axis=-1, keepdims=True)
                     + jnp.sum(pc, axis=-1, keepdims=True))
            inv = 1.0 / denom
            for j in range(_BAND // 2):
                val = (ps[j - j0] * inv).astype(BF16) if j0 <= j < j1 else zero_tile
                p_ref[slot, rsl, j * pair:(j + 1) * pair] = val
            pc_ref[slot, rsl, :] = (pc * inv).astype(BF16)
        o = (jnp.dot(p_ref[slot], vbf[ksl, :], preferred_element_type=F32)
             + jnp.dot(pc_ref[slot], vc, preferred_element_type=F32))
        o_ref[qsl, :] = o.astype(o_ref.dtype)


def _nbr_attention(q, k, v, cache_k, cache_v, bias, layer, st):
    seq = st.seq
    rows = seq // GRID_W
    assert rows >= _BAND and rows % _Q_GROUP == 0
    past = cache_k.shape[2]
    ck = cache_k.reshape(st.batch, DEPTH, past, ATTN_WIDTH)
    cv = cache_v.reshape(st.batch, DEPTH, past, ATTN_WIDTH)
    tok_spec = pl.BlockSpec((seq, HEAD_DIM), lambda b, h: (b, h))
    cache_spec = pl.BlockSpec((None, None, past, HEAD_DIM), lambda b, h: (b, layer, 0, h))
    nq = _Q_GROUP * GRID_W
    nk = _BAND * GRID_W
    return pl.pallas_call(
        functools.partial(_nbr_attn_kernel, rows=rows),
        out_shape=jax.ShapeDtypeStruct((st.batch * seq, ATTN_WIDTH), BF16),
        grid=(st.batch, N_HEADS),
        in_specs=[tok_spec, tok_spec, tok_spec, cache_spec, cache_spec,
                  pl.BlockSpec((None, _BIAS_TILES, GRID_W, 2 * GRID_W), lambda b, h: (h, 0, 0, 0))],
        out_specs=tok_spec,
        scratch_shapes=[pltpu.VMEM((seq, HEAD_DIM), BF16), pltpu.VMEM((seq, HEAD_DIM), BF16),
                        pltpu.VMEM((2, nq, nk), F32), pltpu.VMEM((2, nq, past), F32),
                        pltpu.VMEM((2, nq, nk), BF16), pltpu.VMEM((2, nq, past), BF16)],
        compiler_params=_cparams("parallel", "parallel"),
        name="neighbourhood_attention",
    )(q, k, v, ck, cv, bias)


_CONV_PAD = 8


def _short_conv_kernel(u_ref, w_ref, b_ref, o_ref, *, seq):
    width = u_ref.shape[1]
    w0 = w_ref[0:1, :]
    w1 = w_ref[1:2, :]
    w2 = w_ref[2:3, :]
    b = b_ref[...]
    rc = max(8, _ROW_CHUNK * CH_TILE // width)

    def emit(base, window):
        prev = window[_CONV_PAD - 1:_CONV_PAD - 1 + rc]
        cur = window[_CONV_PAD:_CONV_PAD + rc]
        nxt = window[_CONV_PAD + 1:_CONV_PAD + 1 + rc]
        o_ref[pl.ds(base, rc), :] = prev * w0 + cur * w1 + nxt * w2 + b

    _halo_chunks(u_ref, seq, rc, _CONV_PAD, emit)


def _short_conv(u, conv_w, conv_b, layer, st):
    width = u.shape[1]
    seq = st.seq
    tc = HYENA_WIDTH if seq <= SHORT_SEQ else CH_TILE
    return pl.pallas_call(
        functools.partial(_short_conv_kernel, seq=seq),
        out_shape=jax.ShapeDtypeStruct((st.batch * seq, width), F32),
        grid=(st.batch, width // tc),
        in_specs=[pl.BlockSpec((seq, tc), lambda b, j: (b, j)),
                  pl.BlockSpec((None, 3, tc), lambda b, j: (layer, 0, j)),
                  pl.BlockSpec((None, 1, tc), lambda b, j: (layer, 0, j))],
        out_specs=pl.BlockSpec((seq, tc), lambda b, j: (b, j)),
        compiler_params=_cparams("parallel", "parallel"),
        name="hyena_short_conv",
    )(u, conv_w, conv_b.reshape(DEPTH, 1, width))


def _dft_matrices(seq):
    f = jnp.arange(seq, dtype=jnp.int32)[:, None]
    t = jnp.arange(seq, dtype=jnp.int32)[None, :]
    ang = ((f * t) % (2 * seq)).astype(F32) * (math.pi / seq)
    top = jnp.cos(ang)
    bot = jnp.where(f == 0, (1 - 2 * (t % 2)).astype(F32), -jnp.sin(ang))
    colscale = jnp.where(jnp.arange(seq) == 0, 0.5, 1.0).astype(F32)[None, :] / seq
    inv = jnp.concatenate([top.T * colscale, bot.T * colscale], axis=1)
    return top.astype(BF16), bot.astype(BF16), inv.astype(BF16)


def _filter_mlp_kernel(z_ref, w1_ref, b1_ref, w2_ref, b2_ref, fr_ref, o_ref):
    fr = fr_ref[...]
    h = jnp.sin(fr * (jnp.dot(z_ref[...], w1_ref[...], preferred_element_type=F32) + b1_ref[...]))
    o_ref[...] = jnp.sin(fr * (jnp.dot(h, w2_ref[...], preferred_element_type=F32) + b2_ref[...]))


def _filter_mlp(seq, w1, b1, w2, b2, freq, layer):
    t = jnp.linspace(0.0, 1.0, seq, dtype=F32)[:, None]
    bands = (FILTER_EMB - 1) // 2
    ang = (2.0 * math.pi / seq) * jnp.arange(seq, dtype=F32)[:, None] * jnp.linspace(
        1e-4, bands - 1, bands, dtype=F32)[None]
    z = jnp.concatenate([t, jnp.cos(ang), -jnp.sin(ang)], axis=-1)
    z = jnp.pad(z, ((0, 0), (0, LANES - FILTER_EMB)))
    w1p = jnp.pad(w1, ((0, 0), (0, LANES - FILTER_EMB), (0, 0)))
    vec = lambda a: a.reshape(DEPTH, 1, FILTER_HIDDEN)
    vspec = pl.BlockSpec((None, 1, FILTER_HIDDEN), lambda i: (layer, 0, 0))
    return pl.pallas_call(
        _filter_mlp_kernel,
        out_shape=jax.ShapeDtypeStruct((seq, FILTER_HIDDEN), F32),
        grid=(1,),
        in_specs=[pl.BlockSpec((seq, LANES), lambda i: (0, 0)),
                  pl.BlockSpec((None, LANES, FILTER_HIDDEN), lambda i: (layer, 0, 0)),
                  vspec,
                  pl.BlockSpec((None, FILTER_HIDDEN, FILTER_HIDDEN), lambda i: (layer, 0, 0)),
                  vspec, vspec],
        out_specs=pl.BlockSpec((seq, FILTER_HIDDEN), lambda i: (0, 0)),
        compiler_params=_cparams("arbitrary"),
        name="hyena_filter_mlp",
    )(z, w1p, vec(b1), w2, vec(b2), vec(freq))


def _filter_taps_kernel(h_ref, dl_ref, wf_ref, wb_ref, a_ref, d_ref, nyq_ref, *, seq):
    h = h_ref[...]
    shape = (seq, dl_ref.shape[1])
    row = lax.broadcasted_iota(jnp.int32, shape, 0)
    t = row.astype(F32) / float(seq - 1)
    dec = jnp.exp(-t * dl_ref[...])
    fwd = jnp.dot(h, wf_ref[...], preferred_element_type=F32) * dec
    bwd = jnp.dot(h, wb_ref[...], preferred_element_type=F32) * dec
    bwd = jnp.where(row == 0, 0.0, bwd)
    inv = 1.0 / jnp.sum(jnp.abs(fwd) + jnp.abs(bwd), axis=0, keepdims=True)
    a = (fwd + bwd) * inv
    a_ref[...] = a
    d_ref[...] = (fwd - bwd) * inv
    sign = jnp.where(row % 2 == 0, 1.0, -1.0)
    nyq_ref[0:1, :] = jnp.sum(a * sign, axis=0, keepdims=True)
    quarter = jnp.where(row % 4 == 0, 1.0, jnp.where(row % 4 == 2, -1.0, 0.0))
    nyq_ref[1:2, :] = jnp.sum(a * quarter, axis=0, keepdims=True)


def _filter_taps(hidden, w3, layer, *, seq):
    tc = 256
    nct = HYENA_WIDTH // tc
    deltas = jnp.abs(jnp.linspace(math.log(DECAY_TARGET) / SLOW_DECAY_PCT,
                                  math.log(DECAY_TARGET) / FAST_DECAY_PCT, HYENA_WIDTH, dtype=F32))
    tap_shape = jax.ShapeDtypeStruct((HYENA_ORDER, seq, HYENA_WIDTH), F32)
    tap_spec = pl.BlockSpec((None, seq, tc), lambda o, j: (o, 0, j))
    return pl.pallas_call(
        functools.partial(_filter_taps_kernel, seq=seq),
        out_shape=(tap_shape, tap_shape, jax.ShapeDtypeStruct((HYENA_ORDER, 2, HYENA_WIDTH), F32)),
        grid=(HYENA_ORDER, nct),
        in_specs=[pl.BlockSpec((seq, FILTER_HIDDEN), lambda o, j: (0, 0)),
                  pl.BlockSpec((1, tc), lambda o, j: (0, j)),
                  pl.BlockSpec((None, FILTER_HIDDEN, tc), lambda o, j: (layer, 0, (2 * o) * nct + j)),
                  pl.BlockSpec((None, FILTER_HIDDEN, tc), lambda o, j: (layer, 0, (2 * o + 1) * nct + j))],
        out_specs=(tap_spec, tap_spec, pl.BlockSpec((None, 2, tc), lambda o, j: (o, 0, j))),
        compiler_params=_cparams("parallel", "parallel"),
        name="hyena_filter_taps",
    )(hidden, deltas.reshape(1, HYENA_WIDTH), w3, w3)


def _filter_spec_kernel(wt_ref, wb_ref, a_ref, d_ref, nyq_ref, o_ref, abf, dbf):
    i = pl.program_id(2)

    @pl.when(i == 0)
    def _():
        abf[...] = a_ref[...].astype(BF16)
        dbf[...] = d_ref[...].astype(BF16)

    for rows in _row_chunks(wt_ref.shape[0]):
        top = jnp.dot(wt_ref[rows, :], abf[...], preferred_element_type=F32)
        bot = jnp.dot(wb_ref[rows, :], dbf[...], preferred_element_type=F32)
        if rows.start == 0:
            row = lax.broadcasted_iota(jnp.int32, bot.shape, 0)
            bot = jnp.where((row == 0) & (i == 0), nyq_ref[0:1, :], bot)
        o_ref[0, rows, :] = top
        o_ref[1, rows, :] = bot


def _dft_tiles(seq):
    return seq, (HYENA_WIDTH if seq <= SHORT_SEQ else CH_TILE)


_RESIDENT = pl.Buffered(1)


def _filter_spectrum(dft_top, dft_bot, taps_a, taps_d, nyq, *, seq):
    tf, tc = _dft_tiles(seq)
    tc = min(tc, CH_TILE // 2)
    wspec = pl.BlockSpec((tf, seq), lambda o, c, i: (i, 0), pipeline_mode=_RESIDENT)
    tspec = pl.BlockSpec((None, seq, tc), lambda o, c, i: (o, 0, c))
    return pl.pallas_call(
        _filter_spec_kernel,
        out_shape=jax.ShapeDtypeStruct((HYENA_ORDER, 2, seq, HYENA_WIDTH), F32),
        grid=(HYENA_ORDER, HYENA_WIDTH // tc, seq // tf),
        in_specs=[wspec, wspec, tspec, tspec,
                  pl.BlockSpec((None, 2, tc), lambda o, c, i: (o, 0, c))],
        out_specs=pl.BlockSpec((None, 2, tf, tc), lambda o, c, i: (o, 0, i, c)),
        scratch_shapes=[pltpu.VMEM((seq, tc), BF16)] * 2,
        compiler_params=_cparams("parallel", "parallel", "arbitrary"),
        name="hyena_filter_spectrum",
    )(dft_top, dft_bot, taps_a, taps_d, nyq)


def _fwd_dft_kernel(wt_ref, wb_ref, z_ref, p_ref, o_ref, zbf):
    i = pl.program_id(2)

    @pl.when(i == 0)
    def _():
        zbf[...] = z_ref[...].astype(BF16)

    for rows in _row_chunks(wt_ref.shape[0]):
        top = jnp.dot(wt_ref[rows, :], zbf[...], preferred_element_type=F32)
        bot = jnp.dot(wb_ref[rows, :], zbf[...], preferred_element_type=F32)
        pt = p_ref[0, rows, :]
        pb = p_ref[1, rows, :]
        re = top * pt - bot * pb
        im = top * pb + bot * pt
        if rows.start == 0:
            row = lax.broadcasted_iota(jnp.int32, top.shape, 0)
            real_row = (row == 0) & (i == 0)
            re = jnp.where(real_row, top * pt, re)
            im = jnp.where(real_row, bot * pb, im)
        o_ref[0, rows, :] = re.astype(o_ref.dtype)
        o_ref[1, rows, :] = im.astype(o_ref.dtype)


def _fwd_dft(dft_top, dft_bot, z, z_col_off, spec, order, *, batch, seq):
    tf, tc = _dft_tiles(seq)
    zc = z_col_off // tc
    wspec = pl.BlockSpec((tf, seq), lambda c, b, i: (i, 0), pipeline_mode=_RESIDENT)
    return pl.pallas_call(
        _fwd_dft_kernel,
        out_shape=jax.ShapeDtypeStruct((batch, 2, seq, HYENA_WIDTH), BF16),
        grid=(HYENA_WIDTH // tc, batch, seq // tf),
        in_specs=[wspec, wspec,
                  pl.BlockSpec((seq, tc), lambda c, b, i: (b, zc + c)),
                  pl.BlockSpec((None, 2, tf, tc), lambda c, b, i: (order, 0, i, c), pipeline_mode=_RESIDENT)],
        out_specs=pl.BlockSpec((None, 2, tf, tc), lambda c, b, i: (b, 0, i, c)),
        scratch_shapes=[pltpu.VMEM((seq, tc), BF16)],
        compiler_params=_cparams("parallel", "parallel", "arbitrary"),
        name="hyena_fwd_dft",
    )(dft_top, dft_bot, z, spec)


def _inv_dft_kernel(wi_ref, s_ref, gate_ref, z_ref, bias_ref, o_ref):
    s = s_ref[...]
    s2 = s.reshape(s.shape[0] * s.shape[1], s.shape[2])
    for rows in _row_chunks(wi_ref.shape[0]):
        conv = jnp.dot(wi_ref[rows, :], s2, preferred_element_type=F32)
        o_ref[rows, :] = (gate_ref[rows, :] * (conv + bias_ref[...] * z_ref[rows, :])).astype(o_ref.dtype)


def _inv_dft(dft_inv, s, gate, gate_col_off, z, z_col_off, hy_bias, layer, order, out_dtype, *, batch, seq):
    tl, tc = _dft_tiles(seq)
    nrb = seq // tl
    gc = gate_col_off // tc
    zc = z_col_off // tc
    return pl.pallas_call(
        _inv_dft_kernel,
        out_shape=jax.ShapeDtypeStruct((batch * seq, HYENA_WIDTH), out_dtype),
        grid=(HYENA_WIDTH // tc, batch, nrb),
        in_specs=[pl.BlockSpec((tl, 2 * seq), lambda c, b, i: (i, 0), pipeline_mode=_RESIDENT),
                  pl.BlockSpec((None, 2, seq, tc), lambda c, b, i: (b, 0, 0, c)),
                  pl.BlockSpec((tl, tc), lambda c, b, i: (b * nrb + i, gc + c)),
                  pl.BlockSpec((tl, tc), lambda c, b, i: (b * nrb + i, zc + c)),
                  pl.BlockSpec((None, 1, tc), lambda c, b, i: (layer * HYENA_ORDER + order, 0, c))],
        out_specs=pl.BlockSpec((tl, tc), lambda c, b, i: (b * nrb + i, c)),
        compiler_params=_cparams("parallel", "parallel", "arbitrary"),
        name="hyena_inv_dft",
    )(dft_inv, s, gate, z, hy_bias.reshape(DEPTH * HYENA_ORDER, 1, HYENA_WIDTH))


def _dft_matrices_r2(seq):
    half = seq // 2
    g = jnp.arange(half, dtype=jnp.int32)[:, None]
    m = jnp.arange(half, dtype=jnp.int32)[None, :]
    ang_e = ((2 * g * m) % (2 * seq)).astype(F32) * (math.pi / seq)
    ang_o = ((g * (2 * m + 1)) % (2 * seq)).astype(F32) * (math.pi / seq)
    alt = (1 - 2 * (m % 2)).astype(F32)
    ce = jnp.cos(ang_e)
    co = jnp.cos(ang_o)
    se = jnp.where(g == 0, alt, -jnp.sin(ang_e))
    so = jnp.where(g == 0, -alt, -jnp.sin(ang_o))
    colscale = jnp.where(jnp.arange(half) == 0, 0.5, 1.0).astype(F32)[None, :] / seq
    we = jnp.concatenate([ce.T * colscale, se.T / seq], axis=1)
    wo = jnp.concatenate([co.T * colscale, so.T / seq], axis=1)
    return tuple(a.astype(BF16) for a in (ce, co, se, so)), (we.astype(BF16), wo.astype(BF16))


def _half_transforms(mats, xe, xo, ye, yo, rows):
    ce_ref, co_ref, se_ref, so_ref = mats
    ec = jnp.dot(ce_ref[rows, :], xe[...], preferred_element_type=F32)
    oc = jnp.dot(co_ref[rows, :], xo[...], preferred_element_type=F32)
    es = jnp.dot(se_ref[rows, :], ye[...], preferred_element_type=F32)
    os_ = jnp.dot(so_ref[rows, :], yo[...], preferred_element_type=F32)
    return ec, oc, es, os_


def _filter_spec_r2_kernel(ce_ref, co_ref, se_ref, so_ref, ae_ref, ao_ref, de_ref, do_ref, edge_ref, o_ref,
                           aeb, aob, deb, dob):
    aeb[...] = ae_ref[...].astype(BF16)
    aob[...] = ao_ref[...].astype(BF16)
    deb[...] = de_ref[...].astype(BF16)
    dob[...] = do_ref[...].astype(BF16)
    for rows in _row_chunks(ce_ref.shape[0]):
        ec, oc, es, os_ = _half_transforms((ce_ref, co_ref, se_ref, so_ref), aeb, aob, deb, dob, rows)
        im_a = es + os_
        im_b = os_ - es
        if rows.start == 0:
            row0 = lax.broadcasted_iota(jnp.int32, ec.shape, 0) == 0
            im_a = jnp.where(row0, edge_ref[1:2, :], im_a)
            im_b = jnp.where(row0, os_, im_b)
        o_ref[0, rows, :] = ec + oc
        o_ref[1, rows, :] = im_a
        o_ref[2, rows, :] = ec - oc
        o_ref[3, rows, :] = im_b


def _pair_view(a):
    return a.reshape(a.shape[:-2] + (a.shape[-2] // 2, 2 * a.shape[-1]))


def _filter_spectrum_r2(mats, taps_a, taps_d, edge, *, seq):
    half = seq // 2
    tc = CH_TILE // 2
    nct = HYENA_WIDTH // tc
    mspec = pl.BlockSpec((half, half), lambda o, c: (0, 0), pipeline_mode=_RESIDENT)
    even = pl.BlockSpec((None, half, tc), lambda o, c: (o, 0, c))
    odd = pl.BlockSpec((None, half, tc), lambda o, c: (o, 0, nct + c))
    a2, d2 = _pair_view(taps_a), _pair_view(taps_d)
    return pl.pallas_call(
        _filter_spec_r2_kernel,
        out_shape=jax.ShapeDtypeStruct((HYENA_ORDER, 4, half, HYENA_WIDTH), F32),
        grid=(HYENA_ORDER, nct),
        in_specs=[mspec] * 4 + [even, odd, even, odd, pl.BlockSpec((None, 2, tc), lambda o, c: (o, 0, c))],
        out_specs=pl.BlockSpec((None, 4, half, tc), lambda o, c: (o, 0, 0, c)),
        scratch_shapes=[pltpu.VMEM((half, tc), BF16)] * 4,
        compiler_params=_cparams("parallel", "parallel"),
        name="hyena_filter_spectrum_r2",
    )(*mats, a2, a2, d2, d2, edge)


def _fwd_dft_r2_kernel(ce_ref, co_ref, se_ref, so_ref, ze_ref, zo_ref, p_ref, o_ref, zeb, zob):
    zeb[...] = ze_ref[...].astype(BF16)
    zob[...] = zo_ref[...].astype(BF16)
    for rows in _row_chunks(ce_ref.shape[0]):
        ec, oc, es, os_ = _half_transforms((ce_ref, co_ref, se_ref, so_ref), zeb, zob, zeb, zob, rows)
        re_a, re_b = ec + oc, ec - oc
        im_a, im_b = es + os_, os_ - es
        pra, pia, prb, pib = (p_ref[j, rows, :] for j in range(4))
        first = rows.start == 0
        if first:
            row0 = lax.broadcasted_iota(jnp.int32, ec.shape, 0) == 0
            im_a = jnp.where(row0, es, im_a)
            im_b = jnp.where(row0, os_, im_b)
        re_a2 = re_a * pra - im_a * pia
        im_a2 = re_a * pia + im_a * pra
        re_b2 = re_b * prb - im_b * pib
        im_b2 = re_b * pib + im_b * prb
        if first:
            re_a2 = jnp.where(row0, re_a * pra, re_a2)
            re_b2 = jnp.where(row0, re_b * prb, re_b2)
            im_a2 = jnp.where(row0, im_a * pia - im_b * pib, im_a2)
            im_b2 = jnp.where(row0, im_a * pib + im_b * pia, im_b2)
        us = im_a2 - im_b2
        vs = im_a2 + im_b2
        if first:
            us = jnp.where(row0, im_a2, us)
            vs = jnp.where(row0, im_b2, vs)
        o_ref[0, rows, :] = (re_a2 + re_b2).astype(o_ref.dtype)
        o_ref[1, rows, :] = us.astype(o_ref.dtype)
        o_ref[2, rows, :] = (re_a2 - re_b2).astype(o_ref.dtype)
        o_ref[3, rows, :] = vs.astype(o_ref.dtype)


def _fwd_dft_r2(mats, ze, ze_cb, zo, zo_cb, spec, order, *, batch, seq):
    half = seq // 2
    tc = CH_TILE
    mspec = pl.BlockSpec((half, half), lambda c, b: (0, 0), pipeline_mode=_RESIDENT)
    return pl.pallas_call(
        _fwd_dft_r2_kernel,
        out_shape=jax.ShapeDtypeStruct((batch, 4, half, HYENA_WIDTH), BF16),
        grid=(HYENA_WIDTH // tc, batch),
        in_specs=[mspec] * 4 + [
            pl.BlockSpec((half, tc), lambda c, b: (b, ze_cb + c)),
            pl.BlockSpec((half, tc), lambda c, b: (b, zo_cb + c)),
            pl.BlockSpec((None, 4, half, tc), lambda c, b: (order, 0, 0, c), pipeline_mode=_RESIDENT)],
        out_specs=pl.BlockSpec((None, 4, half, tc), lambda c, b: (b, 0, 0, c)),
        scratch_shapes=[pltpu.VMEM((half, tc), BF16)] * 2,
        compiler_params=_cparams("parallel", "parallel"),
        name="hyena_fwd_dft_r2",
    )(*mats, ze, zo, spec)


def _inv_dft_r2_kernel(we_ref, wo_ref, s_ref, ge_ref, go_ref, ze_ref, zo_ref, bias_ref, oe_ref, oo_ref):
    s = s_ref[...]
    half, tc = s.shape[1], s.shape[2]
    su = s[0:2].reshape(2 * half, tc)
    sv = s[2:4].reshape(2 * half, tc)
    bias = bias_ref[...]
    for rows in _row_chunks(half):
        ye = jnp.dot(we_ref[rows, :], su, preferred_element_type=F32)
        yo = jnp.dot(wo_ref[rows, :], sv, preferred_element_type=F32)
        oe_ref[rows, :] = (ge_ref[rows, :] * (ye + bias * ze_ref[rows, :])).astype(oe_ref.dtype)
        oo_ref[rows, :] = (go_ref[rows, :] * (yo + bias * zo_ref[rows, :])).astype(oo_ref.dtype)


def _inv_dft_r2(inv_mats, s, uc2, gate_cb, ze, ze_cb, zo, zo_cb, hy_bias, layer, order, out_dtype, *, batch, seq):
    half = seq // 2
    tc = CH_TILE
    odd_off = uc2.shape[1] // 2 // tc
    wspec = pl.BlockSpec((half, seq), lambda c, b: (0, 0), pipeline_mode=_RESIDENT)
    out = jax.ShapeDtypeStruct((batch * half, HYENA_WIDTH), out_dtype)
    ospec = pl.BlockSpec((half, tc), lambda c, b: (b, c))
    return pl.pallas_call(
        _inv_dft_r2_kernel,
        out_shape=(out, out),
        grid=(HYENA_WIDTH // tc, batch),
        in_specs=[wspec, wspec,
                  pl.BlockSpec((None, 4, half, tc), lambda c, b: (b, 0, 0, c)),
                  pl.BlockSpec((half, tc), lambda c, b: (b, gate_cb + c)),
                  pl.BlockSpec((half, tc), lambda c, b: (b, odd_off + gate_cb + c)),
                  pl.BlockSpec((half, tc), lambda c, b: (b, ze_cb + c)),
                  pl.BlockSpec((half, tc), lambda c, b: (b, zo_cb + c)),
                  pl.BlockSpec((None, 1, tc), lambda c, b: (layer * HYENA_ORDER + order, 0, c))],
        out_specs=(ospec, ospec),
        compiler_params=_cparams("parallel", "parallel"),
        name="hyena_inv_dft_r2",
    )(*inv_mats, s, uc2, uc2, ze, zo, hy_bias.reshape(DEPTH * HYENA_ORDER, 1, HYENA_WIDTH))


def _hyena_long(uc, taps_a, taps_d, edge, hy_bias, layer, *, batch, seq):
    mats, inv_mats = _dft_matrices_r2(seq)
    spec = _filter_spectrum_r2(mats, taps_a, taps_d, edge, seq=seq)
    uc2 = _pair_view(uc)
    nb = HYENA_WIDTH // CH_TILE
    odd = uc2.shape[1] // 2 // CH_TILE
    s = _fwd_dft_r2(mats, uc2, 0, uc2, odd, spec, 0, batch=batch, seq=seq)
    z1e, z1o = _inv_dft_r2(inv_mats, s, uc2, nb, uc2, 0, uc2, odd, hy_bias, layer, 0, F32, batch=batch, seq=seq)
    s = _fwd_dft_r2(mats, z1e, 0, z1o, 0, spec, 1, batch=batch, seq=seq)
    he, ho = _inv_dft_r2(inv_mats, s, uc2, 2 * nb, z1e, 0, z1o, 0, hy_bias, layer, 1, BF16, batch=batch, seq=seq)
    return jnp.stack([he, ho], axis=1).reshape(batch * seq, HYENA_WIDTH)


def _hyena(u, p, layer, st):
    batch, seq = st.batch, st.seq
    uc = _short_conv(u, p["hy_conv_w"], p["hy_conv_b"], layer, st)
    if seq > SHORT_SEQ:
        hidden = _filter_mlp(seq, p["filt_w1"], p["filt_b1"], p["filt_w2"], p["filt_b2"], p["filt_freq"], layer)
        taps_a, taps_d, edge = _filter_taps(hidden, p["filt_w3"], layer, seq=seq)
        return _hyena_long(uc, taps_a, taps_d, edge, p["hy_bias"], layer, batch=batch, seq=seq)
    dft_top, dft_bot, dft_inv = _dft_matrices(seq)
    hidden = _filter_mlp(seq, p["filt_w1"], p["filt_b1"], p["filt_w2"], p["filt_b2"], p["filt_freq"], layer)
    taps_a, taps_d, nyq = _filter_taps(hidden, p["filt_w3"], layer, seq=seq)
    spec = _filter_spectrum(dft_top, dft_bot, taps_a, taps_d, nyq, seq=seq)
    s = _fwd_dft(dft_top, dft_bot, uc, 0, spec, 0, batch=batch, seq=seq)
    z1 = _inv_dft(dft_inv, s, uc, HYENA_WIDTH, uc, 0, p["hy_bias"], layer, 0, F32, batch=batch, seq=seq)
    s = _fwd_dft(dft_top, dft_bot, z1, 0, spec, 1, batch=batch, seq=seq)
    return _inv_dft(dft_inv, s, uc, 2 * HYENA_WIDTH, z1, 0, p["hy_bias"], layer, 1, BF16, batch=batch, seq=seq)


def _block(x, p, modflat, layer, st, ctx_kv, shared):
    h = _normmod(x, p["norm1_g"], modflat, st, layer, 1, 0)
    lin = functools.partial(_linear, h, p["w_in"], layer, tn=512, tn_reuse=1024, shared=shared)
    u_pool = lin(COL_POOL, POOL_WIDTH, kind="plain", out_dtype=F32, key="w_in:pool")
    q = lin(COL_Q, ATTN_WIDTH, kind="rmshead", out_dtype=BF16, head_gain=p["q_norm_g"], key="w_in:q")
    k = lin(COL_K, ATTN_WIDTH, kind="rmshead", out_dtype=F32, head_gain=p["k_norm_g"], key="w_in:k")
    v = lin(COL_V, ATTN_WIDTH, kind="plain", out_dtype=F32, key="w_in:v")
    u_hy = lin(COL_HY, (HYENA_ORDER + 1) * HYENA_WIDTH, kind="plain", out_dtype=F32, key="w_in:hyena")
    gates = lin(COL_GATES, 3 * D_MODEL, kind="sigmoid", out_dtype=BF16, key="w_in:gates")

    pool = _pool(u_pool, p["pool_w"], p["pool_scale"], layer, st)
    if ctx_kv is None:
        attn = _ctx_attention(q, k, v, st)
    else:
        attn = _nbr_attention(q, k, v, ctx_kv[0], ctx_kv[1], ctx_kv[2], layer, st)
    hy = _hyena(u_hy, p, layer, st)

    mixed = _mix(pool, attn, hy, p["w_branch_pool"], p["w_branch_attn"], p["w_branch_hyena"], gates, layer,
                 shared)
    x = _linear(mixed, p["w_out"], layer, 0, D_MODEL, tn=512, tn_reuse=1024, kind="residual", out_dtype=F32,
                shared=shared, key="w_out", residual=x, modflat=modflat, st=st, which_gate=2)
    h = _normmod(x, p["norm2_g"], modflat, st, layer, 4, 3)
    a = _ffn1(h, p["w_gate"], p["w_up"], layer, shared)
    half = D_FF // 2
    for k_off in (0, half):
        x = _linear(a, p["w_down"], layer, 0, D_MODEL, tn=512, tm=512, tm_reuse=1024, kind="residual",
                    out_dtype=F32, shared=shared, key="w_down:%d" % k_off, k_off=k_off, k_len=half,
                    residual=x, modflat=modflat, st=st, which_gate=5)
    return x, k, v


_WEIGHT_NAMES = ("ada_w", "ada_b", "norm1_g", "norm2_g", "w_in", "pool_w", "pool_scale",
                 "q_norm_g", "k_norm_g", "rpb", "hy_conv_w", "hy_conv_b", "filt_w1", "filt_b1",
                 "filt_w2", "filt_b2", "filt_w3", "filt_freq", "hy_bias", "w_branch_pool",
                 "w_branch_attn", "w_branch_hyena", "w_out", "w_gate", "w_up", "w_down")


def kernel(x_prompt, x_sample, cache_k, cache_v, c, c_ctx, ada_w, ada_b, norm1_g, norm2_g, w_in, pool_w, pool_scale, q_norm_g, k_norm_g, rpb, hy_conv_w, hy_conv_b, filt_w1, filt_b1, filt_w2, filt_b2, filt_w3, filt_freq, hy_bias, w_branch_pool, w_branch_attn, w_branch_hyena, w_out, w_gate, w_up, w_down):
    p = dict(zip(_WEIGHT_NAMES, (ada_w, ada_b, norm1_g, norm2_g, w_in, pool_w, pool_scale, q_norm_g,
                                 k_norm_g, rpb, hy_conv_w, hy_conv_b, filt_w1, filt_b1, filt_w2, filt_b2,
                                 filt_w3, filt_freq, hy_bias, w_branch_pool, w_branch_attn,
                                 w_branch_hyena, w_out, w_gate, w_up, w_down)))
    nb, ns, d = x_prompt.shape
    lb, ls, _ = x_sample.shape
    assert d == D_MODEL and lb + 1 <= MOD_ROWS
    ctx = _Stream(nb, ns, 0, False)
    lat = _Stream(lb, ls, 1, True)

    cmat = jnp.concatenate([c_ctx[None, :], c, jnp.zeros((MOD_ROWS - 1 - lb, d), F32)], axis=0)
    mod = _modulation(cmat, ada_w, ada_b)
    modflat = mod.reshape(DEPTH * MOD_ROWS * 6, 1, d)

    y_ctx = x_prompt.reshape(nb * ns, d)
    y_lat = x_sample.reshape(lb * ls, d)
    new_k, new_v = [], []
    for layer in range(DEPTH):
        shared = _SharedWeights()
        bias = _bias_table(rpb, layer)
        y_lat, _, _ = _block(y_lat, p, modflat, layer, lat, (cache_k, cache_v, bias), shared)
        y_ctx, k_l, v_l = _block(y_ctx, p, modflat, layer, ctx, None, shared)
        new_k.append(k_l.reshape(nb, ns, N_HEADS, HEAD_DIM))
        new_v.append(v_l.reshape(nb, ns, N_HEADS, HEAD_DIM))

    return (y_ctx.reshape(nb, ns, d), y_lat.reshape(lb, ls, d),
            jnp.stack(new_k, axis=1), jnp.stack(new_v, axis=1))
```

```python
import functools
import math

import jax
import jax.numpy as jnp
from jax import lax
from jax.experimental import pallas as pl
from jax.experimental.pallas import tpu as pltpu

F32 = jnp.float32
BF16 = jnp.bfloat16

D_MODEL = 4096
DEPTH = 2
GRID_W = 64
HEAD_DIM = 128
N_HEADS = D_MODEL // 256
ATTN_WIDTH = N_HEADS * HEAD_DIM
ATTN_SCALE = HEAD_DIM ** -0.5
WIN_R = 8
WIN_C = 16
NEG_INF = -1e30
POOL_WIDTH = D_MODEL // 2
POOL_SIZES = (2, 4, 8, 16)
POOL_GROUP = POOL_WIDTH // len(POOL_SIZES)
HYENA_WIDTH = D_MODEL // 2
HYENA_ORDER = 2
FILTER_EMB = 33
FILTER_HIDDEN = 64
DECAY_TARGET = 1e-2
FAST_DECAY_PCT = 0.3
SLOW_DECAY_PCT = 1.5
D_FF = ((8 * D_MODEL + 767) // 768) * 256
COL_POOL = 0
COL_Q = POOL_WIDTH
COL_K = COL_Q + ATTN_WIDTH
COL_V = COL_K + ATTN_WIDTH
COL_HY = COL_V + ATTN_WIDTH
COL_GATES = COL_HY + (HYENA_ORDER + 1) * HYENA_WIDTH
MOD_ROWS = 8
RMS_EPS = 1e-6

LANES = 128
VMEM_LIMIT = 56 * 1024 * 1024
SUB_M = 256
CH_TILE = 512
SHORT_SEQ = 512


def _cparams(*sem):
    return pltpu.CompilerParams(dimension_semantics=sem, vmem_limit_bytes=VMEM_LIMIT)


def _row_chunks(tm, sub=SUB_M):
    sub = min(sub, tm)
    return [slice(r, r + sub) for r in range(0, tm, sub)]


def _token_tile(m_tok):
    return 1024 if m_tok % 1024 == 0 else 512


def _mod_kernel(c_ref, w_ref, b_ref, o_ref):
    c = c_ref[...]
    s = (c * jax.nn.sigmoid(c)).astype(BF16)
    o_ref[...] = jnp.dot(s, w_ref[...].astype(BF16), preferred_element_type=F32) + b_ref[...]


def _modulation(cmat, ada_w, ada_b):
    tn = 512
    n6 = ada_w.shape[-1]
    return pl.pallas_call(
        _mod_kernel,
        out_shape=jax.ShapeDtypeStruct((DEPTH, MOD_ROWS, n6), F32),
        grid=(DEPTH, n6 // tn),
        in_specs=[pl.BlockSpec((MOD_ROWS, D_MODEL), lambda l, j: (0, 0)),
                  pl.BlockSpec((None, D_MODEL, tn), lambda l, j: (l, 0, j)),
                  pl.BlockSpec((None, 1, tn), lambda l, j: (l, 0, j))],
        out_specs=pl.BlockSpec((None, MOD_ROWS, tn), lambda l, j: (l, 0, j)),
        compiler_params=_cparams("parallel", "parallel"),
        name="modulation",
    )(cmat, ada_w, ada_b.reshape(DEPTH, 1, n6))


class _Stream:
    def __init__(self, batch, seq, mod_base, per_batch_mod):
        self.batch = batch
        self.seq = seq
        self.mod_base = mod_base
        self.per_batch_mod = per_batch_mod

    def mod_index(self, layer, i, tm, which):
        row = self.mod_base + ((i * tm) // self.seq if self.per_batch_mod else 0)
        return (layer * MOD_ROWS + row) * 6 + which


_STAT_ROWS = 32
_NORM_ROWS = 16


def _normmod_kernel(x_ref, g_ref, sc_ref, sh_ref, o_ref, rstd_ref, gain_ref, shift_ref):
    tm, d = x_ref.shape
    gain_ref[...] = jnp.broadcast_to(g_ref[...] * (1.0 + sc_ref[0]), (_NORM_ROWS, d))
    shift_ref[...] = jnp.broadcast_to(sh_ref[0], (_NORM_ROWS, d))

    def stats(ci, carry):
        base = pl.multiple_of(ci * _STAT_ROWS, _STAT_ROWS)
        x = x_ref[pl.ds(base, _STAT_ROWS), :]
        sq = x * x
        while sq.shape[1] > LANES:
            half = sq.shape[1] // 2
            sq = sq[:, :half] + sq[:, half:]
        rstd_ref[pl.ds(base, _STAT_ROWS), :] = sq
        return carry

    lax.fori_loop(0, tm // _STAT_ROWS, stats, 0)
    rstd = lax.rsqrt(jnp.sum(rstd_ref[...], axis=-1, keepdims=True) * (1.0 / d) + RMS_EPS)
    rstd_ref[...] = jnp.broadcast_to(rstd, (tm, LANES))

    def scale(ci, carry):
        base = pl.multiple_of(ci * _NORM_ROWS, _NORM_ROWS)
        x = x_ref[pl.ds(base, _NORM_ROWS), :]
        rstd = jnp.concatenate([rstd_ref[pl.ds(base, _NORM_ROWS), :]] * (d // LANES), axis=1)
        o_ref[pl.ds(base, _NORM_ROWS), :] = (x * rstd * gain_ref[...] + shift_ref[...]).astype(o_ref.dtype)
        return carry

    lax.fori_loop(0, tm // _NORM_ROWS, scale, 0)


def _normmod(x, g, modflat, st, layer, which_scale, which_shift):
    m_tok, d = x.shape
    tm = 256
    return pl.pallas_call(
        _normmod_kernel,
        out_shape=jax.ShapeDtypeStruct((m_tok, d), BF16),
        grid=(m_tok // tm,),
        in_specs=[pl.BlockSpec((tm, d), lambda i: (i, 0)),
                  pl.BlockSpec((None, 1, d), lambda i: (layer, 0, 0)),
                  pl.BlockSpec((1, 1, d), lambda i: (st.mod_index(layer, i, tm, which_scale), 0, 0)),
                  pl.BlockSpec((1, 1, d), lambda i: (st.mod_index(layer, i, tm, which_shift), 0, 0))],
        out_specs=pl.BlockSpec((tm, d), lambda i: (i, 0)),
        scratch_shapes=[pltpu.VMEM((tm, LANES), F32), pltpu.VMEM((_NORM_ROWS, d), F32),
                        pltpu.VMEM((_NORM_ROWS, d), F32)],
        compiler_params=_cparams("parallel"),
        name="normmod",
    )(x, g.reshape(DEPTH, 1, d), modflat, modflat)


def _bf16_weight(w_ref, wbf_ref):
    if wbf_ref is None:
        return w_ref

    @pl.when(pl.program_id(1) == 0)
    def _():
        wbf_ref[...] = w_ref[...].astype(BF16)

    return wbf_ref


def _lin_plain_kernel(x_ref, w_ref, o_ref, wbf_ref=None):
    w = _bf16_weight(w_ref, wbf_ref)
    for rows in _row_chunks(x_ref.shape[0]):
        acc = jnp.dot(x_ref[rows, :], w[...], preferred_element_type=F32)
        o_ref[rows, :] = acc.astype(o_ref.dtype)


def _lin_sigmoid_kernel(x_ref, w_ref, o_ref, wbf_ref=None):
    w = _bf16_weight(w_ref, wbf_ref)
    for rows in _row_chunks(x_ref.shape[0]):
        acc = jnp.dot(x_ref[rows, :], w[...], preferred_element_type=F32)
        o_ref[rows, :] = jax.nn.sigmoid(acc).astype(o_ref.dtype)


def _lin_rmshead_kernel(x_ref, w_ref, g_ref, o_ref, wbf_ref=None):
    w = _bf16_weight(w_ref, wbf_ref)
    g = g_ref[...]
    for rows in _row_chunks(x_ref.shape[0]):
        acc = jnp.dot(x_ref[rows, :], w[...], preferred_element_type=F32)
        for h in range(acc.shape[1] // HEAD_DIM):
            a = acc[:, h * HEAD_DIM:(h + 1) * HEAD_DIM]
            y = a * lax.rsqrt(jnp.mean(a * a, axis=-1, keepdims=True) + RMS_EPS)
            o_ref[rows, h * HEAD_DIM:(h + 1) * HEAD_DIM] = (y * g).astype(o_ref.dtype)


def _lin_residual_kernel(x_ref, w_ref, res_ref, gate_ref, o_ref, wbf_ref=None):
    w = _bf16_weight(w_ref, wbf_ref)
    for rows in _row_chunks(x_ref.shape[0]):
        acc = jnp.dot(x_ref[rows, :], w[...], preferred_element_type=F32)
        o_ref[rows, :] = res_ref[rows, :] + gate_ref[0] * acc


class _SharedWeights:
    def __init__(self):
        self.slabs = {}

    def get(self, key):
        return self.slabs.get(key)

    def put(self, key, value):
        self.slabs[key] = value


def _weight_plumbing(shared, keys, w_arrays, k_len, ncols, tn, wspec_f32):
    have = [shared.get(key) for key in keys]
    slab_spec = pl.BlockSpec((k_len, tn), lambda n, m: (0, n))
    if all(h is not None for h in have):
        return have, [slab_spec] * len(keys), [], [], True
    slab = jax.ShapeDtypeStruct((k_len, ncols), BF16)
    return list(w_arrays), [wspec_f32] * len(keys), [slab] * len(keys), [slab_spec] * len(keys), False


def _linear(x, w, layer, col_off, ncols, *, tn, kind, out_dtype, shared, key, tn_reuse=None, tm=None,
            tm_reuse=None, k_off=0, k_len=None, head_gain=None, residual=None, modflat=None, st=None,
            which_gate=None):
    m_tok = x.shape[0]
    k_len = x.shape[1] if k_len is None else k_len
    if shared.get(key) is not None:
        tn = tn if tn_reuse is None else tn_reuse
        tm = tm if tm_reuse is None else tm_reuse
    tm = _token_tile(m_tok) if tm is None else tm
    cb = col_off // tn
    kb = k_off // k_len
    wspec = pl.BlockSpec((None, k_len, tn), lambda n, m: (layer, kb, cb + n))
    wargs, wspecs, extra_shapes, extra_specs, reuse = _weight_plumbing(
        shared, [key], [w], k_len, ncols, tn, wspec)
    in_specs = [pl.BlockSpec((tm, k_len), lambda n, m: (m, kb))] + wspecs
    args = [x] + wargs
    if kind == "plain":
        body = _lin_plain_kernel
    elif kind == "sigmoid":
        body = _lin_sigmoid_kernel
    elif kind == "rmshead":
        body = _lin_rmshead_kernel
        in_specs.append(pl.BlockSpec((None, 1, HEAD_DIM), lambda n, m: (layer, 0, 0)))
        args.append(head_gain.reshape(DEPTH, 1, HEAD_DIM))
    else:
        body = _lin_residual_kernel
        in_specs.append(pl.BlockSpec((tm, tn), lambda n, m: (m, n)))
        in_specs.append(pl.BlockSpec(
            (1, 1, tn), lambda n, m: (st.mod_index(layer, m, tm, which_gate), 0, n)))
        args += [residual, modflat]
    outs = pl.pallas_call(
        body,
        out_shape=[jax.ShapeDtypeStruct((m_tok, ncols), out_dtype)] + extra_shapes,
        grid=(ncols // tn, m_tok // tm),
        in_specs=in_specs,
        out_specs=[pl.BlockSpec((tm, tn), lambda n, m: (m, n))] + extra_specs,
        compiler_params=_cparams("parallel", "arbitrary"),
        name="linear_" + kind,
    )(*args)
    if not reuse:
        shared.put(key, outs[1])
    return outs[0]


def _mix_kernel(xp_ref, xa_ref, xh_ref, wp_ref, wa_ref, wh_ref, gp_ref, ga_ref, gh_ref, o_ref,
                wpb=None, wab=None, whb=None):
    wp = _bf16_weight(wp_ref, wpb)
    wa = _bf16_weight(wa_ref, wab)
    wh = _bf16_weight(wh_ref, whb)
    for rows in _row_chunks(xp_ref.shape[0], 2 * SUB_M):
        acc = gp_ref[rows, :].astype(F32) * jnp.dot(xp_ref[rows, :], wp[...], preferred_element_type=F32)
        acc = acc + ga_ref[rows, :].astype(F32) * jnp.dot(xa_ref[rows, :], wa[...], preferred_element_type=F32)
        acc = acc + gh_ref[rows, :].astype(F32) * jnp.dot(xh_ref[rows, :], wh[...], preferred_element_type=F32)
        o_ref[rows, :] = acc.astype(o_ref.dtype)


def _mix(xp, xa, xh, wp, wa, wh, gates, layer, shared):
    m_tok, k = xp.shape
    n = wp.shape[-1]
    tn = 512
    keys = ["w_branch_pool", "w_branch_attn", "w_branch_hyena"]
    wspec = pl.BlockSpec((None, k, tn), lambda j, i: (layer, 0, j))
    wargs, wspecs, extra_shapes, extra_specs, reuse = _weight_plumbing(
        shared, keys, [wp, wa, wh], k, n, tn, wspec)
    tm = _token_tile(m_tok) if reuse else 512
    nb = n // tn
    xspec = pl.BlockSpec((tm, k), lambda j, i: (i, 0))
    outs = pl.pallas_call(
        _mix_kernel,
        out_shape=[jax.ShapeDtypeStruct((m_tok, n), BF16)] + extra_shapes,
        grid=(nb, m_tok // tm),
        in_specs=[xspec, xspec, xspec] + wspecs + [
            pl.BlockSpec((tm, tn), lambda j, i: (i, j)),
            pl.BlockSpec((tm, tn), lambda j, i: (i, nb + j)),
            pl.BlockSpec((tm, tn), lambda j, i: (i, 2 * nb + j))],
        out_specs=[pl.BlockSpec((tm, tn), lambda j, i: (i, j))] + extra_specs,
        compiler_params=_cparams("parallel", "arbitrary"),
        name="mix",
    )(xp, xa, xh, *wargs, gates, gates, gates)
    if not reuse:
        for key, slab in zip(keys, outs[1:]):
            shared.put(key, slab)
    return outs[0]


def _ffn1_kernel(x_ref, wg_ref, wu_ref, o_ref, wgb=None, wub=None):
    wg = _bf16_weight(wg_ref, wgb)
    wu = _bf16_weight(wu_ref, wub)
    for rows in _row_chunks(x_ref.shape[0], 2 * SUB_M):
        x = x_ref[rows, :]
        g = jnp.dot(x, wg[...], preferred_element_type=F32)
        u = jnp.dot(x, wu[...], preferred_element_type=F32)
        o_ref[rows, :] = (g * jax.nn.sigmoid(g) * u).astype(o_ref.dtype)


def _ffn1(x, wg, wu, layer, shared):
    m_tok, k = x.shape
    n = wg.shape[-1]
    tm, tn = _token_tile(m_tok), 256
    keys = ["w_gate", "w_up"]
    wspec = pl.BlockSpec((None, k, tn), lambda j, i: (layer, 0, j))
    wargs, wspecs, extra_shapes, extra_specs, reuse = _weight_plumbing(
        shared, keys, [wg, wu], k, n, tn, wspec)
    outs = pl.pallas_call(
        _ffn1_kernel,
        out_shape=[jax.ShapeDtypeStruct((m_tok, n), BF16)] + extra_shapes,
        grid=(n // tn, m_tok // tm),
        in_specs=[pl.BlockSpec((tm, k), lambda j, i: (i, 0))] + wspecs,
        out_specs=[pl.BlockSpec((tm, tn), lambda j, i: (i, j))] + extra_specs,
        compiler_params=_cparams("parallel", "arbitrary"),
        name="ffn_gate_up",
    )(x, *wargs)
    if not reuse:
        for key, slab in zip(keys, outs[1:]):
            shared.put(key, slab)
    return outs[0]


_POOL_PAD = 16
_ROW_CHUNK = 64


def _halo_chunks(u_ref, seq, rc, pad, emit):
    zeros = jnp.zeros((pad, u_ref.shape[1]), F32)
    emit(0, jnp.concatenate([zeros, u_ref[0:rc + pad, :]], axis=0))

    def chunk(ci, carry):
        base = pl.multiple_of(ci * rc, rc)
        emit(base, u_ref[pl.ds(pl.multiple_of(base - pad, pad), rc + 2 * pad), :])
        return carry

    lax.fori_loop(1, seq // rc - 1, chunk, 0)
    emit(seq - rc, jnp.concatenate([u_ref[seq - rc - pad:seq, :], zeros], axis=0))


def _pool_kernel(u_ref, w_ref, sc_ref, o_ref, d_ref, *, seq):
    g = pl.program_id(1)
    width = u_ref.shape[1]
    rc = _ROW_CHUNK

    for gi, win in enumerate(POOL_SIZES):
        half = win // 2

        @pl.when(g == gi)
        def _(win=win, half=half):
            def emit(base, window):
                lo = _POOL_PAD - half
                acc = window[lo:lo + rc]
                for j in range(1, win):
                    acc = acc + window[lo + j:lo + j + rc]
                t = base + lax.broadcasted_iota(jnp.int32, (rc, width), 0)
                cnt = (jnp.minimum(t + (win - half), seq) - jnp.maximum(t - half, 0)).astype(F32)
                centre = window[_POOL_PAD:_POOL_PAD + rc]
                d_ref[pl.ds(base, rc), :] = (acc / cnt - centre).astype(BF16)

            _halo_chunks(u_ref, seq, rc, _POOL_PAD, emit)

    wbf = w_ref[...].astype(BF16)
    scale = sc_ref[...]
    for rows in _row_chunks(seq):
        y = jnp.dot(d_ref[rows, :], wbf, preferred_element_type=F32) * scale
        o_ref[rows, :] = y.astype(o_ref.dtype)


def _pool(u, pool_w, pool_scale, layer, st):
    seq = st.seq
    return pl.pallas_call(
        functools.partial(_pool_kernel, seq=seq),
        out_shape=jax.ShapeDtypeStruct((st.batch * seq, POOL_WIDTH), BF16),
        grid=(st.batch, len(POOL_SIZES)),
        in_specs=[pl.BlockSpec((seq, POOL_GROUP), lambda b, g: (b, g)),
                  pl.BlockSpec((None, None, POOL_GROUP, POOL_GROUP), lambda b, g: (layer, g, 0, 0)),
                  pl.BlockSpec((None, 1, POOL_GROUP), lambda b, g: (layer, 0, g))],
        out_specs=pl.BlockSpec((seq, POOL_GROUP), lambda b, g: (b, g)),
        scratch_shapes=[pltpu.VMEM((seq, POOL_GROUP), BF16)],
        compiler_params=_cparams("parallel", "parallel"),
        name="pool_mixer",
    )(u, pool_w, pool_scale.reshape(DEPTH, 1, POOL_WIDTH))


def _ctx_attn_kernel(q_ref, k_ref, v_ref, o_ref):
    for h in range(N_HEADS):
        sl = slice(h * HEAD_DIM, (h + 1) * HEAD_DIM)
        q = q_ref[:, sl]
        k = k_ref[:, sl].astype(BF16)
        v = v_ref[:, sl].astype(BF16)
        s = lax.dot_general(q, k, (((1,), (1,)), ((), ())), preferred_element_type=F32) * ATTN_SCALE
        m = jnp.max(s, axis=-1, keepdims=True)
        p = jnp.exp(s - m)
        denom = jnp.sum(p, axis=-1, keepdims=True)
        o = jnp.dot(p.astype(BF16), v, preferred_element_type=F32) / denom
        o_ref[:, sl] = o.astype(o_ref.dtype)


def _ctx_attention(q, k, v, st):
    spec = pl.BlockSpec((st.seq, ATTN_WIDTH), lambda b: (b, 0))
    return pl.pallas_call(
        _ctx_attn_kernel,
        out_shape=jax.ShapeDtypeStruct((st.batch * st.seq, ATTN_WIDTH), BF16),
        grid=(st.batch,),
        in_specs=[spec, spec, spec],
        out_specs=spec,
        compiler_params=_cparams("parallel"),
        name="context_attention",
    )(q, k, v)


_BIAS_TILES = 2 * WIN_R
_Q_GROUP = 8
_BAND = 2 * WIN_R


def _bias_table_kernel(rpb_ref, o_ref, *, layer):
    h = pl.program_id(0)
    shape = (GRID_W, 2 * GRID_W)
    qc = lax.broadcasted_iota(jnp.int32, shape, 0)
    lane = lax.broadcasted_iota(jnp.int32, shape, 1)
    kc = lane % GRID_W
    upper = lane >= GRID_W
    c0 = jnp.clip(qc - WIN_C // 2, 0, GRID_W - WIN_C)
    in_window = (kc >= c0) & (kc < c0 + WIN_C)
    rel = kc - qc + (WIN_C - 1)
    n_dc = 2 * WIN_C - 1
    n_dr = 2 * WIN_R - 1
    row_base = (layer * N_HEADS + h) * n_dr
    for e in range(_BIAS_TILES):
        d_lo = max(e - 1, 0)
        d_hi = min(e, n_dr - 1)
        tile = jnp.zeros(shape, F32)
        for dc in range(n_dc):
            lo = rpb_ref[(row_base + d_lo) * n_dc + dc]
            hi = rpb_ref[(row_base + d_hi) * n_dc + dc]
            tile = jnp.where(rel == dc, jnp.where(upper, hi, lo), tile)
        valid = in_window
        if e == 0:
            valid = valid & upper
        if e == _BIAS_TILES - 1:
            valid = valid & jnp.logical_not(upper)
        o_ref[e] = jnp.where(valid, tile, NEG_INF)


def _bias_table(rpb, layer):
    return pl.pallas_call(
        functools.partial(_bias_table_kernel, layer=layer),
        out_shape=jax.ShapeDtypeStruct((N_HEADS, _BIAS_TILES, GRID_W, 2 * GRID_W), F32),
        grid=(N_HEADS,),
        in_specs=[pl.BlockSpec(memory_space=pltpu.SMEM)],
        out_specs=pl.BlockSpec((None, _BIAS_TILES, GRID_W, 2 * GRID_W), lambda h: (h, 0, 0, 0)),
        compiler_params=_cparams("parallel"),
        name="rel_pos_bias_table",
    )(rpb.reshape(-1))


def _nbr_attn_kernel(q_ref, k_ref, v_ref, kc_ref, vc_ref, bias_ref, o_ref,
                     kbf, vbf, s_ref, sc_ref, p_ref, pc_ref, *, rows):
    kbf[...] = k_ref[...].astype(BF16)
    vbf[...] = v_ref[...].astype(BF16)
    kc = kc_ref[...].astype(BF16)
    vc = vc_ref[...].astype(BF16)
    dims = (((1,), (1,)), ((), ()))
    pair = 2 * GRID_W
    lower = lax.broadcasted_iota(jnp.int32, (GRID_W, pair), 1) < GRID_W
    zero_tile = jnp.zeros((GRID_W, pair), BF16)

    for g in range(rows // _Q_GROUP):
        slot = g % 2
        ks = min(max(g * _Q_GROUP - WIN_R // 2, 0), rows - _BAND)
        qsl = slice(g * _Q_GROUP * GRID_W, (g + 1) * _Q_GROUP * GRID_W)
        ksl = slice(ks * GRID_W, (ks + _BAND) * GRID_W)
        qg = q_ref[qsl, :]
        s_ref[slot] = lax.dot_general(qg, kbf[ksl, :], dims, preferred_element_type=F32) * ATTN_SCALE
        sc_ref[slot] = lax.dot_general(qg, kc, dims, preferred_element_type=F32) * ATTN_SCALE
        for i in range(_Q_GROUP):
            qr = g * _Q_GROUP + i
            r0 = min(max(qr - WIN_R // 2, 0), rows - WIN_R)
            first = r0 - ks
            j0, j1 = first // 2, (first + WIN_R + 1) // 2
            rsl = slice(i * GRID_W, (i + 1) * GRID_W)
            tiles = []
            for j in range(j0, j1):
                kr0 = ks + 2 * j
                t = s_ref[slot, rsl, j * pair:(j + 1) * pair] + bias_ref[kr0 - qr + WIN_R]
                if kr0 < r0:
                    t = jnp.where(lower, NEG_INF, t)
                if kr0 + 1 >= r0 + WIN_R:
                    t = jnp.where(lower, t, NEG_INF)
                tiles.append(t)
            sc = sc_ref[slot, rsl, :]
            mt = functools.reduce(jnp.maximum, tiles)
            m = jnp.maximum(jnp.max(mt, axis=-1, keepdims=True), jnp.max(sc, axis=-1, keepdims=True))
            ps = [jnp.exp(t - m) for t in tiles]
            pc = jnp.exp(sc - m)
            denom = (jnp.sum(functools.reduce(jnp.add, ps), axis=-1, keepdims=True)
                     + jnp.sum(pc, axis=-1, keepdims=True))
            inv = 1.0 / denom
            for j in range(_BAND // 2):
                val = (ps[j - j0] * inv).astype(BF16) if j0 <= j < j1 else zero_tile
                p_ref[slot, rsl, j * pair:(j + 1) * pair] = val
            pc_ref[slot, rsl, :] = (pc * inv).astype(BF16)
        o = (jnp.dot(p_ref[slot], vbf[ksl, :], preferred_element_type=F32)
             + jnp.dot(pc_ref[slot], vc, preferred_element_type=F32))
        o_ref[qsl, :] = o.astype(o_ref.dtype)


def _nbr_attention(q, k, v, cache_k, cache_v, bias, layer, st):
    seq = st.seq
    rows = seq // GRID_W
    assert rows >= _BAND and rows % _Q_GROUP == 0
    past = cache_k.shape[2]
    ck = cache_k.reshape(st.batch, DEPTH, past, ATTN_WIDTH)
    cv = cache_v.reshape(st.batch, DEPTH, past, ATTN_WIDTH)
    tok_spec = pl.BlockSpec((seq, HEAD_DIM), lambda b, h: (b, h))
    cache_spec = pl.BlockSpec((None, None, past, HEAD_DIM), lambda b, h: (b, layer, 0, h))
    nq = _Q_GROUP * GRID_W
    nk = _BAND * GRID_W
    return pl.pallas_call(
        functools.partial(_nbr_attn_kernel, rows=rows),
        out_shape=jax.ShapeDtypeStruct((st.batch * seq, ATTN_WIDTH), BF16),
        grid=(st.batch, N_HEADS),
        in_specs=[tok_spec, tok_spec, tok_spec, cache_spec, cache_spec,
                  pl.BlockSpec((None, _BIAS_TILES, GRID_W, 2 * GRID_W), lambda b, h: (h, 0, 0, 0))],
        out_specs=tok_spec,
        scratch_shapes=[pltpu.VMEM((seq, HEAD_DIM), BF16), pltpu.VMEM((seq, HEAD_DIM), BF16),
                        pltpu.VMEM((2, nq, nk), F32), pltpu.VMEM((2, nq, past), F32),
                        pltpu.VMEM((2, nq, nk), BF16), pltpu.VMEM((2, nq, past), BF16)],
        compiler_params=_cparams("parallel", "parallel"),
        name="neighbourhood_attention",
    )(q, k, v, ck, cv, bias)


_CONV_PAD = 8


def _short_conv_kernel(u_ref, w_ref, b_ref, o_ref, *, seq):
    width = u_ref.shape[1]
    w0 = w_ref[0:1, :]
    w1 = w_ref[1:2, :]
    w2 = w_ref[2:3, :]
    b = b_ref[...]
    rc = max(8, _ROW_CHUNK * CH_TILE // width)

    def emit(base, window):
        prev = window[_CONV_PAD - 1:_CONV_PAD - 1 + rc]
        cur = window[_CONV_PAD:_CONV_PAD + rc]
        nxt = window[_CONV_PAD + 1:_CONV_PAD + 1 + rc]
        o_ref[pl.ds(base, rc), :] = prev * w0 + cur * w1 + nxt * w2 + b

    _halo_chunks(u_ref, seq, rc, _CONV_PAD, emit)


def _short_conv(u, conv_w, conv_b, layer, st):
    width = u.shape[1]
    seq = st.seq
    tc = HYENA_WIDTH if seq <= SHORT_SEQ else CH_TILE
    return pl.pallas_call(
        functools.partial(_short_conv_kernel, seq=seq),
        out_shape=jax.ShapeDtypeStruct((st.batch * seq, width), F32),
        grid=(st.batch, width // tc),
        in_specs=[pl.BlockSpec((seq, tc), lambda b, j: (b, j)),
                  pl.BlockSpec((None, 3, tc), lambda b, j: (layer, 0, j)),
                  pl.BlockSpec((None, 1, tc), lambda b, j: (layer, 0, j))],
        out_specs=pl.BlockSpec((seq, tc), lambda b, j: (b, j)),
        compiler_params=_cparams("parallel", "parallel"),
        name="hyena_short_conv",
    )(u, conv_w, conv_b.reshape(DEPTH, 1, width))


def _dft_matrices(seq):
    f = jnp.arange(seq, dtype=jnp.int32)[:, None]
    t = jnp.arange(seq, dtype=jnp.int32)[None, :]
    ang = ((f * t) % (2 * seq)).astype(F32) * (math.pi / seq)
    top = jnp.cos(ang)
    bot = jnp.where(f == 0, (1 - 2 * (t % 2)).astype(F32), -jnp.sin(ang))
    colscale = jnp.where(jnp.arange(seq) == 0, 0.5, 1.0).astype(F32)[None, :] / seq
    inv = jnp.concatenate([top.T * colscale, bot.T * colscale], axis=1)
    return top.astype(BF16), bot.astype(BF16), inv.astype(BF16)


def _filter_mlp_kernel(z_ref, w1_ref, b1_ref, w2_ref, b2_ref, fr_ref, o_ref):
    fr = fr_ref[...]
    h = jnp.sin(fr * (jnp.dot(z_ref[...], w1_ref[...], preferred_element_type=F32) + b1_ref[...]))
    o_ref[...] = jnp.sin(fr * (jnp.dot(h, w2_ref[...], preferred_element_type=F32) + b2_ref[...]))


def _filter_mlp(seq, w1, b1, w2, b2, freq, layer):
    t = jnp.linspace(0.0, 1.0, seq, dtype=F32)[:, None]
    bands = (FILTER_EMB - 1) // 2
    ang = (2.0 * math.pi / seq) * jnp.arange(seq, dtype=F32)[:, None] * jnp.linspace(
        1e-4, bands - 1, bands, dtype=F32)[None]
    z = jnp.concatenate([t, jnp.cos(ang), -jnp.sin(ang)], axis=-1)
    z = jnp.pad(z, ((0, 0), (0, LANES - FILTER_EMB)))
    w1p = jnp.pad(w1, ((0, 0), (0, LANES - FILTER_EMB), (0, 0)))
    vec = lambda a: a.reshape(DEPTH, 1, FILTER_HIDDEN)
    vspec = pl.BlockSpec((None, 1, FILTER_HIDDEN), lambda i: (layer, 0, 0))
    return pl.pallas_call(
        _filter_mlp_kernel,
        out_shape=jax.ShapeDtypeStruct((seq, FILTER_HIDDEN), F32),
        grid=(1,),
        in_specs=[pl.BlockSpec((seq, LANES), lambda i: (0, 0)),
                  pl.BlockSpec((None, LANES, FILTER_HIDDEN), lambda i: (layer, 0, 0)),
                  vspec,
                  pl.BlockSpec((None, FILTER_HIDDEN, FILTER_HIDDEN), lambda i: (layer, 0, 0)),
                  vspec, vspec],
        out_specs=pl.BlockSpec((seq, FILTER_HIDDEN), lambda i: (0, 0)),
        compiler_params=_cparams("arbitrary"),
        name="hyena_filter_mlp",
    )(z, w1p, vec(b1), w2, vec(b2), vec(freq))


def _filter_taps_kernel(h_ref, dl_ref, wf_ref, wb_ref, a_ref, d_ref, nyq_ref, *, seq):
    h = h_ref[...]
    shape = (seq, dl_ref.shape[1])
    row = lax.broadcasted_iota(jnp.int32, shape, 0)
    t = row.astype(F32) / float(seq - 1)
    dec = jnp.exp(-t * dl_ref[...])
    fwd = jnp.dot(h, wf_ref[...], preferred_element_type=F32) * dec
    bwd = jnp.dot(h, wb_ref[...], preferred_element_type=F32) * dec
    bwd = jnp.where(row == 0, 0.0, bwd)
    inv = 1.0 / jnp.sum(jnp.abs(fwd) + jnp.abs(bwd), axis=0, keepdims=True)
    a = (fwd + bwd) * inv
    a_ref[...] = a
    d_ref[...] = (fwd - bwd) * inv
    sign = jnp.where(row % 2 == 0, 1.0, -1.0)
    nyq_ref[0:1, :] = jnp.sum(a * sign, axis=0, keepdims=True)
    quarter = jnp.where(row % 4 == 0, 1.0, jnp.where(row % 4 == 2, -1.0, 0.0))
    nyq_ref[1:2, :] = jnp.sum(a * quarter, axis=0, keepdims=True)


def _filter_taps(hidden, w3, layer, *, seq):
    tc = 256
    nct = HYENA_WIDTH // tc
    deltas = jnp.abs(jnp.linspace(math.log(DECAY_TARGET) / SLOW_DECAY_PCT,
                                  math.log(DECAY_TARGET) / FAST_DECAY_PCT, HYENA_WIDTH, dtype=F32))
    tap_shape = jax.ShapeDtypeStruct((HYENA_ORDER, seq, HYENA_WIDTH), F32)
    tap_spec = pl.BlockSpec((None, seq, tc), lambda o, j: (o, 0, j))
    return pl.pallas_call(
        functools.partial(_filter_taps_kernel, seq=seq),
        out_shape=(tap_shape, tap_shape, jax.ShapeDtypeStruct((HYENA_ORDER, 2, HYENA_WIDTH), F32)),
        grid=(HYENA_ORDER, nct),
        in_specs=[pl.BlockSpec((seq, FILTER_HIDDEN), lambda o, j: (0, 0)),
                  pl.BlockSpec((1, tc), lambda o, j: (0, j)),
                  pl.BlockSpec((None, FILTER_HIDDEN, tc), lambda o, j: (layer, 0, (2 * o) * nct + j)),
                  pl.BlockSpec((None, FILTER_HIDDEN, tc), lambda o, j: (layer, 0, (2 * o + 1) * nct + j))],
        out_specs=(tap_spec, tap_spec, pl.BlockSpec((None, 2, tc), lambda o, j: (o, 0, j))),
        compiler_params=_cparams("parallel", "parallel"),
        name="hyena_filter_taps",
    )(hidden, deltas.reshape(1, HYENA_WIDTH), w3, w3)


def _filter_spec_kernel(wt_ref, wb_ref, a_ref, d_ref, nyq_ref, o_ref, abf, dbf):
    i = pl.program_id(2)

    @pl.when(i == 0)
    def _():
        abf[...] = a_ref[...].astype(BF16)
        dbf[...] = d_ref[...].astype(BF16)

    for rows in _row_chunks(wt_ref.shape[0]):
        top = jnp.dot(wt_ref[rows, :], abf[...], preferred_element_type=F32)
        bot = jnp.dot(wb_ref[rows, :], dbf[...], preferred_element_type=F32)
        if rows.start == 0:
            row = lax.broadcasted_iota(jnp.int32, bot.shape, 0)
            bot = jnp.where((row == 0) & (i == 0), nyq_ref[0:1, :], bot)
        o_ref[0, rows, :] = top
        o_ref[1, rows, :] = bot


def _dft_tiles(seq):
    return seq, (HYENA_WIDTH if seq <= SHORT_SEQ else CH_TILE)


_RESIDENT = pl.Buffered(1)


def _filter_spectrum(dft_top, dft_bot, taps_a, taps_d, nyq, *, seq):
    tf, tc = _dft_tiles(seq)
    tc = min(tc, CH_TILE // 2)
    wspec = pl.BlockSpec((tf, seq), lambda o, c, i: (i, 0), pipeline_mode=_RESIDENT)
    tspec = pl.BlockSpec((None, seq, tc), lambda o, c, i: (o, 0, c))
    return pl.pallas_call(
        _filter_spec_kernel,
        out_shape=jax.ShapeDtypeStruct((HYENA_ORDER, 2, seq, HYENA_WIDTH), F32),
        grid=(HYENA_ORDER, HYENA_WIDTH // tc, seq // tf),
        in_specs=[wspec, wspec, tspec, tspec,
                  pl.BlockSpec((None, 2, tc), lambda o, c, i: (o, 0, c))],
        out_specs=pl.BlockSpec((None, 2, tf, tc), lambda o, c, i: (o, 0, i, c)),
        scratch_shapes=[pltpu.VMEM((seq, tc), BF16)] * 2,
        compiler_params=_cparams("parallel", "parallel", "arbitrary"),
        name="hyena_filter_spectrum",
    )(dft_top, dft_bot, taps_a, taps_d, nyq)


def _fwd_dft_kernel(wt_ref, wb_ref, z_ref, p_ref, o_ref, zbf):
    i = pl.program_id(2)

    @pl.when(i == 0)
    def _():
        zbf[...] = z_ref[...].astype(BF16)

    for rows in _row_chunks(wt_ref.shape[0]):
        top = jnp.dot(wt_ref[rows, :], zbf[...], preferred_element_type=F32)
        bot = jnp.dot(wb_ref[rows, :], zbf[...], preferred_element_type=F32)
        pt = p_ref[0, rows, :]
        pb = p_ref[1, rows, :]
        re = top * pt - bot * pb
        im = top * pb + bot * pt
        if rows.start == 0:
            row = lax.broadcasted_iota(jnp.int32, top.shape, 0)
            real_row = (row == 0) & (i == 0)
            re = jnp.where(real_row, top * pt, re)
            im = jnp.where(real_row, bot * pb, im)
        o_ref[0, rows, :] = re.astype(o_ref.dtype)
        o_ref[1, rows, :] = im.astype(o_ref.dtype)


def _fwd_dft(dft_top, dft_bot, z, z_col_off, spec, order, *, batch, seq):
    tf, tc = _dft_tiles(seq)
    zc = z_col_off // tc
    wspec = pl.BlockSpec((tf, seq), lambda c, b, i: (i, 0), pipeline_mode=_RESIDENT)
    return pl.pallas_call(
        _fwd_dft_kernel,
        out_shape=jax.ShapeDtypeStruct((batch, 2, seq, HYENA_WIDTH), BF16),
        grid=(HYENA_WIDTH // tc, batch, seq // tf),
        in_specs=[wspec, wspec,
                  pl.BlockSpec((seq, tc), lambda c, b, i: (b, zc + c)),
                  pl.BlockSpec((None, 2, tf, tc), lambda c, b, i: (order, 0, i, c), pipeline_mode=_RESIDENT)],
        out_specs=pl.BlockSpec((None, 2, tf, tc), lambda c, b, i: (b, 0, i, c)),
        scratch_shapes=[pltpu.VMEM((seq, tc), BF16)],
        compiler_params=_cparams("parallel", "parallel", "arbitrary"),
        name="hyena_fwd_dft",
    )(dft_top, dft_bot, z, spec)


def _inv_dft_kernel(wi_ref, s_ref, gate_ref, z_ref, bias_ref, o_ref):
    s = s_ref[...]
    s2 = s.reshape(s.shape[0] * s.shape[1], s.shape[2])
    for rows in _row_chunks(wi_ref.shape[0]):
        conv = jnp.dot(wi_ref[rows, :], s2, preferred_element_type=F32)
        o_ref[rows, :] = (gate_ref[rows, :] * (conv + bias_ref[...] * z_ref[rows, :])).astype(o_ref.dtype)


def _inv_dft(dft_inv, s, gate, gate_col_off, z, z_col_off, hy_bias, layer, order, out_dtype, *, batch, seq):
    tl, tc = _dft_tiles(seq)
    nrb = seq // tl
    gc = gate_col_off // tc
    zc = z_col_off // tc
    return pl.pallas_call(
        _inv_dft_kernel,
        out_shape=jax.ShapeDtypeStruct((batch * seq, HYENA_WIDTH), out_dtype),
        grid=(HYENA_WIDTH // tc, batch, nrb),
        in_specs=[pl.BlockSpec((tl, 2 * seq), lambda c, b, i: (i, 0), pipeline_mode=_RESIDENT),
                  pl.BlockSpec((None, 2, seq, tc), lambda c, b, i: (b, 0, 0, c)),
                  pl.BlockSpec((tl, tc), lambda c, b, i: (b * nrb + i, gc + c)),
                  pl.BlockSpec((tl, tc), lambda c, b, i: (b * nrb + i, zc + c)),
                  pl.BlockSpec((None, 1, tc), lambda c, b, i: (layer * HYENA_ORDER + order, 0, c))],
        out_specs=pl.BlockSpec((tl, tc), lambda c, b, i: (b * nrb + i, c)),
        compiler_params=_cparams("parallel", "parallel", "arbitrary"),
        name="hyena_inv_dft",
    )(dft_inv, s, gate, z, hy_bias.reshape(DEPTH * HYENA_ORDER, 1, HYENA_WIDTH))


def _dft_matrices_r2(seq):
    half = seq // 2
    g = jnp.arange(half, dtype=jnp.int32)[:, None]
    m = jnp.arange(half, dtype=jnp.int32)[None, :]
    ang_e = ((2 * g * m) % (2 * seq)).astype(F32) * (math.pi / seq)
    ang_o = ((g * (2 * m + 1)) % (2 * seq)).astype(F32) * (math.pi / seq)
    alt = (1 - 2 * (m % 2)).astype(F32)
    ce = jnp.cos(ang_e)
    co = jnp.cos(ang_o)
    se = jnp.where(g == 0, alt, -jnp.sin(ang_e))
    so = jnp.where(g == 0, -alt, -jnp.sin(ang_o))
    colscale = jnp.where(jnp.arange(half) == 0, 0.5, 1.0).astype(F32)[None, :] / seq
    we = jnp.concatenate([ce.T * colscale, se.T / seq], axis=1)
    wo = jnp.concatenate([co.T * colscale, so.T / seq], axis=1)
    return tuple(a.astype(BF16) for a in (ce, co, se, so)), (we.astype(BF16), wo.astype(BF16))


def _half_transforms(mats, xe, xo, ye, yo, rows):
    ce_ref, co_ref, se_ref, so_ref = mats
    ec = jnp.dot(ce_ref[rows, :], xe[...], preferred_element_type=F32)
    oc = jnp.dot(co_ref[rows, :], xo[...], preferred_element_type=F32)
    es = jnp.dot(se_ref[rows, :], ye[...], preferred_element_type=F32)
    os_ = jnp.dot(so_ref[rows, :], yo[...], preferred_element_type=F32)
    return ec, oc, es, os_


def _filter_spec_r2_kernel(ce_ref, co_ref, se_ref, so_ref, ae_ref, ao_ref, de_ref, do_ref, edge_ref, o_ref,
                           aeb, aob, deb, dob):
    aeb[...] = ae_ref[...].astype(BF16)
    aob[...] = ao_ref[...].astype(BF16)
    deb[...] = de_ref[...].astype(BF16)
    dob[...] = do_ref[...].astype(BF16)
    for rows in _row_chunks(ce_ref.shape[0]):
        ec, oc, es, os_ = _half_transforms((ce_ref, co_ref, se_ref, so_ref), aeb, aob, deb, dob, rows)
        im_a = es + os_
        im_b = os_ - es
        if rows.start == 0:
            row0 = lax.broadcasted_iota(jnp.int32, ec.shape, 0) == 0
            im_a = jnp.where(row0, edge_ref[1:2, :], im_a)
            im_b = jnp.where(row0, os_, im_b)
        o_ref[0, rows, :] = ec + oc
        o_ref[1, rows, :] = im_a
        o_ref[2, rows, :] = ec - oc
        o_ref[3, rows, :] = im_b


def _filter_spectrum_r2(mats, a_even, a_odd, d_even, d_odd, edge, *, seq):
    half = seq // 2
    tc = CH_TILE // 2
    nct = HYENA_WIDTH // tc
    mspec = pl.BlockSpec((half, half), lambda o, c: (0, 0), pipeline_mode=_RESIDENT)
    tspec = pl.BlockSpec((None, half, tc), lambda o, c: (o, 0, c))
    return pl.pallas_call(
        _filter_spec_r2_kernel,
        out_shape=jax.ShapeDtypeStruct((HYENA_ORDER, 4, half, HYENA_WIDTH), F32),
        grid=(HYENA_ORDER, nct),
        in_specs=[mspec] * 4 + [tspec] * 4 + [pl.BlockSpec((None, 2, tc), lambda o, c: (o, 0, c))],
        out_specs=pl.BlockSpec((None, 4, half, tc), lambda o, c: (o, 0, 0, c)),
        scratch_shapes=[pltpu.VMEM((half, tc), BF16)] * 4,
        compiler_params=_cparams("parallel", "parallel"),
        name="hyena_filter_spectrum_r2",
    )(*mats, a_even, a_odd, d_even, d_odd, edge)


def _fwd_dft_r2_kernel(ce_ref, co_ref, se_ref, so_ref, ze_ref, zo_ref, p_ref, o_ref, zeb, zob):
    zeb[...] = ze_ref[...].astype(BF16)
    zob[...] = zo_ref[...].astype(BF16)
    for rows in _row_chunks(ce_ref.shape[0]):
        ec, oc, es, os_ = _half_transforms((ce_ref, co_ref, se_ref, so_ref), zeb, zob, zeb, zob, rows)
        re_a, re_b = ec + oc, ec - oc
        im_a, im_b = es + os_, os_ - es
        pra, pia, prb, pib = (p_ref[j, rows, :] for j in range(4))
        first = rows.start == 0
        if first:
            row0 = lax.broadcasted_iota(jnp.int32, ec.shape, 0) == 0
            im_a = jnp.where(row0, es, im_a)
            im_b = jnp.where(row0, os_, im_b)
        re_a2 = re_a * pra - im_a * pia
        im_a2 = re_a * pia + im_a * pra
        re_b2 = re_b * prb - im_b * pib
        im_b2 = re_b * pib + im_b * prb
        if first:
            re_a2 = jnp.where(row0, re_a * pra, re_a2)
            re_b2 = jnp.where(row0, re_b * prb, re_b2)
            im_a2 = jnp.where(row0, im_a * pia - im_b * pib, im_a2)
            im_b2 = jnp.where(row0, im_a * pib + im_b * pia, im_b2)
        us = im_a2 - im_b2
        vs = im_a2 + im_b2
        if first:
            us = jnp.where(row0, im_a2, us)
            vs = jnp.where(row0, im_b2, vs)
        o_ref[0, rows, :] = (re_a2 + re_b2).astype(o_ref.dtype)
        o_ref[1, rows, :] = us.astype(o_ref.dtype)
        o_ref[2, rows, :] = (re_a2 - re_b2).astype(o_ref.dtype)
        o_ref[3, rows, :] = vs.astype(o_ref.dtype)


def _fwd_dft_r2(mats, ze, ze_cb, zo, zo_cb, spec, order, *, batch, seq):
    half = seq // 2
    tc = CH_TILE
    mspec = pl.BlockSpec((half, half), lambda c, b: (0, 0), pipeline_mode=_RESIDENT)
    return pl.pallas_call(
        _fwd_dft_r2_kernel,
        out_shape=jax.ShapeDtypeStruct((batch, 4, half, HYENA_WIDTH), BF16),
        grid=(HYENA_WIDTH // tc, batch),
        in_specs=[mspec] * 4 + [
            pl.BlockSpec((half, tc), lambda c, b: (b, ze_cb + c)),
            pl.BlockSpec((half, tc), lambda c, b: (b, zo_cb + c)),
            pl.BlockSpec((None, 4, half, tc), lambda c, b: (order, 0, 0, c), pipeline_mode=_RESIDENT)],
        out_specs=pl.BlockSpec((None, 4, half, tc), lambda c, b: (b, 0, 0, c)),
        scratch_shapes=[pltpu.VMEM((half, tc), BF16)] * 2,
        compiler_params=_cparams("parallel", "parallel"),
        name="hyena_fwd_dft_r2",
    )(*mats, ze, zo, spec)


def _inv_dft_r2_kernel(we_ref, wo_ref, s_ref, ge_ref, go_ref, ze_ref, zo_ref, bias_ref, oe_ref, oo_ref):
    s = s_ref[...]
    half, tc = s.shape[1], s.shape[2]
    su = s[0:2].reshape(2 * half, tc)
    sv = s[2:4].reshape(2 * half, tc)
    bias = bias_ref[...]
    for rows in _row_chunks(half):
        ye = jnp.dot(we_ref[rows, :], su, preferred_element_type=F32)
        yo = jnp.dot(wo_ref[rows, :], sv, preferred_element_type=F32)
        oe_ref[rows, :] = (ge_ref[rows, :] * (ye + bias * ze_ref[rows, :])).astype(oe_ref.dtype)
        oo_ref[rows, :] = (go_ref[rows, :] * (yo + bias * zo_ref[rows, :])).astype(oo_ref.dtype)


def _inv_dft_r2(inv_mats, s, gate_even, gate_odd, gate_cb, *, z_even, z_odd, hy_bias, layer, order, out_dtype,
                batch, seq):
    half = seq // 2
    tc = CH_TILE
    wspec = pl.BlockSpec((half, seq), lambda c, b: (0, 0), pipeline_mode=_RESIDENT)
    out = jax.ShapeDtypeStruct((batch * half, HYENA_WIDTH), out_dtype)
    ospec = pl.BlockSpec((half, tc), lambda c, b: (b, c))
    return pl.pallas_call(
        _inv_dft_r2_kernel,
        out_shape=(out, out),
        grid=(HYENA_WIDTH // tc, batch),
        in_specs=[wspec, wspec,
                  pl.BlockSpec((None, 4, half, tc), lambda c, b: (b, 0, 0, c)),
                  pl.BlockSpec((half, tc), lambda c, b: (b, gate_cb + c)),
                  pl.BlockSpec((half, tc), lambda c, b: (b, gate_cb + c)),
                  pl.BlockSpec((half, tc), lambda c, b: (b, c)),
                  pl.BlockSpec((half, tc), lambda c, b: (b, c)),
                  pl.BlockSpec((None, 1, tc), lambda c, b: (layer * HYENA_ORDER + order, 0, c))],
        out_specs=(ospec, ospec),
        compiler_params=_cparams("parallel", "parallel"),
        name="hyena_inv_dft_r2",
    )(*inv_mats, s, gate_even, gate_odd, z_even, z_odd, hy_bias.reshape(DEPTH * HYENA_ORDER, 1, HYENA_WIDTH))


def _short_conv_split_kernel(u_ref, w_ref, b_ref, oe_ref, oo_ref, stage, *, seq):
    width = u_ref.shape[1]
    w0 = w_ref[0:1, :]
    w1 = w_ref[1:2, :]
    w2 = w_ref[2:3, :]
    b = b_ref[...]
    rc = _ROW_CHUNK
    hr = rc // 2

    def emit(base, window):
        prev = window[_CONV_PAD - 1:_CONV_PAD - 1 + rc]
        cur = window[_CONV_PAD:_CONV_PAD + rc]
        nxt = window[_CONV_PAD + 1:_CONV_PAD + 1 + rc]
        val = prev * w0 + cur * w1 + nxt * w2 + b
        hb = base // 2 if isinstance(base, int) else pl.multiple_of(base // 2, hr)
        for j in range(width // LANES):
            lanes = slice(j * LANES, (j + 1) * LANES)
            stage[j] = val[:, lanes]
            oe_ref[pl.ds(hb, hr), lanes] = stage[j, pl.ds(0, hr, stride=2), :]
            oo_ref[pl.ds(hb, hr), lanes] = stage[j, pl.ds(1, hr, stride=2), :]

    _halo_chunks(u_ref, seq, rc, _CONV_PAD, emit)


def _short_conv_split(u, conv_w, conv_b, layer, st):
    width = u.shape[1]
    seq = st.seq
    half = seq // 2
    tc = CH_TILE
    out = jax.ShapeDtypeStruct((st.batch * half, width), F32)
    ospec = pl.BlockSpec((half, tc), lambda b, j: (b, j))
    return pl.pallas_call(
        functools.partial(_short_conv_split_kernel, seq=seq),
        out_shape=(out, out),
        grid=(st.batch, width // tc),
        in_specs=[pl.BlockSpec((seq, tc), lambda b, j: (b, j)),
                  pl.BlockSpec((None, 3, tc), lambda b, j: (layer, 0, j)),
                  pl.BlockSpec((None, 1, tc), lambda b, j: (layer, 0, j))],
        out_specs=(ospec, ospec),
        scratch_shapes=[pltpu.VMEM((tc // LANES, _ROW_CHUNK, LANES), F32)],
        compiler_params=_cparams("parallel", "parallel"),
        name="hyena_short_conv_split",
    )(u, conv_w, conv_b.reshape(DEPTH, 1, width))


def _filter_taps_split_kernel(he_ref, ho_ref, dl_ref, wf_ref, wb_ref, ae_ref, ao_ref, de_ref, do_ref,
                              edge_ref, *, seq):
    shape = (seq // 2, dl_ref.shape[1])
    m = lax.broadcasted_iota(jnp.int32, shape, 0)

    def taps(h_ref, parity):
        t = (2 * m + parity).astype(F32) / float(seq - 1)
        dec = jnp.exp(-t * dl_ref[...])
        h = h_ref[...]
        return (jnp.dot(h, wf_ref[...], preferred_element_type=F32) * dec,
                jnp.dot(h, wb_ref[...], preferred_element_type=F32) * dec)

    fe, be = taps(he_ref, 0)
    fo, bo = taps(ho_ref, 1)
    be = jnp.where(m == 0, 0.0, be)
    total = (jnp.sum(jnp.abs(fe) + jnp.abs(be), axis=0, keepdims=True)
             + jnp.sum(jnp.abs(fo) + jnp.abs(bo), axis=0, keepdims=True))
    inv = 1.0 / total
    a_e = (fe + be) * inv
    a_o = (fo + bo) * inv
    ae_ref[...] = a_e
    ao_ref[...] = a_o
    de_ref[...] = (fe - be) * inv
    do_ref[...] = (fo - bo) * inv
    edge_ref[0:1, :] = jnp.sum(a_e, axis=0, keepdims=True) - jnp.sum(a_o, axis=0, keepdims=True)
    alt = jnp.where(m % 2 == 0, 1.0, -1.0)
    edge_ref[1:2, :] = jnp.sum(a_e * alt, axis=0, keepdims=True)


def _filter_taps_split(hidden, w3, layer, *, seq):
    tc = 256
    nct = HYENA_WIDTH // tc
    half = seq // 2
    deltas = jnp.abs(jnp.linspace(math.log(DECAY_TARGET) / SLOW_DECAY_PCT,
                                  math.log(DECAY_TARGET) / FAST_DECAY_PCT, HYENA_WIDTH, dtype=F32))
    tap_shape = jax.ShapeDtypeStruct((HYENA_ORDER, half, HYENA_WIDTH), F32)
    tap_spec = pl.BlockSpec((None, half, tc), lambda o, j: (o, 0, j))
    hspec = pl.BlockSpec((half, FILTER_HIDDEN), lambda o, j: (0, 0))
    return pl.pallas_call(
        functools.partial(_filter_taps_split_kernel, seq=seq),
        out_shape=(tap_shape,) * 4 + (jax.ShapeDtypeStruct((HYENA_ORDER, 2, HYENA_WIDTH), F32),),
        grid=(HYENA_ORDER, nct),
        in_specs=[hspec, hspec,
                  pl.BlockSpec((1, tc), lambda o, j: (0, j)),
                  pl.BlockSpec((None, FILTER_HIDDEN, tc), lambda o, j: (layer, 0, (2 * o) * nct + j)),
                  pl.BlockSpec((None, FILTER_HIDDEN, tc), lambda o, j: (layer, 0, (2 * o + 1) * nct + j))],
        out_specs=(tap_spec,) * 4 + (pl.BlockSpec((None, 2, tc), lambda o, j: (o, 0, j)),),
        compiler_params=_cparams("parallel", "parallel"),
        name="hyena_filter_taps_split",
    )(hidden[0::2], hidden[1::2], deltas.reshape(1, HYENA_WIDTH), w3, w3)


def _hyena_long(u, p, layer, st):
    batch, seq = st.batch, st.seq
    uce, uco = _short_conv_split(u, p["hy_conv_w"], p["hy_conv_b"], layer, st)
    hidden = _filter_mlp(seq, p["filt_w1"], p["filt_b1"], p["filt_w2"], p["filt_b2"], p["filt_freq"], layer)
    taps = _filter_taps_split(hidden, p["filt_w3"], layer, seq=seq)
    mats, inv_mats = _dft_matrices_r2(seq)
    spec = _filter_spectrum_r2(mats, *taps, seq=seq)
    nb = HYENA_WIDTH // CH_TILE
    hy_bias = p["hy_bias"]
    s = _fwd_dft_r2(mats, uce, 0, uco, 0, spec, 0, batch=batch, seq=seq)
    z1e, z1o = _inv_dft_r2(inv_mats, s, uce, uco, nb, z_even=uce, z_odd=uco, hy_bias=hy_bias, layer=layer,
                           order=0, out_dtype=F32, batch=batch, seq=seq)
    s = _fwd_dft_r2(mats, z1e, 0, z1o, 0, spec, 1, batch=batch, seq=seq)
    he, ho = _inv_dft_r2(inv_mats, s, uce, uco, 2 * nb, z_even=z1e, z_odd=z1o, hy_bias=hy_bias, layer=layer,
                         order=1, out_dtype=BF16, batch=batch, seq=seq)
    return jnp.stack([he, ho], axis=1).reshape(batch * seq, HYENA_WIDTH)


def _hyena(u, p, layer, st):
    batch, seq = st.batch, st.seq
    if seq > SHORT_SEQ:
        return _hyena_long(u, p, layer, st)
    uc = _short_conv(u, p["hy_conv_w"], p["hy_conv_b"], layer, st)
    dft_top, dft_bot, dft_inv = _dft_matrices(seq)
    hidden = _filter_mlp(seq, p["filt_w1"], p["filt_b1"], p["filt_w2"], p["filt_b2"], p["filt_freq"], layer)
    taps_a, taps_d, nyq = _filter_taps(hidden, p["filt_w3"], layer, seq=seq)
    spec = _filter_spectrum(dft_top, dft_bot, taps_a, taps_d, nyq, seq=seq)
    s = _fwd_dft(dft_top, dft_bot, uc, 0, spec, 0, batch=batch, seq=seq)
    z1 = _inv_dft(dft_inv, s, uc, HYENA_WIDTH, uc, 0, p["hy_bias"], layer, 0, F32, batch=batch, seq=seq)
    s = _fwd_dft(dft_top, dft_bot, z1, 0, spec, 1, batch=batch, seq=seq)
    return _inv_dft(dft_inv, s, uc, 2 * HYENA_WIDTH, z1, 0, p["hy_bias"], layer, 1, BF16, batch=batch, seq=seq)


def _block(x, p, modflat, layer, st, ctx_kv, shared):
    h = _normmod(x, p["norm1_g"], modflat, st, layer, 1, 0)
    lin = functools.partial(_linear, h, p["w_in"], layer, tn=512, tn_reuse=1024, shared=shared)
    u_pool = lin(COL_POOL, POOL_WIDTH, kind="plain", out_dtype=F32, key="w_in:pool")
    q = lin(COL_Q, ATTN_WIDTH, kind="rmshead", out_dtype=BF16, head_gain=p["q_norm_g"], key="w_in:q")
    k = lin(COL_K, ATTN_WIDTH, kind="rmshead", out_dtype=F32, head_gain=p["k_norm_g"], key="w_in:k")
    v = lin(COL_V, ATTN_WIDTH, kind="plain", out_dtype=F32, key="w_in:v")
    u_hy = lin(COL_HY, (HYENA_ORDER + 1) * HYENA_WIDTH, kind="plain", out_dtype=F32, key="w_in:hyena")
    gates = lin(COL_GATES, 3 * D_MODEL, kind="sigmoid", out_dtype=BF16, key="w_in:gates")

    pool = _pool(u_pool, p["pool_w"], p["pool_scale"], layer, st)
    if ctx_kv is None:
        attn = _ctx_attention(q, k, v, st)
    else:
        attn = _nbr_attention(q, k, v, ctx_kv[0], ctx_kv[1], ctx_kv[2], layer, st)
    hy = _hyena(u_hy, p, layer, st)

    mixed = _mix(pool, attn, hy, p["w_branch_pool"], p["w_branch_attn"], p["w_branch_hyena"], gates, layer,
                 shared)
    x = _linear(mixed, p["w_out"], layer, 0, D_MODEL, tn=512, tn_reuse=1024, kind="residual", out_dtype=F32,
                shared=shared, key="w_out", residual=x, modflat=modflat, st=st, which_gate=2)
    h = _normmod(x, p["norm2_g"], modflat, st, layer, 4, 3)
    a = _ffn1(h, p["w_gate"], p["w_up"], layer, shared)
    half = D_FF // 2
    for k_off in (0, half):
        x = _linear(a, p["w_down"], layer, 0, D_MODEL, tn=512, tm=512, tm_reuse=1024, kind="residual",
                    out_dtype=F32, shared=shared, key="w_down:%d" % k_off, k_off=k_off, k_len=half,
                    residual=x, modflat=modflat, st=st, which_gate=5)
    return x, k, v


_WEIGHT_NAMES = ("ada_w", "ada_b", "norm1_g", "norm2_g", "w_in", "pool_w", "pool_scale",
                 "q_norm_g", "k_norm_g", "rpb", "hy_conv_w", "hy_conv_b", "filt_w1", "filt_b1",
                 "filt_w2", "filt_b2", "filt_w3", "filt_freq", "hy_bias", "w_branch_pool",
                 "w_branch_attn", "w_branch_hyena", "w_out", "w_gate", "w_up", "w_down")


def kernel(x_prompt, x_sample, cache_k, cache_v, c, c_ctx, ada_w, ada_b, norm1_g, norm2_g, w_in, pool_w, pool_scale, q_norm_g, k_norm_g, rpb, hy_conv_w, hy_conv_b, filt_w1, filt_b1, filt_w2, filt_b2, filt_w3, filt_freq, hy_bias, w_branch_pool, w_branch_attn, w_branch_hyena, w_out, w_gate, w_up, w_down):
    p = dict(zip(_WEIGHT_NAMES, (ada_w, ada_b, norm1_g, norm2_g, w_in, pool_w, pool_scale, q_norm_g,
                                 k_norm_g, rpb, hy_conv_w, hy_conv_b, filt_w1, filt_b1, filt_w2, filt_b2,
                                 filt_w3, filt_freq, hy_bias, w_branch_pool, w_branch_attn,
                                 w_branch_hyena, w_out, w_gate, w_up, w_down)))
    nb, ns, d = x_prompt.shape
    lb, ls, _ = x_sample.shape
    assert d == D_MODEL and lb + 1 <= MOD_ROWS
    ctx = _Stream(nb, ns, 0, False)
    lat = _Stream(lb, ls, 1, True)

    cmat = jnp.concatenate([c_ctx[None, :], c, jnp.zeros((MOD_ROWS - 1 - lb, d), F32)], axis=0)
    mod = _modulation(cmat, ada_w, ada_b)
    modflat = mod.reshape(DEPTH * MOD_ROWS * 6, 1, d)

    y_ctx = x_prompt.reshape(nb * ns, d)
    y_lat = x_sample.reshape(lb * ls, d)
    new_k, new_v = [], []
    for layer in range(DEPTH):
        shared = _SharedWeights()
        bias = _bias_table(rpb, layer)
        y_lat, _, _ = _block(y_lat, p, modflat, layer, lat, (cache_k, cache_v, bias), shared)
        y_ctx, k_l, v_l = _block(y_ctx, p, modflat, layer, ctx, None, shared)
        new_k.append(k_l.reshape(nb, ns, N_HEADS, HEAD_DIM))
        new_v.append(v_l.reshape(nb, ns, N_HEADS, HEAD_DIM))

    return (y_ctx.reshape(nb, ns, d), y_lat.reshape(lb, ls, d),
            jnp.stack(new_k, axis=1), jnp.stack(new_v, axis=1))
```

```python
import functools
import math

import jax
import jax.numpy as jnp
from jax import lax
from jax.experimental import pallas as pl
from jax.experimental.pallas import tpu as pltpu

F32 = jnp.float32
BF16 = jnp.bfloat16

D_MODEL = 4096
DEPTH = 2
GRID_W = 64
HEAD_DIM = 128
N_HEADS = D_MODEL // 256
ATTN_WIDTH = N_HEADS * HEAD_DIM
ATTN_SCALE = HEAD_DIM ** -0.5
WIN_R = 8
WIN_C = 16
NEG_INF = -1e30
POOL_WIDTH = D_MODEL // 2
POOL_SIZES = (2, 4, 8, 16)
POOL_GROUP = POOL_WIDTH // len(POOL_SIZES)
HYENA_WIDTH = D_MODEL // 2
HYENA_ORDER = 2
FILTER_EMB = 33
FILTER_HIDDEN = 64
DECAY_TARGET = 1e-2
FAST_DECAY_PCT = 0.3
SLOW_DECAY_PCT = 1.5
D_FF = ((8 * D_MODEL + 767) // 768) * 256
COL_POOL = 0
COL_Q = POOL_WIDTH
COL_K = COL_Q + ATTN_WIDTH
COL_V = COL_K + ATTN_WIDTH
COL_HY = COL_V + ATTN_WIDTH
COL_GATES = COL_HY + (HYENA_ORDER + 1) * HYENA_WIDTH
MOD_ROWS = 8
RMS_EPS = 1e-6

LANES = 128
VMEM_LIMIT = 56 * 1024 * 1024
SUB_M = 256
CH_TILE = 512
SHORT_SEQ = 512


def _cparams(*sem):
    return pltpu.CompilerParams(dimension_semantics=sem, vmem_limit_bytes=VMEM_LIMIT)


def _row_chunks(tm, sub=SUB_M):
    sub = min(sub, tm)
    return [slice(r, r + sub) for r in range(0, tm, sub)]


def _token_tile(m_tok):
    return 1024 if m_tok % 1024 == 0 else 512


def _mod_kernel(c_ref, w_ref, b_ref, o_ref):
    c = c_ref[...]
    s = (c * jax.nn.sigmoid(c)).astype(BF16)
    o_ref[...] = jnp.dot(s, w_ref[...].astype(BF16), preferred_element_type=F32) + b_ref[...]


def _modulation(cmat, ada_w, ada_b):
    tn = 512
    n6 = ada_w.shape[-1]
    return pl.pallas_call(
        _mod_kernel,
        out_shape=jax.ShapeDtypeStruct((DEPTH, MOD_ROWS, n6), F32),
        grid=(DEPTH, n6 // tn),
        in_specs=[pl.BlockSpec((MOD_ROWS, D_MODEL), lambda l, j: (0, 0)),
                  pl.BlockSpec((None, D_MODEL, tn), lambda l, j: (l, 0, j)),
                  pl.BlockSpec((None, 1, tn), lambda l, j: (l, 0, j))],
        out_specs=pl.BlockSpec((None, MOD_ROWS, tn), lambda l, j: (l, 0, j)),
        compiler_params=_cparams("parallel", "parallel"),
        name="modulation",
    )(cmat, ada_w, ada_b.reshape(DEPTH, 1, n6))


class _Stream:
    def __init__(self, batch, seq, mod_base, per_batch_mod):
        self.batch = batch
        self.seq = seq
        self.mod_base = mod_base
        self.per_batch_mod = per_batch_mod

    def mod_index(self, layer, i, tm, which):
        row = self.mod_base + ((i * tm) // self.seq if self.per_batch_mod else 0)
        return (layer * MOD_ROWS + row) * 6 + which


_STAT_ROWS = 32
_NORM_ROWS = 16


def _normmod_kernel(x_ref, g_ref, sc_ref, sh_ref, o_ref, rstd_ref, gain_ref, shift_ref):
    tm, d = x_ref.shape
    gain_ref[...] = jnp.broadcast_to(g_ref[...] * (1.0 + sc_ref[0]), (_NORM_ROWS, d))
    shift_ref[...] = jnp.broadcast_to(sh_ref[0], (_NORM_ROWS, d))

    def stats(ci, carry):
        base = pl.multiple_of(ci * _STAT_ROWS, _STAT_ROWS)
        x = x_ref[pl.ds(base, _STAT_ROWS), :]
        sq = x * x
        while sq.shape[1] > LANES:
            half = sq.shape[1] // 2
            sq = sq[:, :half] + sq[:, half:]
        rstd_ref[pl.ds(base, _STAT_ROWS), :] = sq
        return carry

    lax.fori_loop(0, tm // _STAT_ROWS, stats, 0)
    rstd = lax.rsqrt(jnp.sum(rstd_ref[...], axis=-1, keepdims=True) * (1.0 / d) + RMS_EPS)
    rstd_ref[...] = jnp.broadcast_to(rstd, (tm, LANES))

    def scale(ci, carry):
        base = pl.multiple_of(ci * _NORM_ROWS, _NORM_ROWS)
        x = x_ref[pl.ds(base, _NORM_ROWS), :]
        rstd = jnp.concatenate([rstd_ref[pl.ds(base, _NORM_ROWS), :]] * (d // LANES), axis=1)
        o_ref[pl.ds(base, _NORM_ROWS), :] = (x * rstd * gain_ref[...] + shift_ref[...]).astype(o_ref.dtype)
        return carry

    lax.fori_loop(0, tm // _NORM_ROWS, scale, 0)


def _normmod(x, g, modflat, st, layer, which_scale, which_shift):
    m_tok, d = x.shape
    tm = 256
    return pl.pallas_call(
        _normmod_kernel,
        out_shape=jax.ShapeDtypeStruct((m_tok, d), BF16),
        grid=(m_tok // tm,),
        in_specs=[pl.BlockSpec((tm, d), lambda i: (i, 0)),
                  pl.BlockSpec((None, 1, d), lambda i: (layer, 0, 0)),
                  pl.BlockSpec((1, 1, d), lambda i: (st.mod_index(layer, i, tm, which_scale), 0, 0)),
                  pl.BlockSpec((1, 1, d), lambda i: (st.mod_index(layer, i, tm, which_shift), 0, 0))],
        out_specs=pl.BlockSpec((tm, d), lambda i: (i, 0)),
        scratch_shapes=[pltpu.VMEM((tm, LANES), F32), pltpu.VMEM((_NORM_ROWS, d), F32),
                        pltpu.VMEM((_NORM_ROWS, d), F32)],
        compiler_params=_cparams("parallel"),
        name="normmod",
    )(x, g.reshape(DEPTH, 1, d), modflat, modflat)


def _bf16_weight(w_ref, wbf_ref):
    if wbf_ref is None:
        return w_ref

    @pl.when(pl.program_id(1) == 0)
    def _():
        wbf_ref[...] = w_ref[...].astype(BF16)

    return wbf_ref


def _lin_plain_kernel(x_ref, w_ref, o_ref, wbf_ref=None):
    w = _bf16_weight(w_ref, wbf_ref)
    for rows in _row_chunks(x_ref.shape[0]):
        acc = jnp.dot(x_ref[rows, :], w[...], preferred_element_type=F32)
        o_ref[rows, :] = acc.astype(o_ref.dtype)


def _lin_sigmoid_kernel(x_ref, w_ref, o_ref, wbf_ref=None):
    w = _bf16_weight(w_ref, wbf_ref)
    for rows in _row_chunks(x_ref.shape[0]):
        acc = jnp.dot(x_ref[rows, :], w[...], preferred_element_type=F32)
        o_ref[rows, :] = jax.nn.sigmoid(acc).astype(o_ref.dtype)


def _lin_rmshead_kernel(x_ref, w_ref, g_ref, o_ref, wbf_ref=None):
    w = _bf16_weight(w_ref, wbf_ref)
    g = g_ref[...]
    for rows in _row_chunks(x_ref.shape[0]):
        acc = jnp.dot(x_ref[rows, :], w[...], preferred_element_type=F32)
        for h in range(acc.shape[1] // HEAD_DIM):
            a = acc[:, h * HEAD_DIM:(h + 1) * HEAD_DIM]
            y = a * lax.rsqrt(jnp.mean(a * a, axis=-1, keepdims=True) + RMS_EPS)
            o_ref[rows, h * HEAD_DIM:(h + 1) * HEAD_DIM] = (y * g).astype(o_ref.dtype)


def _lin_residual_kernel(x_ref, w_ref, res_ref, gate_ref, o_ref, wbf_ref=None):
    w = _bf16_weight(w_ref, wbf_ref)
    for rows in _row_chunks(x_ref.shape[0]):
        acc = jnp.dot(x_ref[rows, :], w[...], preferred_element_type=F32)
        o_ref[rows, :] = res_ref[rows, :] + gate_ref[0] * acc


class _SharedWeights:
    def __init__(self):
        self.slabs = {}

    def get(self, key):
        return self.slabs.get(key)

    def put(self, key, value):
        self.slabs[key] = value


def _weight_plumbing(shared, keys, w_arrays, k_len, ncols, tn, wspec_f32):
    have = [shared.get(key) for key in keys]
    slab_spec = pl.BlockSpec((k_len, tn), lambda n, m: (0, n))
    if all(h is not None for h in have):
        return have, [slab_spec] * len(keys), [], [], True
    slab = jax.ShapeDtypeStruct((k_len, ncols), BF16)
    return list(w_arrays), [wspec_f32] * len(keys), [slab] * len(keys), [slab_spec] * len(keys), False


def _linear(x, w, layer, col_off, ncols, *, tn, kind, out_dtype, shared, key, tn_reuse=None, tm=None,
            tm_reuse=None, k_off=0, k_len=None, head_gain=None, residual=None, modflat=None, st=None,
            which_gate=None):
    m_tok = x.shape[0]
    k_len = x.shape[1] if k_len is None else k_len
    if shared.get(key) is not None:
        tn = tn if tn_reuse is None else tn_reuse
        tm = tm if tm_reuse is None else tm_reuse
    tm = _token_tile(m_tok) if tm is None else tm
    cb = col_off // tn
    kb = k_off // k_len
    wspec = pl.BlockSpec((None, k_len, tn), lambda n, m: (layer, kb, cb + n))
    wargs, wspecs, extra_shapes, extra_specs, reuse = _weight_plumbing(
        shared, [key], [w], k_len, ncols, tn, wspec)
    in_specs = [pl.BlockSpec((tm, k_len), lambda n, m: (m, kb))] + wspecs
    args = [x] + wargs
    if kind == "plain":
        body = _lin_plain_kernel
    elif kind == "sigmoid":
        body = _lin_sigmoid_kernel
    elif kind == "rmshead":
        body = _lin_rmshead_kernel
        in_specs.append(pl.BlockSpec((None, 1, HEAD_DIM), lambda n, m: (layer, 0, 0)))
        args.append(head_gain.reshape(DEPTH, 1, HEAD_DIM))
    else:
        body = _lin_residual_kernel
        in_specs.append(pl.BlockSpec((tm, tn), lambda n, m: (m, n)))
        in_specs.append(pl.BlockSpec(
            (1, 1, tn), lambda n, m: (st.mod_index(layer, m, tm, which_gate), 0, n)))
        args += [residual, modflat]
    outs = pl.pallas_call(
        body,
        out_shape=[jax.ShapeDtypeStruct((m_tok, ncols), out_dtype)] + extra_shapes,
        grid=(ncols // tn, m_tok // tm),
        in_specs=in_specs,
        out_specs=[pl.BlockSpec((tm, tn), lambda n, m: (m, n))] + extra_specs,
        compiler_params=_cparams("parallel", "arbitrary"),
        name="linear_" + kind,
    )(*args)
    if not reuse:
        shared.put(key, outs[1])
    return outs[0]


def _mix_kernel(xp_ref, xa_ref, xh_ref, wp_ref, wa_ref, wh_ref, gp_ref, ga_ref, gh_ref, o_ref,
                wpb=None, wab=None, whb=None):
    wp = _bf16_weight(wp_ref, wpb)
    wa = _bf16_weight(wa_ref, wab)
    wh = _bf16_weight(wh_ref, whb)
    for rows in _row_chunks(xp_ref.shape[0], 2 * SUB_M):
        acc = gp_ref[rows, :].astype(F32) * jnp.dot(xp_ref[rows, :], wp[...], preferred_element_type=F32)
        acc = acc + ga_ref[rows, :].astype(F32) * jnp.dot(xa_ref[rows, :], wa[...], preferred_element_type=F32)
        acc = acc + gh_ref[rows, :].astype(F32) * jnp.dot(xh_ref[rows, :], wh[...], preferred_element_type=F32)
        o_ref[rows, :] = acc.astype(o_ref.dtype)


def _mix(xp, xa, xh, wp, wa, wh, gates, layer, shared):
    m_tok, k = xp.shape
    n = wp.shape[-1]
    tn = 512
    keys = ["w_branch_pool", "w_branch_attn", "w_branch_hyena"]
    wspec = pl.BlockSpec((None, k, tn), lambda j, i: (layer, 0, j))
    wargs, wspecs, extra_shapes, extra_specs, reuse = _weight_plumbing(
        shared, keys, [wp, wa, wh], k, n, tn, wspec)
    tm = _token_tile(m_tok) if reuse else 512
    nb = n // tn
    xspec = pl.BlockSpec((tm, k), lambda j, i: (i, 0))
    outs = pl.pallas_call(
        _mix_kernel,
        out_shape=[jax.ShapeDtypeStruct((m_tok, n), BF16)] + extra_shapes,
        grid=(nb, m_tok // tm),
        in_specs=[xspec, xspec, xspec] + wspecs + [
            pl.BlockSpec((tm, tn), lambda j, i: (i, j)),
            pl.BlockSpec((tm, tn), lambda j, i: (i, nb + j)),
            pl.BlockSpec((tm, tn), lambda j, i: (i, 2 * nb + j))],
        out_specs=[pl.BlockSpec((tm, tn), lambda j, i: (i, j))] + extra_specs,
        compiler_params=_cparams("parallel", "arbitrary"),
        name="mix",
    )(xp, xa, xh, *wargs, gates, gates, gates)
    if not reuse:
        for key, slab in zip(keys, outs[1:]):
            shared.put(key, slab)
    return outs[0]


def _ffn1_kernel(x_ref, wg_ref, wu_ref, o_ref, wgb=None, wub=None):
    wg = _bf16_weight(wg_ref, wgb)
    wu = _bf16_weight(wu_ref, wub)
    for rows in _row_chunks(x_ref.shape[0], 2 * SUB_M):
        x = x_ref[rows, :]
        g = jnp.dot(x, wg[...], preferred_element_type=F32)
        u = jnp.dot(x, wu[...], preferred_element_type=F32)
        o_ref[rows, :] = (g * jax.nn.sigmoid(g) * u).astype(o_ref.dtype)


def _ffn1(x, wg, wu, layer, shared):
    m_tok, k = x.shape
    n = wg.shape[-1]
    tm, tn = _token_tile(m_tok), 256
    keys = ["w_gate", "w_up"]
    wspec = pl.BlockSpec((None, k, tn), lambda j, i: (layer, 0, j))
    wargs, wspecs, extra_shapes, extra_specs, reuse = _weight_plumbing(
        shared, keys, [wg, wu], k, n, tn, wspec)
    outs = pl.pallas_call(
        _ffn1_kernel,
        out_shape=[jax.ShapeDtypeStruct((m_tok, n), BF16)] + extra_shapes,
        grid=(n // tn, m_tok // tm),
        in_specs=[pl.BlockSpec((tm, k), lambda j, i: (i, 0))] + wspecs,
        out_specs=[pl.BlockSpec((tm, tn), lambda j, i: (i, j))] + extra_specs,
        compiler_params=_cparams("parallel", "arbitrary"),
        name="ffn_gate_up",
    )(x, *wargs)
    if not reuse:
        for key, slab in zip(keys, outs[1:]):
            shared.put(key, slab)
    return outs[0]


_POOL_PAD = 16
_ROW_CHUNK = 64


def _halo_chunks(u_ref, seq, rc, pad, emit, lanes=slice(None)):
    zeros = jnp.zeros((pad, u_ref[0:1, lanes].shape[1]), F32)
    emit(0, jnp.concatenate([zeros, u_ref[0:rc + pad, lanes]], axis=0))

    def chunk(ci, carry):
        base = pl.multiple_of(ci * rc, rc)
        emit(base, u_ref[pl.ds(pl.multiple_of(base - pad, pad), rc + 2 * pad), lanes])
        return carry

    lax.fori_loop(1, seq // rc - 1, chunk, 0)
    emit(seq - rc, jnp.concatenate([u_ref[seq - rc - pad:seq, lanes], zeros], axis=0))


def _pool_deviation(u_ref, d_ref, lanes, win, seq):
    rc = _ROW_CHUNK
    half = win // 2

    def emit(base, window):
        lo = _POOL_PAD - half
        acc = window[lo:lo + rc]
        for j in range(1, win):
            acc = acc + window[lo + j:lo + j + rc]
        t = base + lax.broadcasted_iota(jnp.int32, acc.shape, 0)
        cnt = (jnp.minimum(t + (win - half), seq) - jnp.maximum(t - half, 0)).astype(F32)
        centre = window[_POOL_PAD:_POOL_PAD + rc]
        d_ref[pl.ds(base, rc), lanes] = (acc / cnt - centre).astype(BF16)

    _halo_chunks(u_ref, seq, rc, _POOL_PAD, emit, lanes)


def _pool_kernel(u_ref, w_ref, sc_ref, o_ref, d_ref, *, seq, all_groups):
    if all_groups:
        for gi, win in enumerate(POOL_SIZES):
            lanes = slice(gi * POOL_GROUP, (gi + 1) * POOL_GROUP)
            _pool_deviation(u_ref, d_ref, lanes, win, seq)
            y = jnp.dot(d_ref[:, lanes], w_ref[gi].astype(BF16), preferred_element_type=F32) * sc_ref[:, lanes]
            o_ref[:, lanes] = y.astype(o_ref.dtype)
        return

    g = pl.program_id(1)
    for gi, win in enumerate(POOL_SIZES):
        @pl.when(g == gi)
        def _(win=win):
            _pool_deviation(u_ref, d_ref, slice(None), win, seq)

    wbf = w_ref[...].astype(BF16)
    scale = sc_ref[...]
    for rows in _row_chunks(seq):
        y = jnp.dot(d_ref[rows, :], wbf, preferred_element_type=F32) * scale
        o_ref[rows, :] = y.astype(o_ref.dtype)


def _pool(u, pool_w, pool_scale, layer, st):
    seq = st.seq
    all_groups = seq <= SHORT_SEQ
    ngroups = len(POOL_SIZES)
    width = POOL_WIDTH if all_groups else POOL_GROUP
    wblock = (None, ngroups, POOL_GROUP, POOL_GROUP) if all_groups else (None, None, POOL_GROUP, POOL_GROUP)
    return pl.pallas_call(
        functools.partial(_pool_kernel, seq=seq, all_groups=all_groups),
        out_shape=jax.ShapeDtypeStruct((st.batch * seq, POOL_WIDTH), BF16),
        grid=(st.batch, 1 if all_groups else ngroups),
        in_specs=[pl.BlockSpec((seq, width), lambda b, g: (b, g)),
                  pl.BlockSpec(wblock, lambda b, g: (layer, g, 0, 0)),
                  pl.BlockSpec((None, 1, width), lambda b, g: (layer, 0, g))],
        out_specs=pl.BlockSpec((seq, width), lambda b, g: (b, g)),
        scratch_shapes=[pltpu.VMEM((seq, width), BF16)],
        compiler_params=_cparams("parallel", "parallel"),
        name="pool_mixer",
    )(u, pool_w, pool_scale.reshape(DEPTH, 1, POOL_WIDTH))


def _ctx_attn_kernel(q_ref, k_ref, v_ref, o_ref):
    for h in range(N_HEADS):
        sl = slice(h * HEAD_DIM, (h + 1) * HEAD_DIM)
        q = q_ref[:, sl]
        k = k_ref[:, sl].astype(BF16)
        v = v_ref[:, sl].astype(BF16)
        s = lax.dot_general(q, k, (((1,), (1,)), ((), ())), preferred_element_type=F32) * ATTN_SCALE
        m = jnp.max(s, axis=-1, keepdims=True)
        p = jnp.exp(s - m)
        denom = jnp.sum(p, axis=-1, keepdims=True)
        o = jnp.dot(p.astype(BF16), v, preferred_element_type=F32) / denom
        o_ref[:, sl] = o.astype(o_ref.dtype)


def _ctx_attention(q, k, v, st):
    spec = pl.BlockSpec((st.seq, ATTN_WIDTH), lambda b: (b, 0))
    return pl.pallas_call(
        _ctx_attn_kernel,
        out_shape=jax.ShapeDtypeStruct((st.batch * st.seq, ATTN_WIDTH), BF16),
        grid=(st.batch,),
        in_specs=[spec, spec, spec],
        out_specs=spec,
        compiler_params=_cparams("parallel"),
        name="context_attention",
    )(q, k, v)


_BIAS_TILES = 2 * WIN_R
_Q_GROUP = 8
_BAND = 2 * WIN_R


def _bias_table_kernel(rpb_ref, o_ref, *, layer):
    h = pl.program_id(0)
    shape = (GRID_W, 2 * GRID_W)
    qc = lax.broadcasted_iota(jnp.int32, shape, 0)
    lane = lax.broadcasted_iota(jnp.int32, shape, 1)
    kc = lane % GRID_W
    upper = lane >= GRID_W
    c0 = jnp.clip(qc - WIN_C // 2, 0, GRID_W - WIN_C)
    in_window = (kc >= c0) & (kc < c0 + WIN_C)
    rel = kc - qc + (WIN_C - 1)
    n_dc = 2 * WIN_C - 1
    n_dr = 2 * WIN_R - 1
    row_base = (layer * N_HEADS + h) * n_dr
    for e in range(_BIAS_TILES):
        d_lo = max(e - 1, 0)
        d_hi = min(e, n_dr - 1)
        tile = jnp.zeros(shape, F32)
        for dc in range(n_dc):
            lo = rpb_ref[(row_base + d_lo) * n_dc + dc]
            hi = rpb_ref[(row_base + d_hi) * n_dc + dc]
            tile = jnp.where(rel == dc, jnp.where(upper, hi, lo), tile)
        valid = in_window
        if e == 0:
            valid = valid & upper
        if e == _BIAS_TILES - 1:
            valid = valid & jnp.logical_not(upper)
        o_ref[e] = jnp.where(valid, tile, NEG_INF)


def _bias_table(rpb, layer):
    return pl.pallas_call(
        functools.partial(_bias_table_kernel, layer=layer),
        out_shape=jax.ShapeDtypeStruct((N_HEADS, _BIAS_TILES, GRID_W, 2 * GRID_W), F32),
        grid=(N_HEADS,),
        in_specs=[pl.BlockSpec(memory_space=pltpu.SMEM)],
        out_specs=pl.BlockSpec((None, _BIAS_TILES, GRID_W, 2 * GRID_W), lambda h: (h, 0, 0, 0)),
        compiler_params=_cparams("parallel"),
        name="rel_pos_bias_table",
    )(rpb.reshape(-1))


def _nbr_attn_kernel(q_ref, k_ref, v_ref, kc_ref, vc_ref, bias_ref, o_ref,
                     kbf, vbf, s_ref, sc_ref, p_ref, pc_ref, *, rows):
    kbf[...] = k_ref[...].astype(BF16)
    vbf[...] = v_ref[...].astype(BF16)
    kc = kc_ref[...].astype(BF16)
    vc = vc_ref[...].astype(BF16)
    dims = (((1,), (1,)), ((), ()))
    pair = 2 * GRID_W
    lower = lax.broadcasted_iota(jnp.int32, (GRID_W, pair), 1) < GRID_W
    zero_tile = jnp.zeros((GRID_W, pair), BF16)

    for g in range(rows // _Q_GROUP):
        slot = g % 2
        ks = min(max(g * _Q_GROUP - WIN_R // 2, 0), rows - _BAND)
        qsl = slice(g * _Q_GROUP * GRID_W, (g + 1) * _Q_GROUP * GRID_W)
        ksl = slice(ks * GRID_W, (ks + _BAND) * GRID_W)
        qg = q_ref[qsl, :]
        s_ref[slot] = lax.dot_general(qg, kbf[ksl, :], dims, preferred_element_type=F32) * ATTN_SCALE
        sc_ref[slot] = lax.dot_general(qg, kc, dims, preferred_element_type=F32) * ATTN_SCALE
        for i in range(_Q_GROUP):
            qr = g * _Q_GROUP + i
            r0 = min(max(qr - WIN_R // 2, 0), rows - WIN_R)
            first = r0 - ks
            j0, j1 = first // 2, (first + WIN_R + 1) // 2
            rsl = slice(i * GRID_W, (i + 1) * GRID_W)
            tiles = []
            for j in range(j0, j1):
                kr0 = ks + 2 * j
                t = s_ref[slot, rsl, j * pair:(j + 1) * pair] + bias_ref[kr0 - qr + WIN_R]
                if kr0 < r0:
                    t = jnp.where(lower, NEG_INF, t)
                if kr0 + 1 >= r0 + WIN_R:
                    t = jnp.where(lower, t, NEG_INF)
                tiles.append(t)
            sc = sc_ref[slot, rsl, :]
            mt = functools.reduce(jnp.maximum, tiles)
            m = jnp.maximum(jnp.max(mt, axis=-1, keepdims=True), jnp.max(sc, axis=-1, keepdims=True))
            ps = [jnp.exp(t - m) for t in tiles]
            pc = jnp.exp(sc - m)
            denom = (jnp.sum(functools.reduce(jnp.add, ps), axis=-1, keepdims=True)
                     + jnp.sum(pc, axis=-1, keepdims=True))
            inv = 1.0 / denom
            for j in range(_BAND // 2):
                val = (ps[j - j0] * inv).astype(BF16) if j0 <= j < j1 else zero_tile
                p_ref[slot, rsl, j * pair:(j + 1) * pair] = val
            pc_ref[slot, rsl, :] = (pc * inv).astype(BF16)
        o = (jnp.dot(p_ref[slot], vbf[ksl, :], preferred_element_type=F32)
             + jnp.dot(pc_ref[slot], vc, preferred_element_type=F32))
        o_ref[qsl, :] = o.astype(o_ref.dtype)


def _nbr_attention(q, k, v, cache_k, cache_v, bias, layer, st):
    seq = st.seq
    rows = seq // GRID_W
    assert rows >= _BAND and rows % _Q_GROUP == 0
    past = cache_k.shape[2]
    ck = cache_k.reshape(st.batch, DEPTH, past, ATTN_WIDTH)
    cv = cache_v.reshape(st.batch, DEPTH, past, ATTN_WIDTH)
    tok_spec = pl.BlockSpec((seq, HEAD_DIM), lambda b, h: (b, h))
    cache_spec = pl.BlockSpec((None, None, past, HEAD_DIM), lambda b, h: (b, layer, 0, h))
    nq = _Q_GROUP * GRID_W
    nk = _BAND * GRID_W
    return pl.pallas_call(
        functools.partial(_nbr_attn_kernel, rows=rows),
        out_shape=jax.ShapeDtypeStruct((st.batch * seq, ATTN_WIDTH), BF16),
        grid=(st.batch, N_HEADS),
        in_specs=[tok_spec, tok_spec, tok_spec, cache_spec, cache_spec,
                  pl.BlockSpec((None, _BIAS_TILES, GRID_W, 2 * GRID_W), lambda b, h: (h, 0, 0, 0))],
        out_specs=tok_spec,
        scratch_shapes=[pltpu.VMEM((seq, HEAD_DIM), BF16), pltpu.VMEM((seq, HEAD_DIM), BF16),
                        pltpu.VMEM((2, nq, nk), F32), pltpu.VMEM((2, nq, past), F32),
                        pltpu.VMEM((2, nq, nk), BF16), pltpu.VMEM((2, nq, past), BF16)],
        compiler_params=_cparams("parallel", "parallel"),
        name="neighbourhood_attention",
    )(q, k, v, ck, cv, bias)


_CONV_PAD = 8


def _short_conv_kernel(u_ref, w_ref, b_ref, o_ref, *, seq):
    width = u_ref.shape[1]
    w0 = w_ref[0:1, :]
    w1 = w_ref[1:2, :]
    w2 = w_ref[2:3, :]
    b = b_ref[...]
    rc = max(8, _ROW_CHUNK * CH_TILE // width)

    def emit(base, window):
        prev = window[_CONV_PAD - 1:_CONV_PAD - 1 + rc]
        cur = window[_CONV_PAD:_CONV_PAD + rc]
        nxt = window[_CONV_PAD + 1:_CONV_PAD + 1 + rc]
        o_ref[pl.ds(base, rc), :] = prev * w0 + cur * w1 + nxt * w2 + b

    _halo_chunks(u_ref, seq, rc, _CONV_PAD, emit)


def _short_conv(u, conv_w, conv_b, layer, st):
    width = u.shape[1]
    seq = st.seq
    tc = HYENA_WIDTH if seq <= SHORT_SEQ else CH_TILE
    return pl.pallas_call(
        functools.partial(_short_conv_kernel, seq=seq),
        out_shape=jax.ShapeDtypeStruct((st.batch * seq, width), F32),
        grid=(st.batch, width // tc),
        in_specs=[pl.BlockSpec((seq, tc), lambda b, j: (b, j)),
                  pl.BlockSpec((None, 3, tc), lambda b, j: (layer, 0, j)),
                  pl.BlockSpec((None, 1, tc), lambda b, j: (layer, 0, j))],
        out_specs=pl.BlockSpec((seq, tc), lambda b, j: (b, j)),
        compiler_params=_cparams("parallel", "parallel"),
        name="hyena_short_conv",
    )(u, conv_w, conv_b.reshape(DEPTH, 1, width))


def _dft_matrices(seq):
    f = jnp.arange(seq, dtype=jnp.int32)[:, None]
    t = jnp.arange(seq, dtype=jnp.int32)[None, :]
    ang = ((f * t) % (2 * seq)).astype(F32) * (math.pi / seq)
    top = jnp.cos(ang)
    bot = jnp.where(f == 0, (1 - 2 * (t % 2)).astype(F32), -jnp.sin(ang))
    colscale = jnp.where(jnp.arange(seq) == 0, 0.5, 1.0).astype(F32)[None, :] / seq
    inv = jnp.concatenate([top.T * colscale, bot.T * colscale], axis=1)
    return top.astype(BF16), bot.astype(BF16), inv.astype(BF16)


def _filter_mlp_kernel(z_ref, w1_ref, b1_ref, w2_ref, b2_ref, fr_ref, o_ref):
    fr = fr_ref[...]
    h = jnp.sin(fr * (jnp.dot(z_ref[...], w1_ref[...], preferred_element_type=F32) + b1_ref[...]))
    o_ref[...] = jnp.sin(fr * (jnp.dot(h, w2_ref[...], preferred_element_type=F32) + b2_ref[...]))


def _filter_mlp(seq, w1, b1, w2, b2, freq, layer):
    t = jnp.linspace(0.0, 1.0, seq, dtype=F32)[:, None]
    bands = (FILTER_EMB - 1) // 2
    ang = (2.0 * math.pi / seq) * jnp.arange(seq, dtype=F32)[:, None] * jnp.linspace(
        1e-4, bands - 1, bands, dtype=F32)[None]
    z = jnp.concatenate([t, jnp.cos(ang), -jnp.sin(ang)], axis=-1)
    z = jnp.pad(z, ((0, 0), (0, LANES - FILTER_EMB)))
    w1p = jnp.pad(w1, ((0, 0), (0, LANES - FILTER_EMB), (0, 0)))
    vec = lambda a: a.reshape(DEPTH, 1, FILTER_HIDDEN)
    vspec = pl.BlockSpec((None, 1, FILTER_HIDDEN), lambda i: (layer, 0, 0))
    return pl.pallas_call(
        _filter_mlp_kernel,
        out_shape=jax.ShapeDtypeStruct((seq, FILTER_HIDDEN), F32),
        grid=(1,),
        in_specs=[pl.BlockSpec((seq, LANES), lambda i: (0, 0)),
                  pl.BlockSpec((None, LANES, FILTER_HIDDEN), lambda i: (layer, 0, 0)),
                  vspec,
                  pl.BlockSpec((None, FILTER_HIDDEN, FILTER_HIDDEN), lambda i: (layer, 0, 0)),
                  vspec, vspec],
        out_specs=pl.BlockSpec((seq, FILTER_HIDDEN), lambda i: (0, 0)),
        compiler_params=_cparams("arbitrary"),
        name="hyena_filter_mlp",
    )(z, w1p, vec(b1), w2, vec(b2), vec(freq))


def _filter_taps_kernel(h_ref, dl_ref, wf_ref, wb_ref, a_ref, d_ref, nyq_ref, *, seq):
    h = h_ref[...]
    shape = (seq, dl_ref.shape[1])
    row = lax.broadcasted_iota(jnp.int32, shape, 0)
    t = row.astype(F32) / float(seq - 1)
    dec = jnp.exp(-t * dl_ref[...])
    fwd = jnp.dot(h, wf_ref[...], preferred_element_type=F32) * dec
    bwd = jnp.dot(h, wb_ref[...], preferred_element_type=F32) * dec
    bwd = jnp.where(row == 0, 0.0, bwd)
    inv = 1.0 / jnp.sum(jnp.abs(fwd) + jnp.abs(bwd), axis=0, keepdims=True)
    a = (fwd + bwd) * inv
    a_ref[...] = a
    d_ref[...] = (fwd - bwd) * inv
    sign = jnp.where(row % 2 == 0, 1.0, -1.0)
    nyq_ref[0:1, :] = jnp.sum(a * sign, axis=0, keepdims=True)
    quarter = jnp.where(row % 4 == 0, 1.0, jnp.where(row % 4 == 2, -1.0, 0.0))
    nyq_ref[1:2, :] = jnp.sum(a * quarter, axis=0, keepdims=True)


def _filter_taps(hidden, w3, layer, *, seq):
    tc = 256
    nct = HYENA_WIDTH // tc
    deltas = jnp.abs(jnp.linspace(math.log(DECAY_TARGET) / SLOW_DECAY_PCT,
                                  math.log(DECAY_TARGET) / FAST_DECAY_PCT, HYENA_WIDTH, dtype=F32))
    tap_shape = jax.ShapeDtypeStruct((HYENA_ORDER, seq, HYENA_WIDTH), F32)
    tap_spec = pl.BlockSpec((None, seq, tc), lambda o, j: (o, 0, j))
    return pl.pallas_call(
        functools.partial(_filter_taps_kernel, seq=seq),
        out_shape=(tap_shape, tap_shape, jax.ShapeDtypeStruct((HYENA_ORDER, 2, HYENA_WIDTH), F32)),
        grid=(HYENA_ORDER, nct),
        in_specs=[pl.BlockSpec((seq, FILTER_HIDDEN), lambda o, j: (0, 0)),
                  pl.BlockSpec((1, tc), lambda o, j: (0, j)),
                  pl.BlockSpec((None, FILTER_HIDDEN, tc), lambda o, j: (layer, 0, (2 * o) * nct + j)),
                  pl.BlockSpec((None, FILTER_HIDDEN, tc), lambda o, j: (layer, 0, (2 * o + 1) * nct + j))],
        out_specs=(tap_spec, tap_spec, pl.BlockSpec((None, 2, tc), lambda o, j: (o, 0, j))),
        compiler_params=_cparams("parallel", "parallel"),
        name="hyena_filter_taps",
    )(hidden, deltas.reshape(1, HYENA_WIDTH), w3, w3)


def _filter_spec_kernel(wt_ref, wb_ref, a_ref, d_ref, nyq_ref, o_ref, abf, dbf):
    i = pl.program_id(2)

    @pl.when(i == 0)
    def _():
        abf[...] = a_ref[...].astype(BF16)
        dbf[...] = d_ref[...].astype(BF16)

    for rows in _row_chunks(wt_ref.shape[0]):
        top = jnp.dot(wt_ref[rows, :], abf[...], preferred_element_type=F32)
        bot = jnp.dot(wb_ref[rows, :], dbf[...], preferred_element_type=F32)
        if rows.start == 0:
            row = lax.broadcasted_iota(jnp.int32, bot.shape, 0)
            bot = jnp.where((row == 0) & (i == 0), nyq_ref[0:1, :], bot)
        o_ref[0, rows, :] = top
        o_ref[1, rows, :] = bot


def _dft_tiles(seq):
    return seq, (HYENA_WIDTH if seq <= SHORT_SEQ else CH_TILE)


_RESIDENT = pl.Buffered(1)


def _filter_spectrum(dft_top, dft_bot, taps_a, taps_d, nyq, *, seq):
    tf, tc = _dft_tiles(seq)
    tc = min(tc, CH_TILE // 2)
    wspec = pl.BlockSpec((tf, seq), lambda o, c, i: (i, 0), pipeline_mode=_RESIDENT)
    tspec = pl.BlockSpec((None, seq, tc), lambda o, c, i: (o, 0, c))
    return pl.pallas_call(
        _filter_spec_kernel,
        out_shape=jax.ShapeDtypeStruct((HYENA_ORDER, 2, seq, HYENA_WIDTH), F32),
        grid=(HYENA_ORDER, HYENA_WIDTH // tc, seq // tf),
        in_specs=[wspec, wspec, tspec, tspec,
                  pl.BlockSpec((None, 2, tc), lambda o, c, i: (o, 0, c))],
        out_specs=pl.BlockSpec((None, 2, tf, tc), lambda o, c, i: (o, 0, i, c)),
        scratch_shapes=[pltpu.VMEM((seq, tc), BF16)] * 2,
        compiler_params=_cparams("parallel", "parallel", "arbitrary"),
        name="hyena_filter_spectrum",
    )(dft_top, dft_bot, taps_a, taps_d, nyq)


def _fwd_dft_kernel(wt_ref, wb_ref, z_ref, p_ref, o_ref, zbf):
    i = pl.program_id(2)

    @pl.when(i == 0)
    def _():
        zbf[...] = z_ref[...].astype(BF16)

    for rows in _row_chunks(wt_ref.shape[0]):
        top = jnp.dot(wt_ref[rows, :], zbf[...], preferred_element_type=F32)
        bot = jnp.dot(wb_ref[rows, :], zbf[...], preferred_element_type=F32)
        pt = p_ref[0, rows, :]
        pb = p_ref[1, rows, :]
        re = top * pt - bot * pb
        im = top * pb + bot * pt
        if rows.start == 0:
            row = lax.broadcasted_iota(jnp.int32, top.shape, 0)
            real_row = (row == 0) & (i == 0)
            re = jnp.where(real_row, top * pt, re)
            im = jnp.where(real_row, bot * pb, im)
        o_ref[0, rows, :] = re.astype(o_ref.dtype)
        o_ref[1, rows, :] = im.astype(o_ref.dtype)


def _fwd_dft(dft_top, dft_bot, z, z_col_off, spec, order, *, batch, seq):
    tf, tc = _dft_tiles(seq)
    zc = z_col_off // tc
    wspec = pl.BlockSpec((tf, seq), lambda c, b, i: (i, 0), pipeline_mode=_RESIDENT)
    return pl.pallas_call(
        _fwd_dft_kernel,
        out_shape=jax.ShapeDtypeStruct((batch, 2, seq, HYENA_WIDTH), BF16),
        grid=(HYENA_WIDTH // tc, batch, seq // tf),
        in_specs=[wspec, wspec,
                  pl.BlockSpec((seq, tc), lambda c, b, i: (b, zc + c)),
                  pl.BlockSpec((None, 2, tf, tc), lambda c, b, i: (order, 0, i, c), pipeline_mode=_RESIDENT)],
        out_specs=pl.BlockSpec((None, 2, tf, tc), lambda c, b, i: (b, 0, i, c)),
        scratch_shapes=[pltpu.VMEM((seq, tc), BF16)],
        compiler_params=_cparams("parallel", "parallel", "arbitrary"),
        name="hyena_fwd_dft",
    )(dft_top, dft_bot, z, spec)


def _inv_dft_kernel(wi_ref, s_ref, gate_ref, z_ref, bias_ref, o_ref):
    s = s_ref[...]
    s2 = s.reshape(s.shape[0] * s.shape[1], s.shape[2])
    for rows in _row_chunks(wi_ref.shape[0]):
        conv = jnp.dot(wi_ref[rows, :], s2, preferred_element_type=F32)
        o_ref[rows, :] = (gate_ref[rows, :] * (conv + bias_ref[...] * z_ref[rows, :])).astype(o_ref.dtype)


def _inv_dft(dft_inv, s, gate, gate_col_off, z, z_col_off, hy_bias, layer, order, out_dtype, *, batch, seq):
    tl, tc = _dft_tiles(seq)
    nrb = seq // tl
    gc = gate_col_off // tc
    zc = z_col_off // tc
    return pl.pallas_call(
        _inv_dft_kernel,
        out_shape=jax.ShapeDtypeStruct((batch * seq, HYENA_WIDTH), out_dtype),
        grid=(HYENA_WIDTH // tc, batch, nrb),
        in_specs=[pl.BlockSpec((tl, 2 * seq), lambda c, b, i: (i, 0), pipeline_mode=_RESIDENT),
                  pl.BlockSpec((None, 2, seq, tc), lambda c, b, i: (b, 0, 0, c)),
                  pl.BlockSpec((tl, tc), lambda c, b, i: (b * nrb + i, gc + c)),
                  pl.BlockSpec((tl, tc), lambda c, b, i: (b * nrb + i, zc + c)),
                  pl.BlockSpec((None, 1, tc), lambda c, b, i: (layer * HYENA_ORDER + order, 0, c))],
        out_specs=pl.BlockSpec((tl, tc), lambda c, b, i: (b * nrb + i, c)),
        compiler_params=_cparams("parallel", "parallel", "arbitrary"),
        name="hyena_inv_dft",
    )(dft_inv, s, gate, z, hy_bias.reshape(DEPTH * HYENA_ORDER, 1, HYENA_WIDTH))


def _dft_matrices_r2(seq):
    half = seq // 2
    g = jnp.arange(half, dtype=jnp.int32)[:, None]
    m = jnp.arange(half, dtype=jnp.int32)[None, :]
    ang_e = ((2 * g * m) % (2 * seq)).astype(F32) * (math.pi / seq)
    ang_o = ((g * (2 * m + 1)) % (2 * seq)).astype(F32) * (math.pi / seq)
    alt = (1 - 2 * (m % 2)).astype(F32)
    ce = jnp.cos(ang_e)
    co = jnp.cos(ang_o)
    se = jnp.where(g == 0, alt, -jnp.sin(ang_e))
    so = jnp.where(g == 0, -alt, -jnp.sin(ang_o))
    colscale = jnp.where(jnp.arange(half) == 0, 0.5, 1.0).astype(F32)[None, :] / seq
    we = jnp.concatenate([ce.T * colscale, se.T / seq], axis=1)
    wo = jnp.concatenate([co.T * colscale, so.T / seq], axis=1)
    return tuple(a.astype(BF16) for a in (ce, co, se, so)), (we.astype(BF16), wo.astype(BF16))


def _half_transforms(mats, xe, xo, ye, yo, rows):
    ce_ref, co_ref, se_ref, so_ref = mats
    ec = jnp.dot(ce_ref[rows, :], xe[...], preferred_element_type=F32)
    oc = jnp.dot(co_ref[rows, :], xo[...], preferred_element_type=F32)
    es = jnp.dot(se_ref[rows, :], ye[...], preferred_element_type=F32)
    os_ = jnp.dot(so_ref[rows, :], yo[...], preferred_element_type=F32)
    return ec, oc, es, os_


def _filter_spec_r2_kernel(ce_ref, co_ref, se_ref, so_ref, ae_ref, ao_ref, de_ref, do_ref, edge_ref, o_ref,
                           aeb, aob, deb, dob):
    aeb[...] = ae_ref[...].astype(BF16)
    aob[...] = ao_ref[...].astype(BF16)
    deb[...] = de_ref[...].astype(BF16)
    dob[...] = do_ref[...].astype(BF16)
    for rows in _row_chunks(ce_ref.shape[0]):
        ec, oc, es, os_ = _half_transforms((ce_ref, co_ref, se_ref, so_ref), aeb, aob, deb, dob, rows)
        im_a = es + os_
        im_b = os_ - es
        if rows.start == 0:
            row0 = lax.broadcasted_iota(jnp.int32, ec.shape, 0) == 0
            im_a = jnp.where(row0, edge_ref[1:2, :], im_a)
            im_b = jnp.where(row0, os_, im_b)
        o_ref[0, rows, :] = ec + oc
        o_ref[1, rows, :] = im_a
        o_ref[2, rows, :] = ec - oc
        o_ref[3, rows, :] = im_b


def _filter_spectrum_r2(mats, a_even, a_odd, d_even, d_odd, edge, *, seq):
    half = seq // 2
    tc = CH_TILE // 2
    nct = HYENA_WIDTH // tc
    mspec = pl.BlockSpec((half, half), lambda o, c: (0, 0), pipeline_mode=_RESIDENT)
    tspec = pl.BlockSpec((None, half, tc), lambda o, c: (o, 0, c))
    return pl.pallas_call(
        _filter_spec_r2_kernel,
        out_shape=jax.ShapeDtypeStruct((HYENA_ORDER, 4, half, HYENA_WIDTH), F32),
        grid=(HYENA_ORDER, nct),
        in_specs=[mspec] * 4 + [tspec] * 4 + [pl.BlockSpec((None, 2, tc), lambda o, c: (o, 0, c))],
        out_specs=pl.BlockSpec((None, 4, half, tc), lambda o, c: (o, 0, 0, c)),
        scratch_shapes=[pltpu.VMEM((half, tc), BF16)] * 4,
        compiler_params=_cparams("parallel", "parallel"),
        name="hyena_filter_spectrum_r2",
    )(*mats, a_even, a_odd, d_even, d_odd, edge)


def _fwd_dft_r2_kernel(ce_ref, co_ref, se_ref, so_ref, ze_ref, zo_ref, p_ref, o_ref, zeb, zob):
    zeb[...] = ze_ref[...].astype(BF16)
    zob[...] = zo_ref[...].astype(BF16)
    for rows in _row_chunks(ce_ref.shape[0]):
        ec, oc, es, os_ = _half_transforms((ce_ref, co_ref, se_ref, so_ref), zeb, zob, zeb, zob, rows)
        re_a, re_b = ec + oc, ec - oc
        im_a, im_b = es + os_, os_ - es
        pra, pia, prb, pib = (p_ref[j, rows, :] for j in range(4))
        first = rows.start == 0
        if first:
            row0 = lax.broadcasted_iota(jnp.int32, ec.shape, 0) == 0
            im_a = jnp.where(row0, es, im_a)
            im_b = jnp.where(row0, os_, im_b)
        re_a2 = re_a * pra - im_a * pia
        im_a2 = re_a * pia + im_a * pra
        re_b2 = re_b * prb - im_b * pib
        im_b2 = re_b * pib + im_b * prb
        if first:
            re_a2 = jnp.where(row0, re_a * pra, re_a2)
            re_b2 = jnp.where(row0, re_b * prb, re_b2)
            im_a2 = jnp.where(row0, im_a * pia - im_b * pib, im_a2)
            im_b2 = jnp.where(row0, im_a * pib + im_b * pia, im_b2)
        us = im_a2 - im_b2
        vs = im_a2 + im_b2
        if first:
            us = jnp.where(row0, im_a2, us)
            vs = jnp.where(row0, im_b2, vs)
        o_ref[0, rows, :] = (re_a2 + re_b2).astype(o_ref.dtype)
        o_ref[1, rows, :] = us.astype(o_ref.dtype)
        o_ref[2, rows, :] = (re_a2 - re_b2).astype(o_ref.dtype)
        o_ref[3, rows, :] = vs.astype(o_ref.dtype)


def _fwd_dft_r2(mats, ze, ze_cb, zo, zo_cb, spec, order, *, batch, seq):
    half = seq // 2
    tc = CH_TILE
    mspec = pl.BlockSpec((half, half), lambda c, b: (0, 0), pipeline_mode=_RESIDENT)
    return pl.pallas_call(
        _fwd_dft_r2_kernel,
        out_shape=jax.ShapeDtypeStruct((batch, 4, half, HYENA_WIDTH), BF16),
        grid=(HYENA_WIDTH // tc, batch),
        in_specs=[mspec] * 4 + [
            pl.BlockSpec((half, tc), lambda c, b: (b, ze_cb + c)),
            pl.BlockSpec((half, tc), lambda c, b: (b, zo_cb + c)),
            pl.BlockSpec((None, 4, half, tc), lambda c, b: (order, 0, 0, c))],
        out_specs=pl.BlockSpec((None, 4, half, tc), lambda c, b: (b, 0, 0, c)),
        scratch_shapes=[pltpu.VMEM((half, tc), BF16)] * 2,
        compiler_params=_cparams("parallel", "parallel"),
        name="hyena_fwd_dft_r2",
    )(*mats, ze, zo, spec)


def _inv_dft_r2_kernel(we_ref, wo_ref, s_ref, ge_ref, go_ref, ze_ref, zo_ref, bias_ref, *outs, interleave):
    s = s_ref[...]
    half, tc = s.shape[1], s.shape[2]
    su = s[0:2].reshape(2 * half, tc)
    sv = s[2:4].reshape(2 * half, tc)
    bias = bias_ref[...]
    for rows in _row_chunks(half):
        ye = jnp.dot(we_ref[rows, :], su, preferred_element_type=F32)
        yo = jnp.dot(wo_ref[rows, :], sv, preferred_element_type=F32)
        even = ge_ref[rows, :] * (ye + bias * ze_ref[rows, :])
        odd = go_ref[rows, :] * (yo + bias * zo_ref[rows, :])
        if not interleave:
            oe_ref, oo_ref = outs
            oe_ref[rows, :] = even.astype(oe_ref.dtype)
            oo_ref[rows, :] = odd.astype(oo_ref.dtype)
            continue
        o_ref, stage = outs
        n = rows.stop - rows.start
        for j in range(tc // LANES):
            lanes = slice(j * LANES, (j + 1) * LANES)
            stage[j, pl.ds(0, n, stride=2), :] = even[:, lanes]
            stage[j, pl.ds(1, n, stride=2), :] = odd[:, lanes]
            o_ref[2 * rows.start:2 * rows.stop, lanes] = stage[j].astype(o_ref.dtype)


def _inv_dft_r2(inv_mats, s, gate_even, gate_odd, gate_cb, *, z_even, z_odd, hy_bias, layer, order, out_dtype,
                batch, seq, interleave=False):
    half = seq // 2
    tc = CH_TILE
    wspec = pl.BlockSpec((half, seq), lambda c, b: (0, 0), pipeline_mode=_RESIDENT)
    if interleave:
        out_shape = jax.ShapeDtypeStruct((batch * seq, HYENA_WIDTH), out_dtype)
        out_specs = pl.BlockSpec((seq, tc), lambda c, b: (b, c))
        scratch = [pltpu.VMEM((tc // LANES, 2 * min(SUB_M, half), LANES), F32)]
    else:
        out = jax.ShapeDtypeStruct((batch * half, HYENA_WIDTH), out_dtype)
        ospec = pl.BlockSpec((half, tc), lambda c, b: (b, c))
        out_shape, out_specs, scratch = (out, out), (ospec, ospec), []
    return pl.pallas_call(
        functools.partial(_inv_dft_r2_kernel, interleave=interleave),
        out_shape=out_shape,
        grid=(HYENA_WIDTH // tc, batch),
        in_specs=[wspec, wspec,
                  pl.BlockSpec((None, 4, half, tc), lambda c, b: (b, 0, 0, c)),
                  pl.BlockSpec((half, tc), lambda c, b: (b, gate_cb + c)),
                  pl.BlockSpec((half, tc), lambda c, b: (b, gate_cb + c)),
                  pl.BlockSpec((half, tc), lambda c, b: (b, c)),
                  pl.BlockSpec((half, tc), lambda c, b: (b, c)),
                  pl.BlockSpec((None, 1, tc), lambda c, b: (layer * HYENA_ORDER + order, 0, c))],
        out_specs=out_specs,
        scratch_shapes=scratch,
        compiler_params=_cparams("parallel", "parallel"),
        name="hyena_inv_dft_r2",
    )(*inv_mats, s, gate_even, gate_odd, z_even, z_odd, hy_bias.reshape(DEPTH * HYENA_ORDER, 1, HYENA_WIDTH))


def _short_conv_split_kernel(u_ref, w_ref, b_ref, oe_ref, oo_ref, stage, *, seq):
    width = u_ref.shape[1]
    w0 = w_ref[0:1, :]
    w1 = w_ref[1:2, :]
    w2 = w_ref[2:3, :]
    b = b_ref[...]
    rc = _ROW_CHUNK
    hr = rc // 2

    def emit(base, window):
        prev = window[_CONV_PAD - 1:_CONV_PAD - 1 + rc]
        cur = window[_CONV_PAD:_CONV_PAD + rc]
        nxt = window[_CONV_PAD + 1:_CONV_PAD + 1 + rc]
        val = prev * w0 + cur * w1 + nxt * w2 + b
        hb = base // 2 if isinstance(base, int) else pl.multiple_of(base // 2, hr)
        for j in range(width // LANES):
            lanes = slice(j * LANES, (j + 1) * LANES)
            stage[j] = val[:, lanes]
            oe_ref[pl.ds(hb, hr), lanes] = stage[j, pl.ds(0, hr, stride=2), :]
            oo_ref[pl.ds(hb, hr), lanes] = stage[j, pl.ds(1, hr, stride=2), :]

    _halo_chunks(u_ref, seq, rc, _CONV_PAD, emit)


def _short_conv_split(u, conv_w, conv_b, layer, st):
    width = u.shape[1]
    seq = st.seq
    half = seq // 2
    tc = CH_TILE
    out = jax.ShapeDtypeStruct((st.batch * half, width), F32)
    ospec = pl.BlockSpec((half, tc), lambda b, j: (b, j))
    return pl.pallas_call(
        functools.partial(_short_conv_split_kernel, seq=seq),
        out_shape=(out, out),
        grid=(st.batch, width // tc),
        in_specs=[pl.BlockSpec((seq, tc), lambda b, j: (b, j)),
                  pl.BlockSpec((None, 3, tc), lambda b, j: (layer, 0, j)),
                  pl.BlockSpec((None, 1, tc), lambda b, j: (layer, 0, j))],
        out_specs=(ospec, ospec),
        scratch_shapes=[pltpu.VMEM((tc // LANES, _ROW_CHUNK, LANES), F32)],
        compiler_params=_cparams("parallel", "parallel"),
        name="hyena_short_conv_split",
    )(u, conv_w, conv_b.reshape(DEPTH, 1, width))


def _filter_taps_split_kernel(he_ref, ho_ref, dl_ref, wf_ref, wb_ref, ae_ref, ao_ref, de_ref, do_ref,
                              edge_ref, *, seq):
    shape = (seq // 2, dl_ref.shape[1])
    m = lax.broadcasted_iota(jnp.int32, shape, 0)

    def taps(h_ref, parity):
        t = (2 * m + parity).astype(F32) / float(seq - 1)
        dec = jnp.exp(-t * dl_ref[...])
        h = h_ref[...]
        return (jnp.dot(h, wf_ref[...], preferred_element_type=F32) * dec,
                jnp.dot(h, wb_ref[...], preferred_element_type=F32) * dec)

    fe, be = taps(he_ref, 0)
    fo, bo = taps(ho_ref, 1)
    be = jnp.where(m == 0, 0.0, be)
    total = (jnp.sum(jnp.abs(fe) + jnp.abs(be), axis=0, keepdims=True)
             + jnp.sum(jnp.abs(fo) + jnp.abs(bo), axis=0, keepdims=True))
    inv = 1.0 / total
    a_e = (fe + be) * inv
    a_o = (fo + bo) * inv
    ae_ref[...] = a_e
    ao_ref[...] = a_o
    de_ref[...] = (fe - be) * inv
    do_ref[...] = (fo - bo) * inv
    edge_ref[0:1, :] = jnp.sum(a_e, axis=0, keepdims=True) - jnp.sum(a_o, axis=0, keepdims=True)
    alt = jnp.where(m % 2 == 0, 1.0, -1.0)
    edge_ref[1:2, :] = jnp.sum(a_e * alt, axis=0, keepdims=True)


def _filter_taps_split(hidden, w3, layer, *, seq):
    tc = 256
    nct = HYENA_WIDTH // tc
    half = seq // 2
    deltas = jnp.abs(jnp.linspace(math.log(DECAY_TARGET) / SLOW_DECAY_PCT,
                                  math.log(DECAY_TARGET) / FAST_DECAY_PCT, HYENA_WIDTH, dtype=F32))
    tap_shape = jax.ShapeDtypeStruct((HYENA_ORDER, half, HYENA_WIDTH), F32)
    tap_spec = pl.BlockSpec((None, half, tc), lambda o, j: (o, 0, j))
    hspec = pl.BlockSpec((half, FILTER_HIDDEN), lambda o, j: (0, 0))
    return pl.pallas_call(
        functools.partial(_filter_taps_split_kernel, seq=seq),
        out_shape=(tap_shape,) * 4 + (jax.ShapeDtypeStruct((HYENA_ORDER, 2, HYENA_WIDTH), F32),),
        grid=(HYENA_ORDER, nct),
        in_specs=[hspec, hspec,
                  pl.BlockSpec((1, tc), lambda o, j: (0, j)),
                  pl.BlockSpec((None, FILTER_HIDDEN, tc), lambda o, j: (layer, 0, (2 * o) * nct + j)),
                  pl.BlockSpec((None, FILTER_HIDDEN, tc), lambda o, j: (layer, 0, (2 * o + 1) * nct + j))],
        out_specs=(tap_spec,) * 4 + (pl.BlockSpec((None, 2, tc), lambda o, j: (o, 0, j)),),
        compiler_params=_cparams("parallel", "parallel"),
        name="hyena_filter_taps_split",
    )(hidden[0::2], hidden[1::2], deltas.reshape(1, HYENA_WIDTH), w3, w3)


def _hyena_long(u, p, layer, st):
    batch, seq = st.batch, st.seq
    uce, uco = _short_conv_split(u, p["hy_conv_w"], p["hy_conv_b"], layer, st)
    hidden = _filter_mlp(seq, p["filt_w1"], p["filt_b1"], p["filt_w2"], p["filt_b2"], p["filt_freq"], layer)
    taps = _filter_taps_split(hidden, p["filt_w3"], layer, seq=seq)
    mats, inv_mats = _dft_matrices_r2(seq)
    spec = _filter_spectrum_r2(mats, *taps, seq=seq)
    nb = HYENA_WIDTH // CH_TILE
    hy_bias = p["hy_bias"]
    s = _fwd_dft_r2(mats, uce, 0, uco, 0, spec, 0, batch=batch, seq=seq)
    z1e, z1o = _inv_dft_r2(inv_mats, s, uce, uco, nb, z_even=uce, z_odd=uco, hy_bias=hy_bias, layer=layer,
                           order=0, out_dtype=F32, batch=batch, seq=seq)
    s = _fwd_dft_r2(mats, z1e, 0, z1o, 0, spec, 1, batch=batch, seq=seq)
    return _inv_dft_r2(inv_mats, s, uce, uco, 2 * nb, z_even=z1e, z_odd=z1o, hy_bias=hy_bias, layer=layer,
                       order=1, out_dtype=BF16, batch=batch, seq=seq, interleave=True)


def _hyena(u, p, layer, st):
    batch, seq = st.batch, st.seq
    if seq > SHORT_SEQ:
        return _hyena_long(u, p, layer, st)
    uc = _short_conv(u, p["hy_conv_w"], p["hy_conv_b"], layer, st)
    dft_top, dft_bot, dft_inv = _dft_matrices(seq)
    hidden = _filter_mlp(seq, p["filt_w1"], p["filt_b1"], p["filt_w2"], p["filt_b2"], p["filt_freq"], layer)
    taps_a, taps_d, nyq = _filter_taps(hidden, p["filt_w3"], layer, seq=seq)
    spec = _filter_spectrum(dft_top, dft_bot, taps_a, taps_d, nyq, seq=seq)
    s = _fwd_dft(dft_top, dft_bot, uc, 0, spec, 0, batch=batch, seq=seq)
    z1 = _inv_dft(dft_inv, s, uc, HYENA_WIDTH, uc, 0, p["hy_bias"], layer, 0, F32, batch=batch, seq=seq)
    s = _fwd_dft(dft_top, dft_bot, z1, 0, spec, 1, batch=batch, seq=seq)
    return _inv_dft(dft_inv, s, uc, 2 * HYENA_WIDTH, z1, 0, p["hy_bias"], layer, 1, BF16, batch=batch, seq=seq)


def _block(x, p, modflat, layer, st, ctx_kv, shared):
    h = _normmod(x, p["norm1_g"], modflat, st, layer, 1, 0)
    lin = functools.partial(_linear, h, p["w_in"], layer, tn=512, tn_reuse=1024, shared=shared)
    u_pool = lin(COL_POOL, POOL_WIDTH, kind="plain", out_dtype=F32, key="w_in:pool")
    q = lin(COL_Q, ATTN_WIDTH, kind="rmshead", out_dtype=BF16, head_gain=p["q_norm_g"], key="w_in:q")
    k = lin(COL_K, ATTN_WIDTH, kind="rmshead", out_dtype=F32, head_gain=p["k_norm_g"], key="w_in:k")
    v = lin(COL_V, ATTN_WIDTH, kind="plain", out_dtype=F32, key="w_in:v")
    u_hy = lin(COL_HY, (HYENA_ORDER + 1) * HYENA_WIDTH, kind="plain", out_dtype=F32, key="w_in:hyena")
    gates = lin(COL_GATES, 3 * D_MODEL, kind="sigmoid", out_dtype=BF16, key="w_in:gates")

    pool = _pool(u_pool, p["pool_w"], p["pool_scale"], layer, st)
    if ctx_kv is None:
        attn = _ctx_attention(q, k, v, st)
    else:
        attn = _nbr_attention(q, k, v, ctx_kv[0], ctx_kv[1], ctx_kv[2], layer, st)
    hy = _hyena(u_hy, p, layer, st)

    mixed = _mix(pool, attn, hy, p["w_branch_pool"], p["w_branch_attn"], p["w_branch_hyena"], gates, layer,
                 shared)
    x = _linear(mixed, p["w_out"], layer, 0, D_MODEL, tn=512, tn_reuse=1024, kind="residual", out_dtype=F32,
                shared=shared, key="w_out", residual=x, modflat=modflat, st=st, which_gate=2)
    h = _normmod(x, p["norm2_g"], modflat, st, layer, 4, 3)
    a = _ffn1(h, p["w_gate"], p["w_up"], layer, shared)
    half = D_FF // 2
    for k_off in (0, half):
        x = _linear(a, p["w_down"], layer, 0, D_MODEL, tn=512, tm=512, tm_reuse=1024, kind="residual",
                    out_dtype=F32, shared=shared, key="w_down:%d" % k_off, k_off=k_off, k_len=half,
                    residual=x, modflat=modflat, st=st, which_gate=5)
    return x, k, v


_WEIGHT_NAMES = ("ada_w", "ada_b", "norm1_g", "norm2_g", "w_in", "pool_w", "pool_scale",
                 "q_norm_g", "k_norm_g", "rpb", "hy_conv_w", "hy_conv_b", "filt_w1", "filt_b1",
                 "filt_w2", "filt_b2", "filt_w3", "filt_freq", "hy_bias", "w_branch_pool",
                 "w_branch_attn", "w_branch_hyena", "w_out", "w_gate", "w_up", "w_down")


def kernel(x_prompt, x_sample, cache_k, cache_v, c, c_ctx, ada_w, ada_b, norm1_g, norm2_g, w_in, pool_w, pool_scale, q_norm_g, k_norm_g, rpb, hy_conv_w, hy_conv_b, filt_w1, filt_b1, filt_w2, filt_b2, filt_w3, filt_freq, hy_bias, w_branch_pool, w_branch_attn, w_branch_hyena, w_out, w_gate, w_up, w_down):
    p = dict(zip(_WEIGHT_NAMES, (ada_w, ada_b, norm1_g, norm2_g, w_in, pool_w, pool_scale, q_norm_g,
                                 k_norm_g, rpb, hy_conv_w, hy_conv_b, filt_w1, filt_b1, filt_w2, filt_b2,
                                 filt_w3, filt_freq, hy_bias, w_branch_pool, w_branch_attn,
                                 w_branch_hyena, w_out, w_gate, w_up, w_down)))
    nb, ns, d = x_prompt.shape
    lb, ls, _ = x_sample.shape
    assert d == D_MODEL and lb + 1 <= MOD_ROWS
    ctx = _Stream(nb, ns, 0, False)
    lat = _Stream(lb, ls, 1, True)

    cmat = jnp.concatenate([c_ctx[None, :], c, jnp.zeros((MOD_ROWS - 1 - lb, d), F32)], axis=0)
    mod = _modulation(cmat, ada_w, ada_b)
    modflat = mod.reshape(DEPTH * MOD_ROWS * 6, 1, d)

    y_ctx = x_prompt.reshape(nb * ns, d)
    y_lat = x_sample.reshape(lb * ls, d)
    new_k, new_v = [], []
    for layer in range(DEPTH):
        shared = _SharedWeights()
        bias = _bias_table(rpb, layer)
        y_lat, _, _ = _block(y_lat, p, modflat, layer, lat, (cache_k, cache_v, bias), shared)
        y_ctx, k_l, v_l = _block(y_ctx, p, modflat, layer, ctx, None, shared)
        new_k.append(k_l.reshape(nb, ns, N_HEADS, HEAD_DIM))
        new_v.append(v_l.reshape(nb, ns, N_HEADS, HEAD_DIM))

    return (y_ctx.reshape(nb, ns, d), y_lat.reshape(lb, ls, d),
            jnp.stack(new_k, axis=1), jnp.stack(new_v, axis=1))
```

```python
import functools
import math

import jax
import jax.numpy as jnp
from jax import lax
from jax.experimental import pallas as pl
from jax.experimental.pallas import tpu as pltpu

F32 = jnp.float32
BF16 = jnp.bfloat16

D_MODEL = 4096
DEPTH = 2
GRID_W = 64
HEAD_DIM = 128
N_HEADS = D_MODEL // 256
ATTN_WIDTH = N_HEADS * HEAD_DIM
ATTN_SCALE = HEAD_DIM ** -0.5
WIN_R = 8
WIN_C = 16
NEG_INF = -1e30
POOL_WIDTH = D_MODEL // 2
POOL_SIZES = (2, 4, 8, 16)
POOL_GROUP = POOL_WIDTH // len(POOL_SIZES)
HYENA_WIDTH = D_MODEL // 2
HYENA_ORDER = 2
FILTER_EMB = 33
FILTER_HIDDEN = 64
DECAY_TARGET = 1e-2
FAST_DECAY_PCT = 0.3
SLOW_DECAY_PCT = 1.5
D_FF = ((8 * D_MODEL + 767) // 768) * 256
COL_POOL = 0
COL_Q = POOL_WIDTH
COL_K = COL_Q + ATTN_WIDTH
COL_V = COL_K + ATTN_WIDTH
COL_HY = COL_V + ATTN_WIDTH
COL_GATES = COL_HY + (HYENA_ORDER + 1) * HYENA_WIDTH
MOD_ROWS = 8
RMS_EPS = 1e-6

LANES = 128
VMEM_LIMIT = 56 * 1024 * 1024
SUB_M = 256
CH_TILE = 512
SHORT_SEQ = 512


def _cparams(*sem):
    return pltpu.CompilerParams(dimension_semantics=sem, vmem_limit_bytes=VMEM_LIMIT)


def _row_chunks(tm, sub=SUB_M):
    sub = min(sub, tm)
    return [slice(r, r + sub) for r in range(0, tm, sub)]


def _token_tile(m_tok):
    return 1024 if m_tok % 1024 == 0 else 512


def _mod_kernel(c_ref, w_ref, b_ref, o_ref):
    c = c_ref[...]
    s = (c * jax.nn.sigmoid(c)).astype(BF16)
    o_ref[...] = jnp.dot(s, w_ref[...].astype(BF16), preferred_element_type=F32) + b_ref[...]


def _modulation(cmat, ada_w, ada_b):
    tn = 512
    n6 = ada_w.shape[-1]
    return pl.pallas_call(
        _mod_kernel,
        out_shape=jax.ShapeDtypeStruct((DEPTH, MOD_ROWS, n6), F32),
        grid=(DEPTH, n6 // tn),
        in_specs=[pl.BlockSpec((MOD_ROWS, D_MODEL), lambda l, j: (0, 0)),
                  pl.BlockSpec((None, D_MODEL, tn), lambda l, j: (l, 0, j)),
                  pl.BlockSpec((None, 1, tn), lambda l, j: (l, 0, j))],
        out_specs=pl.BlockSpec((None, MOD_ROWS, tn), lambda l, j: (l, 0, j)),
        compiler_params=_cparams("parallel", "parallel"),
        name="modulation",
    )(cmat, ada_w, ada_b.reshape(DEPTH, 1, n6))


class _Stream:
    def __init__(self, batch, seq, mod_base, per_batch_mod):
        self.batch = batch
        self.seq = seq
        self.mod_base = mod_base
        self.per_batch_mod = per_batch_mod

    def mod_index(self, layer, i, tm, which):
        row = self.mod_base + ((i * tm) // self.seq if self.per_batch_mod else 0)
        return (layer * MOD_ROWS + row) * 6 + which


_STAT_ROWS = 32
_NORM_ROWS = 16


def _normmod_kernel(x_ref, g_ref, sc_ref, sh_ref, o_ref, rstd_ref, gain_ref, shift_ref):
    tm, d = x_ref.shape
    gain_ref[...] = jnp.broadcast_to(g_ref[...] * (1.0 + sc_ref[0]), (_NORM_ROWS, d))
    shift_ref[...] = jnp.broadcast_to(sh_ref[0], (_NORM_ROWS, d))

    def stats(ci, carry):
        base = pl.multiple_of(ci * _STAT_ROWS, _STAT_ROWS)
        x = x_ref[pl.ds(base, _STAT_ROWS), :]
        sq = x * x
        while sq.shape[1] > LANES:
            half = sq.shape[1] // 2
            sq = sq[:, :half] + sq[:, half:]
        rstd_ref[pl.ds(base, _STAT_ROWS), :] = sq
        return carry

    lax.fori_loop(0, tm // _STAT_ROWS, stats, 0)
    rstd = lax.rsqrt(jnp.sum(rstd_ref[...], axis=-1, keepdims=True) * (1.0 / d) + RMS_EPS)
    rstd_ref[...] = jnp.broadcast_to(rstd, (tm, LANES))

    def scale(ci, carry):
        base = pl.multiple_of(ci * _NORM_ROWS, _NORM_ROWS)
        x = x_ref[pl.ds(base, _NORM_ROWS), :]
        rstd = jnp.concatenate([rstd_ref[pl.ds(base, _NORM_ROWS), :]] * (d // LANES), axis=1)
        o_ref[pl.ds(base, _NORM_ROWS), :] = (x * rstd * gain_ref[...] + shift_ref[...]).astype(o_ref.dtype)
        return carry

    lax.fori_loop(0, tm // _NORM_ROWS, scale, 0)


def _normmod(x, g, modflat, st, layer, which_scale, which_shift):
    m_tok, d = x.shape
    tm = 256
    return pl.pallas_call(
        _normmod_kernel,
        out_shape=jax.ShapeDtypeStruct((m_tok, d), BF16),
        grid=(m_tok // tm,),
        in_specs=[pl.BlockSpec((tm, d), lambda i: (i, 0)),
                  pl.BlockSpec((None, 1, d), lambda i: (layer, 0, 0)),
                  pl.BlockSpec((1, 1, d), lambda i: (st.mod_index(layer, i, tm, which_scale), 0, 0)),
                  pl.BlockSpec((1, 1, d), lambda i: (st.mod_index(layer, i, tm, which_shift), 0, 0))],
        out_specs=pl.BlockSpec((tm, d), lambda i: (i, 0)),
        scratch_shapes=[pltpu.VMEM((tm, LANES), F32), pltpu.VMEM((_NORM_ROWS, d), F32),
                        pltpu.VMEM((_NORM_ROWS, d), F32)],
        compiler_params=_cparams("parallel"),
        name="normmod",
    )(x, g.reshape(DEPTH, 1, d), modflat, modflat)


def _bf16_weight(w_ref, wbf_ref):
    if wbf_ref is None:
        return w_ref

    @pl.when(pl.program_id(1) == 0)
    def _():
        wbf_ref[...] = w_ref[...].astype(BF16)

    return wbf_ref


def _lin_plain_kernel(x_ref, w_ref, o_ref, wbf_ref=None):
    w = _bf16_weight(w_ref, wbf_ref)
    for rows in _row_chunks(x_ref.shape[0]):
        acc = jnp.dot(x_ref[rows, :], w[...], preferred_element_type=F32)
        o_ref[rows, :] = acc.astype(o_ref.dtype)


def _lin_sigmoid_kernel(x_ref, w_ref, o_ref, wbf_ref=None):
    w = _bf16_weight(w_ref, wbf_ref)
    for rows in _row_chunks(x_ref.shape[0]):
        acc = jnp.dot(x_ref[rows, :], w[...], preferred_element_type=F32)
        o_ref[rows, :] = jax.nn.sigmoid(acc).astype(o_ref.dtype)


def _lin_rmshead_kernel(x_ref, w_ref, g_ref, o_ref, wbf_ref=None):
    w = _bf16_weight(w_ref, wbf_ref)
    g = g_ref[...]
    for rows in _row_chunks(x_ref.shape[0]):
        acc = jnp.dot(x_ref[rows, :], w[...], preferred_element_type=F32)
        for h in range(acc.shape[1] // HEAD_DIM):
            a = acc[:, h * HEAD_DIM:(h + 1) * HEAD_DIM]
            y = a * lax.rsqrt(jnp.mean(a * a, axis=-1, keepdims=True) + RMS_EPS)
            o_ref[rows, h * HEAD_DIM:(h + 1) * HEAD_DIM] = (y * g).astype(o_ref.dtype)


def _lin_residual_kernel(x_ref, w_ref, res_ref, gate_ref, o_ref, wbf_ref=None):
    w = _bf16_weight(w_ref, wbf_ref)
    for rows in _row_chunks(x_ref.shape[0]):
        acc = jnp.dot(x_ref[rows, :], w[...], preferred_element_type=F32)
        o_ref[rows, :] = res_ref[rows, :] + gate_ref[0] * acc


class _SharedWeights:
    def __init__(self):
        self.slabs = {}

    def get(self, key):
        return self.slabs.get(key)

    def put(self, key, value):
        self.slabs[key] = value


def _weight_plumbing(shared, keys, w_arrays, k_len, ncols, tn, wspec_f32):
    have = [shared.get(key) for key in keys]
    slab_spec = pl.BlockSpec((k_len, tn), lambda n, m: (0, n))
    if all(h is not None for h in have):
        return have, [slab_spec] * len(keys), [], [], True
    slab = jax.ShapeDtypeStruct((k_len, ncols), BF16)
    return list(w_arrays), [wspec_f32] * len(keys), [slab] * len(keys), [slab_spec] * len(keys), False


def _linear(x, w, layer, col_off, ncols, *, tn, kind, out_dtype, shared, key, tn_reuse=None, tm=None,
            tm_reuse=None, k_off=0, k_len=None, head_gain=None, residual=None, modflat=None, st=None,
            which_gate=None):
    m_tok = x.shape[0]
    k_len = x.shape[1] if k_len is None else k_len
    if shared.get(key) is not None:
        tn = tn if tn_reuse is None else tn_reuse
        tm = tm if tm_reuse is None else tm_reuse
    tm = _token_tile(m_tok) if tm is None else tm
    cb = col_off // tn
    kb = k_off // k_len
    wspec = pl.BlockSpec((None, k_len, tn), lambda n, m: (layer, kb, cb + n))
    wargs, wspecs, extra_shapes, extra_specs, reuse = _weight_plumbing(
        shared, [key], [w], k_len, ncols, tn, wspec)
    in_specs = [pl.BlockSpec((tm, k_len), lambda n, m: (m, kb))] + wspecs
    args = [x] + wargs
    if kind == "plain":
        body = _lin_plain_kernel
    elif kind == "sigmoid":
        body = _lin_sigmoid_kernel
    elif kind == "rmshead":
        body = _lin_rmshead_kernel
        in_specs.append(pl.BlockSpec((None, 1, HEAD_DIM), lambda n, m: (layer, 0, 0)))
        args.append(head_gain.reshape(DEPTH, 1, HEAD_DIM))
    else:
        body = _lin_residual_kernel
        in_specs.append(pl.BlockSpec((tm, tn), lambda n, m: (m, n)))
        in_specs.append(pl.BlockSpec(
            (1, 1, tn), lambda n, m: (st.mod_index(layer, m, tm, which_gate), 0, n)))
        args += [residual, modflat]
    outs = pl.pallas_call(
        body,
        out_shape=[jax.ShapeDtypeStruct((m_tok, ncols), out_dtype)] + extra_shapes,
        grid=(ncols // tn, m_tok // tm),
        in_specs=in_specs,
        out_specs=[pl.BlockSpec((tm, tn), lambda n, m: (m, n))] + extra_specs,
        compiler_params=_cparams("parallel", "arbitrary"),
        name="linear_" + kind,
    )(*args)
    if not reuse:
        shared.put(key, outs[1])
    return outs[0]


def _mix_kernel(xp_ref, xa_ref, xh_ref, wp_ref, wa_ref, wh_ref, gp_ref, ga_ref, gh_ref, o_ref,
                wpb=None, wab=None, whb=None):
    wp = _bf16_weight(wp_ref, wpb)
    wa = _bf16_weight(wa_ref, wab)
    wh = _bf16_weight(wh_ref, whb)
    for rows in _row_chunks(xp_ref.shape[0], 2 * SUB_M):
        acc = gp_ref[rows, :].astype(F32) * jnp.dot(xp_ref[rows, :], wp[...], preferred_element_type=F32)
        acc = acc + ga_ref[rows, :].astype(F32) * jnp.dot(xa_ref[rows, :], wa[...], preferred_element_type=F32)
        acc = acc + gh_ref[rows, :].astype(F32) * jnp.dot(xh_ref[rows, :], wh[...], preferred_element_type=F32)
        o_ref[rows, :] = acc.astype(o_ref.dtype)


def _mix(xp, xa, xh, wp, wa, wh, gates, layer, shared):
    m_tok, k = xp.shape
    n = wp.shape[-1]
    tn = 512
    keys = ["w_branch_pool", "w_branch_attn", "w_branch_hyena"]
    wspec = pl.BlockSpec((None, k, tn), lambda j, i: (layer, 0, j))
    wargs, wspecs, extra_shapes, extra_specs, reuse = _weight_plumbing(
        shared, keys, [wp, wa, wh], k, n, tn, wspec)
    tm = _token_tile(m_tok) if reuse else 512
    nb = n // tn
    xspec = pl.BlockSpec((tm, k), lambda j, i: (i, 0))
    outs = pl.pallas_call(
        _mix_kernel,
        out_shape=[jax.ShapeDtypeStruct((m_tok, n), BF16)] + extra_shapes,
        grid=(nb, m_tok // tm),
        in_specs=[xspec, xspec, xspec] + wspecs + [
            pl.BlockSpec((tm, tn), lambda j, i: (i, j)),
            pl.BlockSpec((tm, tn), lambda j, i: (i, nb + j)),
            pl.BlockSpec((tm, tn), lambda j, i: (i, 2 * nb + j))],
        out_specs=[pl.BlockSpec((tm, tn), lambda j, i: (i, j))] + extra_specs,
        compiler_params=_cparams("parallel", "arbitrary"),
        name="mix",
    )(xp, xa, xh, *wargs, gates, gates, gates)
    if not reuse:
        for key, slab in zip(keys, outs[1:]):
            shared.put(key, slab)
    return outs[0]


def _ffn1_kernel(x_ref, wg_ref, wu_ref, o_ref, wgb=None, wub=None):
    wg = _bf16_weight(wg_ref, wgb)
    wu = _bf16_weight(wu_ref, wub)
    for rows in _row_chunks(x_ref.shape[0], 2 * SUB_M):
        x = x_ref[rows, :]
        g = jnp.dot(x, wg[...], preferred_element_type=F32)
        u = jnp.dot(x, wu[...], preferred_element_type=F32)
        o_ref[rows, :] = (g * jax.nn.sigmoid(g) * u).astype(o_ref.dtype)


def _ffn1(x, wg, wu, layer, shared):
    m_tok, k = x.shape
    n = wg.shape[-1]
    tm, tn = _token_tile(m_tok), 256
    keys = ["w_gate", "w_up"]
    wspec = pl.BlockSpec((None, k, tn), lambda j, i: (layer, 0, j))
    wargs, wspecs, extra_shapes, extra_specs, reuse = _weight_plumbing(
        shared, keys, [wg, wu], k, n, tn, wspec)
    outs = pl.pallas_call(
        _ffn1_kernel,
        out_shape=[jax.ShapeDtypeStruct((m_tok, n), BF16)] + extra_shapes,
        grid=(n // tn, m_tok // tm),
        in_specs=[pl.BlockSpec((tm, k), lambda j, i: (i, 0))] + wspecs,
        out_specs=[pl.BlockSpec((tm, tn), lambda j, i: (i, j))] + extra_specs,
        compiler_params=_cparams("parallel", "arbitrary"),
        name="ffn_gate_up",
    )(x, *wargs)
    if not reuse:
        for key, slab in zip(keys, outs[1:]):
            shared.put(key, slab)
    return outs[0]


_POOL_PAD = 16
_ROW_CHUNK = 64


def _halo_chunks(u_ref, seq, rc, pad, emit, lanes=slice(None)):
    zeros = jnp.zeros((pad, u_ref[0:1, lanes].shape[1]), F32)
    emit(0, jnp.concatenate([zeros, u_ref[0:rc + pad, lanes]], axis=0))

    def chunk(ci, carry):
        base = pl.multiple_of(ci * rc, rc)
        emit(base, u_ref[pl.ds(pl.multiple_of(base - pad, pad), rc + 2 * pad), lanes])
        return carry

    lax.fori_loop(1, seq // rc - 1, chunk, 0)
    emit(seq - rc, jnp.concatenate([u_ref[seq - rc - pad:seq, lanes], zeros], axis=0))


def _pool_deviation(u_ref, d_ref, lanes, win, seq):
    rc = _ROW_CHUNK
    half = win // 2

    def emit(base, window):
        lo = _POOL_PAD - half
        acc = window[lo:lo + rc]
        for j in range(1, win):
            acc = acc + window[lo + j:lo + j + rc]
        t = base + lax.broadcasted_iota(jnp.int32, acc.shape, 0)
        cnt = (jnp.minimum(t + (win - half), seq) - jnp.maximum(t - half, 0)).astype(F32)
        centre = window[_POOL_PAD:_POOL_PAD + rc]
        d_ref[pl.ds(base, rc), lanes] = (acc / cnt - centre).astype(BF16)

    _halo_chunks(u_ref, seq, rc, _POOL_PAD, emit, lanes)


def _pool_kernel(u_ref, w_ref, sc_ref, o_ref, d_ref, *, seq, all_groups):
    if all_groups:
        for gi, win in enumerate(POOL_SIZES):
            lanes = slice(gi * POOL_GROUP, (gi + 1) * POOL_GROUP)
            _pool_deviation(u_ref, d_ref, lanes, win, seq)
            y = jnp.dot(d_ref[:, lanes], w_ref[gi].astype(BF16), preferred_element_type=F32) * sc_ref[:, lanes]
            o_ref[:, lanes] = y.astype(o_ref.dtype)
        return

    g = pl.program_id(1)
    for gi, win in enumerate(POOL_SIZES):
        @pl.when(g == gi)
        def _(win=win):
            _pool_deviation(u_ref, d_ref, slice(None), win, seq)

    wbf = w_ref[...].astype(BF16)
    scale = sc_ref[...]
    for rows in _row_chunks(seq):
        y = jnp.dot(d_ref[rows, :], wbf, preferred_element_type=F32) * scale
        o_ref[rows, :] = y.astype(o_ref.dtype)


def _pool(u, pool_w, pool_scale, layer, st):
    seq = st.seq
    all_groups = seq <= SHORT_SEQ
    ngroups = len(POOL_SIZES)
    width = POOL_WIDTH if all_groups else POOL_GROUP
    wblock = (None, ngroups, POOL_GROUP, POOL_GROUP) if all_groups else (None, None, POOL_GROUP, POOL_GROUP)
    return pl.pallas_call(
        functools.partial(_pool_kernel, seq=seq, all_groups=all_groups),
        out_shape=jax.ShapeDtypeStruct((st.batch * seq, POOL_WIDTH), BF16),
        grid=(st.batch, 1 if all_groups else ngroups),
        in_specs=[pl.BlockSpec((seq, width), lambda b, g: (b, g)),
                  pl.BlockSpec(wblock, lambda b, g: (layer, g, 0, 0)),
                  pl.BlockSpec((None, 1, width), lambda b, g: (layer, 0, g))],
        out_specs=pl.BlockSpec((seq, width), lambda b, g: (b, g)),
        scratch_shapes=[pltpu.VMEM((seq, width), BF16)],
        compiler_params=_cparams("parallel", "parallel"),
        name="pool_mixer",
    )(u, pool_w, pool_scale.reshape(DEPTH, 1, POOL_WIDTH))


def _ctx_attn_kernel(q_ref, k_ref, v_ref, o_ref):
    for h in range(N_HEADS):
        sl = slice(h * HEAD_DIM, (h + 1) * HEAD_DIM)
        q = q_ref[:, sl]
        k = k_ref[:, sl].astype(BF16)
        v = v_ref[:, sl].astype(BF16)
        s = lax.dot_general(q, k, (((1,), (1,)), ((), ())), preferred_element_type=F32) * ATTN_SCALE
        m = jnp.max(s, axis=-1, keepdims=True)
        p = jnp.exp(s - m)
        denom = jnp.sum(p, axis=-1, keepdims=True)
        o = jnp.dot(p.astype(BF16), v, preferred_element_type=F32) / denom
        o_ref[:, sl] = o.astype(o_ref.dtype)


def _ctx_attention(q, k, v, st):
    spec = pl.BlockSpec((st.seq, ATTN_WIDTH), lambda b: (b, 0))
    return pl.pallas_call(
        _ctx_attn_kernel,
        out_shape=jax.ShapeDtypeStruct((st.batch * st.seq, ATTN_WIDTH), BF16),
        grid=(st.batch,),
        in_specs=[spec, spec, spec],
        out_specs=spec,
        compiler_params=_cparams("parallel"),
        name="context_attention",
    )(q, k, v)


_BIAS_TILES = 2 * WIN_R
_Q_GROUP = 8
_BAND = 2 * WIN_R


def _bias_table_kernel(rpb_ref, o_ref, *, layer):
    h = pl.program_id(0)
    shape = (GRID_W, 2 * GRID_W)
    qc = lax.broadcasted_iota(jnp.int32, shape, 0)
    lane = lax.broadcasted_iota(jnp.int32, shape, 1)
    kc = lane % GRID_W
    upper = lane >= GRID_W
    c0 = jnp.clip(qc - WIN_C // 2, 0, GRID_W - WIN_C)
    in_window = (kc >= c0) & (kc < c0 + WIN_C)
    rel = kc - qc + (WIN_C - 1)
    n_dc = 2 * WIN_C - 1
    n_dr = 2 * WIN_R - 1
    row_base = (layer * N_HEADS + h) * n_dr
    for e in range(_BIAS_TILES):
        d_lo = max(e - 1, 0)
        d_hi = min(e, n_dr - 1)
        tile = jnp.zeros(shape, F32)
        for dc in range(n_dc):
            lo = rpb_ref[(row_base + d_lo) * n_dc + dc]
            hi = rpb_ref[(row_base + d_hi) * n_dc + dc]
            tile = jnp.where(rel == dc, jnp.where(upper, hi, lo), tile)
        valid = in_window
        if e == 0:
            valid = valid & upper
        if e == _BIAS_TILES - 1:
            valid = valid & jnp.logical_not(upper)
        o_ref[e] = jnp.where(valid, tile, NEG_INF)


def _bias_table(rpb, layer):
    return pl.pallas_call(
        functools.partial(_bias_table_kernel, layer=layer),
        out_shape=jax.ShapeDtypeStruct((N_HEADS, _BIAS_TILES, GRID_W, 2 * GRID_W), F32),
        grid=(N_HEADS,),
        in_specs=[pl.BlockSpec(memory_space=pltpu.SMEM)],
        out_specs=pl.BlockSpec((None, _BIAS_TILES, GRID_W, 2 * GRID_W), lambda h: (h, 0, 0, 0)),
        compiler_params=_cparams("parallel"),
        name="rel_pos_bias_table",
    )(rpb.reshape(-1))


def _nbr_attn_kernel(q_ref, k_ref, v_ref, kc_ref, vc_ref, bias_ref, o_ref,
                     s_ref, sc_ref, p_ref, pc_ref, *kv_scratch, rows):
    if kv_scratch:
        kbf, vbf = kv_scratch
        kbf[...] = k_ref[...].astype(BF16)
        vbf[...] = v_ref[...].astype(BF16)
    else:
        kbf, vbf = k_ref, v_ref
    kc = kc_ref[...].astype(BF16)
    vc = vc_ref[...].astype(BF16)
    dims = (((1,), (1,)), ((), ()))
    pair = 2 * GRID_W
    lower = lax.broadcasted_iota(jnp.int32, (GRID_W, pair), 1) < GRID_W
    zero_tile = jnp.zeros((GRID_W, pair), BF16)

    for g in range(rows // _Q_GROUP):
        slot = g % 2
        ks = min(max(g * _Q_GROUP - WIN_R // 2, 0), rows - _BAND)
        qsl = slice(g * _Q_GROUP * GRID_W, (g + 1) * _Q_GROUP * GRID_W)
        ksl = slice(ks * GRID_W, (ks + _BAND) * GRID_W)
        qg = q_ref[qsl, :]
        s_ref[slot] = lax.dot_general(qg, kbf[ksl, :], dims, preferred_element_type=F32) * ATTN_SCALE
        sc_ref[slot] = lax.dot_general(qg, kc, dims, preferred_element_type=F32) * ATTN_SCALE
        for i in range(_Q_GROUP):
            qr = g * _Q_GROUP + i
            r0 = min(max(qr - WIN_R // 2, 0), rows - WIN_R)
            first = r0 - ks
            j0, j1 = first // 2, (first + WIN_R + 1) // 2
            rsl = slice(i * GRID_W, (i + 1) * GRID_W)
            tiles = []
            for j in range(j0, j1):
                kr0 = ks + 2 * j
                t = s_ref[slot, rsl, j * pair:(j + 1) * pair] + bias_ref[kr0 - qr + WIN_R]
                if kr0 < r0:
                    t = jnp.where(lower, NEG_INF, t)
                if kr0 + 1 >= r0 + WIN_R:
                    t = jnp.where(lower, t, NEG_INF)
                tiles.append(t)
            sc = sc_ref[slot, rsl, :]
            mt = functools.reduce(jnp.maximum, tiles)
            m = jnp.maximum(jnp.max(mt, axis=-1, keepdims=True), jnp.max(sc, axis=-1, keepdims=True))
            ps = [jnp.exp(t - m) for t in tiles]
            pc = jnp.exp(sc - m)
            denom = (jnp.sum(functools.reduce(jnp.add, ps), axis=-1, keepdims=True)
                     + jnp.sum(pc, axis=-1, keepdims=True))
            inv = 1.0 / denom
            for j in range(_BAND // 2):
                val = (ps[j - j0] * inv).astype(BF16) if j0 <= j < j1 else zero_tile
                p_ref[slot, rsl, j * pair:(j + 1) * pair] = val
            pc_ref[slot, rsl, :] = (pc * inv).astype(BF16)
        o = (jnp.dot(p_ref[slot], vbf[ksl, :], preferred_element_type=F32)
             + jnp.dot(pc_ref[slot], vc, preferred_element_type=F32))
        o_ref[qsl, :] = o.astype(o_ref.dtype)


def _nbr_attention(q, k, v, cache_k, cache_v, bias, layer, st):
    seq = st.seq
    rows = seq // GRID_W
    assert rows >= _BAND and rows % _Q_GROUP == 0
    past = cache_k.shape[2]
    ck = cache_k.reshape(st.batch, DEPTH, past, ATTN_WIDTH)
    cv = cache_v.reshape(st.batch, DEPTH, past, ATTN_WIDTH)
    tok_spec = pl.BlockSpec((seq, HEAD_DIM), lambda b, h: (b, h))
    cache_spec = pl.BlockSpec((None, None, past, HEAD_DIM), lambda b, h: (b, layer, 0, h))
    nq = _Q_GROUP * GRID_W
    nk = _BAND * GRID_W
    kv_scratch = [] if k.dtype == BF16 and v.dtype == BF16 else [pltpu.VMEM((seq, HEAD_DIM), BF16)] * 2
    return pl.pallas_call(
        functools.partial(_nbr_attn_kernel, rows=rows),
        out_shape=jax.ShapeDtypeStruct((st.batch * seq, ATTN_WIDTH), BF16),
        grid=(st.batch, N_HEADS),
        in_specs=[tok_spec, tok_spec, tok_spec, cache_spec, cache_spec,
                  pl.BlockSpec((None, _BIAS_TILES, GRID_W, 2 * GRID_W), lambda b, h: (h, 0, 0, 0))],
        out_specs=tok_spec,
        scratch_shapes=[pltpu.VMEM((2, nq, nk), F32), pltpu.VMEM((2, nq, past), F32),
                        pltpu.VMEM((2, nq, nk), BF16), pltpu.VMEM((2, nq, past), BF16)] + kv_scratch,
        compiler_params=_cparams("parallel", "parallel"),
        name="neighbourhood_attention",
    )(q, k, v, ck, cv, bias)


_CONV_PAD = 8


def _short_conv_kernel(u_ref, w_ref, b_ref, o_ref, *, seq):
    width = u_ref.shape[1]
    w0 = w_ref[0:1, :]
    w1 = w_ref[1:2, :]
    w2 = w_ref[2:3, :]
    b = b_ref[...]
    rc = max(8, _ROW_CHUNK * CH_TILE // width)

    def emit(base, window):
        prev = window[_CONV_PAD - 1:_CONV_PAD - 1 + rc]
        cur = window[_CONV_PAD:_CONV_PAD + rc]
        nxt = window[_CONV_PAD + 1:_CONV_PAD + 1 + rc]
        o_ref[pl.ds(base, rc), :] = prev * w0 + cur * w1 + nxt * w2 + b

    _halo_chunks(u_ref, seq, rc, _CONV_PAD, emit)


def _short_conv(u, conv_w, conv_b, layer, st):
    width = u.shape[1]
    seq = st.seq
    tc = HYENA_WIDTH if seq <= SHORT_SEQ else CH_TILE
    return pl.pallas_call(
        functools.partial(_short_conv_kernel, seq=seq),
        out_shape=jax.ShapeDtypeStruct((st.batch * seq, width), F32),
        grid=(st.batch, width // tc),
        in_specs=[pl.BlockSpec((seq, tc), lambda b, j: (b, j)),
                  pl.BlockSpec((None, 3, tc), lambda b, j: (layer, 0, j)),
                  pl.BlockSpec((None, 1, tc), lambda b, j: (layer, 0, j))],
        out_specs=pl.BlockSpec((seq, tc), lambda b, j: (b, j)),
        compiler_params=_cparams("parallel", "parallel"),
        name="hyena_short_conv",
    )(u, conv_w, conv_b.reshape(DEPTH, 1, width))


def _dft_matrices(seq):
    f = jnp.arange(seq, dtype=jnp.int32)[:, None]
    t = jnp.arange(seq, dtype=jnp.int32)[None, :]
    ang = ((f * t) % (2 * seq)).astype(F32) * (math.pi / seq)
    top = jnp.cos(ang)
    bot = jnp.where(f == 0, (1 - 2 * (t % 2)).astype(F32), -jnp.sin(ang))
    colscale = jnp.where(jnp.arange(seq) == 0, 0.5, 1.0).astype(F32)[None, :] / seq
    inv = jnp.concatenate([top.T * colscale, bot.T * colscale], axis=1)
    return top.astype(BF16), bot.astype(BF16), inv.astype(BF16)


def _filter_mlp_kernel(z_ref, w1_ref, b1_ref, w2_ref, b2_ref, fr_ref, o_ref):
    fr = fr_ref[...]
    h = jnp.sin(fr * (jnp.dot(z_ref[...], w1_ref[...], preferred_element_type=F32) + b1_ref[...]))
    o_ref[...] = jnp.sin(fr * (jnp.dot(h, w2_ref[...], preferred_element_type=F32) + b2_ref[...]))


def _filter_mlp(seq, w1, b1, w2, b2, freq, layer):
    t = jnp.linspace(0.0, 1.0, seq, dtype=F32)[:, None]
    bands = (FILTER_EMB - 1) // 2
    ang = (2.0 * math.pi / seq) * jnp.arange(seq, dtype=F32)[:, None] * jnp.linspace(
        1e-4, bands - 1, bands, dtype=F32)[None]
    z = jnp.concatenate([t, jnp.cos(ang), -jnp.sin(ang)], axis=-1)
    z = jnp.pad(z, ((0, 0), (0, LANES - FILTER_EMB)))
    w1p = jnp.pad(w1, ((0, 0), (0, LANES - FILTER_EMB), (0, 0)))
    vec = lambda a: a.reshape(DEPTH, 1, FILTER_HIDDEN)
    vspec = pl.BlockSpec((None, 1, FILTER_HIDDEN), lambda i: (layer, 0, 0))
    return pl.pallas_call(
        _filter_mlp_kernel,
        out_shape=jax.ShapeDtypeStruct((seq, FILTER_HIDDEN), F32),
        grid=(1,),
        in_specs=[pl.BlockSpec((seq, LANES), lambda i: (0, 0)),
                  pl.BlockSpec((None, LANES, FILTER_HIDDEN), lambda i: (layer, 0, 0)),
                  vspec,
                  pl.BlockSpec((None, FILTER_HIDDEN, FILTER_HIDDEN), lambda i: (layer, 0, 0)),
                  vspec, vspec],
        out_specs=pl.BlockSpec((seq, FILTER_HIDDEN), lambda i: (0, 0)),
        compiler_params=_cparams("arbitrary"),
        name="hyena_filter_mlp",
    )(z, w1p, vec(b1), w2, vec(b2), vec(freq))


def _filter_taps_kernel(h_ref, dl_ref, wf_ref, wb_ref, a_ref, d_ref, nyq_ref, *, seq):
    h = h_ref[...]
    shape = (seq, dl_ref.shape[1])
    row = lax.broadcasted_iota(jnp.int32, shape, 0)
    t = row.astype(F32) / float(seq - 1)
    dec = jnp.exp(-t * dl_ref[...])
    fwd = jnp.dot(h, wf_ref[...], preferred_element_type=F32) * dec
    bwd = jnp.dot(h, wb_ref[...], preferred_element_type=F32) * dec
    bwd = jnp.where(row == 0, 0.0, bwd)
    inv = 1.0 / jnp.sum(jnp.abs(fwd) + jnp.abs(bwd), axis=0, keepdims=True)
    a = (fwd + bwd) * inv
    a_ref[...] = a
    d_ref[...] = (fwd - bwd) * inv
    sign = jnp.where(row % 2 == 0, 1.0, -1.0)
    nyq_ref[0:1, :] = jnp.sum(a * sign, axis=0, keepdims=True)
    quarter = jnp.where(row % 4 == 0, 1.0, jnp.where(row % 4 == 2, -1.0, 0.0))
    nyq_ref[1:2, :] = jnp.sum(a * quarter, axis=0, keepdims=True)


def _filter_taps(hidden, w3, layer, *, seq):
    tc = 256
    nct = HYENA_WIDTH // tc
    deltas = jnp.abs(jnp.linspace(math.log(DECAY_TARGET) / SLOW_DECAY_PCT,
                                  math.log(DECAY_TARGET) / FAST_DECAY_PCT, HYENA_WIDTH, dtype=F32))
    tap_shape = jax.ShapeDtypeStruct((HYENA_ORDER, seq, HYENA_WIDTH), F32)
    tap_spec = pl.BlockSpec((None, seq, tc), lambda o, j: (o, 0, j))
    return pl.pallas_call(
        functools.partial(_filter_taps_kernel, seq=seq),
        out_shape=(tap_shape, tap_shape, jax.ShapeDtypeStruct((HYENA_ORDER, 2, HYENA_WIDTH), F32)),
        grid=(HYENA_ORDER, nct),
        in_specs=[pl.BlockSpec((seq, FILTER_HIDDEN), lambda o, j: (0, 0)),
                  pl.BlockSpec((1, tc), lambda o, j: (0, j)),
                  pl.BlockSpec((None, FILTER_HIDDEN, tc), lambda o, j: (layer, 0, (2 * o) * nct + j)),
                  pl.BlockSpec((None, FILTER_HIDDEN, tc), lambda o, j: (layer, 0, (2 * o + 1) * nct + j))],
        out_specs=(tap_spec, tap_spec, pl.BlockSpec((None, 2, tc), lambda o, j: (o, 0, j))),
        compiler_params=_cparams("parallel", "parallel"),
        name="hyena_filter_taps",
    )(hidden, deltas.reshape(1, HYENA_WIDTH), w3, w3)


def _filter_spec_kernel(wt_ref, wb_ref, a_ref, d_ref, nyq_ref, o_ref, abf, dbf):
    i = pl.program_id(2)

    @pl.when(i == 0)
    def _():
        abf[...] = a_ref[...].astype(BF16)
        dbf[...] = d_ref[...].astype(BF16)

    for rows in _row_chunks(wt_ref.shape[0]):
        top = jnp.dot(wt_ref[rows, :], abf[...], preferred_element_type=F32)
        bot = jnp.dot(wb_ref[rows, :], dbf[...], preferred_element_type=F32)
        if rows.start == 0:
            row = lax.broadcasted_iota(jnp.int32, bot.shape, 0)
            bot = jnp.where((row == 0) & (i == 0), nyq_ref[0:1, :], bot)
        o_ref[0, rows, :] = top
        o_ref[1, rows, :] = bot


def _dft_tiles(seq):
    return seq, (HYENA_WIDTH if seq <= SHORT_SEQ else CH_TILE)


_RESIDENT = pl.Buffered(1)


def _filter_spectrum(dft_top, dft_bot, taps_a, taps_d, nyq, *, seq):
    tf, tc = _dft_tiles(seq)
    tc = min(tc, CH_TILE // 2)
    wspec = pl.BlockSpec((tf, seq), lambda o, c, i: (i, 0), pipeline_mode=_RESIDENT)
    tspec = pl.BlockSpec((None, seq, tc), lambda o, c, i: (o, 0, c))
    return pl.pallas_call(
        _filter_spec_kernel,
        out_shape=jax.ShapeDtypeStruct((HYENA_ORDER, 2, seq, HYENA_WIDTH), F32),
        grid=(HYENA_ORDER, HYENA_WIDTH // tc, seq // tf),
        in_specs=[wspec, wspec, tspec, tspec,
                  pl.BlockSpec((None, 2, tc), lambda o, c, i: (o, 0, c))],
        out_specs=pl.BlockSpec((None, 2, tf, tc), lambda o, c, i: (o, 0, i, c)),
        scratch_shapes=[pltpu.VMEM((seq, tc), BF16)] * 2,
        compiler_params=_cparams("parallel", "parallel", "arbitrary"),
        name="hyena_filter_spectrum",
    )(dft_top, dft_bot, taps_a, taps_d, nyq)


def _fwd_dft_kernel(wt_ref, wb_ref, z_ref, p_ref, o_ref, zbf):
    i = pl.program_id(2)

    @pl.when(i == 0)
    def _():
        zbf[...] = z_ref[...].astype(BF16)

    for rows in _row_chunks(wt_ref.shape[0]):
        top = jnp.dot(wt_ref[rows, :], zbf[...], preferred_element_type=F32)
        bot = jnp.dot(wb_ref[rows, :], zbf[...], preferred_element_type=F32)
        pt = p_ref[0, rows, :]
        pb = p_ref[1, rows, :]
        re = top * pt - bot * pb
        im = top * pb + bot * pt
        if rows.start == 0:
            row = lax.broadcasted_iota(jnp.int32, top.shape, 0)
            real_row = (row == 0) & (i == 0)
            re = jnp.where(real_row, top * pt, re)
            im = jnp.where(real_row, bot * pb, im)
        o_ref[0, rows, :] = re.astype(o_ref.dtype)
        o_ref[1, rows, :] = im.astype(o_ref.dtype)


def _fwd_dft(dft_top, dft_bot, z, z_col_off, spec, order, *, batch, seq):
    tf, tc = _dft_tiles(seq)
    zc = z_col_off // tc
    wspec = pl.BlockSpec((tf, seq), lambda c, b, i: (i, 0), pipeline_mode=_RESIDENT)
    return pl.pallas_call(
        _fwd_dft_kernel,
        out_shape=jax.ShapeDtypeStruct((batch, 2, seq, HYENA_WIDTH), BF16),
        grid=(HYENA_WIDTH // tc, batch, seq // tf),
        in_specs=[wspec, wspec,
                  pl.BlockSpec((seq, tc), lambda c, b, i: (b, zc + c)),
                  pl.BlockSpec((None, 2, tf, tc), lambda c, b, i: (order, 0, i, c), pipeline_mode=_RESIDENT)],
        out_specs=pl.BlockSpec((None, 2, tf, tc), lambda c, b, i: (b, 0, i, c)),
        scratch_shapes=[pltpu.VMEM((seq, tc), BF16)],
        compiler_params=_cparams("parallel", "parallel", "arbitrary"),
        name="hyena_fwd_dft",
    )(dft_top, dft_bot, z, spec)


def _inv_dft_kernel(wi_ref, s_ref, gate_ref, z_ref, bias_ref, o_ref):
    s = s_ref[...]
    s2 = s.reshape(s.shape[0] * s.shape[1], s.shape[2])
    for rows in _row_chunks(wi_ref.shape[0]):
        conv = jnp.dot(wi_ref[rows, :], s2, preferred_element_type=F32)
        o_ref[rows, :] = (gate_ref[rows, :] * (conv + bias_ref[...] * z_ref[rows, :])).astype(o_ref.dtype)


def _inv_dft(dft_inv, s, gate, gate_col_off, z, z_col_off, hy_bias, layer, order, out_dtype, *, batch, seq):
    tl, tc = _dft_tiles(seq)
    nrb = seq // tl
    gc = gate_col_off // tc
    zc = z_col_off // tc
    return pl.pallas_call(
        _inv_dft_kernel,
        out_shape=jax.ShapeDtypeStruct((batch * seq, HYENA_WIDTH), out_dtype),
        grid=(HYENA_WIDTH // tc, batch, nrb),
        in_specs=[pl.BlockSpec((tl, 2 * seq), lambda c, b, i: (i, 0), pipeline_mode=_RESIDENT),
                  pl.BlockSpec((None, 2, seq, tc), lambda c, b, i: (b, 0, 0, c)),
                  pl.BlockSpec((tl, tc), lambda c, b, i: (b * nrb + i, gc + c)),
                  pl.BlockSpec((tl, tc), lambda c, b, i: (b * nrb + i, zc + c)),
                  pl.BlockSpec((None, 1, tc), lambda c, b, i: (layer * HYENA_ORDER + order, 0, c))],
        out_specs=pl.BlockSpec((tl, tc), lambda c, b, i: (b * nrb + i, c)),
        compiler_params=_cparams("parallel", "parallel", "arbitrary"),
        name="hyena_inv_dft",
    )(dft_inv, s, gate, z, hy_bias.reshape(DEPTH * HYENA_ORDER, 1, HYENA_WIDTH))


def _dft_matrices_r2(seq):
    half = seq // 2
    g = jnp.arange(half, dtype=jnp.int32)[:, None]
    m = jnp.arange(half, dtype=jnp.int32)[None, :]
    ang_e = ((2 * g * m) % (2 * seq)).astype(F32) * (math.pi / seq)
    ang_o = ((g * (2 * m + 1)) % (2 * seq)).astype(F32) * (math.pi / seq)
    alt = (1 - 2 * (m % 2)).astype(F32)
    ce = jnp.cos(ang_e)
    co = jnp.cos(ang_o)
    se = jnp.where(g == 0, alt, -jnp.sin(ang_e))
    so = jnp.where(g == 0, -alt, -jnp.sin(ang_o))
    colscale = jnp.where(jnp.arange(half) == 0, 0.5, 1.0).astype(F32)[None, :] / seq
    we = jnp.concatenate([ce.T * colscale, se.T / seq], axis=1)
    wo = jnp.concatenate([co.T * colscale, so.T / seq], axis=1)
    return tuple(a.astype(BF16) for a in (ce, co, se, so)), (we.astype(BF16), wo.astype(BF16))


def _half_transforms(mats, xe, xo, ye, yo, rows):
    ce_ref, co_ref, se_ref, so_ref = mats
    ec = jnp.dot(ce_ref[rows, :], xe[...], preferred_element_type=F32)
    oc = jnp.dot(co_ref[rows, :], xo[...], preferred_element_type=F32)
    es = jnp.dot(se_ref[rows, :], ye[...], preferred_element_type=F32)
    os_ = jnp.dot(so_ref[rows, :], yo[...], preferred_element_type=F32)
    return ec, oc, es, os_


def _filter_spec_r2_kernel(ce_ref, co_ref, se_ref, so_ref, ae_ref, ao_ref, de_ref, do_ref, edge_ref, o_ref,
                           aeb, aob, deb, dob):
    aeb[...] = ae_ref[...].astype(BF16)
    aob[...] = ao_ref[...].astype(BF16)
    deb[...] = de_ref[...].astype(BF16)
    dob[...] = do_ref[...].astype(BF16)
    for rows in _row_chunks(ce_ref.shape[0]):
        ec, oc, es, os_ = _half_transforms((ce_ref, co_ref, se_ref, so_ref), aeb, aob, deb, dob, rows)
        im_a = es + os_
        im_b = os_ - es
        if rows.start == 0:
            row0 = lax.broadcasted_iota(jnp.int32, ec.shape, 0) == 0
            im_a = jnp.where(row0, edge_ref[1:2, :], im_a)
            im_b = jnp.where(row0, os_, im_b)
        o_ref[0, rows, :] = ec + oc
        o_ref[1, rows, :] = im_a
        o_ref[2, rows, :] = ec - oc
        o_ref[3, rows, :] = im_b


def _filter_spectrum_r2(mats, a_even, a_odd, d_even, d_odd, edge, *, seq):
    half = seq // 2
    tc = CH_TILE // 2
    nct = HYENA_WIDTH // tc
    mspec = pl.BlockSpec((half, half), lambda o, c: (0, 0), pipeline_mode=_RESIDENT)
    tspec = pl.BlockSpec((None, half, tc), lambda o, c: (o, 0, c))
    return pl.pallas_call(
        _filter_spec_r2_kernel,
        out_shape=jax.ShapeDtypeStruct((HYENA_ORDER, 4, half, HYENA_WIDTH), F32),
        grid=(HYENA_ORDER, nct),
        in_specs=[mspec] * 4 + [tspec] * 4 + [pl.BlockSpec((None, 2, tc), lambda o, c: (o, 0, c))],
        out_specs=pl.BlockSpec((None, 4, half, tc), lambda o, c: (o, 0, 0, c)),
        scratch_shapes=[pltpu.VMEM((half, tc), BF16)] * 4,
        compiler_params=_cparams("parallel", "parallel"),
        name="hyena_filter_spectrum_r2",
    )(*mats, a_even, a_odd, d_even, d_odd, edge)


def _fwd_dft_r2_kernel(ce_ref, co_ref, se_ref, so_ref, ze_ref, zo_ref, p_ref, o_ref, zeb, zob):
    zeb[...] = ze_ref[...].astype(BF16)
    zob[...] = zo_ref[...].astype(BF16)
    for rows in _row_chunks(ce_ref.shape[0]):
        ec, oc, es, os_ = _half_transforms((ce_ref, co_ref, se_ref, so_ref), zeb, zob, zeb, zob, rows)
        re_a, re_b = ec + oc, ec - oc
        im_a, im_b = es + os_, os_ - es
        pra, pia, prb, pib = (p_ref[j, rows, :] for j in range(4))
        first = rows.start == 0
        if first:
            row0 = lax.broadcasted_iota(jnp.int32, ec.shape, 0) == 0
            im_a = jnp.where(row0, es, im_a)
            im_b = jnp.where(row0, os_, im_b)
        re_a2 = re_a * pra - im_a * pia
        im_a2 = re_a * pia + im_a * pra
        re_b2 = re_b * prb - im_b * pib
        im_b2 = re_b * pib + im_b * prb
        if first:
            re_a2 = jnp.where(row0, re_a * pra, re_a2)
            re_b2 = jnp.where(row0, re_b * prb, re_b2)
            im_a2 = jnp.where(row0, im_a * pia - im_b * pib, im_a2)
            im_b2 = jnp.where(row0, im_a * pib + im_b * pia, im_b2)
        us = im_a2 - im_b2
        vs = im_a2 + im_b2
        if first:
            us = jnp.where(row0, im_a2, us)
            vs = jnp.where(row0, im_b2, vs)
        o_ref[0, rows, :] = (re_a2 + re_b2).astype(o_ref.dtype)
        o_ref[1, rows, :] = us.astype(o_ref.dtype)
        o_ref[2, rows, :] = (re_a2 - re_b2).astype(o_ref.dtype)
        o_ref[3, rows, :] = vs.astype(o_ref.dtype)


def _fwd_dft_r2(mats, ze, ze_cb, zo, zo_cb, spec, order, *, batch, seq):
    half = seq // 2
    tc = CH_TILE
    mspec = pl.BlockSpec((half, half), lambda c, b: (0, 0), pipeline_mode=_RESIDENT)
    return pl.pallas_call(
        _fwd_dft_r2_kernel,
        out_shape=jax.ShapeDtypeStruct((batch, 4, half, HYENA_WIDTH), BF16),
        grid=(HYENA_WIDTH // tc, batch),
        in_specs=[mspec] * 4 + [
            pl.BlockSpec((half, tc), lambda c, b: (b, ze_cb + c)),
            pl.BlockSpec((half, tc), lambda c, b: (b, zo_cb + c)),
            pl.BlockSpec((None, 4, half, tc), lambda c, b: (order, 0, 0, c))],
        out_specs=pl.BlockSpec((None, 4, half, tc), lambda c, b: (b, 0, 0, c)),
        scratch_shapes=[pltpu.VMEM((half, tc), BF16)] * 2,
        compiler_params=_cparams("parallel", "parallel"),
        name="hyena_fwd_dft_r2",
    )(*mats, ze, zo, spec)


def _inv_dft_r2_kernel(we_ref, wo_ref, s_ref, ge_ref, go_ref, ze_ref, zo_ref, bias_ref, *outs, interleave):
    s = s_ref[...]
    half, tc = s.shape[1], s.shape[2]
    su = s[0:2].reshape(2 * half, tc)
    sv = s[2:4].reshape(2 * half, tc)
    bias = bias_ref[...]
    for rows in _row_chunks(half):
        ye = jnp.dot(we_ref[rows, :], su, preferred_element_type=F32)
        yo = jnp.dot(wo_ref[rows, :], sv, preferred_element_type=F32)
        even = ge_ref[rows, :] * (ye + bias * ze_ref[rows, :])
        odd = go_ref[rows, :] * (yo + bias * zo_ref[rows, :])
        if not interleave:
            oe_ref, oo_ref = outs
            oe_ref[rows, :] = even.astype(oe_ref.dtype)
            oo_ref[rows, :] = odd.astype(oo_ref.dtype)
            continue
        o_ref, stage = outs
        n = rows.stop - rows.start
        for j in range(tc // LANES):
            lanes = slice(j * LANES, (j + 1) * LANES)
            stage[j, pl.ds(0, n, stride=2), :] = even[:, lanes]
            stage[j, pl.ds(1, n, stride=2), :] = odd[:, lanes]
            o_ref[2 * rows.start:2 * rows.stop, lanes] = stage[j].astype(o_ref.dtype)


def _inv_dft_r2(inv_mats, s, gate_even, gate_odd, gate_cb, *, z_even, z_odd, hy_bias, layer, order, out_dtype,
                batch, seq, interleave=False):
    half = seq // 2
    tc = CH_TILE
    wspec = pl.BlockSpec((half, seq), lambda c, b: (0, 0), pipeline_mode=_RESIDENT)
    if interleave:
        out_shape = jax.ShapeDtypeStruct((batch * seq, HYENA_WIDTH), out_dtype)
        out_specs = pl.BlockSpec((seq, tc), lambda c, b: (b, c))
        scratch = [pltpu.VMEM((tc // LANES, 2 * min(SUB_M, half), LANES), F32)]
    else:
        out = jax.ShapeDtypeStruct((batch * half, HYENA_WIDTH), out_dtype)
        ospec = pl.BlockSpec((half, tc), lambda c, b: (b, c))
        out_shape, out_specs, scratch = (out, out), (ospec, ospec), []
    return pl.pallas_call(
        functools.partial(_inv_dft_r2_kernel, interleave=interleave),
        out_shape=out_shape,
        grid=(HYENA_WIDTH // tc, batch),
        in_specs=[wspec, wspec,
                  pl.BlockSpec((None, 4, half, tc), lambda c, b: (b, 0, 0, c)),
                  pl.BlockSpec((half, tc), lambda c, b: (b, gate_cb + c)),
                  pl.BlockSpec((half, tc), lambda c, b: (b, gate_cb + c)),
                  pl.BlockSpec((half, tc), lambda c, b: (b, c)),
                  pl.BlockSpec((half, tc), lambda c, b: (b, c)),
                  pl.BlockSpec((None, 1, tc), lambda c, b: (layer * HYENA_ORDER + order, 0, c))],
        out_specs=out_specs,
        scratch_shapes=scratch,
        compiler_params=_cparams("parallel", "parallel"),
        name="hyena_inv_dft_r2",
    )(*inv_mats, s, gate_even, gate_odd, z_even, z_odd, hy_bias.reshape(DEPTH * HYENA_ORDER, 1, HYENA_WIDTH))


def _short_conv_split_kernel(u_ref, w_ref, b_ref, oe_ref, oo_ref, stage, *, seq):
    width = u_ref.shape[1]
    w0 = w_ref[0:1, :]
    w1 = w_ref[1:2, :]
    w2 = w_ref[2:3, :]
    b = b_ref[...]
    rc = _ROW_CHUNK
    hr = rc // 2

    def emit(base, window):
        prev = window[_CONV_PAD - 1:_CONV_PAD - 1 + rc]
        cur = window[_CONV_PAD:_CONV_PAD + rc]
        nxt = window[_CONV_PAD + 1:_CONV_PAD + 1 + rc]
        val = prev * w0 + cur * w1 + nxt * w2 + b
        hb = base // 2 if isinstance(base, int) else pl.multiple_of(base // 2, hr)
        for j in range(width // LANES):
            lanes = slice(j * LANES, (j + 1) * LANES)
            stage[j] = val[:, lanes]
            oe_ref[pl.ds(hb, hr), lanes] = stage[j, pl.ds(0, hr, stride=2), :]
            oo_ref[pl.ds(hb, hr), lanes] = stage[j, pl.ds(1, hr, stride=2), :]

    _halo_chunks(u_ref, seq, rc, _CONV_PAD, emit)


def _short_conv_split(u, conv_w, conv_b, layer, st):
    width = u.shape[1]
    seq = st.seq
    half = seq // 2
    tc = CH_TILE
    out = jax.ShapeDtypeStruct((st.batch * half, width), F32)
    ospec = pl.BlockSpec((half, tc), lambda b, j: (b, j))
    return pl.pallas_call(
        functools.partial(_short_conv_split_kernel, seq=seq),
        out_shape=(out, out),
        grid=(st.batch, width // tc),
        in_specs=[pl.BlockSpec((seq, tc), lambda b, j: (b, j)),
                  pl.BlockSpec((None, 3, tc), lambda b, j: (layer, 0, j)),
                  pl.BlockSpec((None, 1, tc), lambda b, j: (layer, 0, j))],
        out_specs=(ospec, ospec),
        scratch_shapes=[pltpu.VMEM((tc // LANES, _ROW_CHUNK, LANES), F32)],
        compiler_params=_cparams("parallel", "parallel"),
        name="hyena_short_conv_split",
    )(u, conv_w, conv_b.reshape(DEPTH, 1, width))


def _filter_taps_split_kernel(he_ref, ho_ref, dl_ref, wf_ref, wb_ref, ae_ref, ao_ref, de_ref, do_ref,
                              edge_ref, *, seq):
    shape = (seq // 2, dl_ref.shape[1])
    m = lax.broadcasted_iota(jnp.int32, shape, 0)

    def taps(h_ref, parity):
        t = (2 * m + parity).astype(F32) / float(seq - 1)
        dec = jnp.exp(-t * dl_ref[...])
        h = h_ref[...]
        return (jnp.dot(h, wf_ref[...], preferred_element_type=F32) * dec,
                jnp.dot(h, wb_ref[...], preferred_element_type=F32) * dec)

    fe, be = taps(he_ref, 0)
    fo, bo = taps(ho_ref, 1)
    be = jnp.where(m == 0, 0.0, be)
    total = (jnp.sum(jnp.abs(fe) + jnp.abs(be), axis=0, keepdims=True)
             + jnp.sum(jnp.abs(fo) + jnp.abs(bo), axis=0, keepdims=True))
    inv = 1.0 / total
    a_e = (fe + be) * inv
    a_o = (fo + bo) * inv
    ae_ref[...] = a_e
    ao_ref[...] = a_o
    de_ref[...] = (fe - be) * inv
    do_ref[...] = (fo - bo) * inv
    edge_ref[0:1, :] = jnp.sum(a_e, axis=0, keepdims=True) - jnp.sum(a_o, axis=0, keepdims=True)
    alt = jnp.where(m % 2 == 0, 1.0, -1.0)
    edge_ref[1:2, :] = jnp.sum(a_e * alt, axis=0, keepdims=True)


def _filter_taps_split(hidden, w3, layer, *, seq):
    tc = 256
    nct = HYENA_WIDTH // tc
    half = seq // 2
    deltas = jnp.abs(jnp.linspace(math.log(DECAY_TARGET) / SLOW_DECAY_PCT,
                                  math.log(DECAY_TARGET) / FAST_DECAY_PCT, HYENA_WIDTH, dtype=F32))
    tap_shape = jax.ShapeDtypeStruct((HYENA_ORDER, half, HYENA_WIDTH), F32)
    tap_spec = pl.BlockSpec((None, half, tc), lambda o, j: (o, 0, j))
    hspec = pl.BlockSpec((half, FILTER_HIDDEN), lambda o, j: (0, 0))
    return pl.pallas_call(
        functools.partial(_filter_taps_split_kernel, seq=seq),
        out_shape=(tap_shape,) * 4 + (jax.ShapeDtypeStruct((HYENA_ORDER, 2, HYENA_WIDTH), F32),),
        grid=(HYENA_ORDER, nct),
        in_specs=[hspec, hspec,
                  pl.BlockSpec((1, tc), lambda o, j: (0, j)),
                  pl.BlockSpec((None, FILTER_HIDDEN, tc), lambda o, j: (layer, 0, (2 * o) * nct + j)),
                  pl.BlockSpec((None, FILTER_HIDDEN, tc), lambda o, j: (layer, 0, (2 * o + 1) * nct + j))],
        out_specs=(tap_spec,) * 4 + (pl.BlockSpec((None, 2, tc), lambda o, j: (o, 0, j)),),
        compiler_params=_cparams("parallel", "parallel"),
        name="hyena_filter_taps_split",
    )(hidden[0::2], hidden[1::2], deltas.reshape(1, HYENA_WIDTH), w3, w3)


def _hyena_long(u, p, layer, st):
    batch, seq = st.batch, st.seq
    uce, uco = _short_conv_split(u, p["hy_conv_w"], p["hy_conv_b"], layer, st)
    hidden = _filter_mlp(seq, p["filt_w1"], p["filt_b1"], p["filt_w2"], p["filt_b2"], p["filt_freq"], layer)
    taps = _filter_taps_split(hidden, p["filt_w3"], layer, seq=seq)
    mats, inv_mats = _dft_matrices_r2(seq)
    spec = _filter_spectrum_r2(mats, *taps, seq=seq)
    nb = HYENA_WIDTH // CH_TILE
    hy_bias = p["hy_bias"]
    s = _fwd_dft_r2(mats, uce, 0, uco, 0, spec, 0, batch=batch, seq=seq)
    z1e, z1o = _inv_dft_r2(inv_mats, s, uce, uco, nb, z_even=uce, z_odd=uco, hy_bias=hy_bias, layer=layer,
                           order=0, out_dtype=F32, batch=batch, seq=seq)
    s = _fwd_dft_r2(mats, z1e, 0, z1o, 0, spec, 1, batch=batch, seq=seq)
    return _inv_dft_r2(inv_mats, s, uce, uco, 2 * nb, z_even=z1e, z_odd=z1o, hy_bias=hy_bias, layer=layer,
                       order=1, out_dtype=BF16, batch=batch, seq=seq, interleave=True)


def _hyena(u, p, layer, st):
    batch, seq = st.batch, st.seq
    if seq > SHORT_SEQ:
        return _hyena_long(u, p, layer, st)
    uc = _short_conv(u, p["hy_conv_w"], p["hy_conv_b"], layer, st)
    dft_top, dft_bot, dft_inv = _dft_matrices(seq)
    hidden = _filter_mlp(seq, p["filt_w1"], p["filt_b1"], p["filt_w2"], p["filt_b2"], p["filt_freq"], layer)
    taps_a, taps_d, nyq = _filter_taps(hidden, p["filt_w3"], layer, seq=seq)
    spec = _filter_spectrum(dft_top, dft_bot, taps_a, taps_d, nyq, seq=seq)
    s = _fwd_dft(dft_top, dft_bot, uc, 0, spec, 0, batch=batch, seq=seq)
    z1 = _inv_dft(dft_inv, s, uc, HYENA_WIDTH, uc, 0, p["hy_bias"], layer, 0, F32, batch=batch, seq=seq)
    s = _fwd_dft(dft_top, dft_bot, z1, 0, spec, 1, batch=batch, seq=seq)
    return _inv_dft(dft_inv, s, uc, 2 * HYENA_WIDTH, z1, 0, p["hy_bias"], layer, 1, BF16, batch=batch, seq=seq)


def _block(x, p, modflat, layer, st, ctx_kv, shared):
    h = _normmod(x, p["norm1_g"], modflat, st, layer, 1, 0)
    lin = functools.partial(_linear, h, p["w_in"], layer, tn=512, tn_reuse=1024, shared=shared)
    u_pool = lin(COL_POOL, POOL_WIDTH, kind="plain", out_dtype=F32, key="w_in:pool")
    q = lin(COL_Q, ATTN_WIDTH, kind="rmshead", out_dtype=BF16, head_gain=p["q_norm_g"], key="w_in:q")
    kv_dtype = F32 if ctx_kv is None else BF16
    k = lin(COL_K, ATTN_WIDTH, kind="rmshead", out_dtype=kv_dtype, head_gain=p["k_norm_g"], key="w_in:k")
    v = lin(COL_V, ATTN_WIDTH, kind="plain", out_dtype=kv_dtype, key="w_in:v")
    u_hy = lin(COL_HY, (HYENA_ORDER + 1) * HYENA_WIDTH, kind="plain", out_dtype=F32, key="w_in:hyena")
    gates = lin(COL_GATES, 3 * D_MODEL, kind="sigmoid", out_dtype=BF16, key="w_in:gates")

    pool = _pool(u_pool, p["pool_w"], p["pool_scale"], layer, st)
    if ctx_kv is None:
        attn = _ctx_attention(q, k, v, st)
    else:
        attn = _nbr_attention(q, k, v, ctx_kv[0], ctx_kv[1], ctx_kv[2], layer, st)
    hy = _hyena(u_hy, p, layer, st)

    mixed = _mix(pool, attn, hy, p["w_branch_pool"], p["w_branch_attn"], p["w_branch_hyena"], gates, layer,
                 shared)
    x = _linear(mixed, p["w_out"], layer, 0, D_MODEL, tn=512, tn_reuse=1024, kind="residual", out_dtype=F32,
                shared=shared, key="w_out", residual=x, modflat=modflat, st=st, which_gate=2)
    h = _normmod(x, p["norm2_g"], modflat, st, layer, 4, 3)
    a = _ffn1(h, p["w_gate"], p["w_up"], layer, shared)
    half = D_FF // 2
    for k_off in (0, half):
        x = _linear(a, p["w_down"], layer, 0, D_MODEL, tn=512, tm=512, tm_reuse=1024, kind="residual",
                    out_dtype=F32, shared=shared, key="w_down:%d" % k_off, k_off=k_off, k_len=half,
                    residual=x, modflat=modflat, st=st, which_gate=5)
    return x, k, v


_WEIGHT_NAMES = ("ada_w", "ada_b", "norm1_g", "norm2_g", "w_in", "pool_w", "pool_scale",
                 "q_norm_g", "k_norm_g", "rpb", "hy_conv_w", "hy_conv_b", "filt_w1", "filt_b1",
                 "filt_w2", "filt_b2", "filt_w3", "filt_freq", "hy_bias", "w_branch_pool",
                 "w_branch_attn", "w_branch_hyena", "w_out", "w_gate", "w_up", "w_down")


def kernel(x_prompt, x_sample, cache_k, cache_v, c, c_ctx, ada_w, ada_b, norm1_g, norm2_g, w_in, pool_w, pool_scale, q_norm_g, k_norm_g, rpb, hy_conv_w, hy_conv_b, filt_w1, filt_b1, filt_w2, filt_b2, filt_w3, filt_freq, hy_bias, w_branch_pool, w_branch_attn, w_branch_hyena, w_out, w_gate, w_up, w_down):
    p = dict(zip(_WEIGHT_NAMES, (ada_w, ada_b, norm1_g, norm2_g, w_in, pool_w, pool_scale, q_norm_g,
                                 k_norm_g, rpb, hy_conv_w, hy_conv_b, filt_w1, filt_b1, filt_w2, filt_b2,
                                 filt_w3, filt_freq, hy_bias, w_branch_pool, w_branch_attn,
                                 w_branch_hyena, w_out, w_gate, w_up, w_down)))
    nb, ns, d = x_prompt.shape
    lb, ls, _ = x_sample.shape
    assert d == D_MODEL and lb + 1 <= MOD_ROWS
    ctx = _Stream(nb, ns, 0, False)
    lat = _Stream(lb, ls, 1, True)

    cmat = jnp.concatenate([c_ctx[None, :], c, jnp.zeros((MOD_ROWS - 1 - lb, d), F32)], axis=0)
    mod = _modulation(cmat, ada_w, ada_b)
    modflat = mod.reshape(DEPTH * MOD_ROWS * 6, 1, d)

    y_ctx = x_prompt.reshape(nb * ns, d)
    y_lat = x_sample.reshape(lb * ls, d)
    new_k, new_v = [], []
    for layer in range(DEPTH):
        shared = _SharedWeights()
        bias = _bias_table(rpb, layer)
        y_lat, _, _ = _block(y_lat, p, modflat, layer, lat, (cache_k, cache_v, bias), shared)
        y_ctx, k_l, v_l = _block(y_ctx, p, modflat, layer, ctx, None, shared)
        new_k.append(k_l.reshape(nb, ns, N_HEADS, HEAD_DIM))
        new_v.append(v_l.reshape(nb, ns, N_HEADS, HEAD_DIM))

    return (y_ctx.reshape(nb, ns, d), y_lat.reshape(lb, ls, d),
            jnp.stack(new_k, axis=1), jnp.stack(new_v, axis=1))
```

```python
import functools
import math

import jax
import jax.numpy as jnp
from jax import lax
from jax.experimental import pallas as pl
from jax.experimental.pallas import tpu as pltpu

F32 = jnp.float32
BF16 = jnp.bfloat16

D_MODEL = 4096
DEPTH = 2
GRID_W = 64
HEAD_DIM = 128
N_HEADS = D_MODEL // 256
ATTN_WIDTH = N_HEADS * HEAD_DIM
ATTN_SCALE = HEAD_DIM ** -0.5
WIN_R = 8
WIN_C = 16
NEG_INF = -1e30
POOL_WIDTH = D_MODEL // 2
POOL_SIZES = (2, 4, 8, 16)
POOL_GROUP = POOL_WIDTH // len(POOL_SIZES)
HYENA_WIDTH = D_MODEL // 2
HYENA_ORDER = 2
FILTER_EMB = 33
FILTER_HIDDEN = 64
DECAY_TARGET = 1e-2
FAST_DECAY_PCT = 0.3
SLOW_DECAY_PCT = 1.5
D_FF = ((8 * D_MODEL + 767) // 768) * 256
COL_POOL = 0
COL_Q = POOL_WIDTH
COL_K = COL_Q + ATTN_WIDTH
COL_V = COL_K + ATTN_WIDTH
COL_HY = COL_V + ATTN_WIDTH
COL_GATES = COL_HY + (HYENA_ORDER + 1) * HYENA_WIDTH
MOD_ROWS = 8
RMS_EPS = 1e-6

LANES = 128
VMEM_LIMIT = 56 * 1024 * 1024
SUB_M = 256
CH_TILE = 512
SHORT_SEQ = 512


def _cparams(*sem):
    return pltpu.CompilerParams(dimension_semantics=sem, vmem_limit_bytes=VMEM_LIMIT)


def _row_chunks(tm, sub=SUB_M):
    sub = min(sub, tm)
    return [slice(r, r + sub) for r in range(0, tm, sub)]


def _token_tile(m_tok):
    return 1024 if m_tok % 1024 == 0 else 512


def _mod_kernel(c_ref, w_ref, b_ref, o_ref):
    c = c_ref[...]
    s = (c * jax.nn.sigmoid(c)).astype(BF16)
    o_ref[...] = jnp.dot(s, w_ref[...].astype(BF16), preferred_element_type=F32) + b_ref[...]


def _modulation(cmat, ada_w, ada_b):
    tn = 512
    n6 = ada_w.shape[-1]
    return pl.pallas_call(
        _mod_kernel,
        out_shape=jax.ShapeDtypeStruct((DEPTH, MOD_ROWS, n6), F32),
        grid=(DEPTH, n6 // tn),
        in_specs=[pl.BlockSpec((MOD_ROWS, D_MODEL), lambda l, j: (0, 0)),
                  pl.BlockSpec((None, D_MODEL, tn), lambda l, j: (l, 0, j)),
                  pl.BlockSpec((None, 1, tn), lambda l, j: (l, 0, j))],
        out_specs=pl.BlockSpec((None, MOD_ROWS, tn), lambda l, j: (l, 0, j)),
        compiler_params=_cparams("parallel", "parallel"),
        name="modulation",
    )(cmat, ada_w, ada_b.reshape(DEPTH, 1, n6))


class _Stream:
    def __init__(self, batch, seq, mod_base, per_batch_mod):
        self.batch = batch
        self.seq = seq
        self.mod_base = mod_base
        self.per_batch_mod = per_batch_mod

    def mod_index(self, layer, i, tm, which):
        row = self.mod_base + ((i * tm) // self.seq if self.per_batch_mod else 0)
        return (layer * MOD_ROWS + row) * 6 + which


_STAT_ROWS = 32
_NORM_ROWS = 16


def _normmod_kernel(x_ref, g_ref, sc_ref, sh_ref, o_ref, rstd_ref, gain_ref, shift_ref):
    tm, d = x_ref.shape
    gain_ref[...] = jnp.broadcast_to(g_ref[...] * (1.0 + sc_ref[0]), (_NORM_ROWS, d))
    shift_ref[...] = jnp.broadcast_to(sh_ref[0], (_NORM_ROWS, d))

    def stats(ci, carry):
        base = pl.multiple_of(ci * _STAT_ROWS, _STAT_ROWS)
        x = x_ref[pl.ds(base, _STAT_ROWS), :]
        sq = x * x
        while sq.shape[1] > LANES:
            half = sq.shape[1] // 2
            sq = sq[:, :half] + sq[:, half:]
        rstd_ref[pl.ds(base, _STAT_ROWS), :] = sq
        return carry

    lax.fori_loop(0, tm // _STAT_ROWS, stats, 0)
    rstd = lax.rsqrt(jnp.sum(rstd_ref[...], axis=-1, keepdims=True) * (1.0 / d) + RMS_EPS)
    rstd_ref[...] = jnp.broadcast_to(rstd, (tm, LANES))

    def scale(ci, carry):
        base = pl.multiple_of(ci * _NORM_ROWS, _NORM_ROWS)
        x = x_ref[pl.ds(base, _NORM_ROWS), :]
        rstd = jnp.concatenate([rstd_ref[pl.ds(base, _NORM_ROWS), :]] * (d // LANES), axis=1)
        o_ref[pl.ds(base, _NORM_ROWS), :] = (x * rstd * gain_ref[...] + shift_ref[...]).astype(o_ref.dtype)
        return carry

    lax.fori_loop(0, tm // _NORM_ROWS, scale, 0)


def _normmod(x, g, modflat, st, layer, which_scale, which_shift):
    m_tok, d = x.shape
    tm = 256
    return pl.pallas_call(
        _normmod_kernel,
        out_shape=jax.ShapeDtypeStruct((m_tok, d), BF16),
        grid=(m_tok // tm,),
        in_specs=[pl.BlockSpec((tm, d), lambda i: (i, 0)),
                  pl.BlockSpec((None, 1, d), lambda i: (layer, 0, 0)),
                  pl.BlockSpec((1, 1, d), lambda i: (st.mod_index(layer, i, tm, which_scale), 0, 0)),
                  pl.BlockSpec((1, 1, d), lambda i: (st.mod_index(layer, i, tm, which_shift), 0, 0))],
        out_specs=pl.BlockSpec((tm, d), lambda i: (i, 0)),
        scratch_shapes=[pltpu.VMEM((tm, LANES), F32), pltpu.VMEM((_NORM_ROWS, d), F32),
                        pltpu.VMEM((_NORM_ROWS, d), F32)],
        compiler_params=_cparams("parallel"),
        name="normmod",
    )(x, g.reshape(DEPTH, 1, d), modflat, modflat)


def _bf16_weight(w_ref, emit, idx, every):
    if not emit:
        return w_ref
    piece_ref, wbf_ref = emit[idx], emit[len(emit) // 2 + idx]
    m = pl.program_id(1)

    @pl.when(m == 0)
    def _():
        wbf_ref[...] = w_ref[...].astype(BF16)

    rows = piece_ref.shape[0]

    @pl.when(m % every == 0)
    def _():
        off = pl.multiple_of((m // every) * rows, rows)
        piece_ref[...] = wbf_ref[pl.ds(off, rows), :]

    return wbf_ref


def _lin_plain_kernel(x_ref, w_ref, o_ref, *emit, every=None):
    w = _bf16_weight(w_ref, emit, 0, every)
    for rows in _row_chunks(x_ref.shape[0]):
        acc = jnp.dot(x_ref[rows, :], w[...], preferred_element_type=F32)
        o_ref[rows, :] = acc.astype(o_ref.dtype)


def _lin_sigmoid_kernel(x_ref, w_ref, o_ref, *emit, every=None):
    w = _bf16_weight(w_ref, emit, 0, every)
    for rows in _row_chunks(x_ref.shape[0]):
        acc = jnp.dot(x_ref[rows, :], w[...], preferred_element_type=F32)
        o_ref[rows, :] = jax.nn.sigmoid(acc).astype(o_ref.dtype)


def _lin_rmshead_kernel(x_ref, w_ref, g_ref, o_ref, *emit, every=None):
    w = _bf16_weight(w_ref, emit, 0, every)
    g = g_ref[...]
    for rows in _row_chunks(x_ref.shape[0]):
        acc = jnp.dot(x_ref[rows, :], w[...], preferred_element_type=F32)
        for h in range(acc.shape[1] // HEAD_DIM):
            a = acc[:, h * HEAD_DIM:(h + 1) * HEAD_DIM]
            y = a * lax.rsqrt(jnp.mean(a * a, axis=-1, keepdims=True) + RMS_EPS)
            o_ref[rows, h * HEAD_DIM:(h + 1) * HEAD_DIM] = (y * g).astype(o_ref.dtype)


def _lin_residual_kernel(x_ref, w_ref, res_ref, gate_ref, o_ref, *emit, every=None):
    w = _bf16_weight(w_ref, emit, 0, every)
    for rows in _row_chunks(x_ref.shape[0]):
        acc = jnp.dot(x_ref[rows, :], w[...], preferred_element_type=F32)
        o_ref[rows, :] = res_ref[rows, :] + gate_ref[0] * acc


class _SharedWeights:
    def __init__(self):
        self.slabs = {}

    def get(self, key):
        return self.slabs.get(key)

    def put(self, key, value):
        self.slabs[key] = value


_BF16_ROWS = 16


def _weight_plumbing(shared, keys, w_arrays, k_len, ncols, tn, wspec_f32, msteps):
    have = [shared.get(key) for key in keys]
    if all(h is not None for h in have):
        return have, [pl.BlockSpec((k_len, tn), lambda n, m: (0, n))] * len(keys), [], [], [], None, True
    pieces = msteps
    while k_len % pieces or (k_len // pieces) % _BF16_ROWS:
        pieces //= 2
    every = msteps // pieces
    slab = jax.ShapeDtypeStruct((k_len, ncols), BF16)
    piece_spec = pl.BlockSpec((k_len // pieces, tn), lambda n, m: (m // every, n))
    scratch = [pltpu.VMEM((k_len, tn), BF16)] * len(keys)
    return (list(w_arrays), [wspec_f32] * len(keys), [slab] * len(keys), [piece_spec] * len(keys), scratch,
            every, False)


def _linear(x, w, layer, col_off, ncols, *, tn, kind, out_dtype, shared, key, tn_reuse=None, tm=None,
            tm_reuse=None, k_off=0, k_len=None, head_gain=None, residual=None, modflat=None, st=None,
            which_gate=None):
    m_tok = x.shape[0]
    k_len = x.shape[1] if k_len is None else k_len
    if shared.get(key) is not None:
        tn = tn if tn_reuse is None else tn_reuse
        tm = tm if tm_reuse is None else tm_reuse
    tm = _token_tile(m_tok) if tm is None else tm
    cb = col_off // tn
    kb = k_off // k_len
    wspec = pl.BlockSpec((None, k_len, tn), lambda n, m: (layer, kb, cb + n))
    wargs, wspecs, extra_shapes, extra_specs, scratch, every, reuse = _weight_plumbing(
        shared, [key], [w], k_len, ncols, tn, wspec, m_tok // tm)
    in_specs = [pl.BlockSpec((tm, k_len), lambda n, m: (m, kb))] + wspecs
    args = [x] + wargs
    if kind == "plain":
        body = _lin_plain_kernel
    elif kind == "sigmoid":
        body = _lin_sigmoid_kernel
    elif kind == "rmshead":
        body = _lin_rmshead_kernel
        in_specs.append(pl.BlockSpec((None, 1, HEAD_DIM), lambda n, m: (layer, 0, 0)))
        args.append(head_gain.reshape(DEPTH, 1, HEAD_DIM))
    else:
        body = _lin_residual_kernel
        in_specs.append(pl.BlockSpec((tm, tn), lambda n, m: (m, n)))
        in_specs.append(pl.BlockSpec(
            (1, 1, tn), lambda n, m: (st.mod_index(layer, m, tm, which_gate), 0, n)))
        args += [residual, modflat]
    outs = pl.pallas_call(
        functools.partial(body, every=every),
        out_shape=[jax.ShapeDtypeStruct((m_tok, ncols), out_dtype)] + extra_shapes,
        grid=(ncols // tn, m_tok // tm),
        in_specs=in_specs,
        out_specs=[pl.BlockSpec((tm, tn), lambda n, m: (m, n))] + extra_specs,
        scratch_shapes=scratch,
        compiler_params=_cparams("parallel", "arbitrary"),
        name="linear_" + kind,
    )(*args)
    if not reuse:
        shared.put(key, outs[1])
    return outs[0]


def _mix_kernel(xp_ref, xa_ref, xh_ref, wp_ref, wa_ref, wh_ref, gp_ref, ga_ref, gh_ref, o_ref,
                *emit, every=None):
    wp = _bf16_weight(wp_ref, emit, 0, every)
    wa = _bf16_weight(wa_ref, emit, 1, every)
    wh = _bf16_weight(wh_ref, emit, 2, every)
    for rows in _row_chunks(xp_ref.shape[0], 2 * SUB_M):
        acc = gp_ref[rows, :].astype(F32) * jnp.dot(xp_ref[rows, :], wp[...], preferred_element_type=F32)
        acc = acc + ga_ref[rows, :].astype(F32) * jnp.dot(xa_ref[rows, :], wa[...], preferred_element_type=F32)
        acc = acc + gh_ref[rows, :].astype(F32) * jnp.dot(xh_ref[rows, :], wh[...], preferred_element_type=F32)
        o_ref[rows, :] = acc.astype(o_ref.dtype)


def _mix(xp, xa, xh, wp, wa, wh, gates, layer, shared):
    m_tok, k = xp.shape
    n = wp.shape[-1]
    tn = 512
    keys = ["w_branch_pool", "w_branch_attn", "w_branch_hyena"]
    wspec = pl.BlockSpec((None, k, tn), lambda j, i: (layer, 0, j))
    reusing = all(shared.get(key) is not None for key in keys)
    tm = _token_tile(m_tok) if reusing else 512
    wargs, wspecs, extra_shapes, extra_specs, scratch, every, reuse = _weight_plumbing(
        shared, keys, [wp, wa, wh], k, n, tn, wspec, m_tok // tm)
    nb = n // tn
    xspec = pl.BlockSpec((tm, k), lambda j, i: (i, 0))
    outs = pl.pallas_call(
        functools.partial(_mix_kernel, every=every),
        out_shape=[jax.ShapeDtypeStruct((m_tok, n), BF16)] + extra_shapes,
        grid=(nb, m_tok // tm),
        in_specs=[xspec, xspec, xspec] + wspecs + [
            pl.BlockSpec((tm, tn), lambda j, i: (i, j)),
            pl.BlockSpec((tm, tn), lambda j, i: (i, nb + j)),
            pl.BlockSpec((tm, tn), lambda j, i: (i, 2 * nb + j))],
        out_specs=[pl.BlockSpec((tm, tn), lambda j, i: (i, j))] + extra_specs,
        scratch_shapes=scratch,
        compiler_params=_cparams("parallel", "arbitrary"),
        name="mix",
    )(xp, xa, xh, *wargs, gates, gates, gates)
    if not reuse:
        for key, slab in zip(keys, outs[1:]):
            shared.put(key, slab)
    return outs[0]


def _ffn1_kernel(x_ref, wg_ref, wu_ref, o_ref, *emit, every=None):
    wg = _bf16_weight(wg_ref, emit, 0, every)
    wu = _bf16_weight(wu_ref, emit, 1, every)
    for rows in _row_chunks(x_ref.shape[0], 2 * SUB_M):
        x = x_ref[rows, :]
        g = jnp.dot(x, wg[...], preferred_element_type=F32)
        u = jnp.dot(x, wu[...], preferred_element_type=F32)
        o_ref[rows, :] = (g * jax.nn.sigmoid(g) * u).astype(o_ref.dtype)


def _ffn1(x, wg, wu, layer, shared):
    m_tok, k = x.shape
    n = wg.shape[-1]
    tm, tn = _token_tile(m_tok), 256
    keys = ["w_gate", "w_up"]
    wspec = pl.BlockSpec((None, k, tn), lambda j, i: (layer, 0, j))
    wargs, wspecs, extra_shapes, extra_specs, scratch, every, reuse = _weight_plumbing(
        shared, keys, [wg, wu], k, n, tn, wspec, m_tok // tm)
    outs = pl.pallas_call(
        functools.partial(_ffn1_kernel, every=every),
        out_shape=[jax.ShapeDtypeStruct((m_tok, n), BF16)] + extra_shapes,
        grid=(n // tn, m_tok // tm),
        in_specs=[pl.BlockSpec((tm, k), lambda j, i: (i, 0))] + wspecs,
        out_specs=[pl.BlockSpec((tm, tn), lambda j, i: (i, j))] + extra_specs,
        scratch_shapes=scratch,
        compiler_params=_cparams("parallel", "arbitrary"),
        name="ffn_gate_up",
    )(x, *wargs)
    if not reuse:
        for key, slab in zip(keys, outs[1:]):
            shared.put(key, slab)
    return outs[0]


_POOL_PAD = 16
_ROW_CHUNK = 64


def _halo_chunks(u_ref, seq, rc, pad, emit, lanes=slice(None)):
    zeros = jnp.zeros((pad, u_ref[0:1, lanes].shape[1]), F32)
    emit(0, jnp.concatenate([zeros, u_ref[0:rc + pad, lanes]], axis=0))

    def chunk(ci, carry):
        base = pl.multiple_of(ci * rc, rc)
        emit(base, u_ref[pl.ds(pl.multiple_of(base - pad, pad), rc + 2 * pad), lanes])
        return carry

    lax.fori_loop(1, seq // rc - 1, chunk, 0)
    emit(seq - rc, jnp.concatenate([u_ref[seq - rc - pad:seq, lanes], zeros], axis=0))


def _pool_deviation(u_ref, d_ref, lanes, win, seq):
    rc = _ROW_CHUNK
    half = win // 2

    def emit(base, window):
        lo = _POOL_PAD - half
        acc = window[lo:lo + rc]
        for j in range(1, win):
            acc = acc + window[lo + j:lo + j + rc]
        t = base + lax.broadcasted_iota(jnp.int32, acc.shape, 0)
        cnt = (jnp.minimum(t + (win - half), seq) - jnp.maximum(t - half, 0)).astype(F32)
        centre = window[_POOL_PAD:_POOL_PAD + rc]
        d_ref[pl.ds(base, rc), lanes] = (acc / cnt - centre).astype(BF16)

    _halo_chunks(u_ref, seq, rc, _POOL_PAD, emit, lanes)


def _pool_kernel(u_ref, w_ref, sc_ref, o_ref, d_ref, *, seq, all_groups):
    if all_groups:
        for gi, win in enumerate(POOL_SIZES):
            lanes = slice(gi * POOL_GROUP, (gi + 1) * POOL_GROUP)
            _pool_deviation(u_ref, d_ref, lanes, win, seq)
            y = jnp.dot(d_ref[:, lanes], w_ref[gi].astype(BF16), preferred_element_type=F32) * sc_ref[:, lanes]
            o_ref[:, lanes] = y.astype(o_ref.dtype)
        return

    g = pl.program_id(1)
    for gi, win in enumerate(POOL_SIZES):
        @pl.when(g == gi)
        def _(win=win):
            _pool_deviation(u_ref, d_ref, slice(None), win, seq)

    wbf = w_ref[...].astype(BF16)
    scale = sc_ref[...]
    for rows in _row_chunks(seq):
        y = jnp.dot(d_ref[rows, :], wbf, preferred_element_type=F32) * scale
        o_ref[rows, :] = y.astype(o_ref.dtype)


def _pool(u, pool_w, pool_scale, layer, st):
    seq = st.seq
    all_groups = seq <= SHORT_SEQ
    ngroups = len(POOL_SIZES)
    width = POOL_WIDTH if all_groups else POOL_GROUP
    wblock = (None, ngroups, POOL_GROUP, POOL_GROUP) if all_groups else (None, None, POOL_GROUP, POOL_GROUP)
    return pl.pallas_call(
        functools.partial(_pool_kernel, seq=seq, all_groups=all_groups),
        out_shape=jax.ShapeDtypeStruct((st.batch * seq, POOL_WIDTH), BF16),
        grid=(st.batch, 1 if all_groups else ngroups),
        in_specs=[pl.BlockSpec((seq, width), lambda b, g: (b, g)),
                  pl.BlockSpec(wblock, lambda b, g: (layer, g, 0, 0)),
                  pl.BlockSpec((None, 1, width), lambda b, g: (layer, 0, g))],
        out_specs=pl.BlockSpec((seq, width), lambda b, g: (b, g)),
        scratch_shapes=[pltpu.VMEM((seq, width), BF16)],
        compiler_params=_cparams("parallel", "parallel"),
        name="pool_mixer",
    )(u, pool_w, pool_scale.reshape(DEPTH, 1, POOL_WIDTH))


def _ctx_attn_kernel(q_ref, k_ref, v_ref, o_ref):
    for h in range(N_HEADS):
        sl = slice(h * HEAD_DIM, (h + 1) * HEAD_DIM)
        q = q_ref[:, sl]
        k = k_ref[:, sl].astype(BF16)
        v = v_ref[:, sl].astype(BF16)
        s = lax.dot_general(q, k, (((1,), (1,)), ((), ())), preferred_element_type=F32) * ATTN_SCALE
        m = jnp.max(s, axis=-1, keepdims=True)
        p = jnp.exp(s - m)
        denom = jnp.sum(p, axis=-1, keepdims=True)
        o = jnp.dot(p.astype(BF16), v, preferred_element_type=F32) / denom
        o_ref[:, sl] = o.astype(o_ref.dtype)


def _ctx_attention(q, k, v, st):
    spec = pl.BlockSpec((st.seq, ATTN_WIDTH), lambda b: (b, 0))
    return pl.pallas_call(
        _ctx_attn_kernel,
        out_shape=jax.ShapeDtypeStruct((st.batch * st.seq, ATTN_WIDTH), BF16),
        grid=(st.batch,),
        in_specs=[spec, spec, spec],
        out_specs=spec,
        compiler_params=_cparams("parallel"),
        name="context_attention",
    )(q, k, v)


_BIAS_TILES = 2 * WIN_R
_Q_GROUP = 8
_BAND = 2 * WIN_R


def _bias_table_kernel(rpb_ref, o_ref, *, layer):
    h = pl.program_id(0)
    shape = (GRID_W, 2 * GRID_W)
    qc = lax.broadcasted_iota(jnp.int32, shape, 0)
    lane = lax.broadcasted_iota(jnp.int32, shape, 1)
    kc = lane % GRID_W
    upper = lane >= GRID_W
    c0 = jnp.clip(qc - WIN_C // 2, 0, GRID_W - WIN_C)
    in_window = (kc >= c0) & (kc < c0 + WIN_C)
    rel = kc - qc + (WIN_C - 1)
    n_dc = 2 * WIN_C - 1
    n_dr = 2 * WIN_R - 1
    row_base = (layer * N_HEADS + h) * n_dr
    for e in range(_BIAS_TILES):
        d_lo = max(e - 1, 0)
        d_hi = min(e, n_dr - 1)
        tile = jnp.zeros(shape, F32)
        for dc in range(n_dc):
            lo = rpb_ref[(row_base + d_lo) * n_dc + dc]
            hi = rpb_ref[(row_base + d_hi) * n_dc + dc]
            tile = jnp.where(rel == dc, jnp.where(upper, hi, lo), tile)
        valid = in_window
        if e == 0:
            valid = valid & upper
        if e == _BIAS_TILES - 1:
            valid = valid & jnp.logical_not(upper)
        o_ref[e] = jnp.where(valid, tile, NEG_INF)


def _bias_table(rpb, layer):
    return pl.pallas_call(
        functools.partial(_bias_table_kernel, layer=layer),
        out_shape=jax.ShapeDtypeStruct((N_HEADS, _BIAS_TILES, GRID_W, 2 * GRID_W), F32),
        grid=(N_HEADS,),
        in_specs=[pl.BlockSpec(memory_space=pltpu.SMEM)],
        out_specs=pl.BlockSpec((None, _BIAS_TILES, GRID_W, 2 * GRID_W), lambda h: (h, 0, 0, 0)),
        compiler_params=_cparams("parallel"),
        name="rel_pos_bias_table",
    )(rpb.reshape(-1))


def _nbr_attn_kernel(q_ref, k_ref, v_ref, kc_ref, vc_ref, bias_ref, o_ref,
                     s_ref, sc_ref, p_ref, pc_ref, *kv_scratch, rows):
    if kv_scratch:
        kbf, vbf = kv_scratch
        kbf[...] = k_ref[...].astype(BF16)
        vbf[...] = v_ref[...].astype(BF16)
    else:
        kbf, vbf = k_ref, v_ref
    kc = kc_ref[...].astype(BF16)
    vc = vc_ref[...].astype(BF16)
    dims = (((1,), (1,)), ((), ()))
    pair = 2 * GRID_W
    lower = lax.broadcasted_iota(jnp.int32, (GRID_W, pair), 1) < GRID_W
    zero_tile = jnp.zeros((GRID_W, pair), BF16)

    for g in range(rows // _Q_GROUP):
        slot = g % 2
        ks = min(max(g * _Q_GROUP - WIN_R // 2, 0), rows - _BAND)
        qsl = slice(g * _Q_GROUP * GRID_W, (g + 1) * _Q_GROUP * GRID_W)
        ksl = slice(ks * GRID_W, (ks + _BAND) * GRID_W)
        qg = q_ref[qsl, :]
        s_ref[slot] = lax.dot_general(qg, kbf[ksl, :], dims, preferred_element_type=F32) * ATTN_SCALE
        sc_ref[slot] = lax.dot_general(qg, kc, dims, preferred_element_type=F32) * ATTN_SCALE
        for i in range(_Q_GROUP):
            qr = g * _Q_GROUP + i
            r0 = min(max(qr - WIN_R // 2, 0), rows - WIN_R)
            first = r0 - ks
            j0, j1 = first // 2, (first + WIN_R + 1) // 2
            rsl = slice(i * GRID_W, (i + 1) * GRID_W)
            tiles = []
            for j in range(j0, j1):
                kr0 = ks + 2 * j
                t = s_ref[slot, rsl, j * pair:(j + 1) * pair] + bias_ref[kr0 - qr + WIN_R]
                if kr0 < r0:
                    t = jnp.where(lower, NEG_INF, t)
                if kr0 + 1 >= r0 + WIN_R:
                    t = jnp.where(lower, t, NEG_INF)
                tiles.append(t)
            sc = sc_ref[slot, rsl, :]
            mt = functools.reduce(jnp.maximum, tiles)
            m = jnp.maximum(jnp.max(mt, axis=-1, keepdims=True), jnp.max(sc, axis=-1, keepdims=True))
            ps = [jnp.exp(t - m) for t in tiles]
            pc = jnp.exp(sc - m)
            denom = (jnp.sum(functools.reduce(jnp.add, ps), axis=-1, keepdims=True)
                     + jnp.sum(pc, axis=-1, keepdims=True))
            inv = 1.0 / denom
            for j in range(_BAND // 2):
                val = (ps[j - j0] * inv).astype(BF16) if j0 <= j < j1 else zero_tile
                p_ref[slot, rsl, j * pair:(j + 1) * pair] = val
            pc_ref[slot, rsl, :] = (pc * inv).astype(BF16)
        o = (jnp.dot(p_ref[slot], vbf[ksl, :], preferred_element_type=F32)
             + jnp.dot(pc_ref[slot], vc, preferred_element_type=F32))
        o_ref[qsl, :] = o.astype(o_ref.dtype)


def _nbr_attention(q, k, v, cache_k, cache_v, bias, layer, st):
    seq = st.seq
    rows = seq // GRID_W
    assert rows >= _BAND and rows % _Q_GROUP == 0
    past = cache_k.shape[2]
    ck = cache_k.reshape(st.batch, DEPTH, past, ATTN_WIDTH)
    cv = cache_v.reshape(st.batch, DEPTH, past, ATTN_WIDTH)
    tok_spec = pl.BlockSpec((seq, HEAD_DIM), lambda b, h: (b, h))
    cache_spec = pl.BlockSpec((None, None, past, HEAD_DIM), lambda b, h: (b, layer, 0, h))
    nq = _Q_GROUP * GRID_W
    nk = _BAND * GRID_W
    kv_scratch = [] if k.dtype == BF16 and v.dtype == BF16 else [pltpu.VMEM((seq, HEAD_DIM), BF16)] * 2
    return pl.pallas_call(
        functools.partial(_nbr_attn_kernel, rows=rows),
        out_shape=jax.ShapeDtypeStruct((st.batch * seq, ATTN_WIDTH), BF16),
        grid=(st.batch, N_HEADS),
        in_specs=[tok_spec, tok_spec, tok_spec, cache_spec, cache_spec,
                  pl.BlockSpec((None, _BIAS_TILES, GRID_W, 2 * GRID_W), lambda b, h: (h, 0, 0, 0))],
        out_specs=tok_spec,
        scratch_shapes=[pltpu.VMEM((2, nq, nk), F32), pltpu.VMEM((2, nq, past), F32),
                        pltpu.VMEM((2, nq, nk), BF16), pltpu.VMEM((2, nq, past), BF16)] + kv_scratch,
        compiler_params=_cparams("parallel", "parallel"),
        name="neighbourhood_attention",
    )(q, k, v, ck, cv, bias)


_CONV_PAD = 8


def _short_conv_kernel(u_ref, w_ref, b_ref, o_ref, *, seq):
    width = u_ref.shape[1]
    w0 = w_ref[0:1, :]
    w1 = w_ref[1:2, :]
    w2 = w_ref[2:3, :]
    b = b_ref[...]
    rc = max(8, _ROW_CHUNK * CH_TILE // width)

    def emit(base, window):
        prev = window[_CONV_PAD - 1:_CONV_PAD - 1 + rc]
        cur = window[_CONV_PAD:_CONV_PAD + rc]
        nxt = window[_CONV_PAD + 1:_CONV_PAD + 1 + rc]
        o_ref[pl.ds(base, rc), :] = prev * w0 + cur * w1 + nxt * w2 + b

    _halo_chunks(u_ref, seq, rc, _CONV_PAD, emit)


def _short_conv(u, conv_w, conv_b, layer, st):
    width = u.shape[1]
    seq = st.seq
    tc = HYENA_WIDTH if seq <= SHORT_SEQ else CH_TILE
    return pl.pallas_call(
        functools.partial(_short_conv_kernel, seq=seq),
        out_shape=jax.ShapeDtypeStruct((st.batch * seq, width), F32),
        grid=(st.batch, width // tc),
        in_specs=[pl.BlockSpec((seq, tc), lambda b, j: (b, j)),
                  pl.BlockSpec((None, 3, tc), lambda b, j: (layer, 0, j)),
                  pl.BlockSpec((None, 1, tc), lambda b, j: (layer, 0, j))],
        out_specs=pl.BlockSpec((seq, tc), lambda b, j: (b, j)),
        compiler_params=_cparams("parallel", "parallel"),
        name="hyena_short_conv",
    )(u, conv_w, conv_b.reshape(DEPTH, 1, width))


def _dft_matrices(seq):
    f = jnp.arange(seq, dtype=jnp.int32)[:, None]
    t = jnp.arange(seq, dtype=jnp.int32)[None, :]
    ang = ((f * t) % (2 * seq)).astype(F32) * (math.pi / seq)
    top = jnp.cos(ang)
    bot = jnp.where(f == 0, (1 - 2 * (t % 2)).astype(F32), -jnp.sin(ang))
    colscale = jnp.where(jnp.arange(seq) == 0, 0.5, 1.0).astype(F32)[None, :] / seq
    inv = jnp.concatenate([top.T * colscale, bot.T * colscale], axis=1)
    return top.astype(BF16), bot.astype(BF16), inv.astype(BF16)


def _filter_mlp_kernel(z_ref, w1_ref, b1_ref, w2_ref, b2_ref, fr_ref, o_ref):
    fr = fr_ref[...]
    h = jnp.sin(fr * (jnp.dot(z_ref[...], w1_ref[...], preferred_element_type=F32) + b1_ref[...]))
    o_ref[...] = jnp.sin(fr * (jnp.dot(h, w2_ref[...], preferred_element_type=F32) + b2_ref[...]))


def _filter_mlp(seq, w1, b1, w2, b2, freq, layer):
    t = jnp.linspace(0.0, 1.0, seq, dtype=F32)[:, None]
    bands = (FILTER_EMB - 1) // 2
    ang = (2.0 * math.pi / seq) * jnp.arange(seq, dtype=F32)[:, None] * jnp.linspace(
        1e-4, bands - 1, bands, dtype=F32)[None]
    z = jnp.concatenate([t, jnp.cos(ang), -jnp.sin(ang)], axis=-1)
    z = jnp.pad(z, ((0, 0), (0, LANES - FILTER_EMB)))
    w1p = jnp.pad(w1, ((0, 0), (0, LANES - FILTER_EMB), (0, 0)))
    vec = lambda a: a.reshape(DEPTH, 1, FILTER_HIDDEN)
    vspec = pl.BlockSpec((None, 1, FILTER_HIDDEN), lambda i: (layer, 0, 0))
    return pl.pallas_call(
        _filter_mlp_kernel,
        out_shape=jax.ShapeDtypeStruct((seq, FILTER_HIDDEN), F32),
        grid=(1,),
        in_specs=[pl.BlockSpec((seq, LANES), lambda i: (0, 0)),
                  pl.BlockSpec((None, LANES, FILTER_HIDDEN), lambda i: (layer, 0, 0)),
                  vspec,
                  pl.BlockSpec((None, FILTER_HIDDEN, FILTER_HIDDEN), lambda i: (layer, 0, 0)),
                  vspec, vspec],
        out_specs=pl.BlockSpec((seq, FILTER_HIDDEN), lambda i: (0, 0)),
        compiler_params=_cparams("arbitrary"),
        name="hyena_filter_mlp",
    )(z, w1p, vec(b1), w2, vec(b2), vec(freq))


def _filter_taps_kernel(h_ref, dl_ref, wf_ref, wb_ref, a_ref, d_ref, nyq_ref, *, seq):
    h = h_ref[...]
    shape = (seq, dl_ref.shape[1])
    row = lax.broadcasted_iota(jnp.int32, shape, 0)
    t = row.astype(F32) / float(seq - 1)
    dec = jnp.exp(-t * dl_ref[...])
    fwd = jnp.dot(h, wf_ref[...], preferred_element_type=F32) * dec
    bwd = jnp.dot(h, wb_ref[...], preferred_element_type=F32) * dec
    bwd = jnp.where(row == 0, 0.0, bwd)
    inv = 1.0 / jnp.sum(jnp.abs(fwd) + jnp.abs(bwd), axis=0, keepdims=True)
    a = (fwd + bwd) * inv
    a_ref[...] = a
    d_ref[...] = (fwd - bwd) * inv
    sign = jnp.where(row % 2 == 0, 1.0, -1.0)
    nyq_ref[0:1, :] = jnp.sum(a * sign, axis=0, keepdims=True)
    quarter = jnp.where(row % 4 == 0, 1.0, jnp.where(row % 4 == 2, -1.0, 0.0))
    nyq_ref[1:2, :] = jnp.sum(a * quarter, axis=0, keepdims=True)


def _filter_taps(hidden, w3, layer, *, seq):
    tc = 256
    nct = HYENA_WIDTH // tc
    deltas = jnp.abs(jnp.linspace(math.log(DECAY_TARGET) / SLOW_DECAY_PCT,
                                  math.log(DECAY_TARGET) / FAST_DECAY_PCT, HYENA_WIDTH, dtype=F32))
    tap_shape = jax.ShapeDtypeStruct((HYENA_ORDER, seq, HYENA_WIDTH), F32)
    tap_spec = pl.BlockSpec((None, seq, tc), lambda o, j: (o, 0, j))
    return pl.pallas_call(
        functools.partial(_filter_taps_kernel, seq=seq),
        out_shape=(tap_shape, tap_shape, jax.ShapeDtypeStruct((HYENA_ORDER, 2, HYENA_WIDTH), F32)),
        grid=(HYENA_ORDER, nct),
        in_specs=[pl.BlockSpec((seq, FILTER_HIDDEN), lambda o, j: (0, 0)),
                  pl.BlockSpec((1, tc), lambda o, j: (0, j)),
                  pl.BlockSpec((None, FILTER_HIDDEN, tc), lambda o, j: (layer, 0, (2 * o) * nct + j)),
                  pl.BlockSpec((None, FILTER_HIDDEN, tc), lambda o, j: (layer, 0, (2 * o + 1) * nct + j))],
        out_specs=(tap_spec, tap_spec, pl.BlockSpec((None, 2, tc), lambda o, j: (o, 0, j))),
        compiler_params=_cparams("parallel", "parallel"),
        name="hyena_filter_taps",
    )(hidden, deltas.reshape(1, HYENA_WIDTH), w3, w3)


def _filter_spec_kernel(wt_ref, wb_ref, a_ref, d_ref, nyq_ref, o_ref, abf, dbf):
    i = pl.program_id(2)

    @pl.when(i == 0)
    def _():
        abf[...] = a_ref[...].astype(BF16)
        dbf[...] = d_ref[...].astype(BF16)

    for rows in _row_chunks(wt_ref.shape[0]):
        top = jnp.dot(wt_ref[rows, :], abf[...], preferred_element_type=F32)
        bot = jnp.dot(wb_ref[rows, :], dbf[...], preferred_element_type=F32)
        if rows.start == 0:
            row = lax.broadcasted_iota(jnp.int32, bot.shape, 0)
            bot = jnp.where((row == 0) & (i == 0), nyq_ref[0:1, :], bot)
        o_ref[0, rows, :] = top
        o_ref[1, rows, :] = bot


def _dft_tiles(seq):
    return seq, (HYENA_WIDTH if seq <= SHORT_SEQ else CH_TILE)


_RESIDENT = pl.Buffered(1)


def _filter_spectrum(dft_top, dft_bot, taps_a, taps_d, nyq, *, seq):
    tf, tc = _dft_tiles(seq)
    tc = min(tc, CH_TILE // 2)
    wspec = pl.BlockSpec((tf, seq), lambda o, c, i: (i, 0), pipeline_mode=_RESIDENT)
    tspec = pl.BlockSpec((None, seq, tc), lambda o, c, i: (o, 0, c))
    return pl.pallas_call(
        _filter_spec_kernel,
        out_shape=jax.ShapeDtypeStruct((HYENA_ORDER, 2, seq, HYENA_WIDTH), F32),
        grid=(HYENA_ORDER, HYENA_WIDTH // tc, seq // tf),
        in_specs=[wspec, wspec, tspec, tspec,
                  pl.BlockSpec((None, 2, tc), lambda o, c, i: (o, 0, c))],
        out_specs=pl.BlockSpec((None, 2, tf, tc), lambda o, c, i: (o, 0, i, c)),
        scratch_shapes=[pltpu.VMEM((seq, tc), BF16)] * 2,
        compiler_params=_cparams("parallel", "parallel", "arbitrary"),
        name="hyena_filter_spectrum",
    )(dft_top, dft_bot, taps_a, taps_d, nyq)


def _fwd_dft_kernel(wt_ref, wb_ref, z_ref, p_ref, o_ref, zbf):
    i = pl.program_id(2)

    @pl.when(i == 0)
    def _():
        zbf[...] = z_ref[...].astype(BF16)

    for rows in _row_chunks(wt_ref.shape[0]):
        top = jnp.dot(wt_ref[rows, :], zbf[...], preferred_element_type=F32)
        bot = jnp.dot(wb_ref[rows, :], zbf[...], preferred_element_type=F32)
        pt = p_ref[0, rows, :]
        pb = p_ref[1, rows, :]
        re = top * pt - bot * pb
        im = top * pb + bot * pt
        if rows.start == 0:
            row = lax.broadcasted_iota(jnp.int32, top.shape, 0)
            real_row = (row == 0) & (i == 0)
            re = jnp.where(real_row, top * pt, re)
            im = jnp.where(real_row, bot * pb, im)
        o_ref[0, rows, :] = re.astype(o_ref.dtype)
        o_ref[1, rows, :] = im.astype(o_ref.dtype)


def _fwd_dft(dft_top, dft_bot, z, z_col_off, spec, order, *, batch, seq):
    tf, tc = _dft_tiles(seq)
    zc = z_col_off // tc
    wspec = pl.BlockSpec((tf, seq), lambda c, b, i: (i, 0), pipeline_mode=_RESIDENT)
    return pl.pallas_call(
        _fwd_dft_kernel,
        out_shape=jax.ShapeDtypeStruct((batch, 2, seq, HYENA_WIDTH), BF16),
        grid=(HYENA_WIDTH // tc, batch, seq // tf),
        in_specs=[wspec, wspec,
                  pl.BlockSpec((seq, tc), lambda c, b, i: (b, zc + c)),
                  pl.BlockSpec((None, 2, tf, tc), lambda c, b, i: (order, 0, i, c), pipeline_mode=_RESIDENT)],
        out_specs=pl.BlockSpec((None, 2, tf, tc), lambda c, b, i: (b, 0, i, c)),
        scratch_shapes=[pltpu.VMEM((seq, tc), BF16)],
        compiler_params=_cparams("parallel", "parallel", "arbitrary"),
        name="hyena_fwd_dft",
    )(dft_top, dft_bot, z, spec)


def _inv_dft_kernel(wi_ref, s_ref, gate_ref, z_ref, bias_ref, o_ref):
    s = s_ref[...]
    s2 = s.reshape(s.shape[0] * s.shape[1], s.shape[2])
    for rows in _row_chunks(wi_ref.shape[0]):
        conv = jnp.dot(wi_ref[rows, :], s2, preferred_element_type=F32)
        o_ref[rows, :] = (gate_ref[rows, :] * (conv + bias_ref[...] * z_ref[rows, :])).astype(o_ref.dtype)


def _inv_dft(dft_inv, s, gate, gate_col_off, z, z_col_off, hy_bias, layer, order, out_dtype, *, batch, seq):
    tl, tc = _dft_tiles(seq)
    nrb = seq // tl
    gc = gate_col_off // tc
    zc = z_col_off // tc
    return pl.pallas_call(
        _inv_dft_kernel,
        out_shape=jax.ShapeDtypeStruct((batch * seq, HYENA_WIDTH), out_dtype),
        grid=(HYENA_WIDTH // tc, batch, nrb),
        in_specs=[pl.BlockSpec((tl, 2 * seq), lambda c, b, i: (i, 0), pipeline_mode=_RESIDENT),
                  pl.BlockSpec((None, 2, seq, tc), lambda c, b, i: (b, 0, 0, c)),
                  pl.BlockSpec((tl, tc), lambda c, b, i: (b * nrb + i, gc + c)),
                  pl.BlockSpec((tl, tc), lambda c, b, i: (b * nrb + i, zc + c)),
                  pl.BlockSpec((None, 1, tc), lambda c, b, i: (layer * HYENA_ORDER + order, 0, c))],
        out_specs=pl.BlockSpec((tl, tc), lambda c, b, i: (b * nrb + i, c)),
        compiler_params=_cparams("parallel", "parallel", "arbitrary"),
        name="hyena_inv_dft",
    )(dft_inv, s, gate, z, hy_bias.reshape(DEPTH * HYENA_ORDER, 1, HYENA_WIDTH))


def _dft_matrices_r2(seq):
    half = seq // 2
    g = jnp.arange(half, dtype=jnp.int32)[:, None]
    m = jnp.arange(half, dtype=jnp.int32)[None, :]
    ang_e = ((2 * g * m) % (2 * seq)).astype(F32) * (math.pi / seq)
    ang_o = ((g * (2 * m + 1)) % (2 * seq)).astype(F32) * (math.pi / seq)
    alt = (1 - 2 * (m % 2)).astype(F32)
    ce = jnp.cos(ang_e)
    co = jnp.cos(ang_o)
    se = jnp.where(g == 0, alt, -jnp.sin(ang_e))
    so = jnp.where(g == 0, -alt, -jnp.sin(ang_o))
    colscale = jnp.where(jnp.arange(half) == 0, 0.5, 1.0).astype(F32)[None, :] / seq
    we = jnp.concatenate([ce.T * colscale, se.T / seq], axis=1)
    wo = jnp.concatenate([co.T * colscale, so.T / seq], axis=1)
    return tuple(a.astype(BF16) for a in (ce, co, se, so)), (we.astype(BF16), wo.astype(BF16))


def _half_transforms(mats, xe, xo, ye, yo, rows):
    ce_ref, co_ref, se_ref, so_ref = mats
    ec = jnp.dot(ce_ref[rows, :], xe[...], preferred_element_type=F32)
    oc = jnp.dot(co_ref[rows, :], xo[...], preferred_element_type=F32)
    es = jnp.dot(se_ref[rows, :], ye[...], preferred_element_type=F32)
    os_ = jnp.dot(so_ref[rows, :], yo[...], preferred_element_type=F32)
    return ec, oc, es, os_


def _filter_spec_r2_kernel(ce_ref, co_ref, se_ref, so_ref, ae_ref, ao_ref, de_ref, do_ref, edge_ref, o_ref,
                           aeb, aob, deb, dob):
    aeb[...] = ae_ref[...].astype(BF16)
    aob[...] = ao_ref[...].astype(BF16)
    deb[...] = de_ref[...].astype(BF16)
    dob[...] = do_ref[...].astype(BF16)
    for rows in _row_chunks(ce_ref.shape[0]):
        ec, oc, es, os_ = _half_transforms((ce_ref, co_ref, se_ref, so_ref), aeb, aob, deb, dob, rows)
        im_a = es + os_
        im_b = os_ - es
        if rows.start == 0:
            row0 = lax.broadcasted_iota(jnp.int32, ec.shape, 0) == 0
            im_a = jnp.where(row0, edge_ref[1:2, :], im_a)
            im_b = jnp.where(row0, os_, im_b)
        o_ref[0, rows, :] = ec + oc
        o_ref[1, rows, :] = im_a
        o_ref[2, rows, :] = ec - oc
        o_ref[3, rows, :] = im_b


def _filter_spectrum_r2(mats, a_even, a_odd, d_even, d_odd, edge, *, seq):
    half = seq // 2
    tc = CH_TILE // 2
    nct = HYENA_WIDTH // tc
    mspec = pl.BlockSpec((half, half), lambda o, c: (0, 0), pipeline_mode=_RESIDENT)
    tspec = pl.BlockSpec((None, half, tc), lambda o, c: (o, 0, c))
    return pl.pallas_call(
        _filter_spec_r2_kernel,
        out_shape=jax.ShapeDtypeStruct((HYENA_ORDER, 4, half, HYENA_WIDTH), F32),
        grid=(HYENA_ORDER, nct),
        in_specs=[mspec] * 4 + [tspec] * 4 + [pl.BlockSpec((None, 2, tc), lambda o, c: (o, 0, c))],
        out_specs=pl.BlockSpec((None, 4, half, tc), lambda o, c: (o, 0, 0, c)),
        scratch_shapes=[pltpu.VMEM((half, tc), BF16)] * 4,
        compiler_params=_cparams("parallel", "parallel"),
        name="hyena_filter_spectrum_r2",
    )(*mats, a_even, a_odd, d_even, d_odd, edge)


def _fwd_dft_r2_kernel(ce_ref, co_ref, se_ref, so_ref, ze_ref, zo_ref, p_ref, o_ref, zeb, zob):
    zeb[...] = ze_ref[...].astype(BF16)
    zob[...] = zo_ref[...].astype(BF16)
    for rows in _row_chunks(ce_ref.shape[0]):
        ec, oc, es, os_ = _half_transforms((ce_ref, co_ref, se_ref, so_ref), zeb, zob, zeb, zob, rows)
        re_a, re_b = ec + oc, ec - oc
        im_a, im_b = es + os_, os_ - es
        pra, pia, prb, pib = (p_ref[j, rows, :] for j in range(4))
        first = rows.start == 0
        if first:
            row0 = lax.broadcasted_iota(jnp.int32, ec.shape, 0) == 0
            im_a = jnp.where(row0, es, im_a)
            im_b = jnp.where(row0, os_, im_b)
        re_a2 = re_a * pra - im_a * pia
        im_a2 = re_a * pia + im_a * pra
        re_b2 = re_b * prb - im_b * pib
        im_b2 = re_b * pib + im_b * prb
        if first:
            re_a2 = jnp.where(row0, re_a * pra, re_a2)
            re_b2 = jnp.where(row0, re_b * prb, re_b2)
            im_a2 = jnp.where(row0, im_a * pia - im_b * pib, im_a2)
            im_b2 = jnp.where(row0, im_a * pib + im_b * pia, im_b2)
        us = im_a2 - im_b2
        vs = im_a2 + im_b2
        if first:
            us = jnp.where(row0, im_a2, us)
            vs = jnp.where(row0, im_b2, vs)
        o_ref[0, rows, :] = (re_a2 + re_b2).astype(o_ref.dtype)
        o_ref[1, rows, :] = us.astype(o_ref.dtype)
        o_ref[2, rows, :] = (re_a2 - re_b2).astype(o_ref.dtype)
        o_ref[3, rows, :] = vs.astype(o_ref.dtype)


def _fwd_dft_r2(mats, ze, ze_cb, zo, zo_cb, spec, order, *, batch, seq):
    half = seq // 2
    tc = CH_TILE
    mspec = pl.BlockSpec((half, half), lambda c, b: (0, 0), pipeline_mode=_RESIDENT)
    return pl.pallas_call(
        _fwd_dft_r2_kernel,
        out_shape=jax.ShapeDtypeStruct((batch, 4, half, HYENA_WIDTH), BF16),
        grid=(HYENA_WIDTH // tc, batch),
        in_specs=[mspec] * 4 + [
            pl.BlockSpec((half, tc), lambda c, b: (b, ze_cb + c)),
            pl.BlockSpec((half, tc), lambda c, b: (b, zo_cb + c)),
            pl.BlockSpec((None, 4, half, tc), lambda c, b: (order, 0, 0, c))],
        out_specs=pl.BlockSpec((None, 4, half, tc), lambda c, b: (b, 0, 0, c)),
        scratch_shapes=[pltpu.VMEM((half, tc), BF16)] * 2,
        compiler_params=_cparams("parallel", "parallel"),
        name="hyena_fwd_dft_r2",
    )(*mats, ze, zo, spec)


def _inv_dft_r2_kernel(we_ref, wo_ref, s_ref, ge_ref, go_ref, ze_ref, zo_ref, bias_ref, *outs, interleave):
    s = s_ref[...]
    half, tc = s.shape[1], s.shape[2]
    su = s[0:2].reshape(2 * half, tc)
    sv = s[2:4].reshape(2 * half, tc)
    bias = bias_ref[...]
    for rows in _row_chunks(half):
        ye = jnp.dot(we_ref[rows, :], su, preferred_element_type=F32)
        yo = jnp.dot(wo_ref[rows, :], sv, preferred_element_type=F32)
        even = ge_ref[rows, :] * (ye + bias * ze_ref[rows, :])
        odd = go_ref[rows, :] * (yo + bias * zo_ref[rows, :])
        if not interleave:
            oe_ref, oo_ref = outs
            oe_ref[rows, :] = even.astype(oe_ref.dtype)
            oo_ref[rows, :] = odd.astype(oo_ref.dtype)
            continue
        o_ref, stage = outs
        n = rows.stop - rows.start
        for j in range(tc // LANES):
            lanes = slice(j * LANES, (j + 1) * LANES)
            stage[j, pl.ds(0, n, stride=2), :] = even[:, lanes]
            stage[j, pl.ds(1, n, stride=2), :] = odd[:, lanes]
            o_ref[2 * rows.start:2 * rows.stop, lanes] = stage[j].astype(o_ref.dtype)


def _inv_dft_r2(inv_mats, s, gate_even, gate_odd, gate_cb, *, z_even, z_odd, hy_bias, layer, order, out_dtype,
                batch, seq, interleave=False):
    half = seq // 2
    tc = CH_TILE
    wspec = pl.BlockSpec((half, seq), lambda c, b: (0, 0), pipeline_mode=_RESIDENT)
    if interleave:
        out_shape = jax.ShapeDtypeStruct((batch * seq, HYENA_WIDTH), out_dtype)
        out_specs = pl.BlockSpec((seq, tc), lambda c, b: (b, c))
        scratch = [pltpu.VMEM((tc // LANES, 2 * min(SUB_M, half), LANES), F32)]
    else:
        out = jax.ShapeDtypeStruct((batch * half, HYENA_WIDTH), out_dtype)
        ospec = pl.BlockSpec((half, tc), lambda c, b: (b, c))
        out_shape, out_specs, scratch = (out, out), (ospec, ospec), []
    return pl.pallas_call(
        functools.partial(_inv_dft_r2_kernel, interleave=interleave),
        out_shape=out_shape,
        grid=(HYENA_WIDTH // tc, batch),
        in_specs=[wspec, wspec,
                  pl.BlockSpec((None, 4, half, tc), lambda c, b: (b, 0, 0, c)),
                  pl.BlockSpec((half, tc), lambda c, b: (b, gate_cb + c)),
                  pl.BlockSpec((half, tc), lambda c, b: (b, gate_cb + c)),
                  pl.BlockSpec((half, tc), lambda c, b: (b, c)),
                  pl.BlockSpec((half, tc), lambda c, b: (b, c)),
                  pl.BlockSpec((None, 1, tc), lambda c, b: (layer * HYENA_ORDER + order, 0, c))],
        out_specs=out_specs,
        scratch_shapes=scratch,
        compiler_params=_cparams("parallel", "parallel"),
        name="hyena_inv_dft_r2",
    )(*inv_mats, s, gate_even, gate_odd, z_even, z_odd, hy_bias.reshape(DEPTH * HYENA_ORDER, 1, HYENA_WIDTH))


def _short_conv_split_kernel(u_ref, w_ref, b_ref, oe_ref, oo_ref, stage, *, seq):
    width = u_ref.shape[1]
    w0 = w_ref[0:1, :]
    w1 = w_ref[1:2, :]
    w2 = w_ref[2:3, :]
    b = b_ref[...]
    rc = _ROW_CHUNK
    hr = rc // 2

    def emit(base, window):
        prev = window[_CONV_PAD - 1:_CONV_PAD - 1 + rc]
        cur = window[_CONV_PAD:_CONV_PAD + rc]
        nxt = window[_CONV_PAD + 1:_CONV_PAD + 1 + rc]
        val = prev * w0 + cur * w1 + nxt * w2 + b
        hb = base // 2 if isinstance(base, int) else pl.multiple_of(base // 2, hr)
        for j in range(width // LANES):
            lanes = slice(j * LANES, (j + 1) * LANES)
            stage[j] = val[:, lanes]
            oe_ref[pl.ds(hb, hr), lanes] = stage[j, pl.ds(0, hr, stride=2), :]
            oo_ref[pl.ds(hb, hr), lanes] = stage[j, pl.ds(1, hr, stride=2), :]

    _halo_chunks(u_ref, seq, rc, _CONV_PAD, emit)


def _short_conv_split(u, conv_w, conv_b, layer, st):
    width = u.shape[1]
    seq = st.seq
    half = seq // 2
    tc = CH_TILE
    out = jax.ShapeDtypeStruct((st.batch * half, width), F32)
    ospec = pl.BlockSpec((half, tc), lambda b, j: (b, j))
    return pl.pallas_call(
        functools.partial(_short_conv_split_kernel, seq=seq),
        out_shape=(out, out),
        grid=(st.batch, width // tc),
        in_specs=[pl.BlockSpec((seq, tc), lambda b, j: (b, j)),
                  pl.BlockSpec((None, 3, tc), lambda b, j: (layer, 0, j)),
                  pl.BlockSpec((None, 1, tc), lambda b, j: (layer, 0, j))],
        out_specs=(ospec, ospec),
        scratch_shapes=[pltpu.VMEM((tc // LANES, _ROW_CHUNK, LANES), F32)],
        compiler_params=_cparams("parallel", "parallel"),
        name="hyena_short_conv_split",
    )(u, conv_w, conv_b.reshape(DEPTH, 1, width))


def _filter_taps_split_kernel(he_ref, ho_ref, dl_ref, wf_ref, wb_ref, ae_ref, ao_ref, de_ref, do_ref,
                              edge_ref, *, seq):
    shape = (seq // 2, dl_ref.shape[1])
    m = lax.broadcasted_iota(jnp.int32, shape, 0)

    def taps(h_ref, parity):
        t = (2 * m + parity).astype(F32) / float(seq - 1)
        dec = jnp.exp(-t * dl_ref[...])
        h = h_ref[...]
        return (jnp.dot(h, wf_ref[...], preferred_element_type=F32) * dec,
                jnp.dot(h, wb_ref[...], preferred_element_type=F32) * dec)

    fe, be = taps(he_ref, 0)
    fo, bo = taps(ho_ref, 1)
    be = jnp.where(m == 0, 0.0, be)
    total = (jnp.sum(jnp.abs(fe) + jnp.abs(be), axis=0, keepdims=True)
             + jnp.sum(jnp.abs(fo) + jnp.abs(bo), axis=0, keepdims=True))
    inv = 1.0 / total
    a_e = (fe + be) * inv
    a_o = (fo + bo) * inv
    ae_ref[...] = a_e
    ao_ref[...] = a_o
    de_ref[...] = (fe - be) * inv
    do_ref[...] = (fo - bo) * inv
    edge_ref[0:1, :] = jnp.sum(a_e, axis=0, keepdims=True) - jnp.sum(a_o, axis=0, keepdims=True)
    alt = jnp.where(m % 2 == 0, 1.0, -1.0)
    edge_ref[1:2, :] = jnp.sum(a_e * alt, axis=0, keepdims=True)


def _filter_taps_split(hidden, w3, layer, *, seq):
    tc = 256
    nct = HYENA_WIDTH // tc
    half = seq // 2
    deltas = jnp.abs(jnp.linspace(math.log(DECAY_TARGET) / SLOW_DECAY_PCT,
                                  math.log(DECAY_TARGET) / FAST_DECAY_PCT, HYENA_WIDTH, dtype=F32))
    tap_shape = jax.ShapeDtypeStruct((HYENA_ORDER, half, HYENA_WIDTH), F32)
    tap_spec = pl.BlockSpec((None, half, tc), lambda o, j: (o, 0, j))
    hspec = pl.BlockSpec((half, FILTER_HIDDEN), lambda o, j: (0, 0))
    return pl.pallas_call(
        functools.partial(_filter_taps_split_kernel, seq=seq),
        out_shape=(tap_shape,) * 4 + (jax.ShapeDtypeStruct((HYENA_ORDER, 2, HYENA_WIDTH), F32),),
        grid=(HYENA_ORDER, nct),
        in_specs=[hspec, hspec,
                  pl.BlockSpec((1, tc), lambda o, j: (0, j)),
                  pl.BlockSpec((None, FILTER_HIDDEN, tc), lambda o, j: (layer, 0, (2 * o) * nct + j)),
                  pl.BlockSpec((None, FILTER_HIDDEN, tc), lambda o, j: (layer, 0, (2 * o + 1) * nct + j))],
        out_specs=(tap_spec,) * 4 + (pl.BlockSpec((None, 2, tc), lambda o, j: (o, 0, j)),),
        compiler_params=_cparams("parallel", "parallel"),
        name="hyena_filter_taps_split",
    )(hidden[0::2], hidden[1::2], deltas.reshape(1, HYENA_WIDTH), w3, w3)


def _hyena_long(u, p, layer, st):
    batch, seq = st.batch, st.seq
    uce, uco = _short_conv_split(u, p["hy_conv_w"], p["hy_conv_b"], layer, st)
    hidden = _filter_mlp(seq, p["filt_w1"], p["filt_b1"], p["filt_w2"], p["filt_b2"], p["filt_freq"], layer)
    taps = _filter_taps_split(hidden, p["filt_w3"], layer, seq=seq)
    mats, inv_mats = _dft_matrices_r2(seq)
    spec = _filter_spectrum_r2(mats, *taps, seq=seq)
    nb = HYENA_WIDTH // CH_TILE
    hy_bias = p["hy_bias"]
    s = _fwd_dft_r2(mats, uce, 0, uco, 0, spec, 0, batch=batch, seq=seq)
    z1e, z1o = _inv_dft_r2(inv_mats, s, uce, uco, nb, z_even=uce, z_odd=uco, hy_bias=hy_bias, layer=layer,
                           order=0, out_dtype=F32, batch=batch, seq=seq)
    s = _fwd_dft_r2(mats, z1e, 0, z1o, 0, spec, 1, batch=batch, seq=seq)
    return _inv_dft_r2(inv_mats, s, uce, uco, 2 * nb, z_even=z1e, z_odd=z1o, hy_bias=hy_bias, layer=layer,
                       order=1, out_dtype=BF16, batch=batch, seq=seq, interleave=True)


def _hyena(u, p, layer, st):
    batch, seq = st.batch, st.seq
    if seq > SHORT_SEQ:
        return _hyena_long(u, p, layer, st)
    uc = _short_conv(u, p["hy_conv_w"], p["hy_conv_b"], layer, st)
    dft_top, dft_bot, dft_inv = _dft_matrices(seq)
    hidden = _filter_mlp(seq, p["filt_w1"], p["filt_b1"], p["filt_w2"], p["filt_b2"], p["filt_freq"], layer)
    taps_a, taps_d, nyq = _filter_taps(hidden, p["filt_w3"], layer, seq=seq)
    spec = _filter_spectrum(dft_top, dft_bot, taps_a, taps_d, nyq, seq=seq)
    s = _fwd_dft(dft_top, dft_bot, uc, 0, spec, 0, batch=batch, seq=seq)
    z1 = _inv_dft(dft_inv, s, uc, HYENA_WIDTH, uc, 0, p["hy_bias"], layer, 0, F32, batch=batch, seq=seq)
    s = _fwd_dft(dft_top, dft_bot, z1, 0, spec, 1, batch=batch, seq=seq)
    return _inv_dft(dft_inv, s, uc, 2 * HYENA_WIDTH, z1, 0, p["hy_bias"], layer, 1, BF16, batch=batch, seq=seq)


def _block(x, p, modflat, layer, st, ctx_kv, shared):
    h = _normmod(x, p["norm1_g"], modflat, st, layer, 1, 0)
    lin = functools.partial(_linear, h, p["w_in"], layer, tn=512, tn_reuse=1024, shared=shared)
    u_pool = lin(COL_POOL, POOL_WIDTH, kind="plain", out_dtype=F32, key="w_in:pool")
    q = lin(COL_Q, ATTN_WIDTH, kind="rmshead", out_dtype=BF16, head_gain=p["q_norm_g"], key="w_in:q")
    kv_dtype = F32 if ctx_kv is None else BF16
    k = lin(COL_K, ATTN_WIDTH, kind="rmshead", out_dtype=kv_dtype, head_gain=p["k_norm_g"], key="w_in:k")
    v = lin(COL_V, ATTN_WIDTH, kind="plain", out_dtype=kv_dtype, key="w_in:v")
    u_hy = lin(COL_HY, (HYENA_ORDER + 1) * HYENA_WIDTH, kind="plain", out_dtype=F32, key="w_in:hyena")
    gates = lin(COL_GATES, 3 * D_MODEL, kind="sigmoid", out_dtype=BF16, key="w_in:gates")

    pool = _pool(u_pool, p["pool_w"], p["pool_scale"], layer, st)
    if ctx_kv is None:
        attn = _ctx_attention(q, k, v, st)
    else:
        attn = _nbr_attention(q, k, v, ctx_kv[0], ctx_kv[1], ctx_kv[2], layer, st)
    hy = _hyena(u_hy, p, layer, st)

    mixed = _mix(pool, attn, hy, p["w_branch_pool"], p["w_branch_attn"], p["w_branch_hyena"], gates, layer,
                 shared)
    x = _linear(mixed, p["w_out"], layer, 0, D_MODEL, tn=512, tn_reuse=1024, kind="residual", out_dtype=F32,
                shared=shared, key="w_out", residual=x, modflat=modflat, st=st, which_gate=2)
    h = _normmod(x, p["norm2_g"], modflat, st, layer, 4, 3)
    a = _ffn1(h, p["w_gate"], p["w_up"], layer, shared)
    half = D_FF // 2
    for k_off in (0, half):
        x = _linear(a, p["w_down"], layer, 0, D_MODEL, tn=512, tm=512, tm_reuse=1024, kind="residual",
                    out_dtype=F32, shared=shared, key="w_down:%d" % k_off, k_off=k_off, k_len=half,
                    residual=x, modflat=modflat, st=st, which_gate=5)
    return x, k, v


_WEIGHT_NAMES = ("ada_w", "ada_b", "norm1_g", "norm2_g", "w_in", "pool_w", "pool_scale",
                 "q_norm_g", "k_norm_g", "rpb", "hy_conv_w", "hy_conv_b", "filt_w1", "filt_b1",
                 "filt_w2", "filt_b2", "filt_w3", "filt_freq", "hy_bias", "w_branch_pool",
                 "w_branch_attn", "w_branch_hyena", "w_out", "w_gate", "w_up", "w_down")


def kernel(x_prompt, x_sample, cache_k, cache_v, c, c_ctx, ada_w, ada_b, norm1_g, norm2_g, w_in, pool_w, pool_scale, q_norm_g, k_norm_g, rpb, hy_conv_w, hy_conv_b, filt_w1, filt_b1, filt_w2, filt_b2, filt_w3, filt_freq, hy_bias, w_branch_pool, w_branch_attn, w_branch_hyena, w_out, w_gate, w_up, w_down):
    p = dict(zip(_WEIGHT_NAMES, (ada_w, ada_b, norm1_g, norm2_g, w_in, pool_w, pool_scale, q_norm_g,
                                 k_norm_g, rpb, hy_conv_w, hy_conv_b, filt_w1, filt_b1, filt_w2, filt_b2,
                                 filt_w3, filt_freq, hy_bias, w_branch_pool, w_branch_attn,
                                 w_branch_hyena, w_out, w_gate, w_up, w_down)))
    nb, ns, d = x_prompt.shape
    lb, ls, _ = x_sample.shape
    assert d == D_MODEL and lb + 1 <= MOD_ROWS
    ctx = _Stream(nb, ns, 0, False)
    lat = _Stream(lb, ls, 1, True)

    cmat = jnp.concatenate([c_ctx[None, :], c, jnp.zeros((MOD_ROWS - 1 - lb, d), F32)], axis=0)
    mod = _modulation(cmat, ada_w, ada_b)
    modflat = mod.reshape(DEPTH * MOD_ROWS * 6, 1, d)

    y_ctx = x_prompt.reshape(nb * ns, d)
    y_lat = x_sample.reshape(lb * ls, d)
    new_k, new_v = [], []
    for layer in range(DEPTH):
        shared = _SharedWeights()
        bias = _bias_table(rpb, layer)
        y_lat, _, _ = _block(y_lat, p, modflat, layer, lat, (cache_k, cache_v, bias), shared)
        y_ctx, k_l, v_l = _block(y_ctx, p, modflat, layer, ctx, None, shared)
        new_k.append(k_l.reshape(nb, ns, N_HEADS, HEAD_DIM))
        new_v.append(v_l.reshape(nb, ns, N_HEADS, HEAD_DIM))

    return (y_ctx.reshape(nb, ns, d), y_lat.reshape(lb, ls, d),
            jnp.stack(new_k, axis=1), jnp.stack(new_v, axis=1))
```

```python
import functools
import math

import jax
import jax.numpy as jnp
from jax import lax
from jax.experimental import pallas as pl
from jax.experimental.pallas import tpu as pltpu

F32 = jnp.float32
BF16 = jnp.bfloat16

D_MODEL = 4096
DEPTH = 2
GRID_W = 64
HEAD_DIM = 128
N_HEADS = D_MODEL // 256
ATTN_WIDTH = N_HEADS * HEAD_DIM
ATTN_SCALE = HEAD_DIM ** -0.5
WIN_R = 8
WIN_C = 16
NEG_INF = -1e30
POOL_WIDTH = D_MODEL // 2
POOL_SIZES = (2, 4, 8, 16)
POOL_GROUP = POOL_WIDTH // len(POOL_SIZES)
HYENA_WIDTH = D_MODEL // 2
HYENA_ORDER = 2
FILTER_EMB = 33
FILTER_HIDDEN = 64
DECAY_TARGET = 1e-2
FAST_DECAY_PCT = 0.3
SLOW_DECAY_PCT = 1.5
D_FF = ((8 * D_MODEL + 767) // 768) * 256
COL_POOL = 0
COL_Q = POOL_WIDTH
COL_K = COL_Q + ATTN_WIDTH
COL_V = COL_K + ATTN_WIDTH
COL_HY = COL_V + ATTN_WIDTH
COL_GATES = COL_HY + (HYENA_ORDER + 1) * HYENA_WIDTH
MOD_ROWS = 8
RMS_EPS = 1e-6

LANES = 128
VMEM_LIMIT = 56 * 1024 * 1024
SUB_M = 256
CH_TILE = 512
SHORT_SEQ = 512


def _cparams(*sem):
    return pltpu.CompilerParams(dimension_semantics=sem, vmem_limit_bytes=VMEM_LIMIT)


def _row_chunks(tm, sub=SUB_M):
    sub = min(sub, tm)
    return [slice(r, r + sub) for r in range(0, tm, sub)]


def _token_tile(m_tok):
    return 1024 if m_tok % 1024 == 0 else 512


def _mod_kernel(c_ref, w_ref, b_ref, o_ref):
    c = c_ref[...]
    s = (c * jax.nn.sigmoid(c)).astype(BF16)
    o_ref[...] = jnp.dot(s, w_ref[...].astype(BF16), preferred_element_type=F32) + b_ref[...]


def _modulation(cmat, ada_w, ada_b):
    tn = 512
    n6 = ada_w.shape[-1]
    return pl.pallas_call(
        _mod_kernel,
        out_shape=jax.ShapeDtypeStruct((DEPTH, MOD_ROWS, n6), F32),
        grid=(DEPTH, n6 // tn),
        in_specs=[pl.BlockSpec((MOD_ROWS, D_MODEL), lambda l, j: (0, 0)),
                  pl.BlockSpec((None, D_MODEL, tn), lambda l, j: (l, 0, j)),
                  pl.BlockSpec((None, 1, tn), lambda l, j: (l, 0, j))],
        out_specs=pl.BlockSpec((None, MOD_ROWS, tn), lambda l, j: (l, 0, j)),
        compiler_params=_cparams("parallel", "parallel"),
        name="modulation",
    )(cmat, ada_w, ada_b.reshape(DEPTH, 1, n6))


class _Stream:
    def __init__(self, batch, seq, mod_base, per_batch_mod):
        self.batch = batch
        self.seq = seq
        self.mod_base = mod_base
        self.per_batch_mod = per_batch_mod

    def mod_index(self, layer, i, tm, which):
        row = self.mod_base + ((i * tm) // self.seq if self.per_batch_mod else 0)
        return (layer * MOD_ROWS + row) * 6 + which


_STAT_ROWS = 32
_NORM_ROWS = 16


def _normmod_kernel(x_ref, g_ref, sc_ref, sh_ref, o_ref, rstd_ref, gain_ref, shift_ref):
    tm, d = x_ref.shape
    gain_ref[...] = jnp.broadcast_to(g_ref[...] * (1.0 + sc_ref[0]), (_NORM_ROWS, d))
    shift_ref[...] = jnp.broadcast_to(sh_ref[0], (_NORM_ROWS, d))

    def stats(ci, carry):
        base = pl.multiple_of(ci * _STAT_ROWS, _STAT_ROWS)
        x = x_ref[pl.ds(base, _STAT_ROWS), :]
        sq = x * x
        while sq.shape[1] > LANES:
            half = sq.shape[1] // 2
            sq = sq[:, :half] + sq[:, half:]
        rstd_ref[pl.ds(base, _STAT_ROWS), :] = sq
        return carry

    lax.fori_loop(0, tm // _STAT_ROWS, stats, 0)
    rstd = lax.rsqrt(jnp.sum(rstd_ref[...], axis=-1, keepdims=True) * (1.0 / d) + RMS_EPS)
    rstd_ref[...] = jnp.broadcast_to(rstd, (tm, LANES))

    def scale(ci, carry):
        base = pl.multiple_of(ci * _NORM_ROWS, _NORM_ROWS)
        x = x_ref[pl.ds(base, _NORM_ROWS), :]
        rstd = jnp.concatenate([rstd_ref[pl.ds(base, _NORM_ROWS), :]] * (d // LANES), axis=1)
        o_ref[pl.ds(base, _NORM_ROWS), :] = (x * rstd * gain_ref[...] + shift_ref[...]).astype(o_ref.dtype)
        return carry

    lax.fori_loop(0, tm // _NORM_ROWS, scale, 0)


def _normmod(x, g, modflat, st, layer, which_scale, which_shift):
    m_tok, d = x.shape
    tm = 512
    return pl.pallas_call(
        _normmod_kernel,
        out_shape=jax.ShapeDtypeStruct((m_tok, d), BF16),
        grid=(m_tok // tm,),
        in_specs=[pl.BlockSpec((tm, d), lambda i: (i, 0)),
                  pl.BlockSpec((None, 1, d), lambda i: (layer, 0, 0)),
                  pl.BlockSpec((1, 1, d), lambda i: (st.mod_index(layer, i, tm, which_scale), 0, 0)),
                  pl.BlockSpec((1, 1, d), lambda i: (st.mod_index(layer, i, tm, which_shift), 0, 0))],
        out_specs=pl.BlockSpec((tm, d), lambda i: (i, 0)),
        scratch_shapes=[pltpu.VMEM((tm, LANES), F32), pltpu.VMEM((_NORM_ROWS, d), F32),
                        pltpu.VMEM((_NORM_ROWS, d), F32)],
        compiler_params=_cparams("parallel"),
        name="normmod",
    )(x, g.reshape(DEPTH, 1, d), modflat, modflat)


def _bf16_weight(w_ref, emit, idx, every):
    if not emit:
        return w_ref
    piece_ref, wbf_ref = emit[idx], emit[len(emit) // 2 + idx]
    m = pl.program_id(1)

    @pl.when(m == 0)
    def _():
        wbf_ref[...] = w_ref[...].astype(BF16)

    rows = piece_ref.shape[0]

    @pl.when(m % every == 0)
    def _():
        off = pl.multiple_of((m // every) * rows, rows)
        piece_ref[...] = wbf_ref[pl.ds(off, rows), :]

    return wbf_ref


def _lin_plain_kernel(x_ref, w_ref, o_ref, *emit, every=None):
    w = _bf16_weight(w_ref, emit, 0, every)
    for rows in _row_chunks(x_ref.shape[0]):
        acc = jnp.dot(x_ref[rows, :], w[...], preferred_element_type=F32)
        o_ref[rows, :] = acc.astype(o_ref.dtype)


def _lin_sigmoid_kernel(x_ref, w_ref, o_ref, *emit, every=None):
    w = _bf16_weight(w_ref, emit, 0, every)
    for rows in _row_chunks(x_ref.shape[0]):
        acc = jnp.dot(x_ref[rows, :], w[...], preferred_element_type=F32)
        o_ref[rows, :] = jax.nn.sigmoid(acc).astype(o_ref.dtype)


def _lin_rmshead_kernel(x_ref, w_ref, g_ref, o_ref, *emit, every=None):
    w = _bf16_weight(w_ref, emit, 0, every)
    g = g_ref[...]
    for rows in _row_chunks(x_ref.shape[0]):
        acc = jnp.dot(x_ref[rows, :], w[...], preferred_element_type=F32)
        for h in range(acc.shape[1] // HEAD_DIM):
            a = acc[:, h * HEAD_DIM:(h + 1) * HEAD_DIM]
            y = a * lax.rsqrt(jnp.mean(a * a, axis=-1, keepdims=True) + RMS_EPS)
            o_ref[rows, h * HEAD_DIM:(h + 1) * HEAD_DIM] = (y * g).astype(o_ref.dtype)


def _lin_residual_kernel(x_ref, w_ref, res_ref, gate_ref, o_ref, *emit, every=None):
    w = _bf16_weight(w_ref, emit, 0, every)
    for rows in _row_chunks(x_ref.shape[0]):
        acc = jnp.dot(x_ref[rows, :], w[...], preferred_element_type=F32)
        o_ref[rows, :] = res_ref[rows, :] + gate_ref[0] * acc


class _SharedWeights:
    def __init__(self):
        self.slabs = {}

    def get(self, key):
        return self.slabs.get(key)

    def put(self, key, value):
        self.slabs[key] = value


_BF16_ROWS = 16


def _weight_plumbing(shared, keys, w_arrays, k_len, ncols, tn, wspec_f32, msteps):
    have = [shared.get(key) for key in keys]
    if all(h is not None for h in have):
        return have, [pl.BlockSpec((k_len, tn), lambda n, m: (0, n))] * len(keys), [], [], [], None, True
    pieces = msteps
    while k_len % pieces or (k_len // pieces) % _BF16_ROWS:
        pieces //= 2
    every = msteps // pieces
    slab = jax.ShapeDtypeStruct((k_len, ncols), BF16)
    piece_spec = pl.BlockSpec((k_len // pieces, tn), lambda n, m: (m // every, n))
    scratch = [pltpu.VMEM((k_len, tn), BF16)] * len(keys)
    return (list(w_arrays), [wspec_f32] * len(keys), [slab] * len(keys), [piece_spec] * len(keys), scratch,
            every, False)


def _linear(x, w, layer, col_off, ncols, *, tn, kind, out_dtype, shared, key, tn_reuse=None, tm=None,
            tm_reuse=None, k_off=0, k_len=None, head_gain=None, residual=None, modflat=None, st=None,
            which_gate=None):
    m_tok = x.shape[0]
    k_len = x.shape[1] if k_len is None else k_len
    if shared.get(key) is not None:
        tn = tn if tn_reuse is None else tn_reuse
        tm = tm if tm_reuse is None else tm_reuse
    tm = _token_tile(m_tok) if tm is None else tm
    cb = col_off // tn
    kb = k_off // k_len
    wspec = pl.BlockSpec((None, k_len, tn), lambda n, m: (layer, kb, cb + n))
    wargs, wspecs, extra_shapes, extra_specs, scratch, every, reuse = _weight_plumbing(
        shared, [key], [w], k_len, ncols, tn, wspec, m_tok // tm)
    in_specs = [pl.BlockSpec((tm, k_len), lambda n, m: (m, kb))] + wspecs
    args = [x] + wargs
    if kind == "plain":
        body = _lin_plain_kernel
    elif kind == "sigmoid":
        body = _lin_sigmoid_kernel
    elif kind == "rmshead":
        body = _lin_rmshead_kernel
        in_specs.append(pl.BlockSpec((None, 1, HEAD_DIM), lambda n, m: (layer, 0, 0)))
        args.append(head_gain.reshape(DEPTH, 1, HEAD_DIM))
    else:
        body = _lin_residual_kernel
        in_specs.append(pl.BlockSpec((tm, tn), lambda n, m: (m, n)))
        in_specs.append(pl.BlockSpec(
            (1, 1, tn), lambda n, m: (st.mod_index(layer, m, tm, which_gate), 0, n)))
        args += [residual, modflat]
    outs = pl.pallas_call(
        functools.partial(body, every=every),
        out_shape=[jax.ShapeDtypeStruct((m_tok, ncols), out_dtype)] + extra_shapes,
        grid=(ncols // tn, m_tok // tm),
        in_specs=in_specs,
        out_specs=[pl.BlockSpec((tm, tn), lambda n, m: (m, n))] + extra_specs,
        scratch_shapes=scratch,
        compiler_params=_cparams("parallel", "arbitrary"),
        name="linear_" + kind,
    )(*args)
    if not reuse:
        shared.put(key, outs[1])
    return outs[0]


def _mix_kernel(xp_ref, xa_ref, xh_ref, wp_ref, wa_ref, wh_ref, gp_ref, ga_ref, gh_ref, o_ref,
                *emit, every=None):
    wp = _bf16_weight(wp_ref, emit, 0, every)
    wa = _bf16_weight(wa_ref, emit, 1, every)
    wh = _bf16_weight(wh_ref, emit, 2, every)
    for rows in _row_chunks(xp_ref.shape[0], 2 * SUB_M):
        acc = gp_ref[rows, :].astype(F32) * jnp.dot(xp_ref[rows, :], wp[...], preferred_element_type=F32)
        acc = acc + ga_ref[rows, :].astype(F32) * jnp.dot(xa_ref[rows, :], wa[...], preferred_element_type=F32)
        acc = acc + gh_ref[rows, :].astype(F32) * jnp.dot(xh_ref[rows, :], wh[...], preferred_element_type=F32)
        o_ref[rows, :] = acc.astype(o_ref.dtype)


def _mix(xp, xa, xh, wp, wa, wh, gates, layer, shared):
    m_tok, k = xp.shape
    n = wp.shape[-1]
    tn = 512
    keys = ["w_branch_pool", "w_branch_attn", "w_branch_hyena"]
    wspec = pl.BlockSpec((None, k, tn), lambda j, i: (layer, 0, j))
    reusing = all(shared.get(key) is not None for key in keys)
    tm = _token_tile(m_tok) if reusing else 512
    wargs, wspecs, extra_shapes, extra_specs, scratch, every, reuse = _weight_plumbing(
        shared, keys, [wp, wa, wh], k, n, tn, wspec, m_tok // tm)
    nb = n // tn
    xspec = pl.BlockSpec((tm, k), lambda j, i: (i, 0))
    outs = pl.pallas_call(
        functools.partial(_mix_kernel, every=every),
        out_shape=[jax.ShapeDtypeStruct((m_tok, n), BF16)] + extra_shapes,
        grid=(nb, m_tok // tm),
        in_specs=[xspec, xspec, xspec] + wspecs + [
            pl.BlockSpec((tm, tn), lambda j, i: (i, j)),
            pl.BlockSpec((tm, tn), lambda j, i: (i, nb + j)),
            pl.BlockSpec((tm, tn), lambda j, i: (i, 2 * nb + j))],
        out_specs=[pl.BlockSpec((tm, tn), lambda j, i: (i, j))] + extra_specs,
        scratch_shapes=scratch,
        compiler_params=_cparams("parallel", "arbitrary"),
        name="mix",
    )(xp, xa, xh, *wargs, gates, gates, gates)
    if not reuse:
        for key, slab in zip(keys, outs[1:]):
            shared.put(key, slab)
    return outs[0]


def _ffn1_kernel(x_ref, wg_ref, wu_ref, o_ref, *emit, every=None):
    wg = _bf16_weight(wg_ref, emit, 0, every)
    wu = _bf16_weight(wu_ref, emit, 1, every)
    for rows in _row_chunks(x_ref.shape[0], 2 * SUB_M):
        x = x_ref[rows, :]
        g = jnp.dot(x, wg[...], preferred_element_type=F32)
        u = jnp.dot(x, wu[...], preferred_element_type=F32)
        o_ref[rows, :] = (g * jax.nn.sigmoid(g) * u).astype(o_ref.dtype)


def _ffn1(x, wg, wu, layer, shared):
    m_tok, k = x.shape
    n = wg.shape[-1]
    keys = ["w_gate", "w_up"]
    reusing = all(shared.get(key) is not None for key in keys)
    tm, tn = (2048 if reusing and m_tok % 2048 == 0 else _token_tile(m_tok)), 256
    wspec = pl.BlockSpec((None, k, tn), lambda j, i: (layer, 0, j))
    wargs, wspecs, extra_shapes, extra_specs, scratch, every, reuse = _weight_plumbing(
        shared, keys, [wg, wu], k, n, tn, wspec, m_tok // tm)
    outs = pl.pallas_call(
        functools.partial(_ffn1_kernel, every=every),
        out_shape=[jax.ShapeDtypeStruct((m_tok, n), BF16)] + extra_shapes,
        grid=(n // tn, m_tok // tm),
        in_specs=[pl.BlockSpec((tm, k), lambda j, i: (i, 0))] + wspecs,
        out_specs=[pl.BlockSpec((tm, tn), lambda j, i: (i, j))] + extra_specs,
        scratch_shapes=scratch,
        compiler_params=_cparams("parallel", "arbitrary"),
        name="ffn_gate_up",
    )(x, *wargs)
    if not reuse:
        for key, slab in zip(keys, outs[1:]):
            shared.put(key, slab)
    return outs[0]


_POOL_PAD = 16
_ROW_CHUNK = 64


def _halo_chunks(u_ref, seq, rc, pad, emit, lanes=slice(None)):
    zeros = jnp.zeros((pad, u_ref[0:1, lanes].shape[1]), F32)
    emit(0, jnp.concatenate([zeros, u_ref[0:rc + pad, lanes]], axis=0))

    def chunk(ci, carry):
        base = pl.multiple_of(ci * rc, rc)
        emit(base, u_ref[pl.ds(pl.multiple_of(base - pad, pad), rc + 2 * pad), lanes])
        return carry

    lax.fori_loop(1, seq // rc - 1, chunk, 0)
    emit(seq - rc, jnp.concatenate([u_ref[seq - rc - pad:seq, lanes], zeros], axis=0))


def _pool_deviation(u_ref, d_ref, lanes, win, seq):
    rc = _ROW_CHUNK
    half = win // 2

    def emit(base, window):
        lo = _POOL_PAD - half
        acc = window[lo:lo + rc]
        for j in range(1, win):
            acc = acc + window[lo + j:lo + j + rc]
        t = base + lax.broadcasted_iota(jnp.int32, acc.shape, 0)
        cnt = (jnp.minimum(t + (win - half), seq) - jnp.maximum(t - half, 0)).astype(F32)
        centre = window[_POOL_PAD:_POOL_PAD + rc]
        d_ref[pl.ds(base, rc), lanes] = (acc / cnt - centre).astype(BF16)

    _halo_chunks(u_ref, seq, rc, _POOL_PAD, emit, lanes)


def _pool_kernel(u_ref, w_ref, sc_ref, o_ref, d_ref, *, seq, all_groups):
    if all_groups:
        for gi, win in enumerate(POOL_SIZES):
            lanes = slice(gi * POOL_GROUP, (gi + 1) * POOL_GROUP)
            _pool_deviation(u_ref, d_ref, lanes, win, seq)
            y = jnp.dot(d_ref[:, lanes], w_ref[gi].astype(BF16), preferred_element_type=F32) * sc_ref[:, lanes]
            o_ref[:, lanes] = y.astype(o_ref.dtype)
        return

    g = pl.program_id(1)
    for gi, win in enumerate(POOL_SIZES):
        @pl.when(g == gi)
        def _(win=win):
            _pool_deviation(u_ref, d_ref, slice(None), win, seq)

    wbf = w_ref[...].astype(BF16)
    scale = sc_ref[...]
    for rows in _row_chunks(seq):
        y = jnp.dot(d_ref[rows, :], wbf, preferred_element_type=F32) * scale
        o_ref[rows, :] = y.astype(o_ref.dtype)


def _pool(u, pool_w, pool_scale, layer, st):
    seq = st.seq
    all_groups = seq <= SHORT_SEQ
    ngroups = len(POOL_SIZES)
    width = POOL_WIDTH if all_groups else POOL_GROUP
    wblock = (None, ngroups, POOL_GROUP, POOL_GROUP) if all_groups else (None, None, POOL_GROUP, POOL_GROUP)
    return pl.pallas_call(
        functools.partial(_pool_kernel, seq=seq, all_groups=all_groups),
        out_shape=jax.ShapeDtypeStruct((st.batch * seq, POOL_WIDTH), BF16),
        grid=(st.batch, 1 if all_groups else ngroups),
        in_specs=[pl.BlockSpec((seq, width), lambda b, g: (b, g)),
                  pl.BlockSpec(wblock, lambda b, g: (layer, g, 0, 0)),
                  pl.BlockSpec((None, 1, width), lambda b, g: (layer, 0, g))],
        out_specs=pl.BlockSpec((seq, width), lambda b, g: (b, g)),
        scratch_shapes=[pltpu.VMEM((seq, width), BF16)],
        compiler_params=_cparams("parallel", "parallel"),
        name="pool_mixer",
    )(u, pool_w, pool_scale.reshape(DEPTH, 1, POOL_WIDTH))


def _ctx_attn_kernel(q_ref, k_ref, v_ref, o_ref):
    for h in range(N_HEADS):
        sl = slice(h * HEAD_DIM, (h + 1) * HEAD_DIM)
        q = q_ref[:, sl]
        k = k_ref[:, sl].astype(BF16)
        v = v_ref[:, sl].astype(BF16)
        s = lax.dot_general(q, k, (((1,), (1,)), ((), ())), preferred_element_type=F32) * ATTN_SCALE
        m = jnp.max(s, axis=-1, keepdims=True)
        p = jnp.exp(s - m)
        denom = jnp.sum(p, axis=-1, keepdims=True)
        o = jnp.dot(p.astype(BF16), v, preferred_element_type=F32) / denom
        o_ref[:, sl] = o.astype(o_ref.dtype)


def _ctx_attention(q, k, v, st):
    spec = pl.BlockSpec((st.seq, ATTN_WIDTH), lambda b: (b, 0))
    return pl.pallas_call(
        _ctx_attn_kernel,
        out_shape=jax.ShapeDtypeStruct((st.batch * st.seq, ATTN_WIDTH), BF16),
        grid=(st.batch,),
        in_specs=[spec, spec, spec],
        out_specs=spec,
        compiler_params=_cparams("parallel"),
        name="context_attention",
    )(q, k, v)


_BIAS_TILES = 2 * WIN_R
_Q_GROUP = 8
_BAND = 2 * WIN_R


def _bias_table_kernel(rpb_ref, o_ref, *, layer):
    h = pl.program_id(0)
    shape = (GRID_W, 2 * GRID_W)
    qc = lax.broadcasted_iota(jnp.int32, shape, 0)
    lane = lax.broadcasted_iota(jnp.int32, shape, 1)
    kc = lane % GRID_W
    upper = lane >= GRID_W
    c0 = jnp.clip(qc - WIN_C // 2, 0, GRID_W - WIN_C)
    in_window = (kc >= c0) & (kc < c0 + WIN_C)
    rel = kc - qc + (WIN_C - 1)
    n_dc = 2 * WIN_C - 1
    n_dr = 2 * WIN_R - 1
    row_base = (layer * N_HEADS + h) * n_dr
    for e in range(_BIAS_TILES):
        d_lo = max(e - 1, 0)
        d_hi = min(e, n_dr - 1)
        tile = jnp.zeros(shape, F32)
        for dc in range(n_dc):
            lo = rpb_ref[(row_base + d_lo) * n_dc + dc]
            hi = rpb_ref[(row_base + d_hi) * n_dc + dc]
            tile = jnp.where(rel == dc, jnp.where(upper, hi, lo), tile)
        valid = in_window
        if e == 0:
            valid = valid & upper
        if e == _BIAS_TILES - 1:
            valid = valid & jnp.logical_not(upper)
        o_ref[e] = jnp.where(valid, tile, NEG_INF)


def _bias_table(rpb, layer):
    return pl.pallas_call(
        functools.partial(_bias_table_kernel, layer=layer),
        out_shape=jax.ShapeDtypeStruct((N_HEADS, _BIAS_TILES, GRID_W, 2 * GRID_W), F32),
        grid=(N_HEADS,),
        in_specs=[pl.BlockSpec(memory_space=pltpu.SMEM)],
        out_specs=pl.BlockSpec((None, _BIAS_TILES, GRID_W, 2 * GRID_W), lambda h: (h, 0, 0, 0)),
        compiler_params=_cparams("parallel"),
        name="rel_pos_bias_table",
    )(rpb.reshape(-1))


def _nbr_attn_kernel(q_ref, k_ref, v_ref, kc_ref, vc_ref, bias_ref, o_ref,
                     s_ref, sc_ref, p_ref, pc_ref, *kv_scratch, rows):
    if kv_scratch:
        kbf, vbf = kv_scratch
        kbf[...] = k_ref[...].astype(BF16)
        vbf[...] = v_ref[...].astype(BF16)
    else:
        kbf, vbf = k_ref, v_ref
    kc = kc_ref[...].astype(BF16)
    vc = vc_ref[...].astype(BF16)
    dims = (((1,), (1,)), ((), ()))
    pair = 2 * GRID_W
    lower = lax.broadcasted_iota(jnp.int32, (GRID_W, pair), 1) < GRID_W
    zero_tile = jnp.zeros((GRID_W, pair), BF16)

    for g in range(rows // _Q_GROUP):
        slot = g % 2
        ks = min(max(g * _Q_GROUP - WIN_R // 2, 0), rows - _BAND)
        qsl = slice(g * _Q_GROUP * GRID_W, (g + 1) * _Q_GROUP * GRID_W)
        ksl = slice(ks * GRID_W, (ks + _BAND) * GRID_W)
        qg = q_ref[qsl, :]
        s_ref[slot] = lax.dot_general(qg, kbf[ksl, :], dims, preferred_element_type=F32) * ATTN_SCALE
        sc_ref[slot] = lax.dot_general(qg, kc, dims, preferred_element_type=F32) * ATTN_SCALE
        for i in range(_Q_GROUP):
            qr = g * _Q_GROUP + i
            r0 = min(max(qr - WIN_R // 2, 0), rows - WIN_R)
            first = r0 - ks
            j0, j1 = first // 2, (first + WIN_R + 1) // 2
            rsl = slice(i * GRID_W, (i + 1) * GRID_W)
            tiles = []
            for j in range(j0, j1):
                kr0 = ks + 2 * j
                t = s_ref[slot, rsl, j * pair:(j + 1) * pair] + bias_ref[kr0 - qr + WIN_R]
                if kr0 < r0:
                    t = jnp.where(lower, NEG_INF, t)
                if kr0 + 1 >= r0 + WIN_R:
                    t = jnp.where(lower, t, NEG_INF)
                tiles.append(t)
            sc = sc_ref[slot, rsl, :]
            mt = functools.reduce(jnp.maximum, tiles)
            m = jnp.maximum(jnp.max(mt, axis=-1, keepdims=True), jnp.max(sc, axis=-1, keepdims=True))
            ps = [jnp.exp(t - m) for t in tiles]
            pc = jnp.exp(sc - m)
            denom = (jnp.sum(functools.reduce(jnp.add, ps), axis=-1, keepdims=True)
                     + jnp.sum(pc, axis=-1, keepdims=True))
            inv = 1.0 / denom
            for j in range(_BAND // 2):
                val = (ps[j - j0] * inv).astype(BF16) if j0 <= j < j1 else zero_tile
                p_ref[slot, rsl, j * pair:(j + 1) * pair] = val
            pc_ref[slot, rsl, :] = (pc * inv).astype(BF16)
        o = (jnp.dot(p_ref[slot], vbf[ksl, :], preferred_element_type=F32)
             + jnp.dot(pc_ref[slot], vc, preferred_element_type=F32))
        o_ref[qsl, :] = o.astype(o_ref.dtype)


def _nbr_attention(q, k, v, cache_k, cache_v, bias, layer, st):
    seq = st.seq
    rows = seq // GRID_W
    assert rows >= _BAND and rows % _Q_GROUP == 0
    past = cache_k.shape[2]
    ck = cache_k.reshape(st.batch, DEPTH, past, ATTN_WIDTH)
    cv = cache_v.reshape(st.batch, DEPTH, past, ATTN_WIDTH)
    tok_spec = pl.BlockSpec((seq, HEAD_DIM), lambda b, h: (b, h))
    cache_spec = pl.BlockSpec((None, None, past, HEAD_DIM), lambda b, h: (b, layer, 0, h))
    nq = _Q_GROUP * GRID_W
    nk = _BAND * GRID_W
    kv_scratch = [] if k.dtype == BF16 and v.dtype == BF16 else [pltpu.VMEM((seq, HEAD_DIM), BF16)] * 2
    return pl.pallas_call(
        functools.partial(_nbr_attn_kernel, rows=rows),
        out_shape=jax.ShapeDtypeStruct((st.batch * seq, ATTN_WIDTH), BF16),
        grid=(st.batch, N_HEADS),
        in_specs=[tok_spec, tok_spec, tok_spec, cache_spec, cache_spec,
                  pl.BlockSpec((None, _BIAS_TILES, GRID_W, 2 * GRID_W), lambda b, h: (h, 0, 0, 0))],
        out_specs=tok_spec,
        scratch_shapes=[pltpu.VMEM((2, nq, nk), F32), pltpu.VMEM((2, nq, past), F32),
                        pltpu.VMEM((2, nq, nk), BF16), pltpu.VMEM((2, nq, past), BF16)] + kv_scratch,
        compiler_params=_cparams("parallel", "parallel"),
        name="neighbourhood_attention",
    )(q, k, v, ck, cv, bias)


_CONV_PAD = 8


def _short_conv_kernel(u_ref, w_ref, b_ref, o_ref, *, seq):
    width = u_ref.shape[1]
    w0 = w_ref[0:1, :]
    w1 = w_ref[1:2, :]
    w2 = w_ref[2:3, :]
    b = b_ref[...]
    rc = max(8, _ROW_CHUNK * CH_TILE // width)

    def emit(base, window):
        prev = window[_CONV_PAD - 1:_CONV_PAD - 1 + rc]
        cur = window[_CONV_PAD:_CONV_PAD + rc]
        nxt = window[_CONV_PAD + 1:_CONV_PAD + 1 + rc]
        o_ref[pl.ds(base, rc), :] = prev * w0 + cur * w1 + nxt * w2 + b

    _halo_chunks(u_ref, seq, rc, _CONV_PAD, emit)


def _short_conv(u, conv_w, conv_b, layer, st):
    width = u.shape[1]
    seq = st.seq
    tc = HYENA_WIDTH if seq <= SHORT_SEQ else CH_TILE
    return pl.pallas_call(
        functools.partial(_short_conv_kernel, seq=seq),
        out_shape=jax.ShapeDtypeStruct((st.batch * seq, width), F32),
        grid=(st.batch, width // tc),
        in_specs=[pl.BlockSpec((seq, tc), lambda b, j: (b, j)),
                  pl.BlockSpec((None, 3, tc), lambda b, j: (layer, 0, j)),
                  pl.BlockSpec((None, 1, tc), lambda b, j: (layer, 0, j))],
        out_specs=pl.BlockSpec((seq, tc), lambda b, j: (b, j)),
        compiler_params=_cparams("parallel", "parallel"),
        name="hyena_short_conv",
    )(u, conv_w, conv_b.reshape(DEPTH, 1, width))


def _dft_matrices(seq):
    f = jnp.arange(seq, dtype=jnp.int32)[:, None]
    t = jnp.arange(seq, dtype=jnp.int32)[None, :]
    ang = ((f * t) % (2 * seq)).astype(F32) * (math.pi / seq)
    top = jnp.cos(ang)
    bot = jnp.where(f == 0, (1 - 2 * (t % 2)).astype(F32), -jnp.sin(ang))
    colscale = jnp.where(jnp.arange(seq) == 0, 0.5, 1.0).astype(F32)[None, :] / seq
    inv = jnp.concatenate([top.T * colscale, bot.T * colscale], axis=1)
    return top.astype(BF16), bot.astype(BF16), inv.astype(BF16)


def _filter_mlp_kernel(z_ref, w1_ref, b1_ref, w2_ref, b2_ref, fr_ref, o_ref):
    fr = fr_ref[...]
    h = jnp.sin(fr * (jnp.dot(z_ref[...], w1_ref[...], preferred_element_type=F32) + b1_ref[...]))
    o_ref[...] = jnp.sin(fr * (jnp.dot(h, w2_ref[...], preferred_element_type=F32) + b2_ref[...]))


def _filter_mlp(seq, w1, b1, w2, b2, freq, layer):
    t = jnp.linspace(0.0, 1.0, seq, dtype=F32)[:, None]
    bands = (FILTER_EMB - 1) // 2
    ang = (2.0 * math.pi / seq) * jnp.arange(seq, dtype=F32)[:, None] * jnp.linspace(
        1e-4, bands - 1, bands, dtype=F32)[None]
    z = jnp.concatenate([t, jnp.cos(ang), -jnp.sin(ang)], axis=-1)
    z = jnp.pad(z, ((0, 0), (0, LANES - FILTER_EMB)))
    w1p = jnp.pad(w1, ((0, 0), (0, LANES - FILTER_EMB), (0, 0)))
    vec = lambda a: a.reshape(DEPTH, 1, FILTER_HIDDEN)
    vspec = pl.BlockSpec((None, 1, FILTER_HIDDEN), lambda i: (layer, 0, 0))
    return pl.pallas_call(
        _filter_mlp_kernel,
        out_shape=jax.ShapeDtypeStruct((seq, FILTER_HIDDEN), F32),
        grid=(1,),
        in_specs=[pl.BlockSpec((seq, LANES), lambda i: (0, 0)),
                  pl.BlockSpec((None, LANES, FILTER_HIDDEN), lambda i: (layer, 0, 0)),
                  vspec,
                  pl.BlockSpec((None, FILTER_HIDDEN, FILTER_HIDDEN), lambda i: (layer, 0, 0)),
                  vspec, vspec],
        out_specs=pl.BlockSpec((seq, FILTER_HIDDEN), lambda i: (0, 0)),
        compiler_params=_cparams("arbitrary"),
        name="hyena_filter_mlp",
    )(z, w1p, vec(b1), w2, vec(b2), vec(freq))


def _filter_taps_kernel(h_ref, dl_ref, wf_ref, wb_ref, a_ref, d_ref, nyq_ref, *, seq):
    h = h_ref[...]
    shape = (seq, dl_ref.shape[1])
    row = lax.broadcasted_iota(jnp.int32, shape, 0)
    t = row.astype(F32) / float(seq - 1)
    dec = jnp.exp(-t * dl_ref[...])
    fwd = jnp.dot(h, wf_ref[...], preferred_element_type=F32) * dec
    bwd = jnp.dot(h, wb_ref[...], preferred_element_type=F32) * dec
    bwd = jnp.where(row == 0, 0.0, bwd)
    inv = 1.0 / jnp.sum(jnp.abs(fwd) + jnp.abs(bwd), axis=0, keepdims=True)
    a = (fwd + bwd) * inv
    a_ref[...] = a
    d_ref[...] = (fwd - bwd) * inv
    sign = jnp.where(row % 2 == 0, 1.0, -1.0)
    nyq_ref[0:1, :] = jnp.sum(a * sign, axis=0, keepdims=True)
    quarter = jnp.where(row % 4 == 0, 1.0, jnp.where(row % 4 == 2, -1.0, 0.0))
    nyq_ref[1:2, :] = jnp.sum(a * quarter, axis=0, keepdims=True)


def _filter_taps(hidden, w3, layer, *, seq):
    tc = 256
    nct = HYENA_WIDTH // tc
    deltas = jnp.abs(jnp.linspace(math.log(DECAY_TARGET) / SLOW_DECAY_PCT,
                                  math.log(DECAY_TARGET) / FAST_DECAY_PCT, HYENA_WIDTH, dtype=F32))
    tap_shape = jax.ShapeDtypeStruct((HYENA_ORDER, seq, HYENA_WIDTH), F32)
    tap_spec = pl.BlockSpec((None, seq, tc), lambda o, j: (o, 0, j))
    return pl.pallas_call(
        functools.partial(_filter_taps_kernel, seq=seq),
        out_shape=(tap_shape, tap_shape, jax.ShapeDtypeStruct((HYENA_ORDER, 2, HYENA_WIDTH), F32)),
        grid=(HYENA_ORDER, nct),
        in_specs=[pl.BlockSpec((seq, FILTER_HIDDEN), lambda o, j: (0, 0)),
                  pl.BlockSpec((1, tc), lambda o, j: (0, j)),
                  pl.BlockSpec((None, FILTER_HIDDEN, tc), lambda o, j: (layer, 0, (2 * o) * nct + j)),
                  pl.BlockSpec((None, FILTER_HIDDEN, tc), lambda o, j: (layer, 0, (2 * o + 1) * nct + j))],
        out_specs=(tap_spec, tap_spec, pl.BlockSpec((None, 2, tc), lambda o, j: (o, 0, j))),
        compiler_params=_cparams("parallel", "parallel"),
        name="hyena_filter_taps",
    )(hidden, deltas.reshape(1, HYENA_WIDTH), w3, w3)


def _filter_spec_kernel(wt_ref, wb_ref, a_ref, d_ref, nyq_ref, o_ref, abf, dbf):
    i = pl.program_id(2)

    @pl.when(i == 0)
    def _():
        abf[...] = a_ref[...].astype(BF16)
        dbf[...] = d_ref[...].astype(BF16)

    for rows in _row_chunks(wt_ref.shape[0]):
        top = jnp.dot(wt_ref[rows, :], abf[...], preferred_element_type=F32)
        bot = jnp.dot(wb_ref[rows, :], dbf[...], preferred_element_type=F32)
        if rows.start == 0:
            row = lax.broadcasted_iota(jnp.int32, bot.shape, 0)
            bot = jnp.where((row == 0) & (i == 0), nyq_ref[0:1, :], bot)
        o_ref[0, rows, :] = top
        o_ref[1, rows, :] = bot


def _dft_tiles(seq):
    return seq, (HYENA_WIDTH if seq <= SHORT_SEQ else CH_TILE)


_RESIDENT = pl.Buffered(1)


def _filter_spectrum(dft_top, dft_bot, taps_a, taps_d, nyq, *, seq):
    tf, tc = _dft_tiles(seq)
    tc = min(tc, CH_TILE // 2)
    wspec = pl.BlockSpec((tf, seq), lambda o, c, i: (i, 0), pipeline_mode=_RESIDENT)
    tspec = pl.BlockSpec((None, seq, tc), lambda o, c, i: (o, 0, c))
    return pl.pallas_call(
        _filter_spec_kernel,
        out_shape=jax.ShapeDtypeStruct((HYENA_ORDER, 2, seq, HYENA_WIDTH), F32),
        grid=(HYENA_ORDER, HYENA_WIDTH // tc, seq // tf),
        in_specs=[wspec, wspec, tspec, tspec,
                  pl.BlockSpec((None, 2, tc), lambda o, c, i: (o, 0, c))],
        out_specs=pl.BlockSpec((None, 2, tf, tc), lambda o, c, i: (o, 0, i, c)),
        scratch_shapes=[pltpu.VMEM((seq, tc), BF16)] * 2,
        compiler_params=_cparams("parallel", "parallel", "arbitrary"),
        name="hyena_filter_spectrum",
    )(dft_top, dft_bot, taps_a, taps_d, nyq)


def _fwd_dft_kernel(wt_ref, wb_ref, z_ref, p_ref, o_ref, zbf):
    i = pl.program_id(2)

    @pl.when(i == 0)
    def _():
        zbf[...] = z_ref[...].astype(BF16)

    for rows in _row_chunks(wt_ref.shape[0]):
        top = jnp.dot(wt_ref[rows, :], zbf[...], preferred_element_type=F32)
        bot = jnp.dot(wb_ref[rows, :], zbf[...], preferred_element_type=F32)
        pt = p_ref[0, rows, :]
        pb = p_ref[1, rows, :]
        re = top * pt - bot * pb
        im = top * pb + bot * pt
        if rows.start == 0:
            row = lax.broadcasted_iota(jnp.int32, top.shape, 0)
            real_row = (row == 0) & (i == 0)
            re = jnp.where(real_row, top * pt, re)
            im = jnp.where(real_row, bot * pb, im)
        o_ref[0, rows, :] = re.astype(o_ref.dtype)
        o_ref[1, rows, :] = im.astype(o_ref.dtype)


def _fwd_dft(dft_top, dft_bot, z, z_col_off, spec, order, *, batch, seq):
    tf, tc = _dft_tiles(seq)
    zc = z_col_off // tc
    wspec = pl.BlockSpec((tf, seq), lambda c, b, i: (i, 0), pipeline_mode=_RESIDENT)
    return pl.pallas_call(
        _fwd_dft_kernel,
        out_shape=jax.ShapeDtypeStruct((batch, 2, seq, HYENA_WIDTH), BF16),
        grid=(HYENA_WIDTH // tc, batch, seq // tf),
        in_specs=[wspec, wspec,
                  pl.BlockSpec((seq, tc), lambda c, b, i: (b, zc + c)),
                  pl.BlockSpec((None, 2, tf, tc), lambda c, b, i: (order, 0, i, c), pipeline_mode=_RESIDENT)],
        out_specs=pl.BlockSpec((None, 2, tf, tc), lambda c, b, i: (b, 0, i, c)),
        scratch_shapes=[pltpu.VMEM((seq, tc), BF16)],
        compiler_params=_cparams("parallel", "parallel", "arbitrary"),
        name="hyena_fwd_dft",
    )(dft_top, dft_bot, z, spec)


def _inv_dft_kernel(wi_ref, s_ref, gate_ref, z_ref, bias_ref, o_ref):
    s = s_ref[...]
    s2 = s.reshape(s.shape[0] * s.shape[1], s.shape[2])
    for rows in _row_chunks(wi_ref.shape[0]):
        conv = jnp.dot(wi_ref[rows, :], s2, preferred_element_type=F32)
        o_ref[rows, :] = (gate_ref[rows, :] * (conv + bias_ref[...] * z_ref[rows, :])).astype(o_ref.dtype)


def _inv_dft(dft_inv, s, gate, gate_col_off, z, z_col_off, hy_bias, layer, order, out_dtype, *, batch, seq):
    tl, tc = _dft_tiles(seq)
    nrb = seq // tl
    gc = gate_col_off // tc
    zc = z_col_off // tc
    return pl.pallas_call(
        _inv_dft_kernel,
        out_shape=jax.ShapeDtypeStruct((batch * seq, HYENA_WIDTH), out_dtype),
        grid=(HYENA_WIDTH // tc, batch, nrb),
        in_specs=[pl.BlockSpec((tl, 2 * seq), lambda c, b, i: (i, 0), pipeline_mode=_RESIDENT),
                  pl.BlockSpec((None, 2, seq, tc), lambda c, b, i: (b, 0, 0, c)),
                  pl.BlockSpec((tl, tc), lambda c, b, i: (b * nrb + i, gc + c)),
                  pl.BlockSpec((tl, tc), lambda c, b, i: (b * nrb + i, zc + c)),
                  pl.BlockSpec((None, 1, tc), lambda c, b, i: (layer * HYENA_ORDER + order, 0, c))],
        out_specs=pl.BlockSpec((tl, tc), lambda c, b, i: (b * nrb + i, c)),
        compiler_params=_cparams("parallel", "parallel", "arbitrary"),
        name="hyena_inv_dft",
    )(dft_inv, s, gate, z, hy_bias.reshape(DEPTH * HYENA_ORDER, 1, HYENA_WIDTH))


def _dft_matrices_r2(seq):
    half = seq // 2
    g = jnp.arange(half, dtype=jnp.int32)[:, None]
    m = jnp.arange(half, dtype=jnp.int32)[None, :]
    ang_e = ((2 * g * m) % (2 * seq)).astype(F32) * (math.pi / seq)
    ang_o = ((g * (2 * m + 1)) % (2 * seq)).astype(F32) * (math.pi / seq)
    alt = (1 - 2 * (m % 2)).astype(F32)
    ce = jnp.cos(ang_e)
    co = jnp.cos(ang_o)
    se = jnp.where(g == 0, alt, -jnp.sin(ang_e))
    so = jnp.where(g == 0, -alt, -jnp.sin(ang_o))
    colscale = jnp.where(jnp.arange(half) == 0, 0.5, 1.0).astype(F32)[None, :] / seq
    we = jnp.concatenate([ce.T * colscale, se.T / seq], axis=1)
    wo = jnp.concatenate([co.T * colscale, so.T / seq], axis=1)
    return tuple(a.astype(BF16) for a in (ce, co, se, so)), (we.astype(BF16), wo.astype(BF16))


def _half_transforms(mats, xe, xo, ye, yo, rows):
    ce_ref, co_ref, se_ref, so_ref = mats
    ec = jnp.dot(ce_ref[rows, :], xe[...], preferred_element_type=F32)
    oc = jnp.dot(co_ref[rows, :], xo[...], preferred_element_type=F32)
    es = jnp.dot(se_ref[rows, :], ye[...], preferred_element_type=F32)
    os_ = jnp.dot(so_ref[rows, :], yo[...], preferred_element_type=F32)
    return ec, oc, es, os_


def _filter_spec_r2_kernel(ce_ref, co_ref, se_ref, so_ref, ae_ref, ao_ref, de_ref, do_ref, edge_ref, o_ref,
                           aeb, aob, deb, dob):
    aeb[...] = ae_ref[...].astype(BF16)
    aob[...] = ao_ref[...].astype(BF16)
    deb[...] = de_ref[...].astype(BF16)
    dob[...] = do_ref[...].astype(BF16)
    for rows in _row_chunks(ce_ref.shape[0]):
        ec, oc, es, os_ = _half_transforms((ce_ref, co_ref, se_ref, so_ref), aeb, aob, deb, dob, rows)
        im_a = es + os_
        im_b = os_ - es
        if rows.start == 0:
            row0 = lax.broadcasted_iota(jnp.int32, ec.shape, 0) == 0
            im_a = jnp.where(row0, edge_ref[1:2, :], im_a)
            im_b = jnp.where(row0, os_, im_b)
        o_ref[0, rows, :] = ec + oc
        o_ref[1, rows, :] = im_a
        o_ref[2, rows, :] = ec - oc
        o_ref[3, rows, :] = im_b


def _filter_spectrum_r2(mats, a_even, a_odd, d_even, d_odd, edge, *, seq):
    half = seq // 2
    tc = CH_TILE // 2
    nct = HYENA_WIDTH // tc
    mspec = pl.BlockSpec((half, half), lambda o, c: (0, 0), pipeline_mode=_RESIDENT)
    tspec = pl.BlockSpec((None, half, tc), lambda o, c: (o, 0, c))
    return pl.pallas_call(
        _filter_spec_r2_kernel,
        out_shape=jax.ShapeDtypeStruct((HYENA_ORDER, 4, half, HYENA_WIDTH), F32),
        grid=(HYENA_ORDER, nct),
        in_specs=[mspec] * 4 + [tspec] * 4 + [pl.BlockSpec((None, 2, tc), lambda o, c: (o, 0, c))],
        out_specs=pl.BlockSpec((None, 4, half, tc), lambda o, c: (o, 0, 0, c)),
        scratch_shapes=[pltpu.VMEM((half, tc), BF16)] * 4,
        compiler_params=_cparams("parallel", "parallel"),
        name="hyena_filter_spectrum_r2",
    )(*mats, a_even, a_odd, d_even, d_odd, edge)


def _fwd_dft_r2_kernel(ce_ref, co_ref, se_ref, so_ref, ze_ref, zo_ref, p_ref, o_ref, zeb, zob):
    zeb[...] = ze_ref[...].astype(BF16)
    zob[...] = zo_ref[...].astype(BF16)
    for rows in _row_chunks(ce_ref.shape[0]):
        ec, oc, es, os_ = _half_transforms((ce_ref, co_ref, se_ref, so_ref), zeb, zob, zeb, zob, rows)
        re_a, re_b = ec + oc, ec - oc
        im_a, im_b = es + os_, os_ - es
        pra, pia, prb, pib = (p_ref[j, rows, :] for j in range(4))
        first = rows.start == 0
        if first:
            row0 = lax.broadcasted_iota(jnp.int32, ec.shape, 0) == 0
            im_a = jnp.where(row0, es, im_a)
            im_b = jnp.where(row0, os_, im_b)
        re_a2 = re_a * pra - im_a * pia
        im_a2 = re_a * pia + im_a * pra
        re_b2 = re_b * prb - im_b * pib
        im_b2 = re_b * pib + im_b * prb
        if first:
            re_a2 = jnp.where(row0, re_a * pra, re_a2)
            re_b2 = jnp.where(row0, re_b * prb, re_b2)
            im_a2 = jnp.where(row0, im_a * pia - im_b * pib, im_a2)
            im_b2 = jnp.where(row0, im_a * pib + im_b * pia, im_b2)
        us = im_a2 - im_b2
        vs = im_a2 + im_b2
        if first:
            us = jnp.where(row0, im_a2, us)
            vs = jnp.where(row0, im_b2, vs)
        o_ref[0, rows, :] = (re_a2 + re_b2).astype(o_ref.dtype)
        o_ref[1, rows, :] = us.astype(o_ref.dtype)
        o_ref[2, rows, :] = (re_a2 - re_b2).astype(o_ref.dtype)
        o_ref[3, rows, :] = vs.astype(o_ref.dtype)


def _fwd_dft_r2(mats, ze, ze_cb, zo, zo_cb, spec, order, *, batch, seq):
    half = seq // 2
    tc = CH_TILE
    mspec = pl.BlockSpec((half, half), lambda c, b: (0, 0), pipeline_mode=_RESIDENT)
    return pl.pallas_call(
        _fwd_dft_r2_kernel,
        out_shape=jax.ShapeDtypeStruct((batch, 4, half, HYENA_WIDTH), BF16),
        grid=(HYENA_WIDTH // tc, batch),
        in_specs=[mspec] * 4 + [
            pl.BlockSpec((half, tc), lambda c, b: (b, ze_cb + c)),
            pl.BlockSpec((half, tc), lambda c, b: (b, zo_cb + c)),
            pl.BlockSpec((None, 4, half, tc), lambda c, b: (order, 0, 0, c))],
        out_specs=pl.BlockSpec((None, 4, half, tc), lambda c, b: (b, 0, 0, c)),
        scratch_shapes=[pltpu.VMEM((half, tc), BF16)] * 2,
        compiler_params=_cparams("parallel", "parallel"),
        name="hyena_fwd_dft_r2",
    )(*mats, ze, zo, spec)


def _inv_dft_r2_kernel(we_ref, wo_ref, s_ref, ge_ref, go_ref, ze_ref, zo_ref, bias_ref, *outs, interleave):
    s = s_ref[...]
    half, tc = s.shape[1], s.shape[2]
    su = s[0:2].reshape(2 * half, tc)
    sv = s[2:4].reshape(2 * half, tc)
    bias = bias_ref[...]
    for rows in _row_chunks(half):
        ye = jnp.dot(we_ref[rows, :], su, preferred_element_type=F32)
        yo = jnp.dot(wo_ref[rows, :], sv, preferred_element_type=F32)
        even = ge_ref[rows, :] * (ye + bias * ze_ref[rows, :])
        odd = go_ref[rows, :] * (yo + bias * zo_ref[rows, :])
        if not interleave:
            oe_ref, oo_ref = outs
            oe_ref[rows, :] = even.astype(oe_ref.dtype)
            oo_ref[rows, :] = odd.astype(oo_ref.dtype)
            continue
        o_ref, stage = outs
        n = rows.stop - rows.start
        for j in range(tc // LANES):
            lanes = slice(j * LANES, (j + 1) * LANES)
            stage[j, pl.ds(0, n, stride=2), :] = even[:, lanes]
            stage[j, pl.ds(1, n, stride=2), :] = odd[:, lanes]
            o_ref[2 * rows.start:2 * rows.stop, lanes] = stage[j].astype(o_ref.dtype)


def _inv_dft_r2(inv_mats, s, gate_even, gate_odd, gate_cb, *, z_even, z_odd, hy_bias, layer, order, out_dtype,
                batch, seq, interleave=False):
    half = seq // 2
    tc = CH_TILE
    wspec = pl.BlockSpec((half, seq), lambda c, b: (0, 0), pipeline_mode=_RESIDENT)
    if interleave:
        out_shape = jax.ShapeDtypeStruct((batch * seq, HYENA_WIDTH), out_dtype)
        out_specs = pl.BlockSpec((seq, tc), lambda c, b: (b, c))
        scratch = [pltpu.VMEM((tc // LANES, 2 * min(SUB_M, half), LANES), F32)]
    else:
        out = jax.ShapeDtypeStruct((batch * half, HYENA_WIDTH), out_dtype)
        ospec = pl.BlockSpec((half, tc), lambda c, b: (b, c))
        out_shape, out_specs, scratch = (out, out), (ospec, ospec), []
    return pl.pallas_call(
        functools.partial(_inv_dft_r2_kernel, interleave=interleave),
        out_shape=out_shape,
        grid=(HYENA_WIDTH // tc, batch),
        in_specs=[wspec, wspec,
                  pl.BlockSpec((None, 4, half, tc), lambda c, b: (b, 0, 0, c)),
                  pl.BlockSpec((half, tc), lambda c, b: (b, gate_cb + c)),
                  pl.BlockSpec((half, tc), lambda c, b: (b, gate_cb + c)),
                  pl.BlockSpec((half, tc), lambda c, b: (b, c)),
                  pl.BlockSpec((half, tc), lambda c, b: (b, c)),
                  pl.BlockSpec((None, 1, tc), lambda c, b: (layer * HYENA_ORDER + order, 0, c))],
        out_specs=out_specs,
        scratch_shapes=scratch,
        compiler_params=_cparams("parallel", "parallel"),
        name="hyena_inv_dft_r2",
    )(*inv_mats, s, gate_even, gate_odd, z_even, z_odd, hy_bias.reshape(DEPTH * HYENA_ORDER, 1, HYENA_WIDTH))


def _short_conv_split_kernel(u_ref, w_ref, b_ref, oe_ref, oo_ref, stage, *, seq):
    width = u_ref.shape[1]
    w0 = w_ref[0:1, :]
    w1 = w_ref[1:2, :]
    w2 = w_ref[2:3, :]
    b = b_ref[...]
    rc = _ROW_CHUNK
    hr = rc // 2

    def emit(base, window):
        prev = window[_CONV_PAD - 1:_CONV_PAD - 1 + rc]
        cur = window[_CONV_PAD:_CONV_PAD + rc]
        nxt = window[_CONV_PAD + 1:_CONV_PAD + 1 + rc]
        val = prev * w0 + cur * w1 + nxt * w2 + b
        hb = base // 2 if isinstance(base, int) else pl.multiple_of(base // 2, hr)
        for j in range(width // LANES):
            lanes = slice(j * LANES, (j + 1) * LANES)
            stage[j] = val[:, lanes]
            oe_ref[pl.ds(hb, hr), lanes] = stage[j, pl.ds(0, hr, stride=2), :]
            oo_ref[pl.ds(hb, hr), lanes] = stage[j, pl.ds(1, hr, stride=2), :]

    _halo_chunks(u_ref, seq, rc, _CONV_PAD, emit)


def _short_conv_split(u, conv_w, conv_b, layer, st):
    width = u.shape[1]
    seq = st.seq
    half = seq // 2
    tc = CH_TILE
    out = jax.ShapeDtypeStruct((st.batch * half, width), F32)
    ospec = pl.BlockSpec((half, tc), lambda b, j: (b, j))
    return pl.pallas_call(
        functools.partial(_short_conv_split_kernel, seq=seq),
        out_shape=(out, out),
        grid=(st.batch, width // tc),
        in_specs=[pl.BlockSpec((seq, tc), lambda b, j: (b, j)),
                  pl.BlockSpec((None, 3, tc), lambda b, j: (layer, 0, j)),
                  pl.BlockSpec((None, 1, tc), lambda b, j: (layer, 0, j))],
        out_specs=(ospec, ospec),
        scratch_shapes=[pltpu.VMEM((tc // LANES, _ROW_CHUNK, LANES), F32)],
        compiler_params=_cparams("parallel", "parallel"),
        name="hyena_short_conv_split",
    )(u, conv_w, conv_b.reshape(DEPTH, 1, width))


def _filter_taps_split_kernel(he_ref, ho_ref, dl_ref, wf_ref, wb_ref, ae_ref, ao_ref, de_ref, do_ref,
                              edge_ref, *, seq):
    shape = (seq // 2, dl_ref.shape[1])
    m = lax.broadcasted_iota(jnp.int32, shape, 0)

    def taps(h_ref, parity):
        t = (2 * m + parity).astype(F32) / float(seq - 1)
        dec = jnp.exp(-t * dl_ref[...])
        h = h_ref[...]
        return (jnp.dot(h, wf_ref[...], preferred_element_type=F32) * dec,
                jnp.dot(h, wb_ref[...], preferred_element_type=F32) * dec)

    fe, be = taps(he_ref, 0)
    fo, bo = taps(ho_ref, 1)
    be = jnp.where(m == 0, 0.0, be)
    total = (jnp.sum(jnp.abs(fe) + jnp.abs(be), axis=0, keepdims=True)
             + jnp.sum(jnp.abs(fo) + jnp.abs(bo), axis=0, keepdims=True))
    inv = 1.0 / total
    a_e = (fe + be) * inv
    a_o = (fo + bo) * inv
    ae_ref[...] = a_e
    ao_ref[...] = a_o
    de_ref[...] = (fe - be) * inv
    do_ref[...] = (fo - bo) * inv
    edge_ref[0:1, :] = jnp.sum(a_e, axis=0, keepdims=True) - jnp.sum(a_o, axis=0, keepdims=True)
    alt = jnp.where(m % 2 == 0, 1.0, -1.0)
    edge_ref[1:2, :] = jnp.sum(a_e * alt, axis=0, keepdims=True)


def _filter_taps_split(hidden, w3, layer, *, seq):
    tc = 256
    nct = HYENA_WIDTH // tc
    half = seq // 2
    deltas = jnp.abs(jnp.linspace(math.log(DECAY_TARGET) / SLOW_DECAY_PCT,
                                  math.log(DECAY_TARGET) / FAST_DECAY_PCT, HYENA_WIDTH, dtype=F32))
    tap_shape = jax.ShapeDtypeStruct((HYENA_ORDER, half, HYENA_WIDTH), F32)
    tap_spec = pl.BlockSpec((None, half, tc), lambda o, j: (o, 0, j))
    hspec = pl.BlockSpec((half, FILTER_HIDDEN), lambda o, j: (0, 0))
    return pl.pallas_call(
        functools.partial(_filter_taps_split_kernel, seq=seq),
        out_shape=(tap_shape,) * 4 + (jax.ShapeDtypeStruct((HYENA_ORDER, 2, HYENA_WIDTH), F32),),
        grid=(HYENA_ORDER, nct),
        in_specs=[hspec, hspec,
                  pl.BlockSpec((1, tc), lambda o, j: (0, j)),
                  pl.BlockSpec((None, FILTER_HIDDEN, tc), lambda o, j: (layer, 0, (2 * o) * nct + j)),
                  pl.BlockSpec((None, FILTER_HIDDEN, tc), lambda o, j: (layer, 0, (2 * o + 1) * nct + j))],
        out_specs=(tap_spec,) * 4 + (pl.BlockSpec((None, 2, tc), lambda o, j: (o, 0, j)),),
        compiler_params=_cparams("parallel", "parallel"),
        name="hyena_filter_taps_split",
    )(hidden[0::2], hidden[1::2], deltas.reshape(1, HYENA_WIDTH), w3, w3)


def _hyena_long(u, p, layer, st):
    batch, seq = st.batch, st.seq
    uce, uco = _short_conv_split(u, p["hy_conv_w"], p["hy_conv_b"], layer, st)
    hidden = _filter_mlp(seq, p["filt_w1"], p["filt_b1"], p["filt_w2"], p["filt_b2"], p["filt_freq"], layer)
    taps = _filter_taps_split(hidden, p["filt_w3"], layer, seq=seq)
    mats, inv_mats = _dft_matrices_r2(seq)
    spec = _filter_spectrum_r2(mats, *taps, seq=seq)
    nb = HYENA_WIDTH // CH_TILE
    hy_bias = p["hy_bias"]
    s = _fwd_dft_r2(mats, uce, 0, uco, 0, spec, 0, batch=batch, seq=seq)
    z1e, z1o = _inv_dft_r2(inv_mats, s, uce, uco, nb, z_even=uce, z_odd=uco, hy_bias=hy_bias, layer=layer,
                           order=0, out_dtype=F32, batch=batch, seq=seq)
    s = _fwd_dft_r2(mats, z1e, 0, z1o, 0, spec, 1, batch=batch, seq=seq)
    return _inv_dft_r2(inv_mats, s, uce, uco, 2 * nb, z_even=z1e, z_odd=z1o, hy_bias=hy_bias, layer=layer,
                       order=1, out_dtype=BF16, batch=batch, seq=seq, interleave=True)


def _hyena(u, p, layer, st):
    batch, seq = st.batch, st.seq
    if seq > SHORT_SEQ:
        return _hyena_long(u, p, layer, st)
    uc = _short_conv(u, p["hy_conv_w"], p["hy_conv_b"], layer, st)
    dft_top, dft_bot, dft_inv = _dft_matrices(seq)
    hidden = _filter_mlp(seq, p["filt_w1"], p["filt_b1"], p["filt_w2"], p["filt_b2"], p["filt_freq"], layer)
    taps_a, taps_d, nyq = _filter_taps(hidden, p["filt_w3"], layer, seq=seq)
    spec = _filter_spectrum(dft_top, dft_bot, taps_a, taps_d, nyq, seq=seq)
    s = _fwd_dft(dft_top, dft_bot, uc, 0, spec, 0, batch=batch, seq=seq)
    z1 = _inv_dft(dft_inv, s, uc, HYENA_WIDTH, uc, 0, p["hy_bias"], layer, 0, F32, batch=batch, seq=seq)
    s = _fwd_dft(dft_top, dft_bot, z1, 0, spec, 1, batch=batch, seq=seq)
    return _inv_dft(dft_inv, s, uc, 2 * HYENA_WIDTH, z1, 0, p["hy_bias"], layer, 1, BF16, batch=batch, seq=seq)


def _block(x, p, modflat, layer, st, ctx_kv, shared):
    h = _normmod(x, p["norm1_g"], modflat, st, layer, 1, 0)
    lin = functools.partial(_linear, h, p["w_in"], layer, tn=512, tn_reuse=1024, shared=shared)
    u_pool = lin(COL_POOL, POOL_WIDTH, kind="plain", out_dtype=F32, key="w_in:pool")
    q = lin(COL_Q, ATTN_WIDTH, kind="rmshead", out_dtype=BF16, head_gain=p["q_norm_g"], key="w_in:q")
    kv_dtype = F32 if ctx_kv is None else BF16
    k = lin(COL_K, ATTN_WIDTH, kind="rmshead", out_dtype=kv_dtype, head_gain=p["k_norm_g"], key="w_in:k")
    v = lin(COL_V, ATTN_WIDTH, kind="plain", out_dtype=kv_dtype, key="w_in:v")
    u_hy = lin(COL_HY, (HYENA_ORDER + 1) * HYENA_WIDTH, kind="plain", out_dtype=F32, key="w_in:hyena")
    gates = lin(COL_GATES, 3 * D_MODEL, kind="sigmoid", out_dtype=BF16, key="w_in:gates")

    pool = _pool(u_pool, p["pool_w"], p["pool_scale"], layer, st)
    if ctx_kv is None:
        attn = _ctx_attention(q, k, v, st)
    else:
        attn = _nbr_attention(q, k, v, ctx_kv[0], ctx_kv[1], ctx_kv[2], layer, st)
    hy = _hyena(u_hy, p, layer, st)

    mixed = _mix(pool, attn, hy, p["w_branch_pool"], p["w_branch_attn"], p["w_branch_hyena"], gates, layer,
                 shared)
    x = _linear(mixed, p["w_out"], layer, 0, D_MODEL, tn=512, tn_reuse=1024, kind="residual", out_dtype=F32,
                shared=shared, key="w_out", residual=x, modflat=modflat, st=st, which_gate=2)
    h = _normmod(x, p["norm2_g"], modflat, st, layer, 4, 3)
    a = _ffn1(h, p["w_gate"], p["w_up"], layer, shared)
    half = D_FF // 2
    for k_off in (0, half):
        x = _linear(a, p["w_down"], layer, 0, D_MODEL, tn=512, tm=512, tm_reuse=1024, kind="residual",
                    out_dtype=F32, shared=shared, key="w_down:%d" % k_off, k_off=k_off, k_len=half,
                    residual=x, modflat=modflat, st=st, which_gate=5)
    return x, k, v


_WEIGHT_NAMES = ("ada_w", "ada_b", "norm1_g", "norm2_g", "w_in", "pool_w", "pool_scale",
                 "q_norm_g", "k_norm_g", "rpb", "hy_conv_w", "hy_conv_b", "filt_w1", "filt_b1",
                 "filt_w2", "filt_b2", "filt_w3", "filt_freq", "hy_bias", "w_branch_pool",
                 "w_branch_attn", "w_branch_hyena", "w_out", "w_gate", "w_up", "w_down")


def kernel(x_prompt, x_sample, cache_k, cache_v, c, c_ctx, ada_w, ada_b, norm1_g, norm2_g, w_in, pool_w, pool_scale, q_norm_g, k_norm_g, rpb, hy_conv_w, hy_conv_b, filt_w1, filt_b1, filt_w2, filt_b2, filt_w3, filt_freq, hy_bias, w_branch_pool, w_branch_attn, w_branch_hyena, w_out, w_gate, w_up, w_down):
    p = dict(zip(_WEIGHT_NAMES, (ada_w, ada_b, norm1_g, norm2_g, w_in, pool_w, pool_scale, q_norm_g,
                                 k_norm_g, rpb, hy_conv_w, hy_conv_b, filt_w1, filt_b1, filt_w2, filt_b2,
                                 filt_w3, filt_freq, hy_bias, w_branch_pool, w_branch_attn,
                                 w_branch_hyena, w_out, w_gate, w_up, w_down)))
    nb, ns, d = x_prompt.shape
    lb, ls, _ = x_sample.shape
    assert d == D_MODEL and lb + 1 <= MOD_ROWS
    ctx = _Stream(nb, ns, 0, False)
    lat = _Stream(lb, ls, 1, True)

    cmat = jnp.concatenate([c_ctx[None, :], c, jnp.zeros((MOD_ROWS - 1 - lb, d), F32)], axis=0)
    mod = _modulation(cmat, ada_w, ada_b)
    modflat = mod.reshape(DEPTH * MOD_ROWS * 6, 1, d)

    y_ctx = x_prompt.reshape(nb * ns, d)
    y_lat = x_sample.reshape(lb * ls, d)
    new_k, new_v = [], []
    for layer in range(DEPTH):
        shared = _SharedWeights()
        bias = _bias_table(rpb, layer)
        y_lat, _, _ = _block(y_lat, p, modflat, layer, lat, (cache_k, cache_v, bias), shared)
        y_ctx, k_l, v_l = _block(y_ctx, p, modflat, layer, ctx, None, shared)
        new_k.append(k_l.reshape(nb, ns, N_HEADS, HEAD_DIM))
        new_v.append(v_l.reshape(nb, ns, N_HEADS, HEAD_DIM))

    return (y_ctx.reshape(nb, ns, d), y_lat.reshape(lb, ls, d),
            jnp.stack(new_k, axis=1), jnp.stack(new_v, axis=1))
```
